```python
import jax, jax.numpy as jnp
from jax import lax
import numpy as np

D_MODEL = 4096
BATCH = 4
SEQ = 2048
DEPTH = 2

HEAD_DIM = 128
ROT_DIM = HEAD_DIM // 4
ROPE_THETA = 500000.0
BLOCK = 128
NEG_INF = -1e30
EPS = 1e-6

A_HEADS = D_MODEL // (2 * HEAD_DIM)
A_WIDTH = A_HEADS * HEAD_DIM
DILATED_PAIRS = ((128, 1), (512, 4), (2048, 16))
B_HEADS = D_MODEL // (2 * HEAD_DIM)
Q_LORA = 1536
KV_LORA = 512
NOPE_DIM = 128
ROPE_DIM = 64
V_DIM = 128
QK_DIM = NOPE_DIM + ROPE_DIM
MIX_WIDTH = A_WIDTH + B_HEADS * V_DIM
IN_COLS = 3 * A_WIDTH + Q_LORA + KV_LORA + ROPE_DIM
C_HEADS = D_MODEL // HEAD_DIM
C_WIDTH = C_HEADS * HEAD_DIM
MEM_LEN = 256
X_HEADS = 4
X_DIM = X_HEADS * HEAD_DIM
D_FF = 14336
N_EXPERTS = 8
TOP_K = 2
D_FF_EXPERT = 4096

N_EVEN = (DEPTH + 1) // 2
N_ODD = DEPTH // 2

kernel_name = 'hybrid_dilated_mla_stickbreak_moe'


def rms_norm(x, g):
    xf = x.astype(jnp.float32)
    y = xf * lax.rsqrt(jnp.mean(xf * xf, axis=-1, keepdims=True) + EPS)
    return (y * g.astype(jnp.float32)).astype(x.dtype)


def rope_cos_sin(positions, dim):
    inv_freq = ROPE_THETA ** (-jnp.arange(0, dim, 2, dtype=jnp.float32) / dim)
    ang = positions.astype(jnp.float32)[..., None] * inv_freq
    return jnp.cos(ang)[:, :, None, :], jnp.sin(ang)[:, :, None, :]


def apply_rope(x, cos, sin):
    half = x.shape[-1] // 2
    x1 = x[..., :half].astype(jnp.float32)
    x2 = x[..., half:].astype(jnp.float32)
    return jnp.concatenate([x1 * cos - x2 * sin, x2 * cos + x1 * sin], axis=-1).astype(x.dtype)


def partial_rope(x, cos, sin):
    return jnp.concatenate([apply_rope(x[..., :ROT_DIM], cos, sin), x[..., ROT_DIM:]], axis=-1)


def band_attention(q, k, v, n_back):
    G, N, H, hd = q.shape
    nb = -(-N // BLOCK)
    pad = nb * BLOCK - N
    qp = jnp.pad(q, ((0, 0), (0, pad), (0, 0), (0, 0)))
    kp = jnp.pad(k, ((0, 0), (BLOCK, pad), (0, 0), (0, 0)))
    vp = jnp.pad(v, ((0, 0), (BLOCK, pad), (0, 0), (0, 0)))
    qb = qp.reshape(G, nb, BLOCK, H, hd)
    kb = kp.reshape(G, nb + 1, BLOCK, H, hd)
    vb = vp.reshape(G, nb + 1, BLOCK, H, hd)
    kw = jnp.concatenate([kb[:, :-1], kb[:, 1:]], axis=2)
    vw = jnp.concatenate([vb[:, :-1], vb[:, 1:]], axis=2)
    s = jnp.einsum('gnqhd,gnkhd->gnhqk', qb, kw).astype(jnp.float32) * (hd ** -0.5)
    qi = jnp.arange(BLOCK)[:, None]
    kj = jnp.arange(2 * BLOCK)[None, :]
    dist = qi + BLOCK - kj
    kpos = jnp.arange(nb)[:, None, None] * BLOCK - BLOCK + kj[None]
    valid = (dist >= 0) & (dist <= n_back) & (kpos >= 0)
    s = jnp.where(valid[None, :, None], s, NEG_INF)
    m = jnp.max(s, axis=-1, keepdims=True)
    lse = (m + jnp.log(jnp.sum(jnp.exp(s - m), axis=-1, keepdims=True)))[..., 0]
    p = jnp.exp(s - lse[..., None])
    o = jnp.einsum('gnhqk,gnkhd->gnqhd', p.astype(v.dtype), vw)
    o = o.reshape(G, nb * BLOCK, H, hd)[:, :N]
    lse = lse.transpose(0, 1, 3, 2).reshape(G, nb * BLOCK, H)[:, :N]
    return o, lse


def dilated_attention(q, k, v):
    B, S, H, hd = q.shape
    outs, lses = [], []
    for window, dil in DILATED_PAIRS:
        N = S // dil
        def stride(t):
            return t.reshape(B, N, dil, H, hd).transpose(0, 2, 1, 3, 4).reshape(B * dil, N, H, hd)
        o, lse = band_attention(stride(q), stride(k), stride(v), window // dil)
        outs.append(o.reshape(B, dil, N, H, hd).transpose(0, 2, 1, 3, 4).reshape(B, S, H, hd))
        lses.append(lse.reshape(B, dil, N, H).transpose(0, 2, 1, 3).reshape(B, S, H))
    w = jax.nn.softmax(jnp.stack(lses), axis=0)
    o = jnp.einsum('rbsh,rbshd->bshd', w, jnp.stack(outs).astype(jnp.float32))
    return o.astype(q.dtype)


def causal_block_attention(q, k, v, scale):
    B, S, H, dq = q.shape
    nb = S // BLOCK
    qb = q.reshape(B, nb, BLOCK, H, dq).swapaxes(0, 1)
    kpos = jnp.arange(S)

    def block(args):
        qblk, i = args
        s = jnp.einsum('bqhd,bkhd->bhqk', qblk, k).astype(jnp.float32) * scale
        qpos = i * BLOCK + jnp.arange(BLOCK)
        s = jnp.where(kpos[None, :] <= qpos[:, None], s, NEG_INF)
        p = jax.nn.softmax(s, axis=-1)
        return jnp.einsum('bhqk,bkhd->bqhd', p.astype(v.dtype), v)

    o = lax.map(block, (qb, jnp.arange(nb)))
    return o.swapaxes(0, 1).reshape(B, S, H, v.shape[-1])


def stick_breaking_attention(q, k, v):
    B, S, H, hd = q.shape
    nb = S // BLOCK
    scale = hd ** -0.5
    qb = q.reshape(B, nb, BLOCK, H, hd).swapaxes(0, 1)
    kpos = jnp.arange(S)

    def block(args):
        qblk, i = args
        z = jnp.einsum('bqhd,bkhd->bhqk', qblk, k).astype(jnp.float32) * scale
        qpos = i * BLOCK + jnp.arange(BLOCK)
        earlier = kpos[None, :] < qpos[:, None]
        log_beta = jax.nn.log_sigmoid(z)
        log_keep = jnp.where(earlier, jax.nn.log_sigmoid(-z), 0.0)
        later = lax.cumsum(log_keep, axis=3, reverse=True) - log_keep
        a = jnp.where(earlier, jnp.exp(log_beta + later), 0.0)
        return jnp.einsum('bhqk,bkhd->bqhd', a.astype(v.dtype), v)

    o = lax.map(block, (qb, jnp.arange(nb)))
    return o.swapaxes(0, 1).reshape(B, S, H, hd)


def even_mixer(h, cos_a, sin_a, cos_b, sin_b, w_in, ga_q, ga_k, g_cq, w_uq, g_ckv, w_ukv,
               gb_q, gb_kn, gb_kr, w_o):
    B, S, _ = h.shape
    proj = h @ w_in
    offs = [A_WIDTH, 2 * A_WIDTH, 3 * A_WIDTH, 3 * A_WIDTH + Q_LORA, 3 * A_WIDTH + Q_LORA + KV_LORA]
    qa, ka, va, cq, ckv, kr = jnp.split(proj, offs, axis=-1)
    qa = partial_rope(rms_norm(qa.reshape(B, S, A_HEADS, HEAD_DIM), ga_q), cos_a, sin_a)
    ka = partial_rope(rms_norm(ka.reshape(B, S, A_HEADS, HEAD_DIM), ga_k), cos_a, sin_a)
    va = va.reshape(B, S, A_HEADS, HEAD_DIM)
    o_a = dilated_attention(qa, ka, va).reshape(B, S, A_WIDTH)
    q = (rms_norm(cq, g_cq) @ w_uq).reshape(B, S, B_HEADS, QK_DIM)
    q = rms_norm(q, gb_q)
    q = jnp.concatenate([q[..., :NOPE_DIM], apply_rope(q[..., NOPE_DIM:], cos_b, sin_b)], axis=-1)
    kv = (rms_norm(ckv, g_ckv) @ w_ukv).reshape(B, S, B_HEADS, NOPE_DIM + V_DIM)
    k_nope = rms_norm(kv[..., :NOPE_DIM], gb_kn)
    vb = kv[..., NOPE_DIM:]
    k_rope = apply_rope(rms_norm(kr.reshape(B, S, 1, ROPE_DIM), gb_kr), cos_b, sin_b)
    kb = jnp.concatenate([k_nope, jnp.broadcast_to(k_rope, (B, S, B_HEADS, ROPE_DIM))], axis=-1)
    o_b = causal_block_attention(q, kb, vb, QK_DIM ** -0.5).reshape(B, S, B_HEADS * V_DIM)
    return jnp.concatenate([o_a, o_b], axis=-1) @ w_o


def odd_mixer(h, w_qkv, w_o):
    B, S, _ = h.shape
    qkv = (h @ w_qkv).reshape(B, S, 3, C_HEADS, HEAD_DIM)
    o = stick_breaking_attention(qkv[:, :, 0], qkv[:, :, 1], qkv[:, :, 2])
    return o.reshape(B, S, C_WIDTH) @ w_o


def memory_cross_attention(h, mem_k, mem_v, w_xq, g_xq, w_xo):
    B, S, _ = h.shape
    q = rms_norm((h @ w_xq).reshape(B, S, X_HEADS, HEAD_DIM), g_xq)
    s = jnp.einsum('bshd,bmhd->bhsm', q, mem_k).astype(jnp.float32) * (HEAD_DIM ** -0.5)
    p = jax.nn.softmax(s, axis=-1)
    o = jnp.einsum('bhsm,bmhd->bshd', p.astype(mem_v.dtype), mem_v).reshape(B, S, X_DIM)
    return o @ w_xo


def swiglu(t, w_gate, w_up, w_down):
    return (jax.nn.silu(t @ w_gate) * (t @ w_up)) @ w_down


def moe_swiglu(h, w_router, b_router, w_egate, w_eup, w_edown):
    B, S, D = h.shape
    t = h.reshape(B * S, D)
    logits = (t @ w_router).astype(jnp.float32) + b_router.astype(jnp.float32)
    top_logit, top_idx = lax.top_k(logits, TOP_K)
    top_w = jax.nn.softmax(top_logit, axis=-1)
    gates = jnp.einsum('tk,tke->te', top_w, jax.nn.one_hot(top_idx, N_EXPERTS, dtype=jnp.float32))
    y = jnp.zeros_like(t)
    for e in range(N_EXPERTS):
        y = y + gates[:, e:e + 1].astype(t.dtype) * swiglu(t, w_egate[e], w_eup[e], w_edown[e])
    return y.reshape(B, S, D)


def setup_inputs(seed: int = 0) -> dict:
    key = jax.random.key(seed)
    ks = jax.random.split(key, 40)
    f32 = jnp.float32

    def w(i, shape, fan_in):
        return jax.random.normal(ks[i], shape, f32) * (fan_in ** -0.5)

    def g(i, shape):
        return 1.0 + 0.02 * jax.random.normal(ks[i], shape, f32)

    E, O = N_EVEN, N_ODD
    positions = (jax.random.randint(ks[2], (BATCH, 1), 0, 4096, dtype=jnp.int32)
                 + jnp.arange(SEQ, dtype=jnp.int32)[None, :])
    return {
        'x': jax.random.normal(ks[0], (BATCH, SEQ, D_MODEL), f32),
        'mem': jax.random.normal(ks[1], (BATCH, MEM_LEN, D_MODEL), f32),
        'positions': positions,
        'g_mem': g(3, (D_MODEL,)),
        'w_mem_kv': w(4, (D_MODEL, 2 * X_DIM), D_MODEL),
        'g_mem_k': g(5, (HEAD_DIM,)),
        'g_mix': g(6, (DEPTH, D_MODEL)),
        'g_x': g(7, (DEPTH, D_MODEL)),
        'w_xq': w(8, (DEPTH, D_MODEL, X_DIM), D_MODEL),
        'g_xq': g(9, (DEPTH, HEAD_DIM)),
        'w_xo': w(10, (DEPTH, X_DIM, D_MODEL), X_DIM),
        'g_ffn': g(11, (DEPTH, D_MODEL)),
        'w_in': w(12, (E, D_MODEL, IN_COLS), D_MODEL),
        'ga_q': g(13, (E, HEAD_DIM)),
        'ga_k': g(14, (E, HEAD_DIM)),
        'g_cq': g(15, (E, Q_LORA)),
        'w_uq': w(16, (E, Q_LORA, B_HEADS * QK_DIM), Q_LORA),
        'g_ckv': g(17, (E, KV_LORA)),
        'w_ukv': w(18, (E, KV_LORA, B_HEADS * (NOPE_DIM + V_DIM)), KV_LORA),
        'gb_q': g(19, (E, QK_DIM)),
        'gb_kn': g(20, (E, NOPE_DIM)),
        'gb_kr': g(21, (E, ROPE_DIM)),
        'w_o_even': w(22, (E, MIX_WIDTH, D_MODEL), MIX_WIDTH),
        'w_gate': w(23, (E, D_MODEL, D_FF), D_MODEL),
        'w_up': w(24, (E, D_MODEL, D_FF), D_MODEL),
        'w_down': w(25, (E, D_FF, D_MODEL), D_FF),
        'w_qkv': w(26, (O, D_MODEL, 3 * C_WIDTH), D_MODEL),
        'w_o_odd': w(27, (O, C_WIDTH, D_MODEL), C_WIDTH),
        'w_router': w(28, (O, D_MODEL, N_EXPERTS), D_MODEL),
        'b_router': 0.01 * jax.random.normal(ks[29], (O, N_EXPERTS), f32),
        'w_egate': w(30, (O, N_EXPERTS, D_MODEL, D_FF_EXPERT), D_MODEL),
        'w_eup': w(31, (O, N_EXPERTS, D_MODEL, D_FF_EXPERT), D_MODEL),
        'w_edown': w(32, (O, N_EXPERTS, D_FF_EXPERT, D_MODEL), D_FF_EXPERT),
    }


def reference(x, mem, positions, g_mem, w_mem_kv, g_mem_k, g_mix, g_x, w_xq, g_xq, w_xo, g_ffn,
              w_in, ga_q, ga_k, g_cq, w_uq, g_ckv, w_ukv, gb_q, gb_kn, gb_kr, w_o_even,
              w_gate, w_up, w_down, w_qkv, w_o_odd, w_router, b_router, w_egate, w_eup, w_edown):
    B, M = mem.shape[0], mem.shape[1]
    cos_a, sin_a = rope_cos_sin(positions, ROT_DIM)
    cos_b, sin_b = rope_cos_sin(positions, ROPE_DIM)
    mk, mv = jnp.split(rms_norm(mem, g_mem) @ w_mem_kv, 2, axis=-1)
    mem_k = rms_norm(mk.reshape(B, M, X_HEADS, HEAD_DIM), g_mem_k)
    mem_v = mv.reshape(B, M, X_HEADS, HEAD_DIM)
    for layer in range(DEPTH):
        i = layer // 2
        h = rms_norm(x, g_mix[layer])
        if layer % 2 == 0:
            x = x + even_mixer(h, cos_a, sin_a, cos_b, sin_b, w_in[i], ga_q[i], ga_k[i], g_cq[i],
                               w_uq[i], g_ckv[i], w_ukv[i], gb_q[i], gb_kn[i], gb_kr[i], w_o_even[i])
        else:
            x = x + odd_mixer(h, w_qkv[i], w_o_odd[i])
        h = rms_norm(x, g_x[layer])
        x = x + memory_cross_attention(h, mem_k, mem_v, w_xq[layer], g_xq[layer], w_xo[layer])
        h = rms_norm(x, g_ffn[layer])
        if layer % 2 == 0:
            x = x + swiglu(h, w_gate[i], w_up[i], w_down[i])
        else:
            x = x + moe_swiglu(h, w_router[i], b_router[i], w_egate[i], w_eup[i], w_edown[i])
    return x
```

```python
import functools

import jax
import jax.numpy as jnp
from jax import lax
from jax.experimental import pallas as pl
from jax.experimental.pallas import tpu as pltpu

F32 = jnp.float32
BF16 = jnp.bfloat16

HEAD_DIM = 128
ROT_DIM = HEAD_DIM // 4
ROPE_THETA = 500000.0
BLOCK = 128
NEG_INF = -1e30
EPS = 1e-6
DILATED_PAIRS = ((128, 1), (512, 4), (2048, 16))
A_HEADS = 16
B_HEADS = 16
Q_LORA = 1536
KV_LORA = 512
NOPE_DIM = 128
ROPE_DIM = 64
QK_DIM = NOPE_DIM + ROPE_DIM
X_HEADS = 4
N_EXPERTS = 8
TOP_K = 2

LANES = 128
V7X_VMEM_BYTES = 64 * 1024 * 1024
VMEM_CAP = V7X_VMEM_BYTES - 6 * 1024 * 1024


def _params(semantics, vmem_bytes):
    return pltpu.CompilerParams(dimension_semantics=semantics,
                                vmem_limit_bytes=int(min(max(vmem_bytes, 32 * 1024 * 1024), VMEM_CAP)))


def _nbytes(shape, dtype):
    n = 1
    for s in shape:
        n *= s
    return n * jnp.dtype(dtype).itemsize


def _rmsnorm_kernel(x_ref, g_ref, o_ref):
    x = x_ref[...].astype(F32)
    ms = jnp.mean(x * x, axis=-1, keepdims=True)
    o_ref[...] = (x * lax.rsqrt(ms + EPS) * g_ref[...]).astype(o_ref.dtype)


def rmsnorm(x, g, *, width=None, col_block=0, tm=256, out_dtype=BF16):
    T = x.shape[0]
    width = x.shape[1] if width is None else width
    tm = min(tm, T)
    return pl.pallas_call(
        _rmsnorm_kernel,
        grid=(T // tm,),
        in_specs=[pl.BlockSpec((tm, width), lambda i: (i, col_block)),
                  pl.BlockSpec((1, width), lambda i: (0, 0))],
        out_specs=pl.BlockSpec((tm, width), lambda i: (i, 0)),
        out_shape=jax.ShapeDtypeStruct((T, width), out_dtype),
        compiler_params=_params(("parallel",), 6 * _nbytes((tm, width), F32)),
        name="rmsnorm",
    )(x, g.reshape(1, width).astype(F32))


def _mm_kernel(*refs, nk, n_x, has_res):
    x_refs = refs[:n_x]
    w_ref = refs[n_x]
    rest = refs[n_x + 1:]
    res_ref = rest[0] if has_res else None
    o_ref = rest[1] if has_res else rest[0]
    acc_ref = rest[-1] if nk > 1 else None
    k = pl.program_id(2)
    w = w_ref[...].astype(BF16)

    def finish(part):
        out = part
        if has_res:
            out = out + res_ref[...]
        o_ref[...] = out.astype(o_ref.dtype)

    if n_x == 2:
        @pl.when(k == 0)
        def _():
            acc_ref[...] = jnp.dot(x_refs[0][...], w, preferred_element_type=F32)

        @pl.when(k == 1)
        def _():
            finish(acc_ref[...] + jnp.dot(x_refs[1][...], w, preferred_element_type=F32))
    elif nk == 1:
        finish(jnp.dot(x_refs[0][...], w, preferred_element_type=F32))
    else:
        part = jnp.dot(x_refs[0][...], w, preferred_element_type=F32)

        @pl.when(k == 0)
        def _():
            acc_ref[...] = part

        @pl.when((k > 0) & (k < nk - 1))
        def _():
            acc_ref[...] += part

        @pl.when(k == nk - 1)
        def _():
            finish(acc_ref[...] + part)


def matmul(xs, w, *, n_out=None, res=None, out_dtype=F32, tm=1024, tn=512, tk=None, name="matmul"):
    if not isinstance(xs, (tuple, list)):
        xs = (xs,)
    n_x = len(xs)
    T = xs[0].shape[0]
    K = sum(x.shape[1] for x in xs)
    n_out = w.shape[1] if n_out is None else n_out
    tm = min(tm, T)
    tn = min(tn, n_out)
    if n_x == 2:
        tk = xs[0].shape[1]
        assert xs[1].shape[1] == tk
    tk = K if tk is None else tk
    nk = K // tk
    assert K == nk * tk and T % tm == 0 and n_out % tn == 0
    if n_x == 2:
        x_specs = [pl.BlockSpec((tm, tk), lambda i, j, k: (i, 0)) for _ in xs]
    else:
        x_specs = [pl.BlockSpec((tm, tk), lambda i, j, k: (i, k))]
    in_specs = x_specs + [pl.BlockSpec((tk, tn), lambda i, j, k: (k, j))]
    args = list(xs) + [w]
    if res is not None:
        in_specs.append(pl.BlockSpec((tm, tn), lambda i, j, k: (i, j)))
        args.append(res)
    scratch = [pltpu.VMEM((tm, tn), F32)] if nk > 1 else []
    vmem = (2 * n_x * _nbytes((tm, tk), xs[0].dtype) + 2 * _nbytes((tk, tn), F32) + _nbytes((tk, tn), BF16)
            + (2 * _nbytes((tm, tn), F32) if res is not None else 0)
            + 2 * _nbytes((tm, tn), out_dtype) + 2 * _nbytes((tm, tn), F32))
    return pl.pallas_call(
        functools.partial(_mm_kernel, nk=nk, n_x=n_x, has_res=res is not None),
        grid=(T // tm, n_out // tn, nk),
        in_specs=in_specs,
        out_specs=pl.BlockSpec((tm, tn), lambda i, j, k: (i, j)),
        out_shape=jax.ShapeDtypeStruct((T, n_out), out_dtype),
        scratch_shapes=scratch,
        compiler_params=_params(("parallel", "parallel", "arbitrary"), vmem),
        name=name,
    )(*args)


def _rope_tables(positions, dim):
    half = dim // 2
    inv_freq = ROPE_THETA ** (-jnp.arange(0, dim, 2, dtype=F32) / dim)
    ang = positions.astype(F32)[..., None] * inv_freq
    cos, sin = jnp.cos(ang), jnp.sin(ang)
    B, S = positions.shape
    ones = jnp.ones((B, S, LANES - dim), F32)
    zeros = jnp.zeros((B, S, LANES - dim), F32)
    zh = jnp.zeros((B, S, half), F32)
    c = jnp.concatenate([cos, cos, ones], axis=-1)
    sa = jnp.concatenate([-sin, zh, zeros], axis=-1)
    sb = jnp.concatenate([zh, sin, zeros], axis=-1)
    return [t.reshape(B * S, LANES) for t in (c, sa, sb)]


def _rope_lanes(y, c, sa, sb, half):
    return y * c + pltpu.roll(y, LANES - half, 1) * sa + pltpu.roll(y, half, 1) * sb


def _mixer_a_kernel(q_ref, k_ref, v_ref, c_ref, sa_ref, sb_ref, gq_ref, gk_ref, o_ref,
                    qs, ks, acc, m_sc, l_sc, *, seq):
    scale = HEAD_DIM ** -0.5
    c, sa, sb = c_ref[...], sa_ref[...], sb_ref[...]

    def prep(x, g):
        y = x * lax.rsqrt(jnp.mean(x * x, axis=-1, keepdims=True) + EPS) * g
        return _rope_lanes(y, c, sa, sb, ROT_DIM // 2)

    qs[...] = prep(q_ref[...], gq_ref[...])
    ks[...] = prep(k_ref[...], gk_ref[...])
    m_sc[...] = jnp.full(m_sc.shape, NEG_INF, F32)
    l_sc[...] = jnp.zeros(l_sc.shape, F32)
    acc[...] = jnp.zeros(acc.shape, F32)

    for window, dil in DILATED_PAIRS:
        n_back = window // dil
        nb = seq // dil // BLOCK
        assert n_back <= BLOCK and nb * dil * BLOCK == seq

        def rows_at(r, n, dil=dil):
            start = r + dil * BLOCK * n
            if dil == 1:
                return pl.ds(pl.multiple_of(start, BLOCK), BLOCK)
            return pl.ds(start, BLOCK, stride=dil)

        def block(idx, carry, dil=dil, nb=nb, n_back=n_back, rows_at=rows_at):
            r = idx // nb
            n = idx % nb
            rows = rows_at(r, n)
            qb = qs[rows, :].astype(BF16)
            kk = ks[rows, :].astype(BF16)
            vv = v_ref[rows, :].astype(BF16)
            qi = lax.broadcasted_iota(jnp.int32, (BLOCK, 1), 0)
            if nb > 1:
                prev = rows_at(r, jnp.maximum(n - 1, 0))
                kk = jnp.concatenate([ks[prev, :].astype(BF16), kk], axis=0)
                vv = jnp.concatenate([v_ref[prev, :].astype(BF16), vv], axis=0)
                kj = lax.broadcasted_iota(jnp.int32, (1, 2 * BLOCK), 1)
                dist = qi + BLOCK - kj
                valid = (dist >= 0) & (dist <= n_back) & ((kj >= BLOCK) | (n > 0))
            else:
                kj = lax.broadcasted_iota(jnp.int32, (1, BLOCK), 1)
                dist = qi - kj
                valid = (dist >= 0) & (dist <= n_back)
            s = lax.dot_general(qb, kk, (((1,), (1,)), ((), ())), preferred_element_type=F32) * scale
            s = jnp.where(valid, s, NEG_INF)
            m_old = m_sc[rows, :]
            m_new = jnp.maximum(m_old, jnp.max(s, axis=-1, keepdims=True))
            alpha = jnp.exp(m_old - m_new)
            p = jnp.exp(s - m_new[:, :1])
            l_sc[rows, :] = alpha * l_sc[rows, :] + jnp.sum(p, axis=-1, keepdims=True)
            acc[rows, :] = alpha * acc[rows, :] + jnp.dot(p.astype(BF16), vv, preferred_element_type=F32)
            m_sc[rows, :] = m_new
            return carry

        lax.fori_loop(0, dil * nb, block, 0)

    o_ref[...] = (acc[...] / l_sc[...]).astype(o_ref.dtype)


def mixer_a(proj, tabs, gq, gk, *, batch, seq):
    T = batch * seq
    blk = (seq, HEAD_DIM)
    head = lambda off: pl.BlockSpec(blk, lambda b, h: (b, off + h))
    tab = pl.BlockSpec(blk, lambda b, h: (b, 0))
    gspec = pl.BlockSpec((1, HEAD_DIM), lambda b, h: (0, 0))
    return pl.pallas_call(
        functools.partial(_mixer_a_kernel, seq=seq),
        grid=(batch, A_HEADS),
        in_specs=[head(0), head(A_HEADS), head(2 * A_HEADS), tab, tab, tab, gspec, gspec],
        out_specs=pl.BlockSpec(blk, lambda b, h: (b, h)),
        out_shape=jax.ShapeDtypeStruct((T, A_HEADS * HEAD_DIM), BF16),
        scratch_shapes=[pltpu.VMEM(blk, F32) for _ in range(5)],
        compiler_params=_params(("parallel", "parallel"), 24 * _nbytes(blk, F32)),
        name="mixer_a",
    )(proj, proj, proj, *tabs, gq.reshape(1, HEAD_DIM), gk.reshape(1, HEAD_DIM))


def _mla_prep_kernel(qn_ref, qr_ref, kn_ref, v_ref, kr_ref, c_ref, sa_ref, sb_ref,
                     gqn_ref, gqr_ref, gkn_ref, gkr_ref, qf_ref, kf_ref, vb_ref):
    c, sa, sb = c_ref[...], sa_ref[...], sb_ref[...]
    half = ROPE_DIM // 2
    qn, qr = qn_ref[...], qr_ref[...]
    ss = jnp.sum(qn * qn, axis=-1, keepdims=True) + jnp.sum(qr * qr, axis=-1, keepdims=True)
    inv = lax.rsqrt(ss * (1.0 / QK_DIM) + EPS)
    qf_ref[:, :NOPE_DIM] = (qn * inv * gqn_ref[...]).astype(qf_ref.dtype)
    qf_ref[:, NOPE_DIM:] = _rope_lanes(qr * inv * gqr_ref[...], c, sa, sb, half).astype(qf_ref.dtype)
    kn = kn_ref[...]
    kn = kn * lax.rsqrt(jnp.mean(kn * kn, axis=-1, keepdims=True) + EPS) * gkn_ref[...]
    kr = kr_ref[...]
    kr = kr * lax.rsqrt(jnp.sum(kr * kr, axis=-1, keepdims=True) * (1.0 / ROPE_DIM) + EPS) * gkr_ref[...]
    kf_ref[:, :NOPE_DIM] = kn.astype(kf_ref.dtype)
    kf_ref[:, NOPE_DIM:] = _rope_lanes(kr, c, sa, sb, half).astype(kf_ref.dtype)
    vb_ref[...] = v_ref[...].astype(vb_ref.dtype)


def mla_prep(q_up, kv_up, krp, tabs, gqn, gqr, gkn, gkr, *, tm=512):
    T = q_up.shape[0]
    H = B_HEADS
    blk = lambda f: pl.BlockSpec((tm, LANES), f)
    g = pl.BlockSpec((1, LANES), lambda i, h: (0, 0))
    return pl.pallas_call(
        _mla_prep_kernel,
        grid=(T // tm, H),
        in_specs=[blk(lambda i, h: (i, h)), blk(lambda i, h: (i, H + h)),
                  blk(lambda i, h: (i, 2 * h)), blk(lambda i, h: (i, 2 * h + 1)),
                  blk(lambda i, h: (i, 0)), blk(lambda i, h: (i, 0)), blk(lambda i, h: (i, 0)),
                  blk(lambda i, h: (i, 0)), g, g, g, g],
        out_specs=[pl.BlockSpec((tm, 2 * LANES), lambda i, h: (i, h)),
                   pl.BlockSpec((tm, 2 * LANES), lambda i, h: (i, h)),
                   pl.BlockSpec((tm, LANES), lambda i, h: (i, h))],
        out_shape=[jax.ShapeDtypeStruct((T, H * 2 * LANES), BF16),
                   jax.ShapeDtypeStruct((T, H * 2 * LANES), BF16),
                   jax.ShapeDtypeStruct((T, H * LANES), BF16)],
        compiler_params=_params(("parallel", "parallel"), 40 * _nbytes((tm, LANES), F32)),
        name="mla_prep",
    )(q_up, q_up, kv_up, kv_up, krp, *tabs, gqn, gqr, gkn, gkr)


def _causal_attn_kernel(q_ref, k_ref, v_ref, o_ref, m_sc, l_sc, acc, *, tq, tk, scale):
    qi = pl.program_id(2)
    m_sc[...] = jnp.full(m_sc.shape, NEG_INF, F32)
    l_sc[...] = jnp.zeros(l_sc.shape, F32)
    acc[...] = jnp.zeros(acc.shape, F32)
    q = q_ref[...]
    qpos = qi * tq + lax.broadcasted_iota(jnp.int32, (tq, 1), 0)

    def chunk(j, carry):
        rows = pl.ds(pl.multiple_of(j * tk, tk), tk)
        s = lax.dot_general(q, k_ref[rows, :], (((1,), (1,)), ((), ())), preferred_element_type=F32) * scale
        kpos = j * tk + lax.broadcasted_iota(jnp.int32, (1, tk), 1)
        s = jnp.where(kpos <= qpos, s, NEG_INF)
        m_old = m_sc[...]
        m_new = jnp.maximum(m_old, jnp.max(s, axis=-1, keepdims=True))
        alpha = jnp.exp(m_old - m_new)
        p = jnp.exp(s - m_new)
        l_sc[...] = alpha * l_sc[...] + jnp.sum(p, axis=-1, keepdims=True)
        acc[...] = alpha * acc[...] + jnp.dot(p.astype(BF16), v_ref[rows, :], preferred_element_type=F32)
        m_sc[...] = m_new
        return carry

    lax.fori_loop(0, (qi + 1) * (tq // tk), chunk, 0)
    o_ref[...] = (acc[...] / l_sc[...]).astype(o_ref.dtype)


def mla_attention(qf, kf, vb, *, batch, seq, tq=256, tk=256):
    T = batch * seq
    nq = seq // tq
    dq = qf.shape[1] // B_HEADS
    dv = vb.shape[1] // B_HEADS
    return pl.pallas_call(
        functools.partial(_causal_attn_kernel, tq=tq, tk=tk, scale=QK_DIM ** -0.5),
        grid=(batch, B_HEADS, nq),
        in_specs=[pl.BlockSpec((tq, dq), lambda b, h, i: (b * nq + i, h)),
                  pl.BlockSpec((seq, dq), lambda b, h, i: (b, h)),
                  pl.BlockSpec((seq, dv), lambda b, h, i: (b, h))],
        out_specs=pl.BlockSpec((tq, dv), lambda b, h, i: (b * nq + i, h)),
        out_shape=jax.ShapeDtypeStruct((T, B_HEADS * dv), BF16),
        scratch_shapes=[pltpu.VMEM((tq, 1), F32), pltpu.VMEM((tq, 1), F32), pltpu.VMEM((tq, dv), F32)],
        compiler_params=_params(("parallel", "parallel", "parallel"), 0),
        name="mla_attention",
    )(qf, kf, vb)


def _stickbreak_kernel(q_ref, k_ref, v_ref, o_ref, c_sc, acc, *, tq, tk, scale):
    qi = pl.program_id(2)
    c_sc[...] = jnp.zeros(c_sc.shape, F32)
    acc[...] = jnp.zeros(acc.shape, F32)
    q = q_ref[...]
    qpos = qi * tq + lax.broadcasted_iota(jnp.int32, (tq, 1), 0)
    strict = (lax.broadcasted_iota(jnp.int32, (tk, tk), 0)
              > lax.broadcasted_iota(jnp.int32, (tk, tk), 1)).astype(BF16)
    nch = (qi + 1) * (tq // tk)

    def chunk(jj, carry):
        j = nch - 1 - jj
        rows = pl.ds(pl.multiple_of(j * tk, tk), tk)
        z = lax.dot_general(q, k_ref[rows, :], (((1,), (1,)), ((), ())), preferred_element_type=F32) * scale
        kpos = j * tk + lax.broadcasted_iota(jnp.int32, (1, tk), 1)
        earlier = kpos < qpos
        softplus = jnp.maximum(z, 0.0) + jnp.log1p(jnp.exp(-jnp.abs(z)))
        log_keep = jnp.where(earlier, -softplus, 0.0)
        hi = log_keep.astype(BF16)
        lo = (log_keep - hi.astype(F32)).astype(BF16)
        within = (jnp.dot(hi, strict, preferred_element_type=F32)
                  + jnp.dot(lo, strict, preferred_element_type=F32))
        later = within + c_sc[...]
        a = jnp.where(earlier, jnp.exp(z - softplus + later), 0.0)
        acc[...] += jnp.dot(a.astype(BF16), v_ref[rows, :], preferred_element_type=F32)
        c_sc[...] += jnp.sum(log_keep, axis=-1, keepdims=True)
        return carry

    lax.fori_loop(0, nch, chunk, 0)
    o_ref[...] = acc[...].astype(o_ref.dtype)


def stickbreak_attention(qkv, *, batch, seq, heads, tq=256, tk=256):
    T = batch * seq
    nq = seq // tq
    return pl.pallas_call(
        functools.partial(_stickbreak_kernel, tq=tq, tk=tk, scale=HEAD_DIM ** -0.5),
        grid=(batch, heads, nq),
        in_specs=[pl.BlockSpec((tq, HEAD_DIM), lambda b, h, i: (b * nq + i, h)),
                  pl.BlockSpec((seq, HEAD_DIM), lambda b, h, i: (b, heads + h)),
                  pl.BlockSpec((seq, HEAD_DIM), lambda b, h, i: (b, 2 * heads + h))],
        out_specs=pl.BlockSpec((tq, HEAD_DIM), lambda b, h, i: (b * nq + i, h)),
        out_shape=jax.ShapeDtypeStruct((T, heads * HEAD_DIM), BF16),
        scratch_shapes=[pltpu.VMEM((tq, 1), F32), pltpu.VMEM((tq, HEAD_DIM), F32)],
        compiler_params=_params(("parallel", "parallel", "parallel"), 0),
        name="stickbreak",
    )(qkv, qkv, qkv)


def _memkv_kernel(kv_ref, g_ref, mk_ref, mv_ref):
    for h in range(X_HEADS):
        cols = slice(h * HEAD_DIM, (h + 1) * HEAD_DIM)
        k = kv_ref[:, cols]
        k = k * lax.rsqrt(jnp.mean(k * k, axis=-1, keepdims=True) + EPS) * g_ref[...]
        mk_ref[:, cols] = k.astype(mk_ref.dtype)
    mv_ref[...] = kv_ref[:, X_HEADS * HEAD_DIM:].astype(mv_ref.dtype)


def memkv_post(kv, g):
    M = kv.shape[0]
    xd = X_HEADS * HEAD_DIM
    return pl.pallas_call(
        _memkv_kernel,
        grid=(1,),
        in_specs=[pl.BlockSpec((M, 2 * xd), lambda i: (0, 0)), pl.BlockSpec((1, HEAD_DIM), lambda i: (0, 0))],
        out_specs=[pl.BlockSpec((M, xd), lambda i: (0, 0)), pl.BlockSpec((M, xd), lambda i: (0, 0))],
        out_shape=[jax.ShapeDtypeStruct((M, xd), BF16), jax.ShapeDtypeStruct((M, xd), BF16)],
        compiler_params=_params(("arbitrary",), 0),
        name="memkv_post",
    )(kv, g.reshape(1, HEAD_DIM))


def _xattn_kernel(q_ref, mk_ref, mv_ref, g_ref, o_ref):
    scale = HEAD_DIM ** -0.5
    for h in range(X_HEADS):
        cols = slice(h * HEAD_DIM, (h + 1) * HEAD_DIM)
        q = q_ref[:, cols]
        q = (q * lax.rsqrt(jnp.mean(q * q, axis=-1, keepdims=True) + EPS) * g_ref[...]).astype(BF16)
        s = lax.dot_general(q, mk_ref[:, cols], (((1,), (1,)), ((), ())), preferred_element_type=F32) * scale
        p = jnp.exp(s - jnp.max(s, axis=-1, keepdims=True))
        p = p / jnp.sum(p, axis=-1, keepdims=True)
        o_ref[:, cols] = jnp.dot(p.astype(BF16), mv_ref[:, cols], preferred_element_type=F32).astype(o_ref.dtype)


def cross_attention_core(qx, mk, mv, g, *, seq, mem_len, tm=512):
    T = qx.shape[0]
    xd = X_HEADS * HEAD_DIM
    per_seq = seq // tm
    return pl.pallas_call(
        _xattn_kernel,
        grid=(T // tm,),
        in_specs=[pl.BlockSpec((tm, xd), lambda i: (i, 0)),
                  pl.BlockSpec((mem_len, xd), lambda i: (i // per_seq, 0)),
                  pl.BlockSpec((mem_len, xd), lambda i: (i // per_seq, 0)),
                  pl.BlockSpec((1, HEAD_DIM), lambda i: (0, 0))],
        out_specs=pl.BlockSpec((tm, xd), lambda i: (i, 0)),
        out_shape=jax.ShapeDtypeStruct((T, xd), BF16),
        compiler_params=_params(("parallel",), 0),
        name="cross_attention",
    )(qx, mk, mv, g.reshape(1, HEAD_DIM))


def _gateup_kernel(x_ref, wg_ref, wu_ref, o_ref):
    x = x_ref[...]
    g = jnp.dot(x, wg_ref[...].astype(BF16), preferred_element_type=F32)
    u = jnp.dot(x, wu_ref[...].astype(BF16), preferred_element_type=F32)
    o_ref[...] = (g * jax.nn.sigmoid(g) * u).astype(o_ref.dtype)


def swiglu_gateup(x, wg, wu, *, tm=1024, tf=256):
    T, K = x.shape
    F = wg.shape[1]
    vmem = (2 * _nbytes((tm, K), BF16) + 4 * _nbytes((K, tf), F32) + 2 * _nbytes((K, tf), BF16)
            + 6 * _nbytes((tm, tf), F32))
    return pl.pallas_call(
        _gateup_kernel,
        grid=(T // tm, F // tf),
        in_specs=[pl.BlockSpec((tm, K), lambda i, j: (i, 0)),
                  pl.BlockSpec((K, tf), lambda i, j: (0, j)),
                  pl.BlockSpec((K, tf), lambda i, j: (0, j))],
        out_specs=pl.BlockSpec((tm, tf), lambda i, j: (i, j)),
        out_shape=jax.ShapeDtypeStruct((T, F), BF16),
        compiler_params=_params(("parallel", "parallel"), vmem),
        name="swiglu_gateup",
    )(x, wg, wu)


def _router_kernel(x_ref, g_ref, w_ref, b_ref, idx_ref, gw_ref):
    x = x_ref[...]
    h = x * lax.rsqrt(jnp.mean(x * x, axis=-1, keepdims=True) + EPS) * g_ref[...]
    logits = jnp.dot(h, w_ref[...], preferred_element_type=F32, precision=lax.Precision.HIGHEST) + b_ref[...]
    lane = lax.broadcasted_iota(jnp.int32, logits.shape, 1)
    logits = jnp.where(lane < N_EXPERTS, logits, -jnp.inf)
    m1 = jnp.max(logits, axis=-1, keepdims=True)
    i1 = jnp.min(jnp.where(logits == m1, lane, LANES), axis=-1, keepdims=True)
    rest = jnp.where(lane == i1, -jnp.inf, logits)
    m2 = jnp.max(rest, axis=-1, keepdims=True)
    i2 = jnp.min(jnp.where(rest == m2, lane, LANES), axis=-1, keepdims=True)
    e = jnp.exp(m2 - m1)
    w1 = 1.0 / (1.0 + e)
    w2 = e / (1.0 + e)
    idx_ref[...] = jnp.where(lane == 0, i1, jnp.where(lane == 1, i2, 0))
    gw_ref[...] = jnp.where(lane == 0, w1, jnp.where(lane == 1, w2, 0.0))


def moe_router(x, g, w_router, b_router, *, tm=256):
    T, D = x.shape
    E = w_router.shape[1]
    w = jnp.pad(w_router, ((0, 0), (0, LANES - E)))
    b = jnp.pad(b_router, (0, LANES - E)).reshape(1, LANES)
    idx, gw = pl.pallas_call(
        _router_kernel,
        grid=(T // tm,),
        in_specs=[pl.BlockSpec((tm, D), lambda i: (i, 0)), pl.BlockSpec((1, D), lambda i: (0, 0)),
                  pl.BlockSpec((D, LANES), lambda i: (0, 0)), pl.BlockSpec((1, LANES), lambda i: (0, 0))],
        out_specs=[pl.BlockSpec((tm, LANES), lambda i: (i, 0)), pl.BlockSpec((tm, LANES), lambda i: (i, 0))],
        out_shape=[jax.ShapeDtypeStruct((T, LANES), jnp.int32), jax.ShapeDtypeStruct((T, LANES), F32)],
        compiler_params=_params(("parallel",), 8 * _nbytes((tm, D), F32)),
        name="moe_router",
    )(x, g.reshape(1, D), w, b)
    return idx[:, :TOP_K], gw[:, :TOP_K]


def _moe_dispatch(idx, gw, tm):
    T = idx.shape[0]
    A = T * TOP_K
    e_flat = idx.reshape(A)
    onehot = (e_flat[:, None] == jnp.arange(N_EXPERTS, dtype=jnp.int32)[None, :]).astype(jnp.int32)
    csum = jnp.cumsum(onehot, axis=0)
    pos_in = jnp.sum(csum * onehot, axis=1) - 1
    counts = csum[-1]
    padded = ((counts + tm - 1) // tm) * tm
    gend = jnp.cumsum(padded)
    gstart = gend - padded
    dest = (jnp.sum(onehot * gstart[None, :], axis=1) + pos_in).astype(jnp.int32)
    n_rows = A + N_EXPERTS * tm
    row_tok = jnp.zeros((n_rows,), jnp.int32).at[dest].set(jnp.arange(A, dtype=jnp.int32) // TOP_K)
    row_gate = jnp.zeros((n_rows,), F32).at[dest].set(gw.reshape(A))
    nb = n_rows // tm
    n_used = (gend[-1] // tm).astype(jnp.int32)
    blk = jnp.arange(nb, dtype=jnp.int32)
    be = jnp.sum((blk[:, None] * tm >= gend[None, :]).astype(jnp.int32), axis=1)
    be = jnp.minimum(be, N_EXPERTS - 1)
    be = jnp.where(blk < n_used, be, be[jnp.maximum(n_used - 1, 0)]).astype(jnp.int32)
    return dest, row_tok, row_gate.reshape(n_rows, 1), be, n_used.reshape(1)


def _moe_gather_kernel(tok_ref, nu_ref, x_hbm, g_ref, o_ref, buf, sem, *, rows):
    i = pl.program_id(0)

    def row_copy(r, src_row):
        return pltpu.make_async_copy(x_hbm.at[pl.ds(src_row, 1)], buf.at[pl.ds(r, 1)], sem)

    @pl.when(i < nu_ref[0])
    def _():
        def issue(r, c):
            row_copy(r, tok_ref[i * rows + r]).start()
            return c

        lax.fori_loop(0, rows, issue, 0)

        def drain(r, c):
            row_copy(r, 0).wait()
            return c

        lax.fori_loop(0, rows, drain, 0)
        x = buf[...]
        o_ref[...] = (x * lax.rsqrt(jnp.mean(x * x, axis=-1, keepdims=True) + EPS) * g_ref[...]).astype(o_ref.dtype)

    @pl.when(i >= nu_ref[0])
    def _():
        o_ref[...] = jnp.zeros(o_ref.shape, o_ref.dtype)


def moe_gather_norm(x, g, row_tok, n_used, *, tm, rows=128):
    T, D = x.shape
    n_rows = row_tok.shape[0]
    per = tm // rows
    return pl.pallas_call(
        functools.partial(_moe_gather_kernel, rows=rows),
        grid_spec=pltpu.PrefetchScalarGridSpec(
            num_scalar_prefetch=2,
            grid=(n_rows // rows,),
            in_specs=[pl.BlockSpec(memory_space=pl.ANY), pl.BlockSpec((1, D), lambda i, tok, nu: (0, 0))],
            out_specs=pl.BlockSpec((rows, D), lambda i, tok, nu: (i, 0)),
            scratch_shapes=[pltpu.VMEM((rows, D), F32), pltpu.SemaphoreType.DMA(())]),
        out_shape=jax.ShapeDtypeStruct((n_rows, D), BF16),
        compiler_params=_params(("arbitrary",), 8 * _nbytes((rows, D), F32)),
        name="moe_gather",
    )(row_tok, n_used * per, x, g.reshape(1, D))


def _moe_gateup_kernel(be_ref, nu_ref, x_ref, wg_ref, wu_ref, o_ref, wgb, wub):
    i = pl.program_id(1)
    changed = (i == 0) | (be_ref[i] != be_ref[jnp.maximum(i - 1, 0)])

    @pl.when(changed)
    def _():
        wgb[...] = wg_ref[...].astype(BF16)
        wub[...] = wu_ref[...].astype(BF16)

    @pl.when(i < nu_ref[0])
    def _():
        x = x_ref[...]
        g = jnp.dot(x, wgb[...], preferred_element_type=F32)
        u = jnp.dot(x, wub[...], preferred_element_type=F32)
        o_ref[...] = (g * jax.nn.sigmoid(g) * u).astype(o_ref.dtype)

    @pl.when(i >= nu_ref[0])
    def _():
        o_ref[...] = jnp.zeros(o_ref.shape, o_ref.dtype)


def moe_gateup(xs, wg, wu, be, n_used, *, tm, tf=512):
    n_rows, K = xs.shape
    F = wg.shape[2]
    nb = n_rows // tm
    xmap = lambda j, i, be, nu: (jnp.minimum(i, nu[0] - 1), 0)
    wmap = lambda j, i, be, nu: (be[i], 0, j)
    vmem = (2 * _nbytes((tm, K), BF16) + 4 * _nbytes((K, tf), F32) + 2 * _nbytes((K, tf), BF16)
            + 6 * _nbytes((tm, tf), F32))
    return pl.pallas_call(
        _moe_gateup_kernel,
        grid_spec=pltpu.PrefetchScalarGridSpec(
            num_scalar_prefetch=2,
            grid=(F // tf, nb),
            in_specs=[pl.BlockSpec((tm, K), xmap),
                      pl.BlockSpec((None, K, tf), wmap),
                      pl.BlockSpec((None, K, tf), wmap)],
            out_specs=pl.BlockSpec((tm, tf), lambda j, i, be, nu: (i, j)),
            scratch_shapes=[pltpu.VMEM((K, tf), BF16), pltpu.VMEM((K, tf), BF16)]),
        out_shape=jax.ShapeDtypeStruct((n_rows, F), BF16),
        compiler_params=_params(("arbitrary", "arbitrary"), vmem),
        name="moe_gateup",
    )(be, n_used, xs, wg, wu)


def _moe_down_kernel(be_ref, nu_ref, x_ref, w_ref, gate_ref, o_ref, wb):
    i = pl.program_id(1)
    changed = (i == 0) | (be_ref[i] != be_ref[jnp.maximum(i - 1, 0)])

    @pl.when(changed)
    def _():
        wb[...] = w_ref[...].astype(BF16)

    @pl.when(i < nu_ref[0])
    def _():
        o_ref[...] = jnp.dot(x_ref[...], wb[...], preferred_element_type=F32) * gate_ref[...]

    @pl.when(i >= nu_ref[0])
    def _():
        o_ref[...] = jnp.zeros(o_ref.shape, o_ref.dtype)


def moe_down(hm, wd, row_gate, be, n_used, *, tm, tn=512):
    n_rows, K = hm.shape
    N = wd.shape[2]
    nb = n_rows // tm
    xmap = lambda j, i, be, nu: (jnp.minimum(i, nu[0] - 1), 0)
    vmem = (2 * _nbytes((tm, K), BF16) + 2 * _nbytes((K, tn), F32) + _nbytes((K, tn), BF16)
            + 4 * _nbytes((tm, tn), F32))
    return pl.pallas_call(
        _moe_down_kernel,
        grid_spec=pltpu.PrefetchScalarGridSpec(
            num_scalar_prefetch=2,
            grid=(N // tn, nb),
            in_specs=[pl.BlockSpec((tm, K), xmap),
                      pl.BlockSpec((None, K, tn), lambda j, i, be, nu: (be[i], 0, j)),
                      pl.BlockSpec((tm, 1), xmap)],
            out_specs=pl.BlockSpec((tm, tn), lambda j, i, be, nu: (i, j)),
            scratch_shapes=[pltpu.VMEM((K, tn), BF16)]),
        out_shape=jax.ShapeDtypeStruct((n_rows, N), F32),
        compiler_params=_params(("arbitrary", "arbitrary"), vmem),
        name="moe_down",
    )(be, n_used, hm, wd, row_gate)


def _moe_combine_kernel(pos_ref, x_ref, ys_hbm, o_ref, buf, sem, *, rows):
    i = pl.program_id(0)

    def row_copy(slot, r, src_row):
        return pltpu.make_async_copy(ys_hbm.at[pl.ds(src_row, 1)], buf.at[slot, pl.ds(r, 1)], sem)

    def issue(r, c):
        a = (i * rows + r) * TOP_K
        for slot in range(TOP_K):
            row_copy(slot, r, pos_ref[a + slot]).start()
        return c

    lax.fori_loop(0, rows, issue, 0)

    def drain(r, c):
        for slot in range(TOP_K):
            row_copy(slot, r, 0).wait()
        return c

    lax.fori_loop(0, rows, drain, 0)
    out = x_ref[...]
    for slot in range(TOP_K):
        out = out + buf[slot]
    o_ref[...] = out


def moe_combine(x, ys, dest, *, rows=128):
    T, D = x.shape
    return pl.pallas_call(
        functools.partial(_moe_combine_kernel, rows=rows),
        grid_spec=pltpu.PrefetchScalarGridSpec(
            num_scalar_prefetch=1,
            grid=(T // rows,),
            in_specs=[pl.BlockSpec((rows, D), lambda i, pos: (i, 0)), pl.BlockSpec(memory_space=pl.ANY)],
            out_specs=pl.BlockSpec((rows, D), lambda i, pos: (i, 0)),
            scratch_shapes=[pltpu.VMEM((TOP_K, rows, D), F32), pltpu.SemaphoreType.DMA(())]),
        out_shape=jax.ShapeDtypeStruct((T, D), F32),
        compiler_params=_params(("arbitrary",), 10 * _nbytes((rows, D), F32)),
        name="moe_combine",
    )(dest, x, ys)


def moe_block(x, g, w_router, b_router, w_egate, w_eup, w_edown, *, tm=512):
    idx, gw = moe_router(x, g, w_router, b_router)
    dest, row_tok, row_gate, be, n_used = _moe_dispatch(idx, gw, tm)
    xs = moe_gather_norm(x, g, row_tok, n_used, tm=tm)
    hm = moe_gateup(xs, w_egate, w_eup, be, n_used, tm=tm)
    ys = moe_down(hm, w_edown, row_gate, be, n_used, tm=tm)
    return moe_combine(x, ys, dest)


def cross_attention_block(x, mk, mv, g_x, w_xq, g_xq, w_xo, *, seq, mem_len):
    h = rmsnorm(x, g_x)
    qx = matmul(h, w_xq, name="xattn_q")
    o = cross_attention_core(qx, mk, mv, g_xq, seq=seq, mem_len=mem_len)
    return matmul(o, w_xo, res=x, name="xattn_o")


def even_mixer_block(x, tabs_a, tabs_b, w_in, ga_q, ga_k, g_cq, w_uq, g_ckv, w_ukv, gb_q, gb_kn, gb_kr,
                     w_o, g_mix, *, batch, seq):
    a_width = A_HEADS * HEAD_DIM
    main_cols = 3 * a_width + Q_LORA + KV_LORA
    h = rmsnorm(x, g_mix)
    proj = matmul(h, w_in, n_out=main_cols, name="in_proj")
    w_kr = jnp.pad(w_in[:, main_cols:], ((0, 0), (0, LANES - ROPE_DIM)))
    krp = matmul(h, w_kr, name="in_proj_kr")
    o_a = mixer_a(proj, tabs_a, ga_q, ga_k, batch=batch, seq=seq)
    cq = rmsnorm(proj, g_cq, width=Q_LORA, col_block=3 * a_width // Q_LORA)
    ckv = rmsnorm(proj, g_ckv, width=KV_LORA, col_block=(3 * a_width + Q_LORA) // KV_LORA)
    w3 = w_uq.reshape(Q_LORA, B_HEADS, QK_DIM)
    w_uq_p = jnp.concatenate(
        [w3[:, :, :NOPE_DIM].reshape(Q_LORA, B_HEADS * NOPE_DIM),
         jnp.pad(w3[:, :, NOPE_DIM:], ((0, 0), (0, 0), (0, LANES - ROPE_DIM))).reshape(Q_LORA, B_HEADS * LANES)],
        axis=1)
    q_up = matmul(cq, w_uq_p, name="q_up")
    kv_up = matmul(ckv, w_ukv, name="kv_up")
    pad_r = lambda v: jnp.pad(v, (0, LANES - ROPE_DIM)).reshape(1, LANES)
    qf, kf, vb = mla_prep(q_up, kv_up, krp, tabs_b, gb_q[:NOPE_DIM].reshape(1, LANES), pad_r(gb_q[NOPE_DIM:]),
                          gb_kn.reshape(1, LANES), pad_r(gb_kr))
    o_b = mla_attention(qf, kf, vb, batch=batch, seq=seq)
    return matmul((o_a, o_b), w_o, res=x, name="mixer_out")


def kernel(x, mem, positions, g_mem, w_mem_kv, g_mem_k, g_mix, g_x, w_xq, g_xq, w_xo, g_ffn,
           w_in, ga_q, ga_k, g_cq, w_uq, g_ckv, w_ukv, gb_q, gb_kn, gb_kr, w_o_even,
           w_gate, w_up, w_down, w_qkv, w_o_odd, w_router, b_router, w_egate, w_eup, w_edown):
    B, S, D = x.shape
    M = mem.shape[1]
    depth = g_mix.shape[0]
    T = B * S
    xt = x.reshape(T, D)
    tabs_a = _rope_tables(positions, ROT_DIM)
    tabs_b = _rope_tables(positions, ROPE_DIM)
    kv = matmul(rmsnorm(mem.reshape(B * M, D), g_mem), w_mem_kv, tm=B * M, name="mem_kv")
    mk, mv = memkv_post(kv, g_mem_k)
    for layer in range(depth):
        i = layer // 2
        if layer % 2 == 0:
            xt = even_mixer_block(xt, tabs_a, tabs_b, w_in[i], ga_q[i], ga_k[i], g_cq[i], w_uq[i], g_ckv[i],
                                  w_ukv[i], gb_q[i], gb_kn[i], gb_kr[i], w_o_even[i], g_mix[layer],
                                  batch=B, seq=S)
        else:
            h = rmsnorm(xt, g_mix[layer])
            qkv = matmul(h, w_qkv[i], out_dtype=BF16, name="qkv_proj")
            o = stickbreak_attention(qkv, batch=B, seq=S, heads=D // HEAD_DIM)
            xt = matmul(o, w_o_odd[i], res=xt, name="mixer_out")
        xt = cross_attention_block(xt, mk, mv, g_x[layer], w_xq[layer], g_xq[layer], w_xo[layer], seq=S, mem_len=M)
        if layer % 2 == 0:
            h = rmsnorm(xt, g_ffn[layer])
            hm = swiglu_gateup(h, w_gate[i], w_up[i])
            xt = matmul(hm, w_down[i], res=xt, tn=1024, tk=2048, name="swiglu_down")
        else:
            xt = moe_block(xt, g_ffn[layer], w_router[i], b_router[i], w_egate[i], w_eup[i], w_edown[i])
    return xt.reshape(B, S, D)
```

```python
import functools

import jax
import jax.numpy as jnp
from jax import lax
from jax.experimental import pallas as pl
from jax.experimental.pallas import tpu as pltpu

F32 = jnp.float32
BF16 = jnp.bfloat16

HEAD_DIM = 128
ROT_DIM = HEAD_DIM // 4
ROPE_THETA = 500000.0
BLOCK = 128
NEG_INF = -1e30
EPS = 1e-6
DILATED_PAIRS = ((128, 1), (512, 4), (2048, 16))
A_HEADS = 16
B_HEADS = 16
Q_LORA = 1536
KV_LORA = 512
NOPE_DIM = 128
ROPE_DIM = 64
QK_DIM = NOPE_DIM + ROPE_DIM
X_HEADS = 4
N_EXPERTS = 8
TOP_K = 2

LANES = 128
V7X_VMEM_BYTES = 64 * 1024 * 1024
VMEM_CAP = V7X_VMEM_BYTES - 6 * 1024 * 1024


def _params(semantics, vmem_bytes):
    return pltpu.CompilerParams(dimension_semantics=semantics,
                                vmem_limit_bytes=int(min(max(vmem_bytes, 32 * 1024 * 1024), VMEM_CAP)))


def _nbytes(shape, dtype):
    n = 1
    for s in shape:
        n *= s
    return n * jnp.dtype(dtype).itemsize


def _rmsnorm_kernel(x_ref, g_ref, o_ref):
    x = x_ref[...].astype(F32)
    ms = jnp.mean(x * x, axis=-1, keepdims=True)
    o_ref[...] = (x * lax.rsqrt(ms + EPS) * g_ref[...]).astype(o_ref.dtype)


def rmsnorm(x, g, *, width=None, col_block=0, tm=256, out_dtype=BF16):
    T = x.shape[0]
    width = x.shape[1] if width is None else width
    tm = min(tm, T)
    return pl.pallas_call(
        _rmsnorm_kernel,
        grid=(T // tm,),
        in_specs=[pl.BlockSpec((tm, width), lambda i: (i, col_block)),
                  pl.BlockSpec((1, width), lambda i: (0, 0))],
        out_specs=pl.BlockSpec((tm, width), lambda i: (i, 0)),
        out_shape=jax.ShapeDtypeStruct((T, width), out_dtype),
        compiler_params=_params(("parallel",), 6 * _nbytes((tm, width), F32)),
        name="rmsnorm",
    )(x, g.reshape(1, width).astype(F32))


def _mm_kernel(*refs, nk, n_x, has_res):
    x_refs = refs[:n_x]
    w_ref = refs[n_x]
    rest = refs[n_x + 1:]
    res_ref = rest[0] if has_res else None
    o_ref = rest[1] if has_res else rest[0]
    acc_ref = rest[-1] if nk > 1 else None
    k = pl.program_id(2)
    w = w_ref[...].astype(BF16)

    def finish(part):
        out = part
        if has_res:
            out = out + res_ref[...]
        o_ref[...] = out.astype(o_ref.dtype)

    if n_x == 2:
        @pl.when(k == 0)
        def _():
            acc_ref[...] = jnp.dot(x_refs[0][...], w, preferred_element_type=F32)

        @pl.when(k == 1)
        def _():
            finish(acc_ref[...] + jnp.dot(x_refs[1][...], w, preferred_element_type=F32))
    elif nk == 1:
        finish(jnp.dot(x_refs[0][...], w, preferred_element_type=F32))
    else:
        part = jnp.dot(x_refs[0][...], w, preferred_element_type=F32)

        @pl.when(k == 0)
        def _():
            acc_ref[...] = part

        @pl.when((k > 0) & (k < nk - 1))
        def _():
            acc_ref[...] += part

        @pl.when(k == nk - 1)
        def _():
            finish(acc_ref[...] + part)


def matmul(xs, w, *, n_out=None, res=None, out_dtype=F32, tm=1024, tn=512, tk=None, name="matmul"):
    if not isinstance(xs, (tuple, list)):
        xs = (xs,)
    n_x = len(xs)
    T = xs[0].shape[0]
    K = sum(x.shape[1] for x in xs)
    n_out = w.shape[1] if n_out is None else n_out
    tm = min(tm, T)
    tn = min(tn, n_out)
    if n_x == 2:
        tk = xs[0].shape[1]
        assert xs[1].shape[1] == tk
    tk = K if tk is None else tk
    nk = K // tk
    assert K == nk * tk and T % tm == 0 and n_out % tn == 0
    if n_x == 2:
        x_specs = [pl.BlockSpec((tm, tk), lambda i, j, k: (i, 0)) for _ in xs]
    else:
        x_specs = [pl.BlockSpec((tm, tk), lambda i, j, k: (i, k))]
    in_specs = x_specs + [pl.BlockSpec((tk, tn), lambda i, j, k: (k, j))]
    args = list(xs) + [w]
    if res is not None:
        in_specs.append(pl.BlockSpec((tm, tn), lambda i, j, k: (i, j)))
        args.append(res)
    scratch = [pltpu.VMEM((tm, tn), F32)] if nk > 1 else []
    vmem = (2 * n_x * _nbytes((tm, tk), xs[0].dtype) + 2 * _nbytes((tk, tn), F32) + _nbytes((tk, tn), BF16)
            + (2 * _nbytes((tm, tn), F32) if res is not None else 0)
            + 2 * _nbytes((tm, tn), out_dtype) + 2 * _nbytes((tm, tn), F32))
    return pl.pallas_call(
        functools.partial(_mm_kernel, nk=nk, n_x=n_x, has_res=res is not None),
        grid=(T // tm, n_out // tn, nk),
        in_specs=in_specs,
        out_specs=pl.BlockSpec((tm, tn), lambda i, j, k: (i, j)),
        out_shape=jax.ShapeDtypeStruct((T, n_out), out_dtype),
        scratch_shapes=scratch,
        compiler_params=_params(("parallel", "parallel", "arbitrary"), vmem),
        name=name,
    )(*args)


def _rope_tables(positions, dim):
    half = dim // 2
    inv_freq = ROPE_THETA ** (-jnp.arange(0, dim, 2, dtype=F32) / dim)
    ang = positions.astype(F32)[..., None] * inv_freq
    cos, sin = jnp.cos(ang), jnp.sin(ang)
    B, S = positions.shape
    ones = jnp.ones((B, S, LANES - dim), F32)
    zeros = jnp.zeros((B, S, LANES - dim), F32)
    zh = jnp.zeros((B, S, half), F32)
    c = jnp.concatenate([cos, cos, ones], axis=-1)
    sa = jnp.concatenate([-sin, zh, zeros], axis=-1)
    sb = jnp.concatenate([zh, sin, zeros], axis=-1)
    return [t.reshape(B * S, LANES) for t in (c, sa, sb)]


def _rope_lanes(y, c, sa, sb, half):
    return y * c + pltpu.roll(y, LANES - half, 1) * sa + pltpu.roll(y, half, 1) * sb


def _mixer_a_kernel(q_ref, k_ref, v_ref, c_ref, sa_ref, sb_ref, gq_ref, gk_ref, o_ref,
                    qs, ks, *state, seq):
    scale = HEAD_DIM ** -0.5
    c, sa, sb = c_ref[...], sa_ref[...], sb_ref[...]

    def prep(x, g):
        y = x * lax.rsqrt(jnp.mean(x * x, axis=-1, keepdims=True) + EPS) * g
        return _rope_lanes(y, c, sa, sb, ROT_DIM // 2)

    qs[...] = prep(q_ref[...], gq_ref[...])
    ks[...] = prep(k_ref[...], gk_ref[...])
    qi = lax.broadcasted_iota(jnp.int32, (BLOCK, 1), 0)
    kj = lax.broadcasted_iota(jnp.int32, (1, BLOCK), 1)
    trans_b = (((1,), (1,)), ((), ()))

    for bi, (window, dil) in enumerate(DILATED_PAIRS):
        acc, m_sc, l_sc = state[3 * bi:3 * bi + 3]
        n_back = window // dil
        nb = seq // dil // BLOCK
        assert n_back <= BLOCK and nb * dil * BLOCK == seq
        cur_ok = (qi >= kj) & (qi - kj <= n_back)
        prev_ok = qi + BLOCK - kj <= n_back

        def rows_at(r, n, dil=dil):
            if dil == 1:
                return pl.ds(BLOCK * n, BLOCK)
            return pl.ds(r + dil * BLOCK * n, BLOCK, stride=dil)

        for r in range(dil):
            for n in range(nb):
                rows = rows_at(r, n)
                qb = qs[rows, :].astype(BF16)
                s_c = lax.dot_general(qb, ks[rows, :].astype(BF16), trans_b, preferred_element_type=F32) * scale
                s_c = jnp.where(cur_ok, s_c, NEG_INF)
                m = jnp.max(s_c, axis=-1, keepdims=True)
                if n > 0:
                    prev = rows_at(r, n - 1)
                    s_p = lax.dot_general(qb, ks[prev, :].astype(BF16), trans_b,
                                          preferred_element_type=F32) * scale
                    s_p = jnp.where(prev_ok, s_p, NEG_INF)
                    m = jnp.maximum(m, jnp.max(s_p, axis=-1, keepdims=True))
                p_c = jnp.exp(s_c - m)
                l = jnp.sum(p_c, axis=-1, keepdims=True)
                o = jnp.dot(p_c.astype(BF16), v_ref[rows, :].astype(BF16), preferred_element_type=F32)
                if n > 0:
                    p_p = jnp.exp(s_p - m)
                    l = l + jnp.sum(p_p, axis=-1, keepdims=True)
                    o = o + jnp.dot(p_p.astype(BF16), v_ref[prev, :].astype(BF16), preferred_element_type=F32)
                acc[rows, :] = o
                m_sc[rows, :] = jnp.broadcast_to(m, (BLOCK, HEAD_DIM))
                l_sc[rows, :] = jnp.broadcast_to(l, (BLOCK, HEAD_DIM))

    n_br = len(DILATED_PAIRS)
    m_all = state[1][...]
    for bi in range(1, n_br):
        m_all = jnp.maximum(m_all, state[3 * bi + 1][...])
    num = jnp.zeros(o_ref.shape, F32)
    den = jnp.zeros(o_ref.shape, F32)
    for bi in range(n_br):
        w = jnp.exp(state[3 * bi + 1][...] - m_all)
        num = num + w * state[3 * bi][...]
        den = den + w * state[3 * bi + 2][...]
    o_ref[...] = (num / den).astype(o_ref.dtype)


def mixer_a(proj, tabs, gq, gk, *, batch, seq):
    T = batch * seq
    blk = (seq, HEAD_DIM)
    head = lambda off: pl.BlockSpec(blk, lambda b, h: (b, off + h))
    tab = pl.BlockSpec(blk, lambda b, h: (b, 0))
    gspec = pl.BlockSpec((1, HEAD_DIM), lambda b, h: (0, 0))
    return pl.pallas_call(
        functools.partial(_mixer_a_kernel, seq=seq),
        grid=(batch, A_HEADS),
        in_specs=[head(0), head(A_HEADS), head(2 * A_HEADS), tab, tab, tab, gspec, gspec],
        out_specs=pl.BlockSpec(blk, lambda b, h: (b, h)),
        out_shape=jax.ShapeDtypeStruct((T, A_HEADS * HEAD_DIM), BF16),
        scratch_shapes=[pltpu.VMEM(blk, F32) for _ in range(2 + 3 * len(DILATED_PAIRS))],
        compiler_params=_params(("parallel", "parallel"), 30 * _nbytes(blk, F32)),
        name="mixer_a",
    )(proj, proj, proj, *tabs, gq.reshape(1, HEAD_DIM), gk.reshape(1, HEAD_DIM))


def _mla_prep_kernel(qn_ref, qr_ref, kn_ref, v_ref, kr_ref, c_ref, sa_ref, sb_ref,
                     gqn_ref, gqr_ref, gkn_ref, gkr_ref, qf_ref, kf_ref, vb_ref):
    c, sa, sb = c_ref[...], sa_ref[...], sb_ref[...]
    half = ROPE_DIM // 2
    qn, qr = qn_ref[...], qr_ref[...]
    ss = jnp.sum(qn * qn, axis=-1, keepdims=True) + jnp.sum(qr * qr, axis=-1, keepdims=True)
    inv = lax.rsqrt(ss * (1.0 / QK_DIM) + EPS)
    qf_ref[:, :NOPE_DIM] = (qn * inv * gqn_ref[...]).astype(qf_ref.dtype)
    qf_ref[:, NOPE_DIM:] = _rope_lanes(qr * inv * gqr_ref[...], c, sa, sb, half).astype(qf_ref.dtype)
    kn = kn_ref[...]
    kn = kn * lax.rsqrt(jnp.mean(kn * kn, axis=-1, keepdims=True) + EPS) * gkn_ref[...]
    kr = kr_ref[...]
    kr = kr * lax.rsqrt(jnp.sum(kr * kr, axis=-1, keepdims=True) * (1.0 / ROPE_DIM) + EPS) * gkr_ref[...]
    kf_ref[:, :NOPE_DIM] = kn.astype(kf_ref.dtype)
    kf_ref[:, NOPE_DIM:] = _rope_lanes(kr, c, sa, sb, half).astype(kf_ref.dtype)
    vb_ref[...] = v_ref[...].astype(vb_ref.dtype)


def mla_prep(q_up, kv_up, krp, tabs, gqn, gqr, gkn, gkr, *, tm=512):
    T = q_up.shape[0]
    H = B_HEADS
    blk = lambda f: pl.BlockSpec((tm, LANES), f)
    g = pl.BlockSpec((1, LANES), lambda i, h: (0, 0))
    return pl.pallas_call(
        _mla_prep_kernel,
        grid=(T // tm, H),
        in_specs=[blk(lambda i, h: (i, h)), blk(lambda i, h: (i, H + h)),
                  blk(lambda i, h: (i, 2 * h)), blk(lambda i, h: (i, 2 * h + 1)),
                  blk(lambda i, h: (i, 0)), blk(lambda i, h: (i, 0)), blk(lambda i, h: (i, 0)),
                  blk(lambda i, h: (i, 0)), g, g, g, g],
        out_specs=[pl.BlockSpec((tm, 2 * LANES), lambda i, h: (i, h)),
                   pl.BlockSpec((tm, 2 * LANES), lambda i, h: (i, h)),
                   pl.BlockSpec((tm, LANES), lambda i, h: (i, h))],
        out_shape=[jax.ShapeDtypeStruct((T, H * 2 * LANES), BF16),
                   jax.ShapeDtypeStruct((T, H * 2 * LANES), BF16),
                   jax.ShapeDtypeStruct((T, H * LANES), BF16)],
        compiler_params=_params(("parallel", "parallel"), 40 * _nbytes((tm, LANES), F32)),
        name="mla_prep",
    )(q_up, q_up, kv_up, kv_up, krp, *tabs, gqn, gqr, gkn, gkr)


def _causal_attn_kernel(q_ref, k_ref, v_ref, o_ref, m_sc, l_sc, acc, *, tq, hg, dq, dv, scale):
    qi = pl.program_id(2)
    m_sc[...] = jnp.full(m_sc.shape, NEG_INF, F32)
    l_sc[...] = jnp.zeros(l_sc.shape, F32)
    acc[...] = jnp.zeros(acc.shape, F32)
    on_or_below = (lax.broadcasted_iota(jnp.int32, (tq, 1), 0) >= lax.broadcasted_iota(jnp.int32, (1, tq), 1))

    def chunk(j, diagonal):
        rows = pl.ds(pl.multiple_of(j * tq, tq), tq)
        for g in range(hg):
            qc = slice(g * dq, (g + 1) * dq)
            vc = slice(g * dv, (g + 1) * dv)
            s = lax.dot_general(q_ref[:, qc], k_ref[rows, qc], (((1,), (1,)), ((), ())),
                                preferred_element_type=F32) * scale
            if diagonal:
                s = jnp.where(on_or_below, s, NEG_INF)
            m_old = m_sc[g]
            m_new = jnp.maximum(m_old, jnp.max(s, axis=-1, keepdims=True))
            alpha = jnp.exp(m_old - m_new)
            p = jnp.exp(s - m_new)
            l_sc[g] = alpha * l_sc[g] + jnp.sum(p, axis=-1, keepdims=True)
            acc[:, vc] = alpha * acc[:, vc] + jnp.dot(p.astype(BF16), v_ref[rows, vc], preferred_element_type=F32)
            m_sc[g] = m_new

    def below(j, carry):
        chunk(j, False)
        return carry

    lax.fori_loop(0, qi, below, 0)
    chunk(qi, True)
    for g in range(hg):
        vc = slice(g * dv, (g + 1) * dv)
        o_ref[:, vc] = (acc[:, vc] / l_sc[g]).astype(o_ref.dtype)


def mla_attention(qf, kf, vb, *, batch, seq, tq=256, hg=2):
    T = batch * seq
    nq = seq // tq
    dq = qf.shape[1] // B_HEADS
    dv = vb.shape[1] // B_HEADS
    assert B_HEADS % hg == 0
    return pl.pallas_call(
        functools.partial(_causal_attn_kernel, tq=tq, hg=hg, dq=dq, dv=dv, scale=QK_DIM ** -0.5),
        grid=(batch, B_HEADS // hg, nq),
        in_specs=[pl.BlockSpec((tq, hg * dq), lambda b, h, i: (b * nq + i, h)),
                  pl.BlockSpec((seq, hg * dq), lambda b, h, i: (b, h)),
                  pl.BlockSpec((seq, hg * dv), lambda b, h, i: (b, h))],
        out_specs=pl.BlockSpec((tq, hg * dv), lambda b, h, i: (b * nq + i, h)),
        out_shape=jax.ShapeDtypeStruct((T, B_HEADS * dv), BF16),
        scratch_shapes=[pltpu.VMEM((hg, tq, 1), F32), pltpu.VMEM((hg, tq, 1), F32), pltpu.VMEM((tq, hg * dv), F32)],
        compiler_params=_params(("parallel", "parallel", "parallel"), 0),
        name="mla_attention",
    )(qf, kf, vb)


STICK_EXHAUSTED_LOG = -110.0


def _stickbreak_kernel(q_ref, k_ref, v_ref, o_ref, c_sc, acc, *, tq, hg, scale):
    qi = pl.program_id(2)
    c_sc[...] = jnp.zeros(c_sc.shape, F32)
    acc[...] = jnp.zeros(acc.shape, F32)
    row = lax.broadcasted_iota(jnp.int32, (tq, tq), 0)
    col = lax.broadcasted_iota(jnp.int32, (tq, tq), 1)
    strict = (row > col).astype(BF16)
    earlier = col < row

    def chunk(j, diagonal):
        rows = pl.ds(pl.multiple_of(j * tq, tq), tq)
        for g in range(hg):
            cols = slice(g * HEAD_DIM, (g + 1) * HEAD_DIM)
            z = lax.dot_general(q_ref[:, cols], k_ref[rows, cols], (((1,), (1,)), ((), ())),
                                preferred_element_type=F32) * scale
            softplus = jnp.maximum(z, 0.0) + jnp.log(1.0 + jnp.exp(-jnp.abs(z)))
            log_keep = -softplus
            if diagonal:
                log_keep = jnp.where(earlier, log_keep, 0.0)
            hi = log_keep.astype(BF16)
            lo = (log_keep - hi.astype(F32)).astype(BF16)
            within = (jnp.dot(hi, strict, preferred_element_type=F32)
                      + jnp.dot(lo, strict, preferred_element_type=F32))
            a = jnp.exp(z - softplus + (within + c_sc[g]))
            if diagonal:
                a = jnp.where(earlier, a, 0.0)
            acc[:, cols] += jnp.dot(a.astype(BF16), v_ref[rows, cols], preferred_element_type=F32)
            c_sc[g] += jnp.sum(log_keep, axis=-1, keepdims=True)

    def stick_left():
        return jnp.max(c_sc[...]) > STICK_EXHAUSTED_LOG

    chunk(qi, True)

    def more(state):
        j, left = state
        return (j >= 0) & left

    def step(state):
        j, _ = state
        chunk(j, False)
        return j - 1, stick_left()

    lax.while_loop(more, step, (qi - 1, stick_left()))
    o_ref[...] = acc[...].astype(o_ref.dtype)


def stickbreak_attention(qkv, *, batch, seq, heads, tq=256, hg=4):
    T = batch * seq
    nq = seq // tq
    ng = heads // hg
    assert heads == ng * hg
    blk = hg * HEAD_DIM
    return pl.pallas_call(
        functools.partial(_stickbreak_kernel, tq=tq, hg=hg, scale=HEAD_DIM ** -0.5),
        grid=(batch, ng, nq),
        in_specs=[pl.BlockSpec((tq, blk), lambda b, h, i: (b * nq + i, h)),
                  pl.BlockSpec((seq, blk), lambda b, h, i: (b, ng + h)),
                  pl.BlockSpec((seq, blk), lambda b, h, i: (b, 2 * ng + h))],
        out_specs=pl.BlockSpec((tq, blk), lambda b, h, i: (b * nq + i, h)),
        out_shape=jax.ShapeDtypeStruct((T, heads * HEAD_DIM), BF16),
        scratch_shapes=[pltpu.VMEM((hg, tq, 1), F32), pltpu.VMEM((tq, blk), F32)],
        compiler_params=_params(("parallel", "parallel", "parallel"), 0),
        name="stickbreak",
    )(qkv, qkv, qkv)


def _memkv_kernel(kv_ref, g_ref, mk_ref, mv_ref):
    for h in range(X_HEADS):
        cols = slice(h * HEAD_DIM, (h + 1) * HEAD_DIM)
        k = kv_ref[:, cols]
        k = k * lax.rsqrt(jnp.mean(k * k, axis=-1, keepdims=True) + EPS) * g_ref[...]
        mk_ref[:, cols] = k.astype(mk_ref.dtype)
    mv_ref[...] = kv_ref[:, X_HEADS * HEAD_DIM:].astype(mv_ref.dtype)


def memkv_post(kv, g):
    M = kv.shape[0]
    xd = X_HEADS * HEAD_DIM
    return pl.pallas_call(
        _memkv_kernel,
        grid=(1,),
        in_specs=[pl.BlockSpec((M, 2 * xd), lambda i: (0, 0)), pl.BlockSpec((1, HEAD_DIM), lambda i: (0, 0))],
        out_specs=[pl.BlockSpec((M, xd), lambda i: (0, 0)), pl.BlockSpec((M, xd), lambda i: (0, 0))],
        out_shape=[jax.ShapeDtypeStruct((M, xd), BF16), jax.ShapeDtypeStruct((M, xd), BF16)],
        compiler_params=_params(("arbitrary",), 0),
        name="memkv_post",
    )(kv, g.reshape(1, HEAD_DIM))


def _xattn_kernel(q_ref, mk_ref, mv_ref, g_ref, o_ref):
    scale = HEAD_DIM ** -0.5
    for h in range(X_HEADS):
        cols = slice(h * HEAD_DIM, (h + 1) * HEAD_DIM)
        q = q_ref[:, cols]
        q = (q * lax.rsqrt(jnp.mean(q * q, axis=-1, keepdims=True) + EPS) * g_ref[...]).astype(BF16)
        s = lax.dot_general(q, mk_ref[:, cols], (((1,), (1,)), ((), ())), preferred_element_type=F32) * scale
        p = jnp.exp(s - jnp.max(s, axis=-1, keepdims=True))
        p = p / jnp.sum(p, axis=-1, keepdims=True)
        o_ref[:, cols] = jnp.dot(p.astype(BF16), mv_ref[:, cols], preferred_element_type=F32).astype(o_ref.dtype)


def cross_attention_core(qx, mk, mv, g, *, seq, mem_len, tm=512):
    T = qx.shape[0]
    xd = X_HEADS * HEAD_DIM
    per_seq = seq // tm
    return pl.pallas_call(
        _xattn_kernel,
        grid=(T // tm,),
        in_specs=[pl.BlockSpec((tm, xd), lambda i: (i, 0)),
                  pl.BlockSpec((mem_len, xd), lambda i: (i // per_seq, 0)),
                  pl.BlockSpec((mem_len, xd), lambda i: (i // per_seq, 0)),
                  pl.BlockSpec((1, HEAD_DIM), lambda i: (0, 0))],
        out_specs=pl.BlockSpec((tm, xd), lambda i: (i, 0)),
        out_shape=jax.ShapeDtypeStruct((T, xd), BF16),
        compiler_params=_params(("parallel",), 0),
        name="cross_attention",
    )(qx, mk, mv, g.reshape(1, HEAD_DIM))


def _gateup_kernel(x_ref, wg_ref, wu_ref, o_ref):
    x = x_ref[...]
    g = jnp.dot(x, wg_ref[...].astype(BF16), preferred_element_type=F32)
    u = jnp.dot(x, wu_ref[...].astype(BF16), preferred_element_type=F32)
    o_ref[...] = (g * jax.nn.sigmoid(g) * u).astype(o_ref.dtype)


def swiglu_gateup(x, wg, wu, *, tm=1024, tf=256):
    T, K = x.shape
    F = wg.shape[1]
    vmem = (2 * _nbytes((tm, K), BF16) + 4 * _nbytes((K, tf), F32) + 2 * _nbytes((K, tf), BF16)
            + 6 * _nbytes((tm, tf), F32))
    return pl.pallas_call(
        _gateup_kernel,
        grid=(T // tm, F // tf),
        in_specs=[pl.BlockSpec((tm, K), lambda i, j: (i, 0)),
                  pl.BlockSpec((K, tf), lambda i, j: (0, j)),
                  pl.BlockSpec((K, tf), lambda i, j: (0, j))],
        out_specs=pl.BlockSpec((tm, tf), lambda i, j: (i, j)),
        out_shape=jax.ShapeDtypeStruct((T, F), BF16),
        compiler_params=_params(("parallel", "parallel"), vmem),
        name="swiglu_gateup",
    )(x, wg, wu)


def _router_kernel(x_ref, g_ref, w_ref, b_ref, idx_ref, gw_ref):
    x = x_ref[...]
    h = x * lax.rsqrt(jnp.mean(x * x, axis=-1, keepdims=True) + EPS) * g_ref[...]
    logits = jnp.dot(h, w_ref[...], preferred_element_type=F32, precision=lax.Precision.HIGHEST) + b_ref[...]
    lane = lax.broadcasted_iota(jnp.int32, logits.shape, 1)
    logits = jnp.where(lane < N_EXPERTS, logits, -jnp.inf)
    m1 = jnp.max(logits, axis=-1, keepdims=True)
    i1 = jnp.min(jnp.where(logits == m1, lane, LANES), axis=-1, keepdims=True)
    rest = jnp.where(lane == i1, -jnp.inf, logits)
    m2 = jnp.max(rest, axis=-1, keepdims=True)
    i2 = jnp.min(jnp.where(rest == m2, lane, LANES), axis=-1, keepdims=True)
    e = jnp.exp(m2 - m1)
    w1 = 1.0 / (1.0 + e)
    w2 = e / (1.0 + e)
    idx_ref[...] = jnp.where(lane == 0, i1, jnp.where(lane == 1, i2, 0))
    gw_ref[...] = jnp.where(lane == 0, w1, jnp.where(lane == 1, w2, 0.0))


def moe_router(x, g, w_router, b_router, *, tm=256):
    T, D = x.shape
    E = w_router.shape[1]
    w = jnp.pad(w_router, ((0, 0), (0, LANES - E)))
    b = jnp.pad(b_router, (0, LANES - E)).reshape(1, LANES)
    idx, gw = pl.pallas_call(
        _router_kernel,
        grid=(T // tm,),
        in_specs=[pl.BlockSpec((tm, D), lambda i: (i, 0)), pl.BlockSpec((1, D), lambda i: (0, 0)),
                  pl.BlockSpec((D, LANES), lambda i: (0, 0)), pl.BlockSpec((1, LANES), lambda i: (0, 0))],
        out_specs=[pl.BlockSpec((tm, LANES), lambda i: (i, 0)), pl.BlockSpec((tm, LANES), lambda i: (i, 0))],
        out_shape=[jax.ShapeDtypeStruct((T, LANES), jnp.int32), jax.ShapeDtypeStruct((T, LANES), F32)],
        compiler_params=_params(("parallel",), 8 * _nbytes((tm, D), F32)),
        name="moe_router",
    )(x, g.reshape(1, D), w, b)
    return idx[:, :TOP_K], gw[:, :TOP_K]


def _moe_dispatch(idx, gw, tm):
    T = idx.shape[0]
    A = T * TOP_K
    e_flat = idx.reshape(A)
    onehot = (e_flat[:, None] == jnp.arange(N_EXPERTS, dtype=jnp.int32)[None, :]).astype(jnp.int32)
    csum = jnp.cumsum(onehot, axis=0)
    pos_in = jnp.sum(csum * onehot, axis=1) - 1
    counts = csum[-1]
    padded = ((counts + tm - 1) // tm) * tm
    gend = jnp.cumsum(padded)
    gstart = gend - padded
    dest = (jnp.sum(onehot * gstart[None, :], axis=1) + pos_in).astype(jnp.int32)
    n_rows = A + N_EXPERTS * tm
    row_tok = jnp.zeros((n_rows,), jnp.int32).at[dest].set(jnp.arange(A, dtype=jnp.int32) // TOP_K)
    row_gate = jnp.zeros((n_rows,), F32).at[dest].set(gw.reshape(A))
    nb = n_rows // tm
    n_used = (gend[-1] // tm).astype(jnp.int32)
    blk = jnp.arange(nb, dtype=jnp.int32)
    be = jnp.sum((blk[:, None] * tm >= gend[None, :]).astype(jnp.int32), axis=1)
    be = jnp.minimum(be, N_EXPERTS - 1)
    be = jnp.where(blk < n_used, be, be[jnp.maximum(n_used - 1, 0)]).astype(jnp.int32)
    return dest, row_tok, row_gate.reshape(n_rows, 1), be, n_used.reshape(1)


def _moe_gather_kernel(tok_ref, nu_ref, x_hbm, g_ref, o_ref, buf, sem, *, rows):
    i = pl.program_id(0)

    def row_copy(r, src_row):
        return pltpu.make_async_copy(x_hbm.at[pl.ds(src_row, 1)], buf.at[pl.ds(r, 1)], sem)

    @pl.when(i < nu_ref[0])
    def _():
        def issue(r, c):
            row_copy(r, tok_ref[i * rows + r]).start()
            return c

        lax.fori_loop(0, rows, issue, 0)

        def drain(r, c):
            row_copy(r, 0).wait()
            return c

        lax.fori_loop(0, rows, drain, 0)
        x = buf[...]
        o_ref[...] = (x * lax.rsqrt(jnp.mean(x * x, axis=-1, keepdims=True) + EPS) * g_ref[...]).astype(o_ref.dtype)

    @pl.when(i >= nu_ref[0])
    def _():
        o_ref[...] = jnp.zeros(o_ref.shape, o_ref.dtype)


def moe_gather_norm(x, g, row_tok, n_used, *, tm, rows=128):
    T, D = x.shape
    n_rows = row_tok.shape[0]
    per = tm // rows
    return pl.pallas_call(
        functools.partial(_moe_gather_kernel, rows=rows),
        grid_spec=pltpu.PrefetchScalarGridSpec(
            num_scalar_prefetch=2,
            grid=(n_rows // rows,),
            in_specs=[pl.BlockSpec(memory_space=pl.ANY), pl.BlockSpec((1, D), lambda i, tok, nu: (0, 0))],
            out_specs=pl.BlockSpec((rows, D), lambda i, tok, nu: (i, 0)),
            scratch_shapes=[pltpu.VMEM((rows, D), F32), pltpu.SemaphoreType.DMA(())]),
        out_shape=jax.ShapeDtypeStruct((n_rows, D), BF16),
        compiler_params=_params(("arbitrary",), 8 * _nbytes((rows, D), F32)),
        name="moe_gather",
    )(row_tok, n_used * per, x, g.reshape(1, D))


def _moe_gateup_kernel(be_ref, nu_ref, x_ref, wg_ref, wu_ref, o_ref, wgb, wub):
    i = pl.program_id(1)
    changed = (i == 0) | (be_ref[i] != be_ref[jnp.maximum(i - 1, 0)])

    @pl.when(changed)
    def _():
        wgb[...] = wg_ref[...].astype(BF16)
        wub[...] = wu_ref[...].astype(BF16)

    @pl.when(i < nu_ref[0])
    def _():
        x = x_ref[...]
        g = jnp.dot(x, wgb[...], preferred_element_type=F32)
        u = jnp.dot(x, wub[...], preferred_element_type=F32)
        o_ref[...] = (g * jax.nn.sigmoid(g) * u).astype(o_ref.dtype)

    @pl.when(i >= nu_ref[0])
    def _():
        o_ref[...] = jnp.zeros(o_ref.shape, o_ref.dtype)


def moe_gateup(xs, wg, wu, be, n_used, *, tm, tf=512):
    n_rows, K = xs.shape
    F = wg.shape[2]
    nb = n_rows // tm
    xmap = lambda j, i, be, nu: (jnp.minimum(i, nu[0] - 1), 0)
    wmap = lambda j, i, be, nu: (be[i], 0, j)
    vmem = (2 * _nbytes((tm, K), BF16) + 4 * _nbytes((K, tf), F32) + 2 * _nbytes((K, tf), BF16)
            + 6 * _nbytes((tm, tf), F32))
    return pl.pallas_call(
        _moe_gateup_kernel,
        grid_spec=pltpu.PrefetchScalarGridSpec(
            num_scalar_prefetch=2,
            grid=(F // tf, nb),
            in_specs=[pl.BlockSpec((tm, K), xmap),
                      pl.BlockSpec((None, K, tf), wmap),
                      pl.BlockSpec((None, K, tf), wmap)],
            out_specs=pl.BlockSpec((tm, tf), lambda j, i, be, nu: (i, j)),
            scratch_shapes=[pltpu.VMEM((K, tf), BF16), pltpu.VMEM((K, tf), BF16)]),
        out_shape=jax.ShapeDtypeStruct((n_rows, F), BF16),
        compiler_params=_params(("arbitrary", "arbitrary"), vmem),
        name="moe_gateup",
    )(be, n_used, xs, wg, wu)


def _moe_down_kernel(be_ref, nu_ref, x_ref, w_ref, gate_ref, o_ref, wb):
    i = pl.program_id(1)
    changed = (i == 0) | (be_ref[i] != be_ref[jnp.maximum(i - 1, 0)])

    @pl.when(changed)
    def _():
        wb[...] = w_ref[...].astype(BF16)

    @pl.when(i < nu_ref[0])
    def _():
        o_ref[...] = jnp.dot(x_ref[...], wb[...], preferred_element_type=F32) * gate_ref[...]

    @pl.when(i >= nu_ref[0])
    def _():
        o_ref[...] = jnp.zeros(o_ref.shape, o_ref.dtype)


def moe_down(hm, wd, row_gate, be, n_used, *, tm, tn=512):
    n_rows, K = hm.shape
    N = wd.shape[2]
    nb = n_rows // tm
    xmap = lambda j, i, be, nu: (jnp.minimum(i, nu[0] - 1), 0)
    vmem = (2 * _nbytes((tm, K), BF16) + 2 * _nbytes((K, tn), F32) + _nbytes((K, tn), BF16)
            + 4 * _nbytes((tm, tn), F32))
    return pl.pallas_call(
        _moe_down_kernel,
        grid_spec=pltpu.PrefetchScalarGridSpec(
            num_scalar_prefetch=2,
            grid=(N // tn, nb),
            in_specs=[pl.BlockSpec((tm, K), xmap),
                      pl.BlockSpec((None, K, tn), lambda j, i, be, nu: (be[i], 0, j)),
                      pl.BlockSpec((tm, 1), xmap)],
            out_specs=pl.BlockSpec((tm, tn), lambda j, i, be, nu: (i, j)),
            scratch_shapes=[pltpu.VMEM((K, tn), BF16)]),
        out_shape=jax.ShapeDtypeStruct((n_rows, N), F32),
        compiler_params=_params(("arbitrary", "arbitrary"), vmem),
        name="moe_down",
    )(be, n_used, hm, wd, row_gate)


def _moe_combine_kernel(pos_ref, x_ref, ys_hbm, o_ref, buf, sem, *, rows):
    i = pl.program_id(0)

    def row_copy(slot, r, src_row):
        return pltpu.make_async_copy(ys_hbm.at[pl.ds(src_row, 1)], buf.at[slot, pl.ds(r, 1)], sem)

    def issue(r, c):
        a = (i * rows + r) * TOP_K
        for slot in range(TOP_K):
            row_copy(slot, r, pos_ref[a + slot]).start()
        return c

    lax.fori_loop(0, rows, issue, 0)

    def drain(r, c):
        for slot in range(TOP_K):
            row_copy(slot, r, 0).wait()
        return c

    lax.fori_loop(0, rows, drain, 0)
    out = x_ref[...]
    for slot in range(TOP_K):
        out = out + buf[slot]
    o_ref[...] = out


def moe_combine(x, ys, dest, *, rows=128):
    T, D = x.shape
    return pl.pallas_call(
        functools.partial(_moe_combine_kernel, rows=rows),
        grid_spec=pltpu.PrefetchScalarGridSpec(
            num_scalar_prefetch=1,
            grid=(T // rows,),
            in_specs=[pl.BlockSpec((rows, D), lambda i, pos: (i, 0)), pl.BlockSpec(memory_space=pl.ANY)],
            out_specs=pl.BlockSpec((rows, D), lambda i, pos: (i, 0)),
            scratch_shapes=[pltpu.VMEM((TOP_K, rows, D), F32), pltpu.SemaphoreType.DMA(())]),
        out_shape=jax.ShapeDtypeStruct((T, D), F32),
        compiler_params=_params(("arbitrary",), 10 * _nbytes((rows, D), F32)),
        name="moe_combine",
    )(dest, x, ys)


def moe_block(x, g, w_router, b_router, w_egate, w_eup, w_edown, *, tm=512):
    idx, gw = moe_router(x, g, w_router, b_router)
    dest, row_tok, row_gate, be, n_used = _moe_dispatch(idx, gw, tm)
    xs = moe_gather_norm(x, g, row_tok, n_used, tm=tm)
    hm = moe_gateup(xs, w_egate, w_eup, be, n_used, tm=tm)
    ys = moe_down(hm, w_edown, row_gate, be, n_used, tm=tm)
    return moe_combine(x, ys, dest)


def cross_attention_block(x, mk, mv, g_x, w_xq, g_xq, w_xo, *, seq, mem_len):
    h = rmsnorm(x, g_x)
    qx = matmul(h, w_xq, name="xattn_q")
    o = cross_attention_core(qx, mk, mv, g_xq, seq=seq, mem_len=mem_len)
    return matmul(o, w_xo, res=x, name="xattn_o")


def even_mixer_block(x, tabs_a, tabs_b, w_in, ga_q, ga_k, g_cq, w_uq, g_ckv, w_ukv, gb_q, gb_kn, gb_kr,
                     w_o, g_mix, *, batch, seq):
    a_width = A_HEADS * HEAD_DIM
    main_cols = 3 * a_width + Q_LORA + KV_LORA
    h = rmsnorm(x, g_mix)
    proj = matmul(h, w_in, n_out=main_cols, name="in_proj")
    w_kr = jnp.pad(w_in[:, main_cols:], ((0, 0), (0, LANES - ROPE_DIM)))
    krp = matmul(h, w_kr, name="in_proj_kr")
    o_a = mixer_a(proj, tabs_a, ga_q, ga_k, batch=batch, seq=seq)
    cq = rmsnorm(proj, g_cq, width=Q_LORA, col_block=3 * a_width // Q_LORA)
    ckv = rmsnorm(proj, g_ckv, width=KV_LORA, col_block=(3 * a_width + Q_LORA) // KV_LORA)
    w3 = w_uq.reshape(Q_LORA, B_HEADS, QK_DIM)
    w_uq_p = jnp.concatenate(
        [w3[:, :, :NOPE_DIM].reshape(Q_LORA, B_HEADS * NOPE_DIM),
         jnp.pad(w3[:, :, NOPE_DIM:], ((0, 0), (0, 0), (0, LANES - ROPE_DIM))).reshape(Q_LORA, B_HEADS * LANES)],
        axis=1)
    q_up = matmul(cq, w_uq_p, name="q_up")
    kv_up = matmul(ckv, w_ukv, name="kv_up")
    pad_r = lambda v: jnp.pad(v, (0, LANES - ROPE_DIM)).reshape(1, LANES)
    qf, kf, vb = mla_prep(q_up, kv_up, krp, tabs_b, gb_q[:NOPE_DIM].reshape(1, LANES), pad_r(gb_q[NOPE_DIM:]),
                          gb_kn.reshape(1, LANES), pad_r(gb_kr))
    o_b = mla_attention(qf, kf, vb, batch=batch, seq=seq)
    return matmul((o_a, o_b), w_o, res=x, name="mixer_out")


def kernel(x, mem, positions, g_mem, w_mem_kv, g_mem_k, g_mix, g_x, w_xq, g_xq, w_xo, g_ffn,
           w_in, ga_q, ga_k, g_cq, w_uq, g_ckv, w_ukv, gb_q, gb_kn, gb_kr, w_o_even,
           w_gate, w_up, w_down, w_qkv, w_o_odd, w_router, b_router, w_egate, w_eup, w_edown):
    B, S, D = x.shape
    M = mem.shape[1]
    depth = g_mix.shape[0]
    T = B * S
    xt = x.reshape(T, D)
    tabs_a = _rope_tables(positions, ROT_DIM)
    tabs_b = _rope_tables(positions, ROPE_DIM)
    kv = matmul(rmsnorm(mem.reshape(B * M, D), g_mem), w_mem_kv, tm=B * M, name="mem_kv")
    mk, mv = memkv_post(kv, g_mem_k)
    for layer in range(depth):
        i = layer // 2
        if layer % 2 == 0:
            xt = even_mixer_block(xt, tabs_a, tabs_b, w_in[i], ga_q[i], ga_k[i], g_cq[i], w_uq[i], g_ckv[i],
                                  w_ukv[i], gb_q[i], gb_kn[i], gb_kr[i], w_o_even[i], g_mix[layer],
                                  batch=B, seq=S)
        else:
            h = rmsnorm(xt, g_mix[layer])
            qkv = matmul(h, w_qkv[i], out_dtype=BF16, name="qkv_proj")
            o = stickbreak_attention(qkv, batch=B, seq=S, heads=D // HEAD_DIM)
            xt = matmul(o, w_o_odd[i], res=xt, name="mixer_out")
        xt = cross_attention_block(xt, mk, mv, g_x[layer], w_xq[layer], g_xq[layer], w_xo[layer], seq=S, mem_len=M)
        if layer % 2 == 0:
            h = rmsnorm(xt, g_ffn[layer])
            hm = swiglu_gateup(h, w_gate[i], w_up[i])
            xt = matmul(hm, w_down[i], res=xt, tn=1024, tk=2048, name="swiglu_down")
        else:
            xt = moe_block(xt, g_ffn[layer], w_router[i], b_router[i], w_egate[i], w_eup[i], w_edown[i])
    return xt.reshape(B, S, D)
```

```python
import functools

import jax
import jax.numpy as jnp
from jax import lax
from jax.experimental import pallas as pl
from jax.experimental.pallas import tpu as pltpu

F32 = jnp.float32
BF16 = jnp.bfloat16

HEAD_DIM = 128
ROT_DIM = HEAD_DIM // 4
ROPE_THETA = 500000.0
BLOCK = 128
NEG_INF = -1e30
EPS = 1e-6
DILATED_PAIRS = ((128, 1), (512, 4), (2048, 16))
A_HEADS = 16
B_HEADS = 16
Q_LORA = 1536
KV_LORA = 512
NOPE_DIM = 128
ROPE_DIM = 64
QK_DIM = NOPE_DIM + ROPE_DIM
X_HEADS = 4
N_EXPERTS = 8
TOP_K = 2
MIXER_A_GROUP = 8
LOG2_E = 1.4426950408889634
MLA_Q_SCALE = QK_DIM ** -0.5 * LOG2_E

LANES = 128
V7X_VMEM_BYTES = 64 * 1024 * 1024
VMEM_CAP = V7X_VMEM_BYTES - 6 * 1024 * 1024


def _params(semantics, vmem_bytes):
    return pltpu.CompilerParams(dimension_semantics=semantics,
                                vmem_limit_bytes=int(min(max(vmem_bytes, 32 * 1024 * 1024), VMEM_CAP)))


def _nbytes(shape, dtype):
    n = 1
    for s in shape:
        n *= s
    return n * jnp.dtype(dtype).itemsize


def _rmsnorm_kernel(x_ref, g_ref, o_ref):
    x = x_ref[...].astype(F32)
    ms = jnp.mean(x * x, axis=-1, keepdims=True)
    o_ref[...] = (x * lax.rsqrt(ms + EPS) * g_ref[...]).astype(o_ref.dtype)


def rmsnorm(x, g, *, width=None, col_block=0, tm=256, out_dtype=BF16):
    T = x.shape[0]
    width = x.shape[1] if width is None else width
    tm = min(tm, T)
    return pl.pallas_call(
        _rmsnorm_kernel,
        grid=(T // tm,),
        in_specs=[pl.BlockSpec((tm, width), lambda i: (i, col_block)),
                  pl.BlockSpec((1, width), lambda i: (0, 0))],
        out_specs=pl.BlockSpec((tm, width), lambda i: (i, 0)),
        out_shape=jax.ShapeDtypeStruct((T, width), out_dtype),
        compiler_params=_params(("parallel",), 6 * _nbytes((tm, width), F32)),
        name="rmsnorm",
    )(x, g.reshape(1, width).astype(F32))


def _mm_kernel(*refs, nk, n_x, has_res):
    x_refs = refs[:n_x]
    w_ref = refs[n_x]
    rest = refs[n_x + 1:]
    res_ref = rest[0] if has_res else None
    o_ref = rest[1] if has_res else rest[0]
    acc_ref = rest[-1] if nk > 1 else None
    k = pl.program_id(2)
    w = w_ref[...].astype(BF16)

    def finish(part):
        out = part
        if has_res:
            out = out + res_ref[...]
        o_ref[...] = out.astype(o_ref.dtype)

    if n_x == 2:
        @pl.when(k == 0)
        def _():
            acc_ref[...] = jnp.dot(x_refs[0][...], w, preferred_element_type=F32)

        @pl.when(k == 1)
        def _():
            finish(acc_ref[...] + jnp.dot(x_refs[1][...], w, preferred_element_type=F32))
    elif nk == 1:
        finish(jnp.dot(x_refs[0][...], w, preferred_element_type=F32))
    else:
        part = jnp.dot(x_refs[0][...], w, preferred_element_type=F32)

        @pl.when(k == 0)
        def _():
            acc_ref[...] = part

        @pl.when((k > 0) & (k < nk - 1))
        def _():
            acc_ref[...] += part

        @pl.when(k == nk - 1)
        def _():
            finish(acc_ref[...] + part)


def matmul(xs, w, *, n_out=None, res=None, out_dtype=F32, tm=1024, tn=512, tk=None, name="matmul"):
    if not isinstance(xs, (tuple, list)):
        xs = (xs,)
    n_x = len(xs)
    T = xs[0].shape[0]
    K = sum(x.shape[1] for x in xs)
    n_out = w.shape[1] if n_out is None else n_out
    tm = min(tm, T)
    tn = min(tn, n_out)
    if n_x == 2:
        tk = xs[0].shape[1]
        assert xs[1].shape[1] == tk
    tk = K if tk is None else tk
    nk = K // tk
    assert K == nk * tk and T % tm == 0 and n_out % tn == 0
    if n_x == 2:
        x_specs = [pl.BlockSpec((tm, tk), lambda i, j, k: (i, 0)) for _ in xs]
    else:
        x_specs = [pl.BlockSpec((tm, tk), lambda i, j, k: (i, k))]
    in_specs = x_specs + [pl.BlockSpec((tk, tn), lambda i, j, k: (k, j))]
    args = list(xs) + [w]
    if res is not None:
        in_specs.append(pl.BlockSpec((tm, tn), lambda i, j, k: (i, j)))
        args.append(res)
    scratch = [pltpu.VMEM((tm, tn), F32)] if nk > 1 else []
    vmem = (2 * n_x * _nbytes((tm, tk), xs[0].dtype) + 2 * _nbytes((tk, tn), F32) + _nbytes((tk, tn), BF16)
            + (2 * _nbytes((tm, tn), F32) if res is not None else 0)
            + 2 * _nbytes((tm, tn), out_dtype) + 2 * _nbytes((tm, tn), F32))
    return pl.pallas_call(
        functools.partial(_mm_kernel, nk=nk, n_x=n_x, has_res=res is not None),
        grid=(T // tm, n_out // tn, nk),
        in_specs=in_specs,
        out_specs=pl.BlockSpec((tm, tn), lambda i, j, k: (i, j)),
        out_shape=jax.ShapeDtypeStruct((T, n_out), out_dtype),
        scratch_shapes=scratch,
        compiler_params=_params(("parallel", "parallel", "arbitrary"), vmem),
        name=name,
    )(*args)


def _rope_tables(positions, dim):
    half = dim // 2
    inv_freq = ROPE_THETA ** (-jnp.arange(0, dim, 2, dtype=F32) / dim)
    ang = positions.astype(F32)[..., None] * inv_freq
    cos, sin = jnp.cos(ang), jnp.sin(ang)
    B, S = positions.shape
    ones = jnp.ones((B, S, LANES - dim), F32)
    zeros = jnp.zeros((B, S, LANES - dim), F32)
    zh = jnp.zeros((B, S, half), F32)
    c = jnp.concatenate([cos, cos, ones], axis=-1)
    sa = jnp.concatenate([-sin, zh, zeros], axis=-1)
    sb = jnp.concatenate([zh, sin, zeros], axis=-1)
    return [t.reshape(B * S, LANES) for t in (c, sa, sb)]


def _rope_lanes(y, c, sa, sb, half):
    return y * c + pltpu.roll(y, LANES - half, 1) * sa + pltpu.roll(y, half, 1) * sb


def _mixer_a_kernel(q_ref, k_ref, v_ref, c_ref, sa_ref, sb_ref, gq_ref, gk_ref, o_ref,
                    qs, ks, *state, seq):
    scale = HEAD_DIM ** -0.5
    c, sa, sb = c_ref[...], sa_ref[...], sb_ref[...]

    def prep(x, g):
        y = x * lax.rsqrt(jnp.mean(x * x, axis=-1, keepdims=True) + EPS) * g
        return _rope_lanes(y, c, sa, sb, ROT_DIM // 2)

    qs[...] = prep(q_ref[...], gq_ref[...])
    ks[...] = prep(k_ref[...], gk_ref[...])
    qi = lax.broadcasted_iota(jnp.int32, (BLOCK, 1), 0)
    kj = lax.broadcasted_iota(jnp.int32, (1, BLOCK), 1)
    trans_b = (((1,), (1,)), ((), ()))
    ones = jnp.ones((BLOCK, HEAD_DIM), BF16)

    for bi, (window, dil) in enumerate(DILATED_PAIRS):
        acc, m_sc, l_sc = state[3 * bi:3 * bi + 3]
        n_back = window // dil
        nb = seq // dil // BLOCK
        assert n_back <= BLOCK and nb * dil * BLOCK == seq
        cur_ok = (qi >= kj) & (qi - kj <= n_back)
        prev_ok = qi + BLOCK - kj <= n_back

        def rows_at(r, n, dil=dil):
            if dil == 1:
                return pl.ds(BLOCK * n, BLOCK)
            return pl.ds(r + dil * BLOCK * n, BLOCK, stride=dil)

        blocks = [(rows_at(r, n), rows_at(r, n - 1) if n > 0 else None) for r in range(dil) for n in range(nb)]
        for first in range(0, len(blocks), MIXER_A_GROUP):
            group = blocks[first:first + MIXER_A_GROUP]
            s_cur, s_prev = [], []
            for rows, prev in group:
                qb = qs[rows, :].astype(BF16)
                s_c = lax.dot_general(qb, ks[rows, :].astype(BF16), trans_b, preferred_element_type=F32) * scale
                s_cur.append(jnp.where(cur_ok, s_c, NEG_INF))
                if prev is None:
                    s_prev.append(None)
                else:
                    s_p = lax.dot_general(qb, ks[prev, :].astype(BF16), trans_b,
                                          preferred_element_type=F32) * scale
                    s_prev.append(jnp.where(prev_ok, s_p, NEG_INF))
            stats = []
            for s_c, s_p in zip(s_cur, s_prev):
                m = jnp.max(s_c, axis=-1, keepdims=True)
                if s_p is not None:
                    m = jnp.maximum(m, jnp.max(s_p, axis=-1, keepdims=True))
                p_c = jnp.exp(s_c - m).astype(BF16)
                p_p = None if s_p is None else jnp.exp(s_p - m).astype(BF16)
                stats.append((m, p_c, p_p))
            for (rows, prev), (m, p_c, p_p) in zip(group, stats):
                v_ext = jnp.concatenate([v_ref[rows, :].astype(BF16), ones], axis=1)
                o = jnp.dot(p_c, v_ext, preferred_element_type=F32)
                if p_p is not None:
                    v_ext = jnp.concatenate([v_ref[prev, :].astype(BF16), ones], axis=1)
                    o = o + jnp.dot(p_p, v_ext, preferred_element_type=F32)
                acc[rows, :] = o[:, :HEAD_DIM]
                l_sc[rows, :] = o[:, HEAD_DIM:]
                m_sc[rows, :] = jnp.broadcast_to(m, (BLOCK, HEAD_DIM))

    n_br = len(DILATED_PAIRS)
    m_all = state[1][...]
    for bi in range(1, n_br):
        m_all = jnp.maximum(m_all, state[3 * bi + 1][...])
    num = jnp.zeros(o_ref.shape, F32)
    den = jnp.zeros(o_ref.shape, F32)
    for bi in range(n_br):
        w = jnp.exp(state[3 * bi + 1][...] - m_all)
        num = num + w * state[3 * bi][...]
        den = den + w * state[3 * bi + 2][...]
    o_ref[...] = (num / den).astype(o_ref.dtype)


def mixer_a(proj, tabs, gq, gk, *, batch, seq):
    T = batch * seq
    blk = (seq, HEAD_DIM)
    head = lambda off: pl.BlockSpec(blk, lambda b, h: (b, off + h))
    tab = pl.BlockSpec(blk, lambda b, h: (b, 0))
    gspec = pl.BlockSpec((1, HEAD_DIM), lambda b, h: (0, 0))
    return pl.pallas_call(
        functools.partial(_mixer_a_kernel, seq=seq),
        grid=(batch, A_HEADS),
        in_specs=[head(0), head(A_HEADS), head(2 * A_HEADS), tab, tab, tab, gspec, gspec],
        out_specs=pl.BlockSpec(blk, lambda b, h: (b, h)),
        out_shape=jax.ShapeDtypeStruct((T, A_HEADS * HEAD_DIM), BF16),
        scratch_shapes=[pltpu.VMEM(blk, F32) for _ in range(2 + 3 * len(DILATED_PAIRS))],
        compiler_params=_params(("parallel", "parallel"), 30 * _nbytes(blk, F32)),
        name="mixer_a",
    )(proj, proj, proj, *tabs, gq.reshape(1, HEAD_DIM), gk.reshape(1, HEAD_DIM))


def _mla_prep_kernel(qn_ref, qr_ref, kn_ref, v_ref, kr_ref, c_ref, sa_ref, sb_ref,
                     gqn_ref, gqr_ref, gkn_ref, gkr_ref, qt_ref, kf_ref, vb_ref):
    c, sa, sb = c_ref[...], sa_ref[...], sb_ref[...]
    half = ROPE_DIM // 2
    qn, qr = qn_ref[...], qr_ref[...]
    ss = jnp.sum(qn * qn, axis=-1, keepdims=True) + jnp.sum(qr * qr, axis=-1, keepdims=True)
    inv = lax.rsqrt(ss * (1.0 / QK_DIM) + EPS) * MLA_Q_SCALE
    qf = jnp.concatenate([qn * inv * gqn_ref[...], _rope_lanes(qr * inv * gqr_ref[...], c, sa, sb, half)], axis=1)
    qt_ref[...] = qf.T.astype(qt_ref.dtype)
    kn = kn_ref[...]
    kn = kn * lax.rsqrt(jnp.mean(kn * kn, axis=-1, keepdims=True) + EPS) * gkn_ref[...]
    kr = kr_ref[...]
    kr = kr * lax.rsqrt(jnp.sum(kr * kr, axis=-1, keepdims=True) * (1.0 / ROPE_DIM) + EPS) * gkr_ref[...]
    kf_ref[:, :NOPE_DIM] = kn.astype(kf_ref.dtype)
    kf_ref[:, NOPE_DIM:] = _rope_lanes(kr, c, sa, sb, half).astype(kf_ref.dtype)
    vb_ref[...] = v_ref[...].astype(vb_ref.dtype)


def mla_prep(q_up, kv_up, krp, tabs, gqn, gqr, gkn, gkr, *, batch, seq, tm=512):
    T = q_up.shape[0]
    H = B_HEADS
    per_seq = seq // tm
    blk = lambda f: pl.BlockSpec((tm, LANES), f)
    g = pl.BlockSpec((1, LANES), lambda i, h: (0, 0))
    return pl.pallas_call(
        _mla_prep_kernel,
        grid=(T // tm, H),
        in_specs=[blk(lambda i, h: (i, h)), blk(lambda i, h: (i, H + h)),
                  blk(lambda i, h: (i, 2 * h)), blk(lambda i, h: (i, 2 * h + 1)),
                  blk(lambda i, h: (i, 0)), blk(lambda i, h: (i, 0)), blk(lambda i, h: (i, 0)),
                  blk(lambda i, h: (i, 0)), g, g, g, g],
        out_specs=[pl.BlockSpec((None, None, 2 * LANES, tm), lambda i, h: (i // per_seq, h, 0, i % per_seq)),
                   pl.BlockSpec((tm, 2 * LANES), lambda i, h: (i, h)),
                   pl.BlockSpec((tm, LANES), lambda i, h: (i, h))],
        out_shape=[jax.ShapeDtypeStruct((batch, H, 2 * LANES, seq), BF16),
                   jax.ShapeDtypeStruct((T, H * 2 * LANES), BF16),
                   jax.ShapeDtypeStruct((T, H * LANES), BF16)],
        compiler_params=_params(("parallel", "parallel"), 40 * _nbytes((tm, LANES), F32)),
        name="mla_prep",
    )(q_up, q_up, kv_up, kv_up, krp, *tabs, gqn, gqr, gkn, gkr)


def _causal_attn_kernel(qt_ref, k_ref, v_ref, o_ref, acc, *, tq, hg, dq, dv):
    qi = pl.program_id(2)
    key = lax.broadcasted_iota(jnp.int32, (tq, tq), 0)
    qry = lax.broadcasted_iota(jnp.int32, (tq, tq), 1)
    visible = key <= qry
    ones = jnp.ones((tq, dv), BF16)
    contract_rows = (((0,), (0,)), ((), ()))

    heads = range(hg)

    def scores(j, diagonal):
        rows = pl.ds(pl.multiple_of(j * tq, tq), tq)
        s = [jnp.dot(k_ref[rows, g * dq:(g + 1) * dq], qt_ref[g], preferred_element_type=F32) for g in heads]
        if diagonal:
            s = [jnp.where(visible, sg, NEG_INF) for sg in s]
        return rows, s

    def col_max(j, ms):
        s = scores(j, False)[1]
        return tuple(jnp.maximum(ms[g], jnp.max(s[g], axis=0, keepdims=True)) for g in heads)

    ms = lax.fori_loop(0, qi, col_max, tuple(jnp.max(sg, axis=0, keepdims=True) for sg in scores(qi, True)[1]))

    def weighted(j, diagonal):
        rows, s = scores(j, diagonal)
        p = [jnp.exp2(s[g] - ms[g]).astype(BF16) for g in heads]
        return [lax.dot_general(p[g], jnp.concatenate([v_ref[rows, g * dv:(g + 1) * dv], ones], axis=1),
                                contract_rows, preferred_element_type=F32) for g in heads]

    for g, w in enumerate(weighted(qi, True)):
        acc[g] = w

    def below(j, carry):
        for g, w in enumerate(weighted(j, False)):
            acc[g] += w
        return carry

    lax.fori_loop(0, qi, below, 0)
    for g in heads:
        a = acc[g]
        o_ref[:, g * dv:(g + 1) * dv] = (a[:, :dv] / a[:, dv:]).astype(o_ref.dtype)


def mla_attention(qt, kf, vb, *, batch, seq, tq=256, hg=4):
    T = batch * seq
    nq = seq // tq
    dq = qt.shape[2]
    dv = vb.shape[1] // B_HEADS
    assert B_HEADS % hg == 0
    return pl.pallas_call(
        functools.partial(_causal_attn_kernel, tq=tq, hg=hg, dq=dq, dv=dv),
        grid=(batch, B_HEADS // hg, nq),
        in_specs=[pl.BlockSpec((None, hg, dq, tq), lambda b, h, i: (b, h, 0, i)),
                  pl.BlockSpec((seq, hg * dq), lambda b, h, i: (b, h)),
                  pl.BlockSpec((seq, hg * dv), lambda b, h, i: (b, h))],
        out_specs=pl.BlockSpec((tq, hg * dv), lambda b, h, i: (b * nq + i, h)),
        out_shape=jax.ShapeDtypeStruct((T, B_HEADS * dv), BF16),
        scratch_shapes=[pltpu.VMEM((hg, tq, 2 * dv), F32)],
        compiler_params=_params(("parallel", "parallel", "parallel"), 0),
        name="mla_attention",
    )(qt, kf, vb)


STICK_EXHAUSTED_LOG = -110.0


def _stickbreak_kernel(q_ref, k_ref, v_ref, o_ref, qt_sc, c_sc, acc, *, tq, hg, scale):
    qi = pl.program_id(2)
    c_sc[...] = jnp.zeros(c_sc.shape, F32)
    acc[...] = jnp.zeros(acc.shape, F32)
    for g in range(hg):
        qt_sc[g] = q_ref[:, g * HEAD_DIM:(g + 1) * HEAD_DIM].astype(F32).T.astype(BF16)
    key = lax.broadcasted_iota(jnp.int32, (tq, tq), 0)
    qry = lax.broadcasted_iota(jnp.int32, (tq, tq), 1)
    after = (qry > key).astype(BF16)
    earlier = key < qry
    contract_rows = (((0,), (0,)), ((), ()))

    def chunk(j, diagonal):
        rows = pl.ds(pl.multiple_of(j * tq, tq), tq)
        heads = range(hg)
        cols = [slice(g * HEAD_DIM, (g + 1) * HEAD_DIM) for g in heads]
        z = [jnp.dot(k_ref[rows, cols[g]], qt_sc[g], preferred_element_type=F32) * scale for g in heads]
        log_beta, parts = [], []
        for g in heads:
            softplus = jnp.maximum(z[g], 0.0) + jnp.log(1.0 + jnp.exp(-jnp.abs(z[g])))
            log_keep = -softplus
            if diagonal:
                log_keep = jnp.where(earlier, log_keep, 0.0)
            hi = log_keep.astype(BF16)
            parts += [hi, (log_keep - hi.astype(F32)).astype(BF16)]
            log_beta.append(z[g] - softplus + c_sc[g])
            c_sc[g] += jnp.sum(log_keep, axis=0, keepdims=True)
        within = jnp.dot(after, jnp.concatenate(parts, axis=1), preferred_element_type=F32)
        for g in heads:
            later = within[:, 2 * g * tq:(2 * g + 1) * tq] + within[:, (2 * g + 1) * tq:(2 * g + 2) * tq]
            a = jnp.exp(log_beta[g] + later)
            if diagonal:
                a = jnp.where(earlier, a, 0.0)
            acc[:, cols[g]] += lax.dot_general(a.astype(BF16), v_ref[rows, cols[g]], contract_rows,
                                               preferred_element_type=F32)

    def stick_left():
        return jnp.max(c_sc[...]) > STICK_EXHAUSTED_LOG

    chunk(qi, True)

    def more(state):
        j, left = state
        return (j >= 0) & left

    def step(state):
        j, _ = state
        chunk(j, False)
        return j - 1, stick_left()

    lax.while_loop(more, step, (qi - 1, stick_left()))
    o_ref[...] = acc[...].astype(o_ref.dtype)


def stickbreak_attention(qkv, *, batch, seq, heads, tq=256, hg=4):
    T = batch * seq
    nq = seq // tq
    ng = heads // hg
    assert heads == ng * hg
    blk = hg * HEAD_DIM
    return pl.pallas_call(
        functools.partial(_stickbreak_kernel, tq=tq, hg=hg, scale=HEAD_DIM ** -0.5),
        grid=(batch, ng, nq),
        in_specs=[pl.BlockSpec((tq, blk), lambda b, h, i: (b * nq + i, h)),
                  pl.BlockSpec((seq, blk), lambda b, h, i: (b, ng + h)),
                  pl.BlockSpec((seq, blk), lambda b, h, i: (b, 2 * ng + h))],
        out_specs=pl.BlockSpec((tq, blk), lambda b, h, i: (b * nq + i, h)),
        out_shape=jax.ShapeDtypeStruct((T, heads * HEAD_DIM), BF16),
        scratch_shapes=[pltpu.VMEM((hg, HEAD_DIM, tq), BF16), pltpu.VMEM((hg, 1, tq), F32),
                        pltpu.VMEM((tq, blk), F32)],
        compiler_params=_params(("parallel", "parallel", "parallel"), 0),
        name="stickbreak",
    )(qkv, qkv, qkv)


def _memkv_kernel(kv_ref, g_ref, mk_ref, mv_ref):
    for h in range(X_HEADS):
        cols = slice(h * HEAD_DIM, (h + 1) * HEAD_DIM)
        k = kv_ref[:, cols]
        k = k * lax.rsqrt(jnp.mean(k * k, axis=-1, keepdims=True) + EPS) * g_ref[...]
        mk_ref[:, cols] = k.astype(mk_ref.dtype)
    mv_ref[...] = kv_ref[:, X_HEADS * HEAD_DIM:].astype(mv_ref.dtype)


def memkv_post(kv, g):
    M = kv.shape[0]
    xd = X_HEADS * HEAD_DIM
    return pl.pallas_call(
        _memkv_kernel,
        grid=(1,),
        in_specs=[pl.BlockSpec((M, 2 * xd), lambda i: (0, 0)), pl.BlockSpec((1, HEAD_DIM), lambda i: (0, 0))],
        out_specs=[pl.BlockSpec((M, xd), lambda i: (0, 0)), pl.BlockSpec((M, xd), lambda i: (0, 0))],
        out_shape=[jax.ShapeDtypeStruct((M, xd), BF16), jax.ShapeDtypeStruct((M, xd), BF16)],
        compiler_params=_params(("arbitrary",), 0),
        name="memkv_post",
    )(kv, g.reshape(1, HEAD_DIM))


def _xattn_kernel(q_ref, mk_ref, mv_ref, g_ref, o_ref):
    scale = HEAD_DIM ** -0.5
    for h in range(X_HEADS):
        cols = slice(h * HEAD_DIM, (h + 1) * HEAD_DIM)
        q = q_ref[:, cols]
        q = (q * lax.rsqrt(jnp.mean(q * q, axis=-1, keepdims=True) + EPS) * g_ref[...]).astype(BF16)
        s = lax.dot_general(q, mk_ref[:, cols], (((1,), (1,)), ((), ())), preferred_element_type=F32) * scale
        p = jnp.exp(s - jnp.max(s, axis=-1, keepdims=True))
        p = p / jnp.sum(p, axis=-1, keepdims=True)
        o_ref[:, cols] = jnp.dot(p.astype(BF16), mv_ref[:, cols], preferred_element_type=F32).astype(o_ref.dtype)


def cross_attention_core(qx, mk, mv, g, *, seq, mem_len, tm=512):
    T = qx.shape[0]
    xd = X_HEADS * HEAD_DIM
    per_seq = seq // tm
    return pl.pallas_call(
        _xattn_kernel,
        grid=(T // tm,),
        in_specs=[pl.BlockSpec((tm, xd), lambda i: (i, 0)),
                  pl.BlockSpec((mem_len, xd), lambda i: (i // per_seq, 0)),
                  pl.BlockSpec((mem_len, xd), lambda i: (i // per_seq, 0)),
                  pl.BlockSpec((1, HEAD_DIM), lambda i: (0, 0))],
        out_specs=pl.BlockSpec((tm, xd), lambda i: (i, 0)),
        out_shape=jax.ShapeDtypeStruct((T, xd), BF16),
        compiler_params=_params(("parallel",), 0),
        name="cross_attention",
    )(qx, mk, mv, g.reshape(1, HEAD_DIM))


def _gateup_kernel(x_ref, wg_ref, wu_ref, o_ref):
    x = x_ref[...]
    g = jnp.dot(x, wg_ref[...].astype(BF16), preferred_element_type=F32)
    u = jnp.dot(x, wu_ref[...].astype(BF16), preferred_element_type=F32)
    o_ref[...] = (g * jax.nn.sigmoid(g) * u).astype(o_ref.dtype)


def swiglu_gateup(x, wg, wu, *, tm=1024, tf=256):
    T, K = x.shape
    F = wg.shape[1]
    vmem = (2 * _nbytes((tm, K), BF16) + 4 * _nbytes((K, tf), F32) + 2 * _nbytes((K, tf), BF16)
            + 6 * _nbytes((tm, tf), F32))
    return pl.pallas_call(
        _gateup_kernel,
        grid=(T // tm, F // tf),
        in_specs=[pl.BlockSpec((tm, K), lambda i, j: (i, 0)),
                  pl.BlockSpec((K, tf), lambda i, j: (0, j)),
                  pl.BlockSpec((K, tf), lambda i, j: (0, j))],
        out_specs=pl.BlockSpec((tm, tf), lambda i, j: (i, j)),
        out_shape=jax.ShapeDtypeStruct((T, F), BF16),
        compiler_params=_params(("parallel", "parallel"), vmem),
        name="swiglu_gateup",
    )(x, wg, wu)


def _router_kernel(x_ref, g_ref, w_ref, b_ref, idx_ref, gw_ref):
    x = x_ref[...]
    h = x * lax.rsqrt(jnp.mean(x * x, axis=-1, keepdims=True) + EPS) * g_ref[...]
    logits = jnp.dot(h, w_ref[...], preferred_element_type=F32, precision=lax.Precision.HIGHEST) + b_ref[...]
    lane = lax.broadcasted_iota(jnp.int32, logits.shape, 1)
    logits = jnp.where(lane < N_EXPERTS, logits, -jnp.inf)
    m1 = jnp.max(logits, axis=-1, keepdims=True)
    i1 = jnp.min(jnp.where(logits == m1, lane, LANES), axis=-1, keepdims=True)
    rest = jnp.where(lane == i1, -jnp.inf, logits)
    m2 = jnp.max(rest, axis=-1, keepdims=True)
    i2 = jnp.min(jnp.where(rest == m2, lane, LANES), axis=-1, keepdims=True)
    e = jnp.exp(m2 - m1)
    w1 = 1.0 / (1.0 + e)
    w2 = e / (1.0 + e)
    idx_ref[...] = jnp.where(lane == 0, i1, jnp.where(lane == 1, i2, 0))
    gw_ref[...] = jnp.where(lane == 0, w1, jnp.where(lane == 1, w2, 0.0))


def moe_router(x, g, w_router, b_router, *, tm=256):
    T, D = x.shape
    E = w_router.shape[1]
    w = jnp.pad(w_router, ((0, 0), (0, LANES - E)))
    b = jnp.pad(b_router, (0, LANES - E)).reshape(1, LANES)
    idx, gw = pl.pallas_call(
        _router_kernel,
        grid=(T // tm,),
        in_specs=[pl.BlockSpec((tm, D), lambda i: (i, 0)), pl.BlockSpec((1, D), lambda i: (0, 0)),
                  pl.BlockSpec((D, LANES), lambda i: (0, 0)), pl.BlockSpec((1, LANES), lambda i: (0, 0))],
        out_specs=[pl.BlockSpec((tm, LANES), lambda i: (i, 0)), pl.BlockSpec((tm, LANES), lambda i: (i, 0))],
        out_shape=[jax.ShapeDtypeStruct((T, LANES), jnp.int32), jax.ShapeDtypeStruct((T, LANES), F32)],
        compiler_params=_params(("parallel",), 8 * _nbytes((tm, D), F32)),
        name="moe_router",
    )(x, g.reshape(1, D), w, b)
    return idx[:, :TOP_K], gw[:, :TOP_K]


def _moe_dispatch(idx, gw, tm):
    T = idx.shape[0]
    A = T * TOP_K
    e_flat = idx.reshape(A)
    onehot = (e_flat[:, None] == jnp.arange(N_EXPERTS, dtype=jnp.int32)[None, :]).astype(jnp.int32)
    csum = jnp.cumsum(onehot, axis=0)
    pos_in = jnp.sum(csum * onehot, axis=1) - 1
    counts = csum[-1]
    padded = ((counts + tm - 1) // tm) * tm
    gend = jnp.cumsum(padded)
    gstart = gend - padded
    dest = (jnp.sum(onehot * gstart[None, :], axis=1) + pos_in).astype(jnp.int32)
    n_rows = A + N_EXPERTS * tm
    row_tok = jnp.zeros((n_rows,), jnp.int32).at[dest].set(jnp.arange(A, dtype=jnp.int32) // TOP_K)
    row_gate = jnp.zeros((n_rows,), F32).at[dest].set(gw.reshape(A))
    nb = n_rows // tm
    n_used = (gend[-1] // tm).astype(jnp.int32)
    blk = jnp.arange(nb, dtype=jnp.int32)
    be = jnp.sum((blk[:, None] * tm >= gend[None, :]).astype(jnp.int32), axis=1)
    be = jnp.minimum(be, N_EXPERTS - 1)
    be = jnp.where(blk < n_used, be, be[jnp.maximum(n_used - 1, 0)]).astype(jnp.int32)
    return dest, row_tok, row_gate.reshape(n_rows, 1), be, n_used.reshape(1)


def _moe_gather_kernel(tok_ref, nu_ref, x_hbm, g_ref, o_ref, buf, sem, *, rows):
    i = pl.program_id(0)
    n_used = nu_ref[0]

    def row_copy(slot, r, src_row):
        return pltpu.make_async_copy(x_hbm.at[pl.ds(src_row, 1)], buf.at[slot, pl.ds(r, 1)], sem.at[slot])

    def fetch(block, slot):
        def issue(r, c):
            row_copy(slot, r, tok_ref[block * rows + r]).start()
            return c

        lax.fori_loop(0, rows, issue, 0, unroll=8)

    @pl.when(i == 0)
    def _():
        fetch(0, 0)

    @pl.when(i + 1 < n_used)
    def _():
        fetch(i + 1, (i + 1) % 2)

    @pl.when(i < n_used)
    def _():
        slot = i % 2

        def drain(r, c):
            row_copy(slot, r, 0).wait()
            return c

        lax.fori_loop(0, rows, drain, 0, unroll=8)
        x = buf[slot]
        o_ref[...] = (x * lax.rsqrt(jnp.mean(x * x, axis=-1, keepdims=True) + EPS) * g_ref[...]).astype(o_ref.dtype)

    @pl.when(i >= n_used)
    def _():
        o_ref[...] = jnp.zeros(o_ref.shape, o_ref.dtype)


def moe_gather_norm(x, g, row_tok, n_used, *, tm, rows=128):
    T, D = x.shape
    n_rows = row_tok.shape[0]
    per = tm // rows
    return pl.pallas_call(
        functools.partial(_moe_gather_kernel, rows=rows),
        grid_spec=pltpu.PrefetchScalarGridSpec(
            num_scalar_prefetch=2,
            grid=(n_rows // rows,),
            in_specs=[pl.BlockSpec(memory_space=pl.ANY), pl.BlockSpec((1, D), lambda i, tok, nu: (0, 0))],
            out_specs=pl.BlockSpec((rows, D), lambda i, tok, nu: (i, 0)),
            scratch_shapes=[pltpu.VMEM((2, rows, D), F32), pltpu.SemaphoreType.DMA((2,))]),
        out_shape=jax.ShapeDtypeStruct((n_rows, D), BF16),
        compiler_params=_params(("arbitrary",), 10 * _nbytes((rows, D), F32)),
        name="moe_gather",
    )(row_tok, n_used * per, x, g.reshape(1, D))


def _moe_gateup_kernel(be_ref, nu_ref, x_ref, wg_ref, wu_ref, o_ref, wgb, wub):
    i = pl.program_id(1)
    changed = (i == 0) | (be_ref[i] != be_ref[jnp.maximum(i - 1, 0)])

    @pl.when(changed)
    def _():
        wgb[...] = wg_ref[...].astype(BF16)
        wub[...] = wu_ref[...].astype(BF16)

    @pl.when(i < nu_ref[0])
    def _():
        x = x_ref[...]
        g = jnp.dot(x, wgb[...], preferred_element_type=F32)
        u = jnp.dot(x, wub[...], preferred_element_type=F32)
        o_ref[...] = (g * jax.nn.sigmoid(g) * u).astype(o_ref.dtype)

    @pl.when(i >= nu_ref[0])
    def _():
        o_ref[...] = jnp.zeros(o_ref.shape, o_ref.dtype)


def moe_gateup(xs, wg, wu, be, n_used, *, tm, tf=512):
    n_rows, K = xs.shape
    F = wg.shape[2]
    nb = n_rows // tm
    xmap = lambda j, i, be, nu: (jnp.minimum(i, nu[0] - 1), 0)
    wmap = lambda j, i, be, nu: (be[i], 0, j)
    vmem = (2 * _nbytes((tm, K), BF16) + 4 * _nbytes((K, tf), F32) + 2 * _nbytes((K, tf), BF16)
            + 6 * _nbytes((tm, tf), F32))
    return pl.pallas_call(
        _moe_gateup_kernel,
        grid_spec=pltpu.PrefetchScalarGridSpec(
            num_scalar_prefetch=2,
            grid=(F // tf, nb),
            in_specs=[pl.BlockSpec((tm, K), xmap),
                      pl.BlockSpec((None, K, tf), wmap),
                      pl.BlockSpec((None, K, tf), wmap)],
            out_specs=pl.BlockSpec((tm, tf), lambda j, i, be, nu: (i, j)),
            scratch_shapes=[pltpu.VMEM((K, tf), BF16), pltpu.VMEM((K, tf), BF16)]),
        out_shape=jax.ShapeDtypeStruct((n_rows, F), BF16),
        compiler_params=_params(("arbitrary", "arbitrary"), vmem),
        name="moe_gateup",
    )(be, n_used, xs, wg, wu)


def _moe_down_kernel(be_ref, nu_ref, x_ref, w_ref, gate_ref, o_ref, wb):
    i = pl.program_id(1)
    changed = (i == 0) | (be_ref[i] != be_ref[jnp.maximum(i - 1, 0)])

    @pl.when(changed)
    def _():
        wb[...] = w_ref[...].astype(BF16)

    @pl.when(i < nu_ref[0])
    def _():
        o_ref[...] = jnp.dot(x_ref[...], wb[...], preferred_element_type=F32) * gate_ref[...]

    @pl.when(i >= nu_ref[0])
    def _():
        o_ref[...] = jnp.zeros(o_ref.shape, o_ref.dtype)


def moe_down(hm, wd, row_gate, be, n_used, *, tm, tn=512):
    n_rows, K = hm.shape
    N = wd.shape[2]
    nb = n_rows // tm
    xmap = lambda j, i, be, nu: (jnp.minimum(i, nu[0] - 1), 0)
    vmem = (2 * _nbytes((tm, K), BF16) + 2 * _nbytes((K, tn), F32) + _nbytes((K, tn), BF16)
            + 4 * _nbytes((tm, tn), F32))
    return pl.pallas_call(
        _moe_down_kernel,
        grid_spec=pltpu.PrefetchScalarGridSpec(
            num_scalar_prefetch=2,
            grid=(N // tn, nb),
            in_specs=[pl.BlockSpec((tm, K), xmap),
                      pl.BlockSpec((None, K, tn), lambda j, i, be, nu: (be[i], 0, j)),
                      pl.BlockSpec((tm, 1), xmap)],
            out_specs=pl.BlockSpec((tm, tn), lambda j, i, be, nu: (i, j)),
            scratch_shapes=[pltpu.VMEM((K, tn), BF16)]),
        out_shape=jax.ShapeDtypeStruct((n_rows, N), F32),
        compiler_params=_params(("arbitrary", "arbitrary"), vmem),
        name="moe_down",
    )(be, n_used, hm, wd, row_gate)


def _moe_combine_kernel(pos_ref, x_ref, ys_hbm, o_ref, buf, sem, *, rows):
    i = pl.program_id(0)
    n_blocks = pl.num_programs(0)

    def row_copy(slot, k, r, src_row):
        return pltpu.make_async_copy(ys_hbm.at[pl.ds(src_row, 1)], buf.at[slot, k, pl.ds(r, 1)], sem.at[slot])

    def fetch(block, slot):
        def issue(r, c):
            a = (block * rows + r) * TOP_K
            for k in range(TOP_K):
                row_copy(slot, k, r, pos_ref[a + k]).start()
            return c

        lax.fori_loop(0, rows, issue, 0, unroll=4)

    @pl.when(i == 0)
    def _():
        fetch(0, 0)

    @pl.when(i + 1 < n_blocks)
    def _():
        fetch(i + 1, (i + 1) % 2)

    slot = i % 2

    def drain(r, c):
        for k in range(TOP_K):
            row_copy(slot, k, r, 0).wait()
        return c

    lax.fori_loop(0, rows, drain, 0, unroll=4)
    out = x_ref[...]
    for k in range(TOP_K):
        out = out + buf[slot, k]
    o_ref[...] = out


def moe_combine(x, ys, dest, *, rows=128):
    T, D = x.shape
    return pl.pallas_call(
        functools.partial(_moe_combine_kernel, rows=rows),
        grid_spec=pltpu.PrefetchScalarGridSpec(
            num_scalar_prefetch=1,
            grid=(T // rows,),
            in_specs=[pl.BlockSpec((rows, D), lambda i, pos: (i, 0)), pl.BlockSpec(memory_space=pl.ANY)],
            out_specs=pl.BlockSpec((rows, D), lambda i, pos: (i, 0)),
            scratch_shapes=[pltpu.VMEM((2, TOP_K, rows, D), F32), pltpu.SemaphoreType.DMA((2,))]),
        out_shape=jax.ShapeDtypeStruct((T, D), F32),
        compiler_params=_params(("arbitrary",), 12 * _nbytes((rows, D), F32)),
        name="moe_combine",
    )(dest, x, ys)


def moe_block(x, g, w_router, b_router, w_egate, w_eup, w_edown, *, tm=512):
    idx, gw = moe_router(x, g, w_router, b_router)
    dest, row_tok, row_gate, be, n_used = _moe_dispatch(idx, gw, tm)
    xs = moe_gather_norm(x, g, row_tok, n_used, tm=tm)
    hm = moe_gateup(xs, w_egate, w_eup, be, n_used, tm=tm)
    ys = moe_down(hm, w_edown, row_gate, be, n_used, tm=tm)
    return moe_combine(x, ys, dest)


def cross_attention_block(x, mk, mv, g_x, w_xq, g_xq, w_xo, *, seq, mem_len):
    h = rmsnorm(x, g_x)
    qx = matmul(h, w_xq, name="xattn_q")
    o = cross_attention_core(qx, mk, mv, g_xq, seq=seq, mem_len=mem_len)
    return matmul(o, w_xo, res=x, name="xattn_o")


def even_mixer_block(x, tabs_a, tabs_b, w_in, ga_q, ga_k, g_cq, w_uq, g_ckv, w_ukv, gb_q, gb_kn, gb_kr,
                     w_o, g_mix, *, batch, seq):
    a_width = A_HEADS * HEAD_DIM
    main_cols = 3 * a_width + Q_LORA + KV_LORA
    h = rmsnorm(x, g_mix)
    proj = matmul(h, w_in, n_out=main_cols, name="in_proj")
    w_kr = jnp.pad(w_in[:, main_cols:], ((0, 0), (0, LANES - ROPE_DIM)))
    krp = matmul(h, w_kr, name="in_proj_kr")
    o_a = mixer_a(proj, tabs_a, ga_q, ga_k, batch=batch, seq=seq)
    cq = rmsnorm(proj, g_cq, width=Q_LORA, col_block=3 * a_width // Q_LORA)
    ckv = rmsnorm(proj, g_ckv, width=KV_LORA, col_block=(3 * a_width + Q_LORA) // KV_LORA)
    w3 = w_uq.reshape(Q_LORA, B_HEADS, QK_DIM)
    w_uq_p = jnp.concatenate(
        [w3[:, :, :NOPE_DIM].reshape(Q_LORA, B_HEADS * NOPE_DIM),
         jnp.pad(w3[:, :, NOPE_DIM:], ((0, 0), (0, 0), (0, LANES - ROPE_DIM))).reshape(Q_LORA, B_HEADS * LANES)],
        axis=1)
    q_up = matmul(cq, w_uq_p, name="q_up")
    kv_up = matmul(ckv, w_ukv, name="kv_up")
    pad_r = lambda v: jnp.pad(v, (0, LANES - ROPE_DIM)).reshape(1, LANES)
    qt, kf, vb = mla_prep(q_up, kv_up, krp, tabs_b, gb_q[:NOPE_DIM].reshape(1, LANES), pad_r(gb_q[NOPE_DIM:]),
                          gb_kn.reshape(1, LANES), pad_r(gb_kr), batch=batch, seq=seq)
    o_b = mla_attention(qt, kf, vb, batch=batch, seq=seq)
    return matmul((o_a, o_b), w_o, res=x, name="mixer_out")


def kernel(x, mem, positions, g_mem, w_mem_kv, g_mem_k, g_mix, g_x, w_xq, g_xq, w_xo, g_ffn,
           w_in, ga_q, ga_k, g_cq, w_uq, g_ckv, w_ukv, gb_q, gb_kn, gb_kr, w_o_even,
           w_gate, w_up, w_down, w_qkv, w_o_odd, w_router, b_router, w_egate, w_eup, w_edown):
    B, S, D = x.shape
    M = mem.shape[1]
    depth = g_mix.shape[0]
    T = B * S
    xt = x.reshape(T, D)
    tabs_a = _rope_tables(positions, ROT_DIM)
    tabs_b = _rope_tables(positions, ROPE_DIM)
    kv = matmul(rmsnorm(mem.reshape(B * M, D), g_mem), w_mem_kv, tm=B * M, name="mem_kv")
    mk, mv = memkv_post(kv, g_mem_k)
    for layer in range(depth):
        i = layer // 2
        if layer % 2 == 0:
            xt = even_mixer_block(xt, tabs_a, tabs_b, w_in[i], ga_q[i], ga_k[i], g_cq[i], w_uq[i], g_ckv[i],
                                  w_ukv[i], gb_q[i], gb_kn[i], gb_kr[i], w_o_even[i], g_mix[layer],
                                  batch=B, seq=S)
        else:
            h = rmsnorm(xt, g_mix[layer])
            qkv = matmul(h, w_qkv[i], out_dtype=BF16, name="qkv_proj")
            o = stickbreak_attention(qkv, batch=B, seq=S, heads=D // HEAD_DIM)
            xt = matmul(o, w_o_odd[i], res=xt, name="mixer_out")
        xt = cross_attention_block(xt, mk, mv, g_x[layer], w_xq[layer], g_xq[layer], w_xo[layer], seq=S, mem_len=M)
        if layer % 2 == 0:
            h = rmsnorm(xt, g_ffn[layer])
            hm = swiglu_gateup(h, w_gate[i], w_up[i])
            xt = matmul(hm, w_down[i], res=xt, tn=1024, tk=2048, name="swiglu_down")
        else:
            xt = moe_block(xt, g_ffn[layer], w_router[i], b_router[i], w_egate[i], w_eup[i], w_edown[i])
    return xt.reshape(B, S, D)
```

```python
import functools

import jax
import jax.numpy as jnp
from jax import lax
from jax.experimental import pallas as pl
from jax.experimental.pallas import tpu as pltpu

F32 = jnp.float32
BF16 = jnp.bfloat16

HEAD_DIM = 128
ROT_DIM = HEAD_DIM // 4
ROPE_THETA = 500000.0
BLOCK = 128
NEG_INF = -1e30
EPS = 1e-6
DILATED_PAIRS = ((128, 1), (512, 4), (2048, 16))
A_HEADS = 16
B_HEADS = 16
Q_LORA = 1536
KV_LORA = 512
NOPE_DIM = 128
ROPE_DIM = 64
QK_DIM = NOPE_DIM + ROPE_DIM
X_HEADS = 4
N_EXPERTS = 8
TOP_K = 2
MIXER_A_GROUP = 8
LOG2_E = 1.4426950408889634
MLA_Q_SCALE = QK_DIM ** -0.5 * LOG2_E

LANES = 128
V7X_VMEM_BYTES = 64 * 1024 * 1024
VMEM_CAP = V7X_VMEM_BYTES - 6 * 1024 * 1024


def _params(semantics, vmem_bytes):
    return pltpu.CompilerParams(dimension_semantics=semantics,
                                vmem_limit_bytes=int(min(max(vmem_bytes, 32 * 1024 * 1024), VMEM_CAP)))


def _nbytes(shape, dtype):
    n = 1
    for s in shape:
        n *= s
    return n * jnp.dtype(dtype).itemsize


def _rmsnorm_kernel(x_ref, g_ref, o_ref):
    x = x_ref[...].astype(F32)
    ms = jnp.mean(x * x, axis=-1, keepdims=True)
    o_ref[...] = (x * lax.rsqrt(ms + EPS) * g_ref[...]).astype(o_ref.dtype)


def rmsnorm(x, g, *, width=None, col_block=0, tm=256, out_dtype=BF16):
    T = x.shape[0]
    width = x.shape[1] if width is None else width
    tm = min(tm, T)
    return pl.pallas_call(
        _rmsnorm_kernel,
        grid=(T // tm,),
        in_specs=[pl.BlockSpec((tm, width), lambda i: (i, col_block)),
                  pl.BlockSpec((1, width), lambda i: (0, 0))],
        out_specs=pl.BlockSpec((tm, width), lambda i: (i, 0)),
        out_shape=jax.ShapeDtypeStruct((T, width), out_dtype),
        compiler_params=_params(("parallel",), 6 * _nbytes((tm, width), F32)),
        name="rmsnorm",
    )(x, g.reshape(1, width).astype(F32))


def _mm_kernel(*refs, nk, n_x, has_res):
    x_refs = refs[:n_x]
    w_ref = refs[n_x]
    rest = refs[n_x + 1:]
    res_ref = rest[0] if has_res else None
    o_ref = rest[1] if has_res else rest[0]
    acc_ref = rest[-1] if nk > 1 else None
    k = pl.program_id(2)
    w = w_ref[...].astype(BF16)

    def finish(part):
        out = part
        if has_res:
            out = out + res_ref[...]
        o_ref[...] = out.astype(o_ref.dtype)

    if n_x == 2:
        @pl.when(k == 0)
        def _():
            acc_ref[...] = jnp.dot(x_refs[0][...], w, preferred_element_type=F32)

        @pl.when(k == 1)
        def _():
            finish(acc_ref[...] + jnp.dot(x_refs[1][...], w, preferred_element_type=F32))
    elif nk == 1:
        finish(jnp.dot(x_refs[0][...], w, preferred_element_type=F32))
    else:
        part = jnp.dot(x_refs[0][...], w, preferred_element_type=F32)

        @pl.when(k == 0)
        def _():
            acc_ref[...] = part

        @pl.when((k > 0) & (k < nk - 1))
        def _():
            acc_ref[...] += part

        @pl.when(k == nk - 1)
        def _():
            finish(acc_ref[...] + part)


def matmul(xs, w, *, n_out=None, res=None, out_dtype=F32, tm=1024, tn=512, tk=None, name="matmul"):
    if not isinstance(xs, (tuple, list)):
        xs = (xs,)
    n_x = len(xs)
    T = xs[0].shape[0]
    K = sum(x.shape[1] for x in xs)
    n_out = w.shape[1] if n_out is None else n_out
    tm = min(tm, T)
    tn = min(tn, n_out)
    if n_x == 2:
        tk = xs[0].shape[1]
        assert xs[1].shape[1] == tk
    tk = K if tk is None else tk
    nk = K // tk
    assert K == nk * tk and T % tm == 0 and n_out % tn == 0
    if n_x == 2:
        x_specs = [pl.BlockSpec((tm, tk), lambda i, j, k: (i, 0)) for _ in xs]
    else:
        x_specs = [pl.BlockSpec((tm, tk), lambda i, j, k: (i, k))]
    in_specs = x_specs + [pl.BlockSpec((tk, tn), lambda i, j, k: (k, j))]
    args = list(xs) + [w]
    if res is not None:
        in_specs.append(pl.BlockSpec((tm, tn), lambda i, j, k: (i, j)))
        args.append(res)
    scratch = [pltpu.VMEM((tm, tn), F32)] if nk > 1 else []
    vmem = (2 * n_x * _nbytes((tm, tk), xs[0].dtype) + 2 * _nbytes((tk, tn), F32) + _nbytes((tk, tn), BF16)
            + (2 * _nbytes((tm, tn), F32) if res is not None else 0)
            + 2 * _nbytes((tm, tn), out_dtype) + 2 * _nbytes((tm, tn), F32))
    return pl.pallas_call(
        functools.partial(_mm_kernel, nk=nk, n_x=n_x, has_res=res is not None),
        grid=(T // tm, n_out // tn, nk),
        in_specs=in_specs,
        out_specs=pl.BlockSpec((tm, tn), lambda i, j, k: (i, j)),
        out_shape=jax.ShapeDtypeStruct((T, n_out), out_dtype),
        scratch_shapes=scratch,
        compiler_params=_params(("parallel", "parallel", "arbitrary"), vmem),
        name=name,
    )(*args)


def _rope_tables(positions, dim):
    half = dim // 2
    inv_freq = ROPE_THETA ** (-jnp.arange(0, dim, 2, dtype=F32) / dim)
    ang = positions.astype(F32)[..., None] * inv_freq
    cos, sin = jnp.cos(ang), jnp.sin(ang)
    B, S = positions.shape
    ones = jnp.ones((B, S, LANES - dim), F32)
    zeros = jnp.zeros((B, S, LANES - dim), F32)
    zh = jnp.zeros((B, S, half), F32)
    c = jnp.concatenate([cos, cos, ones], axis=-1)
    sa = jnp.concatenate([-sin, zh, zeros], axis=-1)
    sb = jnp.concatenate([zh, sin, zeros], axis=-1)
    return [t.reshape(B * S, LANES) for t in (c, sa, sb)]


def _rope_lanes(y, c, sa, sb, half):
    return y * c + pltpu.roll(y, LANES - half, 1) * sa + pltpu.roll(y, half, 1) * sb


def _mixer_a_kernel(q_ref, k_ref, v_ref, c_ref, sa_ref, sb_ref, gq_ref, gk_ref, o_ref,
                    qs, ks, *state, seq):
    scale = HEAD_DIM ** -0.5
    c, sa, sb = c_ref[...], sa_ref[...], sb_ref[...]

    def prep(x, g):
        y = x * lax.rsqrt(jnp.mean(x * x, axis=-1, keepdims=True) + EPS) * g
        return _rope_lanes(y, c, sa, sb, ROT_DIM // 2)

    qs[...] = prep(q_ref[...], gq_ref[...])
    ks[...] = prep(k_ref[...], gk_ref[...])
    qi = lax.broadcasted_iota(jnp.int32, (BLOCK, 1), 0)
    kj = lax.broadcasted_iota(jnp.int32, (1, BLOCK), 1)
    trans_b = (((1,), (1,)), ((), ()))
    ones = jnp.ones((BLOCK, HEAD_DIM), BF16)

    for bi, (window, dil) in enumerate(DILATED_PAIRS):
        acc, m_sc, l_sc = state[3 * bi:3 * bi + 3]
        n_back = window // dil
        nb = seq // dil // BLOCK
        assert n_back <= BLOCK and nb * dil * BLOCK == seq
        cur_ok = (qi >= kj) & (qi - kj <= n_back)
        prev_ok = qi + BLOCK - kj <= n_back

        def rows_at(r, n, dil=dil):
            if dil == 1:
                return pl.ds(BLOCK * n, BLOCK)
            return pl.ds(r + dil * BLOCK * n, BLOCK, stride=dil)

        blocks = [(rows_at(r, n), rows_at(r, n - 1) if n > 0 else None) for r in range(dil) for n in range(nb)]
        for first in range(0, len(blocks), MIXER_A_GROUP):
            group = blocks[first:first + MIXER_A_GROUP]
            s_cur, s_prev = [], []
            for rows, prev in group:
                qb = qs[rows, :].astype(BF16)
                s_c = lax.dot_general(qb, ks[rows, :].astype(BF16), trans_b, preferred_element_type=F32) * scale
                s_cur.append(jnp.where(cur_ok, s_c, NEG_INF))
                if prev is None:
                    s_prev.append(None)
                else:
                    s_p = lax.dot_general(qb, ks[prev, :].astype(BF16), trans_b,
                                          preferred_element_type=F32) * scale
                    s_prev.append(jnp.where(prev_ok, s_p, NEG_INF))
            stats = []
            for s_c, s_p in zip(s_cur, s_prev):
                m = jnp.max(s_c, axis=-1, keepdims=True)
                if s_p is not None:
                    m = jnp.maximum(m, jnp.max(s_p, axis=-1, keepdims=True))
                p_c = jnp.exp(s_c - m).astype(BF16)
                p_p = None if s_p is None else jnp.exp(s_p - m).astype(BF16)
                stats.append((m, p_c, p_p))
            for (rows, prev), (m, p_c, p_p) in zip(group, stats):
                v_ext = jnp.concatenate([v_ref[rows, :].astype(BF16), ones], axis=1)
                o = jnp.dot(p_c, v_ext, preferred_element_type=F32)
                if p_p is not None:
                    v_ext = jnp.concatenate([v_ref[prev, :].astype(BF16), ones], axis=1)
                    o = o + jnp.dot(p_p, v_ext, preferred_element_type=F32)
                acc[rows, :] = o[:, :HEAD_DIM]
                l_sc[rows, :] = o[:, HEAD_DIM:]
                m_sc[rows, :] = jnp.broadcast_to(m, (BLOCK, HEAD_DIM))

    n_br = len(DILATED_PAIRS)
    m_all = state[1][...]
    for bi in range(1, n_br):
        m_all = jnp.maximum(m_all, state[3 * bi + 1][...])
    num = jnp.zeros(o_ref.shape, F32)
    den = jnp.zeros(o_ref.shape, F32)
    for bi in range(n_br):
        w = jnp.exp(state[3 * bi + 1][...] - m_all)
        num = num + w * state[3 * bi][...]
        den = den + w * state[3 * bi + 2][...]
    o_ref[...] = (num / den).astype(o_ref.dtype)


def mixer_a(proj, tabs, gq, gk, *, batch, seq):
    T = batch * seq
    blk = (seq, HEAD_DIM)
    head = lambda off: pl.BlockSpec(blk, lambda b, h: (b, off + h))
    tab = pl.BlockSpec(blk, lambda b, h: (b, 0))
    gspec = pl.BlockSpec((1, HEAD_DIM), lambda b, h: (0, 0))
    return pl.pallas_call(
        functools.partial(_mixer_a_kernel, seq=seq),
        grid=(batch, A_HEADS),
        in_specs=[head(0), head(A_HEADS), head(2 * A_HEADS), tab, tab, tab, gspec, gspec],
        out_specs=pl.BlockSpec(blk, lambda b, h: (b, h)),
        out_shape=jax.ShapeDtypeStruct((T, A_HEADS * HEAD_DIM), BF16),
        scratch_shapes=[pltpu.VMEM(blk, F32) for _ in range(2 + 3 * len(DILATED_PAIRS))],
        compiler_params=_params(("parallel", "parallel"), 30 * _nbytes(blk, F32)),
        name="mixer_a",
    )(proj, proj, proj, *tabs, gq.reshape(1, HEAD_DIM), gk.reshape(1, HEAD_DIM))


def _mla_prep_kernel(qn_ref, qr_ref, kn_ref, v_ref, kr_ref, c_ref, sa_ref, sb_ref,
                     gqn_ref, gqr_ref, gkn_ref, gkr_ref, qt_ref, kf_ref, vb_ref):
    c, sa, sb = c_ref[...], sa_ref[...], sb_ref[...]
    half = ROPE_DIM // 2
    qn, qr = qn_ref[...], qr_ref[...]
    ss = jnp.sum(qn * qn, axis=-1, keepdims=True) + jnp.sum(qr * qr, axis=-1, keepdims=True)
    inv = lax.rsqrt(ss * (1.0 / QK_DIM) + EPS) * MLA_Q_SCALE
    qf = jnp.concatenate([qn * inv * gqn_ref[...], _rope_lanes(qr * inv * gqr_ref[...], c, sa, sb, half)], axis=1)
    qt_ref[...] = qf.T.astype(qt_ref.dtype)
    kn = kn_ref[...]
    kn = kn * lax.rsqrt(jnp.mean(kn * kn, axis=-1, keepdims=True) + EPS) * gkn_ref[...]
    kr = kr_ref[...]
    kr = kr * lax.rsqrt(jnp.sum(kr * kr, axis=-1, keepdims=True) * (1.0 / ROPE_DIM) + EPS) * gkr_ref[...]
    kf_ref[:, :NOPE_DIM] = kn.astype(kf_ref.dtype)
    kf_ref[:, NOPE_DIM:] = _rope_lanes(kr, c, sa, sb, half).astype(kf_ref.dtype)
    vb_ref[...] = v_ref[...].astype(vb_ref.dtype)


def mla_prep(q_up, kv_up, krp, tabs, gqn, gqr, gkn, gkr, *, batch, seq, tm=512):
    T = q_up.shape[0]
    H = B_HEADS
    per_seq = seq // tm
    blk = lambda f: pl.BlockSpec((tm, LANES), f)
    g = pl.BlockSpec((1, LANES), lambda i, h: (0, 0))
    return pl.pallas_call(
        _mla_prep_kernel,
        grid=(T // tm, H),
        in_specs=[blk(lambda i, h: (i, h)), blk(lambda i, h: (i, H + h)),
                  blk(lambda i, h: (i, 2 * h)), blk(lambda i, h: (i, 2 * h + 1)),
                  blk(lambda i, h: (i, 0)), blk(lambda i, h: (i, 0)), blk(lambda i, h: (i, 0)),
                  blk(lambda i, h: (i, 0)), g, g, g, g],
        out_specs=[pl.BlockSpec((None, None, 2 * LANES, tm), lambda i, h: (i // per_seq, h, 0, i % per_seq)),
                   pl.BlockSpec((tm, 2 * LANES), lambda i, h: (i, h)),
                   pl.BlockSpec((tm, LANES), lambda i, h: (i, h))],
        out_shape=[jax.ShapeDtypeStruct((batch, H, 2 * LANES, seq), BF16),
                   jax.ShapeDtypeStruct((T, H * 2 * LANES), BF16),
                   jax.ShapeDtypeStruct((T, H * LANES), BF16)],
        compiler_params=_params(("parallel", "parallel"), 40 * _nbytes((tm, LANES), F32)),
        name="mla_prep",
    )(q_up, q_up, kv_up, kv_up, krp, *tabs, gqn, gqr, gkn, gkr)


def _causal_attn_kernel(qt_ref, k_ref, v_ref, o_ref, acc, *, tq, hg, dq, dv):
    qi = pl.program_id(2)
    key = lax.broadcasted_iota(jnp.int32, (tq, tq), 0)
    qry = lax.broadcasted_iota(jnp.int32, (tq, tq), 1)
    visible = key <= qry
    ones = jnp.ones((tq, dv), BF16)
    contract_rows = (((0,), (0,)), ((), ()))

    heads = range(hg)

    def scores(j, diagonal):
        rows = pl.ds(pl.multiple_of(j * tq, tq), tq)
        s = [jnp.dot(k_ref[rows, g * dq:(g + 1) * dq], qt_ref[g], preferred_element_type=F32) for g in heads]
        if diagonal:
            s = [jnp.where(visible, sg, NEG_INF) for sg in s]
        return rows, s

    def col_max(j, ms):
        s = scores(j, False)[1]
        return tuple(jnp.maximum(ms[g], jnp.max(s[g], axis=0, keepdims=True)) for g in heads)

    ms = lax.fori_loop(0, qi, col_max, tuple(jnp.max(sg, axis=0, keepdims=True) for sg in scores(qi, True)[1]))

    def weighted(j, diagonal):
        rows, s = scores(j, diagonal)
        p = [jnp.exp2(s[g] - ms[g]).astype(BF16) for g in heads]
        return [lax.dot_general(p[g], jnp.concatenate([v_ref[rows, g * dv:(g + 1) * dv], ones], axis=1),
                                contract_rows, preferred_element_type=F32) for g in heads]

    for g, w in enumerate(weighted(qi, True)):
        acc[g] = w

    def below(j, carry):
        for g, w in enumerate(weighted(j, False)):
            acc[g] += w
        return carry

    lax.fori_loop(0, qi, below, 0)
    for g in heads:
        a = acc[g]
        o_ref[:, g * dv:(g + 1) * dv] = (a[:, :dv] / a[:, dv:]).astype(o_ref.dtype)


def mla_attention(qt, kf, vb, *, batch, seq, tq=256, hg=4):
    T = batch * seq
    nq = seq // tq
    dq = qt.shape[2]
    dv = vb.shape[1] // B_HEADS
    assert B_HEADS % hg == 0
    return pl.pallas_call(
        functools.partial(_causal_attn_kernel, tq=tq, hg=hg, dq=dq, dv=dv),
        grid=(batch, B_HEADS // hg, nq),
        in_specs=[pl.BlockSpec((None, hg, dq, tq), lambda b, h, i: (b, h, 0, i)),
                  pl.BlockSpec((seq, hg * dq), lambda b, h, i: (b, h)),
                  pl.BlockSpec((seq, hg * dv), lambda b, h, i: (b, h))],
        out_specs=pl.BlockSpec((tq, hg * dv), lambda b, h, i: (b * nq + i, h)),
        out_shape=jax.ShapeDtypeStruct((T, B_HEADS * dv), BF16),
        scratch_shapes=[pltpu.VMEM((hg, tq, 2 * dv), F32)],
        compiler_params=_params(("parallel", "parallel", "parallel"), 0),
        name="mla_attention",
    )(qt, kf, vb)


STICK_EXHAUSTED_LOG = -110.0


def _stickbreak_kernel(q_ref, k_ref, v_ref, o_ref, qt_sc, c_sc, acc, *, tq, hg, scale):
    qi = pl.program_id(2)
    c_sc[...] = jnp.zeros(c_sc.shape, F32)
    acc[...] = jnp.zeros(acc.shape, F32)
    for g in range(hg):
        qt_sc[g] = q_ref[:, g * HEAD_DIM:(g + 1) * HEAD_DIM].astype(F32).T.astype(BF16)
    key = lax.broadcasted_iota(jnp.int32, (tq, tq), 0)
    qry = lax.broadcasted_iota(jnp.int32, (tq, tq), 1)
    after = (qry > key).astype(BF16)
    earlier = key < qry
    contract_rows = (((0,), (0,)), ((), ()))

    def chunk(j, diagonal):
        rows = pl.ds(pl.multiple_of(j * tq, tq), tq)
        heads = range(hg)
        cols = [slice(g * HEAD_DIM, (g + 1) * HEAD_DIM) for g in heads]
        z = [jnp.dot(k_ref[rows, cols[g]], qt_sc[g], preferred_element_type=F32) * scale for g in heads]
        log_beta, parts = [], []
        for g in heads:
            softplus = jnp.maximum(z[g], 0.0) + jnp.log(1.0 + jnp.exp(-jnp.abs(z[g])))
            log_keep = -softplus
            if diagonal:
                log_keep = jnp.where(earlier, log_keep, 0.0)
            hi = log_keep.astype(BF16)
            parts += [hi, (log_keep - hi.astype(F32)).astype(BF16)]
            log_beta.append(z[g] - softplus + c_sc[g])
            c_sc[g] += jnp.sum(log_keep, axis=0, keepdims=True)
        within = jnp.dot(after, jnp.concatenate(parts, axis=1), preferred_element_type=F32)
        for g in heads:
            later = within[:, 2 * g * tq:(2 * g + 1) * tq] + within[:, (2 * g + 1) * tq:(2 * g + 2) * tq]
            a = jnp.exp(log_beta[g] + later)
            if diagonal:
                a = jnp.where(earlier, a, 0.0)
            acc[:, cols[g]] += lax.dot_general(a.astype(BF16), v_ref[rows, cols[g]], contract_rows,
                                               preferred_element_type=F32)

    def stick_left():
        return jnp.max(c_sc[...]) > STICK_EXHAUSTED_LOG

    chunk(qi, True)

    def more(state):
        j, left = state
        return (j >= 0) & left

    def step(state):
        j, _ = state
        chunk(j, False)
        return j - 1, stick_left()

    lax.while_loop(more, step, (qi - 1, stick_left()))
    o_ref[...] = acc[...].astype(o_ref.dtype)


def stickbreak_attention(qkv, *, batch, seq, heads, tq=256, hg=4):
    T = batch * seq
    nq = seq // tq
    ng = heads // hg
    assert heads == ng * hg
    blk = hg * HEAD_DIM
    return pl.pallas_call(
        functools.partial(_stickbreak_kernel, tq=tq, hg=hg, scale=HEAD_DIM ** -0.5),
        grid=(batch, ng, nq),
        in_specs=[pl.BlockSpec((tq, blk), lambda b, h, i: (b * nq + i, h)),
                  pl.BlockSpec((seq, blk), lambda b, h, i: (b, ng + h)),
                  pl.BlockSpec((seq, blk), lambda b, h, i: (b, 2 * ng + h))],
        out_specs=pl.BlockSpec((tq, blk), lambda b, h, i: (b * nq + i, h)),
        out_shape=jax.ShapeDtypeStruct((T, heads * HEAD_DIM), BF16),
        scratch_shapes=[pltpu.VMEM((hg, HEAD_DIM, tq), BF16), pltpu.VMEM((hg, 1, tq), F32),
                        pltpu.VMEM((tq, blk), F32)],
        compiler_params=_params(("parallel", "parallel", "parallel"), 0),
        name="stickbreak",
    )(qkv, qkv, qkv)


def _memkv_kernel(kv_ref, g_ref, mk_ref, mv_ref):
    for h in range(X_HEADS):
        cols = slice(h * HEAD_DIM, (h + 1) * HEAD_DIM)
        k = kv_ref[:, cols]
        k = k * lax.rsqrt(jnp.mean(k * k, axis=-1, keepdims=True) + EPS) * g_ref[...]
        mk_ref[:, cols] = k.astype(mk_ref.dtype)
    mv_ref[...] = kv_ref[:, X_HEADS * HEAD_DIM:].astype(mv_ref.dtype)


def memkv_post(kv, g):
    M = kv.shape[0]
    xd = X_HEADS * HEAD_DIM
    return pl.pallas_call(
        _memkv_kernel,
        grid=(1,),
        in_specs=[pl.BlockSpec((M, 2 * xd), lambda i: (0, 0)), pl.BlockSpec((1, HEAD_DIM), lambda i: (0, 0))],
        out_specs=[pl.BlockSpec((M, xd), lambda i: (0, 0)), pl.BlockSpec((M, xd), lambda i: (0, 0))],
        out_shape=[jax.ShapeDtypeStruct((M, xd), BF16), jax.ShapeDtypeStruct((M, xd), BF16)],
        compiler_params=_params(("arbitrary",), 0),
        name="memkv_post",
    )(kv, g.reshape(1, HEAD_DIM))


def _xattn_block_kernel(x_ref, gx_ref, wq_ref, gq_ref, mk_ref, mv_ref, wo_ref, *rest, emit_norm):
    if emit_norm:
        gn_ref, o_ref, hn_ref = rest
    else:
        (o_ref,) = rest
    scale = HEAD_DIM ** -0.5
    x = x_ref[...]
    h = (x * lax.rsqrt(jnp.mean(x * x, axis=-1, keepdims=True) + EPS) * gx_ref[...]).astype(BF16)
    q_all = jnp.dot(h, wq_ref[...], preferred_element_type=F32)
    heads = []
    for hd in range(X_HEADS):
        cols = slice(hd * HEAD_DIM, (hd + 1) * HEAD_DIM)
        q = q_all[:, cols]
        q = (q * lax.rsqrt(jnp.mean(q * q, axis=-1, keepdims=True) + EPS) * gq_ref[...]).astype(BF16)
        s = lax.dot_general(q, mk_ref[:, cols], (((1,), (1,)), ((), ())), preferred_element_type=F32) * scale
        p = jnp.exp(s - jnp.max(s, axis=-1, keepdims=True))
        p = p / jnp.sum(p, axis=-1, keepdims=True)
        heads.append(jnp.dot(p.astype(BF16), mv_ref[:, cols], preferred_element_type=F32).astype(BF16))
    y = x + jnp.dot(jnp.concatenate(heads, axis=1), wo_ref[...], preferred_element_type=F32)
    o_ref[...] = y
    if emit_norm:
        hn_ref[...] = (y * lax.rsqrt(jnp.mean(y * y, axis=-1, keepdims=True) + EPS) * gn_ref[...]).astype(hn_ref.dtype)


def cross_attention_block(x, mk, mv, g_x, w_xq, g_xq, w_xo, g_next=None, *, seq, mem_len, tm=256):
    T, D = x.shape
    xd = X_HEADS * HEAD_DIM
    per_seq = seq // tm
    emit_norm = g_next is not None
    row = lambda w: pl.BlockSpec((1, w), lambda i: (0, 0))
    tok = pl.BlockSpec((tm, D), lambda i: (i, 0))
    in_specs = [tok, row(D), pl.BlockSpec((D, xd), lambda i: (0, 0)), row(HEAD_DIM),
                pl.BlockSpec((mem_len, xd), lambda i: (i // per_seq, 0)),
                pl.BlockSpec((mem_len, xd), lambda i: (i // per_seq, 0)),
                pl.BlockSpec((xd, D), lambda i: (0, 0))]
    args = [x, g_x.reshape(1, D), w_xq.astype(BF16), g_xq.reshape(1, HEAD_DIM), mk, mv, w_xo.astype(BF16)]
    out_specs, out_shape = tok, jax.ShapeDtypeStruct((T, D), F32)
    if emit_norm:
        in_specs.append(row(D))
        args.append(g_next.reshape(1, D))
        out_specs = [tok, tok]
        out_shape = [out_shape, jax.ShapeDtypeStruct((T, D), BF16)]
    vmem = 10 * _nbytes((tm, D), F32) + 4 * _nbytes((D, xd), BF16)
    return pl.pallas_call(
        functools.partial(_xattn_block_kernel, emit_norm=emit_norm),
        grid=(T // tm,),
        in_specs=in_specs,
        out_specs=out_specs,
        out_shape=out_shape,
        compiler_params=_params(("parallel",), vmem),
        name="cross_attention",
    )(*args)


def _gateup_kernel(x_ref, wg_ref, wu_ref, o_ref):
    x = x_ref[...]
    g = jnp.dot(x, wg_ref[...].astype(BF16), preferred_element_type=F32)
    u = jnp.dot(x, wu_ref[...].astype(BF16), preferred_element_type=F32)
    o_ref[...] = (g * jax.nn.sigmoid(g) * u).astype(o_ref.dtype)


def swiglu_gateup(x, wg, wu, *, tm=1024, tf=256):
    T, K = x.shape
    F = wg.shape[1]
    vmem = (2 * _nbytes((tm, K), BF16) + 4 * _nbytes((K, tf), F32) + 2 * _nbytes((K, tf), BF16)
            + 6 * _nbytes((tm, tf), F32))
    return pl.pallas_call(
        _gateup_kernel,
        grid=(T // tm, F // tf),
        in_specs=[pl.BlockSpec((tm, K), lambda i, j: (i, 0)),
                  pl.BlockSpec((K, tf), lambda i, j: (0, j)),
                  pl.BlockSpec((K, tf), lambda i, j: (0, j))],
        out_specs=pl.BlockSpec((tm, tf), lambda i, j: (i, j)),
        out_shape=jax.ShapeDtypeStruct((T, F), BF16),
        compiler_params=_params(("parallel", "parallel"), vmem),
        name="swiglu_gateup",
    )(x, wg, wu)


def _router_kernel(x_ref, g_ref, w_ref, b_ref, idx_ref, gw_ref):
    x = x_ref[...]
    h = x * lax.rsqrt(jnp.mean(x * x, axis=-1, keepdims=True) + EPS) * g_ref[...]
    logits = jnp.dot(h, w_ref[...], preferred_element_type=F32, precision=lax.Precision.HIGHEST) + b_ref[...]
    lane = lax.broadcasted_iota(jnp.int32, logits.shape, 1)
    logits = jnp.where(lane < N_EXPERTS, logits, -jnp.inf)
    m1 = jnp.max(logits, axis=-1, keepdims=True)
    i1 = jnp.min(jnp.where(logits == m1, lane, LANES), axis=-1, keepdims=True)
    rest = jnp.where(lane == i1, -jnp.inf, logits)
    m2 = jnp.max(rest, axis=-1, keepdims=True)
    i2 = jnp.min(jnp.where(rest == m2, lane, LANES), axis=-1, keepdims=True)
    e = jnp.exp(m2 - m1)
    w1 = 1.0 / (1.0 + e)
    w2 = e / (1.0 + e)
    idx_ref[...] = jnp.where(lane == 0, i1, jnp.where(lane == 1, i2, 0))
    gw_ref[...] = jnp.where(lane == 0, w1, jnp.where(lane == 1, w2, 0.0))


def moe_router(x, g, w_router, b_router, *, tm=256):
    T, D = x.shape
    E = w_router.shape[1]
    w = jnp.pad(w_router, ((0, 0), (0, LANES - E)))
    b = jnp.pad(b_router, (0, LANES - E)).reshape(1, LANES)
    idx, gw = pl.pallas_call(
        _router_kernel,
        grid=(T // tm,),
        in_specs=[pl.BlockSpec((tm, D), lambda i: (i, 0)), pl.BlockSpec((1, D), lambda i: (0, 0)),
                  pl.BlockSpec((D, LANES), lambda i: (0, 0)), pl.BlockSpec((1, LANES), lambda i: (0, 0))],
        out_specs=[pl.BlockSpec((tm, LANES), lambda i: (i, 0)), pl.BlockSpec((tm, LANES), lambda i: (i, 0))],
        out_shape=[jax.ShapeDtypeStruct((T, LANES), jnp.int32), jax.ShapeDtypeStruct((T, LANES), F32)],
        compiler_params=_params(("parallel",), 8 * _nbytes((tm, D), F32)),
        name="moe_router",
    )(x, g.reshape(1, D), w, b)
    return idx[:, :TOP_K], gw[:, :TOP_K]


def _moe_dispatch(idx, tm):
    T = idx.shape[0]
    A = T * TOP_K
    e_flat = idx.reshape(A)
    onehot = (e_flat[:, None] == jnp.arange(N_EXPERTS, dtype=jnp.int32)[None, :]).astype(jnp.int32)
    csum = jnp.cumsum(onehot, axis=0)
    pos_in = jnp.sum(csum * onehot, axis=1) - 1
    counts = csum[-1]
    padded = ((counts + tm - 1) // tm) * tm
    gend = jnp.cumsum(padded)
    gstart = gend - padded
    dest = (jnp.sum(onehot * gstart[None, :], axis=1) + pos_in).astype(jnp.int32)
    n_rows = A + N_EXPERTS * tm
    row_tok = jnp.zeros((n_rows,), jnp.int32).at[dest].set(jnp.arange(A, dtype=jnp.int32) // TOP_K)
    nb = n_rows // tm
    n_used = (gend[-1] // tm).astype(jnp.int32)
    blk = jnp.arange(nb, dtype=jnp.int32)
    be = jnp.sum((blk[:, None] * tm >= gend[None, :]).astype(jnp.int32), axis=1)
    be = jnp.minimum(be, N_EXPERTS - 1)
    be = jnp.where(blk < n_used, be, be[jnp.maximum(n_used - 1, 0)]).astype(jnp.int32)
    return dest, row_tok, be, n_used.reshape(1)


def _moe_gather_kernel(tok_ref, nu_ref, x_hbm, g_ref, o_ref, buf, sem, *, rows):
    i = pl.program_id(0)
    n_used = nu_ref[0]

    def row_copy(slot, r, src_row):
        return pltpu.make_async_copy(x_hbm.at[pl.ds(src_row, 1)], buf.at[slot, pl.ds(r, 1)], sem.at[slot])

    def fetch(block, slot):
        def issue(r, c):
            row_copy(slot, r, tok_ref[block * rows + r]).start()
            return c

        lax.fori_loop(0, rows, issue, 0, unroll=8)

    @pl.when(i == 0)
    def _():
        fetch(0, 0)

    @pl.when(i + 1 < n_used)
    def _():
        fetch(i + 1, (i + 1) % 2)

    @pl.when(i < n_used)
    def _():
        slot = i % 2

        def drain(r, c):
            row_copy(slot, r, 0).wait()
            return c

        lax.fori_loop(0, rows, drain, 0, unroll=8)
        x = buf[slot]
        o_ref[...] = (x * lax.rsqrt(jnp.mean(x * x, axis=-1, keepdims=True) + EPS) * g_ref[...]).astype(o_ref.dtype)

    @pl.when(i >= n_used)
    def _():
        o_ref[...] = jnp.zeros(o_ref.shape, o_ref.dtype)


def moe_gather_norm(x, g, row_tok, n_used, *, tm, rows=128):
    T, D = x.shape
    n_rows = row_tok.shape[0]
    per = tm // rows
    return pl.pallas_call(
        functools.partial(_moe_gather_kernel, rows=rows),
        grid_spec=pltpu.PrefetchScalarGridSpec(
            num_scalar_prefetch=2,
            grid=(n_rows // rows,),
            in_specs=[pl.BlockSpec(memory_space=pl.ANY), pl.BlockSpec((1, D), lambda i, tok, nu: (0, 0))],
            out_specs=pl.BlockSpec((rows, D), lambda i, tok, nu: (i, 0)),
            scratch_shapes=[pltpu.VMEM((2, rows, D), F32), pltpu.SemaphoreType.DMA((2,))]),
        out_shape=jax.ShapeDtypeStruct((n_rows, D), BF16),
        compiler_params=_params(("arbitrary",), 10 * _nbytes((rows, D), F32)),
        name="moe_gather",
    )(row_tok, n_used * per, x, g.reshape(1, D))


def _moe_gateup_kernel(be_ref, nu_ref, x_ref, wg_ref, wu_ref, o_ref, wgb, wub):
    i = pl.program_id(1)
    changed = (i == 0) | (be_ref[i] != be_ref[jnp.maximum(i - 1, 0)])

    @pl.when(changed)
    def _():
        wgb[...] = wg_ref[...].astype(BF16)
        wub[...] = wu_ref[...].astype(BF16)

    @pl.when(i < nu_ref[0])
    def _():
        x = x_ref[...]
        g = jnp.dot(x, wgb[...], preferred_element_type=F32)
        u = jnp.dot(x, wub[...], preferred_element_type=F32)
        o_ref[...] = (g * jax.nn.sigmoid(g) * u).astype(o_ref.dtype)

    @pl.when(i >= nu_ref[0])
    def _():
        o_ref[...] = jnp.zeros(o_ref.shape, o_ref.dtype)


def moe_gateup(xs, wg, wu, be, n_used, *, tm, tf=512):
    n_rows, K = xs.shape
    F = wg.shape[2]
    nb = n_rows // tm
    xmap = lambda j, i, be, nu: (jnp.minimum(i, nu[0] - 1), 0)
    wmap = lambda j, i, be, nu: (be[i], 0, j)
    vmem = (2 * _nbytes((tm, K), BF16) + 4 * _nbytes((K, tf), F32) + 2 * _nbytes((K, tf), BF16)
            + 6 * _nbytes((tm, tf), F32))
    return pl.pallas_call(
        _moe_gateup_kernel,
        grid_spec=pltpu.PrefetchScalarGridSpec(
            num_scalar_prefetch=2,
            grid=(F // tf, nb),
            in_specs=[pl.BlockSpec((tm, K), xmap),
                      pl.BlockSpec((None, K, tf), wmap),
                      pl.BlockSpec((None, K, tf), wmap)],
            out_specs=pl.BlockSpec((tm, tf), lambda j, i, be, nu: (i, j)),
            scratch_shapes=[pltpu.VMEM((K, tf), BF16), pltpu.VMEM((K, tf), BF16)]),
        out_shape=jax.ShapeDtypeStruct((n_rows, F), BF16),
        compiler_params=_params(("arbitrary", "arbitrary"), vmem),
        name="moe_gateup",
    )(be, n_used, xs, wg, wu)


def _moe_down_kernel(be_ref, nu_ref, x_ref, w_ref, o_ref, wb):
    i = pl.program_id(1)
    changed = (i == 0) | (be_ref[i] != be_ref[jnp.maximum(i - 1, 0)])

    @pl.when(changed)
    def _():
        wb[...] = w_ref[...].astype(BF16)

    @pl.when(i < nu_ref[0])
    def _():
        o_ref[...] = jnp.dot(x_ref[...], wb[...], preferred_element_type=F32)

    @pl.when(i >= nu_ref[0])
    def _():
        o_ref[...] = jnp.zeros(o_ref.shape, o_ref.dtype)


def moe_down(hm, wd, be, n_used, *, tm, tn=512):
    n_rows, K = hm.shape
    N = wd.shape[2]
    nb = n_rows // tm
    xmap = lambda j, i, be, nu: (jnp.minimum(i, nu[0] - 1), 0)
    vmem = (2 * _nbytes((tm, K), BF16) + 2 * _nbytes((K, tn), F32) + _nbytes((K, tn), BF16)
            + 4 * _nbytes((tm, tn), F32))
    return pl.pallas_call(
        _moe_down_kernel,
        grid_spec=pltpu.PrefetchScalarGridSpec(
            num_scalar_prefetch=2,
            grid=(N // tn, nb),
            in_specs=[pl.BlockSpec((tm, K), xmap),
                      pl.BlockSpec((None, K, tn), lambda j, i, be, nu: (be[i], 0, j))],
            out_specs=pl.BlockSpec((tm, tn), lambda j, i, be, nu: (i, j)),
            scratch_shapes=[pltpu.VMEM((K, tn), BF16)]),
        out_shape=jax.ShapeDtypeStruct((n_rows, N), F32),
        compiler_params=_params(("arbitrary", "arbitrary"), vmem),
        name="moe_down",
    )(be, n_used, hm, wd)


def _moe_combine_kernel(pos_ref, x_ref, gw_ref, ys_hbm, o_ref, buf, sem, *, rows):
    i = pl.program_id(0)
    n_blocks = pl.num_programs(0)

    def row_copy(slot, k, r, src_row):
        return pltpu.make_async_copy(ys_hbm.at[pl.ds(src_row, 1)], buf.at[slot, k, pl.ds(r, 1)], sem.at[slot])

    def fetch(block, slot):
        def issue(r, c):
            a = (block * rows + r) * TOP_K
            for k in range(TOP_K):
                row_copy(slot, k, r, pos_ref[a + k]).start()
            return c

        lax.fori_loop(0, rows, issue, 0, unroll=4)

    @pl.when(i == 0)
    def _():
        fetch(0, 0)

    @pl.when(i + 1 < n_blocks)
    def _():
        fetch(i + 1, (i + 1) % 2)

    slot = i % 2

    def drain(r, c):
        for k in range(TOP_K):
            row_copy(slot, k, r, 0).wait()
        return c

    lax.fori_loop(0, rows, drain, 0, unroll=4)
    out = x_ref[...]
    for k in range(TOP_K):
        out = out + gw_ref[:, k:k + 1] * buf[slot, k]
    o_ref[...] = out


def moe_combine(x, ys, dest, gw, *, rows=128):
    T, D = x.shape
    return pl.pallas_call(
        functools.partial(_moe_combine_kernel, rows=rows),
        grid_spec=pltpu.PrefetchScalarGridSpec(
            num_scalar_prefetch=1,
            grid=(T // rows,),
            in_specs=[pl.BlockSpec((rows, D), lambda i, pos: (i, 0)),
                      pl.BlockSpec((rows, TOP_K), lambda i, pos: (i, 0)),
                      pl.BlockSpec(memory_space=pl.ANY)],
            out_specs=pl.BlockSpec((rows, D), lambda i, pos: (i, 0)),
            scratch_shapes=[pltpu.VMEM((2, TOP_K, rows, D), F32), pltpu.SemaphoreType.DMA((2,))]),
        out_shape=jax.ShapeDtypeStruct((T, D), F32),
        compiler_params=_params(("arbitrary",), 12 * _nbytes((rows, D), F32)),
        name="moe_combine",
    )(dest, x, gw, ys)


def moe_block(x, g, w_router, b_router, w_egate, w_eup, w_edown, *, tm=512):
    idx, gw = moe_router(x, g, w_router, b_router)
    dest, row_tok, be, n_used = _moe_dispatch(idx, tm)
    xs = moe_gather_norm(x, g, row_tok, n_used, tm=tm)
    hm = moe_gateup(xs, w_egate, w_eup, be, n_used, tm=tm)
    ys = moe_down(hm, w_edown, be, n_used, tm=tm)
    return moe_combine(x, ys, dest, gw)


def even_mixer_block(x, tabs_a, tabs_b, w_in, ga_q, ga_k, g_cq, w_uq, g_ckv, w_ukv, gb_q, gb_kn, gb_kr,
                     w_o, g_mix, *, batch, seq):
    a_width = A_HEADS * HEAD_DIM
    main_cols = 3 * a_width + Q_LORA + KV_LORA
    h = rmsnorm(x, g_mix)
    proj = matmul(h, w_in, n_out=main_cols, name="in_proj")
    w_kr = jnp.pad(w_in[:, main_cols:], ((0, 0), (0, LANES - ROPE_DIM)))
    krp = matmul(h, w_kr, name="in_proj_kr")
    o_a = mixer_a(proj, tabs_a, ga_q, ga_k, batch=batch, seq=seq)
    cq = rmsnorm(proj, g_cq, width=Q_LORA, col_block=3 * a_width // Q_LORA)
    ckv = rmsnorm(proj, g_ckv, width=KV_LORA, col_block=(3 * a_width + Q_LORA) // KV_LORA)
    w3 = w_uq.reshape(Q_LORA, B_HEADS, QK_DIM)
    w_uq_p = jnp.concatenate(
        [w3[:, :, :NOPE_DIM].reshape(Q_LORA, B_HEADS * NOPE_DIM),
         jnp.pad(w3[:, :, NOPE_DIM:], ((0, 0), (0, 0), (0, LANES - ROPE_DIM))).reshape(Q_LORA, B_HEADS * LANES)],
        axis=1)
    q_up = matmul(cq, w_uq_p, name="q_up")
    kv_up = matmul(ckv, w_ukv, name="kv_up")
    pad_r = lambda v: jnp.pad(v, (0, LANES - ROPE_DIM)).reshape(1, LANES)
    qt, kf, vb = mla_prep(q_up, kv_up, krp, tabs_b, gb_q[:NOPE_DIM].reshape(1, LANES), pad_r(gb_q[NOPE_DIM:]),
                          gb_kn.reshape(1, LANES), pad_r(gb_kr), batch=batch, seq=seq)
    o_b = mla_attention(qt, kf, vb, batch=batch, seq=seq)
    return matmul((o_a, o_b), w_o, res=x, name="mixer_out")


def kernel(x, mem, positions, g_mem, w_mem_kv, g_mem_k, g_mix, g_x, w_xq, g_xq, w_xo, g_ffn,
           w_in, ga_q, ga_k, g_cq, w_uq, g_ckv, w_ukv, gb_q, gb_kn, gb_kr, w_o_even,
           w_gate, w_up, w_down, w_qkv, w_o_odd, w_router, b_router, w_egate, w_eup, w_edown):
    B, S, D = x.shape
    M = mem.shape[1]
    depth = g_mix.shape[0]
    T = B * S
    xt = x.reshape(T, D)
    tabs_a = _rope_tables(positions, ROT_DIM)
    tabs_b = _rope_tables(positions, ROPE_DIM)
    kv = matmul(rmsnorm(mem.reshape(B * M, D), g_mem), w_mem_kv, tm=B * M, name="mem_kv")
    mk, mv = memkv_post(kv, g_mem_k)
    for layer in range(depth):
        i = layer // 2
        if layer % 2 == 0:
            xt = even_mixer_block(xt, tabs_a, tabs_b, w_in[i], ga_q[i], ga_k[i], g_cq[i], w_uq[i], g_ckv[i],
                                  w_ukv[i], gb_q[i], gb_kn[i], gb_kr[i], w_o_even[i], g_mix[layer],
                                  batch=B, seq=S)
        else:
            h = rmsnorm(xt, g_mix[layer])
            qkv = matmul(h, w_qkv[i], out_dtype=BF16, name="qkv_proj")
            o = stickbreak_attention(qkv, batch=B, seq=S, heads=D // HEAD_DIM)
            xt = matmul(o, w_o_odd[i], res=xt, name="mixer_out")
        xattn = functools.partial(cross_attention_block, xt, mk, mv, g_x[layer], w_xq[layer], g_xq[layer],
                                  w_xo[layer], seq=S, mem_len=M)
        if layer % 2 == 0:
            xt, h = xattn(g_ffn[layer])
            hm = swiglu_gateup(h, w_gate[i], w_up[i])
            xt = matmul(hm, w_down[i], res=xt, tn=1024, tk=2048, name="swiglu_down")
        else:
            xt = moe_block(xattn(), g_ffn[layer], w_router[i], b_router[i], w_egate[i], w_eup[i], w_edown[i])
    return xt.reshape(B, S, D)
```

```python
import functools

import jax
import jax.numpy as jnp
from jax import lax
from jax.experimental import pallas as pl
from jax.experimental.pallas import tpu as pltpu

F32 = jnp.float32
BF16 = jnp.bfloat16

HEAD_DIM = 128
ROT_DIM = HEAD_DIM // 4
ROPE_THETA = 500000.0
BLOCK = 128
NEG_INF = -1e30
EPS = 1e-6
DILATED_PAIRS = ((128, 1), (512, 4), (2048, 16))
A_HEADS = 16
B_HEADS = 16
Q_LORA = 1536
KV_LORA = 512
NOPE_DIM = 128
ROPE_DIM = 64
QK_DIM = NOPE_DIM + ROPE_DIM
X_HEADS = 4
N_EXPERTS = 8
TOP_K = 2
MIXER_A_GROUP = 8
LOG2_E = 1.4426950408889634
MLA_Q_SCALE = QK_DIM ** -0.5 * LOG2_E

LANES = 128
V7X_VMEM_BYTES = 64 * 1024 * 1024
VMEM_CAP = V7X_VMEM_BYTES - 6 * 1024 * 1024


def _params(semantics, vmem_bytes):
    return pltpu.CompilerParams(dimension_semantics=semantics,
                                vmem_limit_bytes=int(min(max(vmem_bytes, 32 * 1024 * 1024), VMEM_CAP)))


def _nbytes(shape, dtype):
    n = 1
    for s in shape:
        n *= s
    return n * jnp.dtype(dtype).itemsize


def _rmsnorm_kernel(x_ref, g_ref, o_ref):
    x = x_ref[...].astype(F32)
    ms = jnp.mean(x * x, axis=-1, keepdims=True)
    o_ref[...] = (x * lax.rsqrt(ms + EPS) * g_ref[...]).astype(o_ref.dtype)


def rmsnorm(x, g, *, width=None, col_block=0, tm=256, out_dtype=BF16):
    T = x.shape[0]
    width = x.shape[1] if width is None else width
    tm = min(tm, T)
    return pl.pallas_call(
        _rmsnorm_kernel,
        grid=(T // tm,),
        in_specs=[pl.BlockSpec((tm, width), lambda i: (i, col_block)),
                  pl.BlockSpec((1, width), lambda i: (0, 0))],
        out_specs=pl.BlockSpec((tm, width), lambda i: (i, 0)),
        out_shape=jax.ShapeDtypeStruct((T, width), out_dtype),
        compiler_params=_params(("parallel",), 6 * _nbytes((tm, width), F32)),
        name="rmsnorm",
    )(x, g.reshape(1, width).astype(F32))


def _mm_kernel(*refs, nk, n_x, has_res):
    x_refs = refs[:n_x]
    w_ref = refs[n_x]
    rest = refs[n_x + 1:]
    res_ref = rest[0] if has_res else None
    o_ref = rest[1] if has_res else rest[0]
    acc_ref = rest[-1] if nk > 1 else None
    k = pl.program_id(2)
    w = w_ref[...].astype(BF16)

    def finish(part):
        out = part
        if has_res:
            out = out + res_ref[...]
        o_ref[...] = out.astype(o_ref.dtype)

    if n_x == 2:
        half = x_refs[0].shape[1]
        finish(jnp.dot(x_refs[0][...], w[:half], preferred_element_type=F32)
               + jnp.dot(x_refs[1][...], w[half:], preferred_element_type=F32))
    elif nk == 1:
        finish(jnp.dot(x_refs[0][...], w, preferred_element_type=F32))
    else:
        part = jnp.dot(x_refs[0][...], w, preferred_element_type=F32)

        @pl.when(k == 0)
        def _():
            acc_ref[...] = part

        @pl.when((k > 0) & (k < nk - 1))
        def _():
            acc_ref[...] += part

        @pl.when(k == nk - 1)
        def _():
            finish(acc_ref[...] + part)


def matmul(xs, w, *, n_out=None, res=None, out_dtype=F32, tm=1024, tn=512, tk=None, name="matmul"):
    if not isinstance(xs, (tuple, list)):
        xs = (xs,)
    n_x = len(xs)
    T = xs[0].shape[0]
    K = sum(x.shape[1] for x in xs)
    n_out = w.shape[1] if n_out is None else n_out
    tm = min(tm, T)
    tn = min(tn, n_out)
    if n_x == 2:
        assert tk is None and xs[1].shape[1] == xs[0].shape[1]
    tk = K if tk is None else tk
    nk = K // tk
    assert K == nk * tk and T % tm == 0 and n_out % tn == 0
    if n_x == 2:
        x_specs = [pl.BlockSpec((tm, K // 2), lambda i, j, k: (i, 0)) for _ in xs]
    else:
        x_specs = [pl.BlockSpec((tm, tk), lambda i, j, k: (i, k))]
    in_specs = x_specs + [pl.BlockSpec((tk, tn), lambda i, j, k: (k, j))]
    args = list(xs) + [w]
    if res is not None:
        in_specs.append(pl.BlockSpec((tm, tn), lambda i, j, k: (i, j)))
        args.append(res)
    scratch = [pltpu.VMEM((tm, tn), F32)] if nk > 1 else []
    vmem = (2 * _nbytes((tm, tk), xs[0].dtype) + 2 * _nbytes((tk, tn), F32) + _nbytes((tk, tn), BF16)
            + (2 * _nbytes((tm, tn), F32) if res is not None else 0)
            + 2 * _nbytes((tm, tn), out_dtype) + 2 * _nbytes((tm, tn), F32))
    return pl.pallas_call(
        functools.partial(_mm_kernel, nk=nk, n_x=n_x, has_res=res is not None),
        grid=(T // tm, n_out // tn, nk),
        in_specs=in_specs,
        out_specs=pl.BlockSpec((tm, tn), lambda i, j, k: (i, j)),
        out_shape=jax.ShapeDtypeStruct((T, n_out), out_dtype),
        scratch_shapes=scratch,
        compiler_params=_params(("parallel", "parallel", "arbitrary"), vmem),
        name=name,
    )(*args)


def _rope_tables(positions, dim):
    half = dim // 2
    inv_freq = ROPE_THETA ** (-jnp.arange(0, dim, 2, dtype=F32) / dim)
    ang = positions.astype(F32)[..., None] * inv_freq
    cos, sin = jnp.cos(ang), jnp.sin(ang)
    B, S = positions.shape
    ones = jnp.ones((B, S, LANES - dim), F32)
    zeros = jnp.zeros((B, S, LANES - dim), F32)
    zh = jnp.zeros((B, S, half), F32)
    c = jnp.concatenate([cos, cos, ones], axis=-1)
    sa = jnp.concatenate([-sin, zh, zeros], axis=-1)
    sb = jnp.concatenate([zh, sin, zeros], axis=-1)
    return [t.reshape(B * S, LANES) for t in (c, sa, sb)]


def _rope_lanes(y, c, sa, sb, half):
    return y * c + pltpu.roll(y, LANES - half, 1) * sa + pltpu.roll(y, half, 1) * sb


def _mixer_a_kernel(q_ref, k_ref, v_ref, c_ref, sa_ref, sb_ref, gq_ref, gk_ref, o_ref,
                    qs, ks, *state, seq):
    scale = HEAD_DIM ** -0.5 * LOG2_E
    c, sa, sb = c_ref[...], sa_ref[...], sb_ref[...]

    def prep(x, g):
        y = x * lax.rsqrt(jnp.mean(x * x, axis=-1, keepdims=True) + EPS) * g
        return _rope_lanes(y, c, sa, sb, ROT_DIM // 2)

    qs[...] = prep(q_ref[...], gq_ref[...])
    ks[...] = prep(k_ref[...], gk_ref[...])
    qi = lax.broadcasted_iota(jnp.int32, (BLOCK, 1), 0)
    kj = lax.broadcasted_iota(jnp.int32, (1, BLOCK), 1)
    trans_b = (((1,), (1,)), ((), ()))
    ones = jnp.ones((BLOCK, HEAD_DIM), BF16)

    for bi, (window, dil) in enumerate(DILATED_PAIRS):
        acc, m_sc, l_sc = state[3 * bi:3 * bi + 3]
        n_back = window // dil
        nb = seq // dil // BLOCK
        assert n_back <= BLOCK and nb * dil * BLOCK == seq
        cur_ok = (qi >= kj) & (qi - kj <= n_back)
        prev_ok = qi + BLOCK - kj <= n_back

        def rows_at(r, n, dil=dil):
            if dil == 1:
                return pl.ds(BLOCK * n, BLOCK)
            return pl.ds(r + dil * BLOCK * n, BLOCK, stride=dil)

        blocks = [(rows_at(r, n), rows_at(r, n - 1) if n > 0 else None) for r in range(dil) for n in range(nb)]
        for first in range(0, len(blocks), MIXER_A_GROUP):
            group = blocks[first:first + MIXER_A_GROUP]
            s_cur, s_prev = [], []
            for rows, prev in group:
                qb = qs[rows, :].astype(BF16)
                s_c = lax.dot_general(qb, ks[rows, :].astype(BF16), trans_b, preferred_element_type=F32) * scale
                s_cur.append(jnp.where(cur_ok, s_c, NEG_INF))
                if prev is None:
                    s_prev.append(None)
                else:
                    s_p = lax.dot_general(qb, ks[prev, :].astype(BF16), trans_b,
                                          preferred_element_type=F32) * scale
                    s_prev.append(jnp.where(prev_ok, s_p, NEG_INF))
            stats = []
            for s_c, s_p in zip(s_cur, s_prev):
                m = jnp.max(s_c, axis=-1, keepdims=True)
                if s_p is not None:
                    m = jnp.maximum(m, jnp.max(s_p, axis=-1, keepdims=True))
                p_c = jnp.exp2(s_c - m).astype(BF16)
                p_p = None if s_p is None else jnp.exp2(s_p - m).astype(BF16)
                stats.append((m, p_c, p_p))
            for (rows, prev), (m, p_c, p_p) in zip(group, stats):
                v_ext = jnp.concatenate([v_ref[rows, :].astype(BF16), ones], axis=1)
                o = jnp.dot(p_c, v_ext, preferred_element_type=F32)
                if p_p is not None:
                    v_ext = jnp.concatenate([v_ref[prev, :].astype(BF16), ones], axis=1)
                    o = o + jnp.dot(p_p, v_ext, preferred_element_type=F32)
                acc[rows, :] = o[:, :HEAD_DIM]
                l_sc[rows, :] = o[:, HEAD_DIM:]
                m_sc[rows, :] = jnp.broadcast_to(m, (BLOCK, HEAD_DIM))

    n_br = len(DILATED_PAIRS)
    m_all = state[1][...]
    for bi in range(1, n_br):
        m_all = jnp.maximum(m_all, state[3 * bi + 1][...])
    num = jnp.zeros(o_ref.shape, F32)
    den = jnp.zeros(o_ref.shape, F32)
    for bi in range(n_br):
        w = jnp.exp2(state[3 * bi + 1][...] - m_all)
        num = num + w * state[3 * bi][...]
        den = den + w * state[3 * bi + 2][...]
    o_ref[...] = (num / den).astype(o_ref.dtype)


def mixer_a(proj, tabs, gq, gk, *, batch, seq):
    T = batch * seq
    blk = (seq, HEAD_DIM)
    head = lambda off: pl.BlockSpec(blk, lambda b, h: (b, off + h))
    tab = pl.BlockSpec(blk, lambda b, h: (b, 0))
    gspec = pl.BlockSpec((1, HEAD_DIM), lambda b, h: (0, 0))
    return pl.pallas_call(
        functools.partial(_mixer_a_kernel, seq=seq),
        grid=(batch, A_HEADS),
        in_specs=[head(0), head(A_HEADS), head(2 * A_HEADS), tab, tab, tab, gspec, gspec],
        out_specs=pl.BlockSpec(blk, lambda b, h: (b, h)),
        out_shape=jax.ShapeDtypeStruct((T, A_HEADS * HEAD_DIM), BF16),
        scratch_shapes=[pltpu.VMEM(blk, F32) for _ in range(2 + 3 * len(DILATED_PAIRS))],
        compiler_params=_params(("parallel", "parallel"), 30 * _nbytes(blk, F32)),
        name="mixer_a",
    )(proj, proj, proj, *tabs, gq.reshape(1, HEAD_DIM), gk.reshape(1, HEAD_DIM))


def _mla_prep_kernel(qn_ref, qr_ref, kn_ref, v_ref, kr_ref, c_ref, sa_ref, sb_ref,
                     gqn_ref, gqr_ref, gkn_ref, gkr_ref, qt_ref, kf_ref, vb_ref):
    c, sa, sb = c_ref[...], sa_ref[...], sb_ref[...]
    half = ROPE_DIM // 2
    qn, qr = qn_ref[...], qr_ref[...]
    ss = jnp.sum(qn * qn, axis=-1, keepdims=True) + jnp.sum(qr * qr, axis=-1, keepdims=True)
    inv = lax.rsqrt(ss * (1.0 / QK_DIM) + EPS) * MLA_Q_SCALE
    qf = jnp.concatenate([qn * inv * gqn_ref[...], _rope_lanes(qr * inv * gqr_ref[...], c, sa, sb, half)], axis=1)
    qt_ref[...] = qf.T.astype(qt_ref.dtype)
    kn = kn_ref[...]
    kn = kn * lax.rsqrt(jnp.mean(kn * kn, axis=-1, keepdims=True) + EPS) * gkn_ref[...]
    kr = kr_ref[...]
    kr = kr * lax.rsqrt(jnp.sum(kr * kr, axis=-1, keepdims=True) * (1.0 / ROPE_DIM) + EPS) * gkr_ref[...]
    kf_ref[:, :NOPE_DIM] = kn.astype(kf_ref.dtype)
    kf_ref[:, NOPE_DIM:] = _rope_lanes(kr, c, sa, sb, half).astype(kf_ref.dtype)
    vb_ref[...] = v_ref[...].astype(vb_ref.dtype)


def mla_prep(q_up, kv_up, krp, tabs, gqn, gqr, gkn, gkr, *, batch, seq, tm=512):
    T = q_up.shape[0]
    H = B_HEADS
    per_seq = seq // tm
    blk = lambda f: pl.BlockSpec((tm, LANES), f)
    g = pl.BlockSpec((1, LANES), lambda i, h: (0, 0))
    return pl.pallas_call(
        _mla_prep_kernel,
        grid=(T // tm, H),
        in_specs=[blk(lambda i, h: (i, h)), blk(lambda i, h: (i, H + h)),
                  blk(lambda i, h: (i, 2 * h)), blk(lambda i, h: (i, 2 * h + 1)),
                  blk(lambda i, h: (i, 0)), blk(lambda i, h: (i, 0)), blk(lambda i, h: (i, 0)),
                  blk(lambda i, h: (i, 0)), g, g, g, g],
        out_specs=[pl.BlockSpec((None, None, 2 * LANES, tm), lambda i, h: (i // per_seq, h, 0, i % per_seq)),
                   pl.BlockSpec((tm, 2 * LANES), lambda i, h: (i, h)),
                   pl.BlockSpec((tm, LANES), lambda i, h: (i, h))],
        out_shape=[jax.ShapeDtypeStruct((batch, H, 2 * LANES, seq), BF16),
                   jax.ShapeDtypeStruct((T, H * 2 * LANES), BF16),
                   jax.ShapeDtypeStruct((T, H * LANES), BF16)],
        compiler_params=_params(("parallel", "parallel"), 40 * _nbytes((tm, LANES), F32)),
        name="mla_prep",
    )(q_up, q_up, kv_up, kv_up, krp, *tabs, gqn, gqr, gkn, gkr)


def _causal_attn_kernel(qt_ref, k_ref, v_ref, o_ref, s_sc, acc, *, tq, hg, dq, dv):
    qi = pl.program_id(2)
    key = lax.broadcasted_iota(jnp.int32, (tq, tq), 0)
    qry = lax.broadcasted_iota(jnp.int32, (tq, tq), 1)
    visible = key <= qry
    ones = jnp.ones((tq, dv), BF16)
    contract_rows = (((0,), (0,)), ((), ()))

    heads = range(hg)

    def scores(j, diagonal):
        rows = pl.ds(pl.multiple_of(j * tq, tq), tq)
        s = [jnp.dot(k_ref[rows, g * dq:(g + 1) * dq], qt_ref[g], preferred_element_type=F32) for g in heads]
        if diagonal:
            s = [jnp.where(visible, sg, NEG_INF) for sg in s]
        for g in heads:
            s_sc[g, j] = s[g]
        return s

    def col_max(j, ms):
        s = scores(j, False)
        return tuple(jnp.maximum(ms[g], jnp.max(s[g], axis=0, keepdims=True)) for g in heads)

    ms = lax.fori_loop(0, qi, col_max, tuple(jnp.max(sg, axis=0, keepdims=True) for sg in scores(qi, True)))

    def weighted(j):
        rows = pl.ds(pl.multiple_of(j * tq, tq), tq)
        p = [jnp.exp2(s_sc[g, j] - ms[g]).astype(BF16) for g in heads]
        return [lax.dot_general(p[g], jnp.concatenate([v_ref[rows, g * dv:(g + 1) * dv], ones], axis=1),
                                contract_rows, preferred_element_type=F32) for g in heads]

    for g, w in enumerate(weighted(qi)):
        acc[g] = w

    def below(j, carry):
        for g, w in enumerate(weighted(j)):
            acc[g] += w
        return carry

    lax.fori_loop(0, qi, below, 0)
    for g in heads:
        a = acc[g]
        o_ref[:, g * dv:(g + 1) * dv] = (a[:, :dv] / a[:, dv:]).astype(o_ref.dtype)


def mla_attention(qt, kf, vb, *, batch, seq, tq=512, hg=4):
    T = batch * seq
    nq = seq // tq
    dq = qt.shape[2]
    dv = vb.shape[1] // B_HEADS
    assert B_HEADS % hg == 0
    return pl.pallas_call(
        functools.partial(_causal_attn_kernel, tq=tq, hg=hg, dq=dq, dv=dv),
        grid=(batch, B_HEADS // hg, nq),
        in_specs=[pl.BlockSpec((None, hg, dq, tq), lambda b, h, i: (b, h, 0, i)),
                  pl.BlockSpec((seq, hg * dq), lambda b, h, i: (b, h)),
                  pl.BlockSpec((seq, hg * dv), lambda b, h, i: (b, h))],
        out_specs=pl.BlockSpec((tq, hg * dv), lambda b, h, i: (b * nq + i, h)),
        out_shape=jax.ShapeDtypeStruct((T, B_HEADS * dv), BF16),
        scratch_shapes=[pltpu.VMEM((hg, nq, tq, tq), F32), pltpu.VMEM((hg, tq, 2 * dv), F32)],
        compiler_params=_params(("parallel", "parallel", "parallel"),
                                _nbytes((hg, nq, tq, tq), F32) + 2 * _nbytes((seq, hg * (dq + dv)), BF16)
                                + 8 * hg * _nbytes((tq, tq), F32)),
        name="mla_attention",
    )(qt, kf, vb)


STICK_EXHAUSTED_LOG = -110.0


def _stickbreak_kernel(q_ref, k_ref, v_ref, o_ref, qt_sc, c_sc, acc, *, tq, hg, scale):
    qi = pl.program_id(2)
    c_sc[...] = jnp.zeros(c_sc.shape, F32)
    acc[...] = jnp.zeros(acc.shape, F32)
    for g in range(hg):
        qt_sc[g] = q_ref[:, g * HEAD_DIM:(g + 1) * HEAD_DIM].astype(F32).T.astype(BF16)
    key = lax.broadcasted_iota(jnp.int32, (tq, tq), 0)
    qry = lax.broadcasted_iota(jnp.int32, (tq, tq), 1)
    after = (qry > key).astype(BF16)
    earlier = key < qry
    contract_rows = (((0,), (0,)), ((), ()))

    def chunk(j, diagonal):
        rows = pl.ds(pl.multiple_of(j * tq, tq), tq)
        heads = range(hg)
        cols = [slice(g * HEAD_DIM, (g + 1) * HEAD_DIM) for g in heads]
        z = [jnp.dot(k_ref[rows, cols[g]], qt_sc[g], preferred_element_type=F32) * (scale * LOG2_E) for g in heads]
        log_beta, parts = [], []
        for g in heads:
            softplus = jnp.maximum(z[g], 0.0) + jnp.log2(1.0 + jnp.exp2(-jnp.abs(z[g])))
            log_keep = -softplus
            if diagonal:
                log_keep = jnp.where(earlier, log_keep, 0.0)
            hi = log_keep.astype(BF16)
            parts += [hi, (log_keep - hi.astype(F32)).astype(BF16)]
            log_beta.append(z[g] - softplus + c_sc[g])
            c_sc[g] += jnp.sum(log_keep, axis=0, keepdims=True)
        within = jnp.dot(after, jnp.concatenate(parts, axis=1), preferred_element_type=F32)
        for g in heads:
            later = within[:, 2 * g * tq:(2 * g + 1) * tq] + within[:, (2 * g + 1) * tq:(2 * g + 2) * tq]
            a = jnp.exp2(log_beta[g] + later)
            if diagonal:
                a = jnp.where(earlier, a, 0.0)
            acc[:, cols[g]] += lax.dot_general(a.astype(BF16), v_ref[rows, cols[g]], contract_rows,
                                               preferred_element_type=F32)

    def stick_left():
        return jnp.max(c_sc[...]) > STICK_EXHAUSTED_LOG * LOG2_E

    chunk(qi, True)

    def more(state):
        j, left = state
        return (j >= 0) & left

    def step(state):
        j, _ = state
        chunk(j, False)
        return j - 1, stick_left()

    lax.while_loop(more, step, (qi - 1, stick_left()))
    o_ref[...] = acc[...].astype(o_ref.dtype)


def stickbreak_attention(qkv, *, batch, seq, heads, tq=256, hg=4):
    T = batch * seq
    nq = seq // tq
    ng = heads // hg
    assert heads == ng * hg
    blk = hg * HEAD_DIM
    return pl.pallas_call(
        functools.partial(_stickbreak_kernel, tq=tq, hg=hg, scale=HEAD_DIM ** -0.5),
        grid=(batch, ng, nq),
        in_specs=[pl.BlockSpec((tq, blk), lambda b, h, i: (b * nq + i, h)),
                  pl.BlockSpec((seq, blk), lambda b, h, i: (b, ng + h)),
                  pl.BlockSpec((seq, blk), lambda b, h, i: (b, 2 * ng + h))],
        out_specs=pl.BlockSpec((tq, blk), lambda b, h, i: (b * nq + i, h)),
        out_shape=jax.ShapeDtypeStruct((T, heads * HEAD_DIM), BF16),
        scratch_shapes=[pltpu.VMEM((hg, HEAD_DIM, tq), BF16), pltpu.VMEM((hg, 1, tq), F32),
                        pltpu.VMEM((tq, blk), F32)],
        compiler_params=_params(("parallel", "parallel", "parallel"), 0),
        name="stickbreak",
    )(qkv, qkv, qkv)


def _memkv_kernel(kv_ref, g_ref, mk_ref, mv_ref):
    for h in range(X_HEADS):
        cols = slice(h * HEAD_DIM, (h + 1) * HEAD_DIM)
        k = kv_ref[:, cols]
        k = k * lax.rsqrt(jnp.mean(k * k, axis=-1, keepdims=True) + EPS) * g_ref[...]
        mk_ref[:, cols] = k.astype(mk_ref.dtype)
    mv_ref[...] = kv_ref[:, X_HEADS * HEAD_DIM:].astype(mv_ref.dtype)


def memkv_post(kv, g):
    M = kv.shape[0]
    xd = X_HEADS * HEAD_DIM
    return pl.pallas_call(
        _memkv_kernel,
        grid=(1,),
        in_specs=[pl.BlockSpec((M, 2 * xd), lambda i: (0, 0)), pl.BlockSpec((1, HEAD_DIM), lambda i: (0, 0))],
        out_specs=[pl.BlockSpec((M, xd), lambda i: (0, 0)), pl.BlockSpec((M, xd), lambda i: (0, 0))],
        out_shape=[jax.ShapeDtypeStruct((M, xd), BF16), jax.ShapeDtypeStruct((M, xd), BF16)],
        compiler_params=_params(("arbitrary",), 0),
        name="memkv_post",
    )(kv, g.reshape(1, HEAD_DIM))


def _xattn_block_kernel(x_ref, gx_ref, wq_ref, gq_ref, mk_ref, mv_ref, wo_ref, *rest, emit_norm):
    if emit_norm:
        gn_ref, o_ref, hn_ref = rest
    else:
        (o_ref,) = rest
    scale = HEAD_DIM ** -0.5
    x = x_ref[...]
    h = (x * lax.rsqrt(jnp.mean(x * x, axis=-1, keepdims=True) + EPS) * gx_ref[...]).astype(BF16)
    q_all = jnp.dot(h, wq_ref[...], preferred_element_type=F32)
    heads = []
    for hd in range(X_HEADS):
        cols = slice(hd * HEAD_DIM, (hd + 1) * HEAD_DIM)
        q = q_all[:, cols]
        q = (q * lax.rsqrt(jnp.mean(q * q, axis=-1, keepdims=True) + EPS) * gq_ref[...]).astype(BF16)
        s = lax.dot_general(q, mk_ref[:, cols], (((1,), (1,)), ((), ())), preferred_element_type=F32) * scale
        p = jnp.exp(s - jnp.max(s, axis=-1, keepdims=True))
        p = p / jnp.sum(p, axis=-1, keepdims=True)
        heads.append(jnp.dot(p.astype(BF16), mv_ref[:, cols], preferred_element_type=F32).astype(BF16))
    y = x + jnp.dot(jnp.concatenate(heads, axis=1), wo_ref[...], preferred_element_type=F32)
    o_ref[...] = y
    if emit_norm:
        hn_ref[...] = (y * lax.rsqrt(jnp.mean(y * y, axis=-1, keepdims=True) + EPS) * gn_ref[...]).astype(hn_ref.dtype)


def cross_attention_block(x, mk, mv, g_x, w_xq, g_xq, w_xo, g_next=None, *, seq, mem_len, tm=256):
    T, D = x.shape
    xd = X_HEADS * HEAD_DIM
    per_seq = seq // tm
    emit_norm = g_next is not None
    row = lambda w: pl.BlockSpec((1, w), lambda i: (0, 0))
    tok = pl.BlockSpec((tm, D), lambda i: (i, 0))
    in_specs = [tok, row(D), pl.BlockSpec((D, xd), lambda i: (0, 0)), row(HEAD_DIM),
                pl.BlockSpec((mem_len, xd), lambda i: (i // per_seq, 0)),
                pl.BlockSpec((mem_len, xd), lambda i: (i // per_seq, 0)),
                pl.BlockSpec((xd, D), lambda i: (0, 0))]
    args = [x, g_x.reshape(1, D), w_xq.astype(BF16), g_xq.reshape(1, HEAD_DIM), mk, mv, w_xo.astype(BF16)]
    out_specs, out_shape = tok, jax.ShapeDtypeStruct((T, D), F32)
    if emit_norm:
        in_specs.append(row(D))
        args.append(g_next.reshape(1, D))
        out_specs = [tok, tok]
        out_shape = [out_shape, jax.ShapeDtypeStruct((T, D), BF16)]
    vmem = 10 * _nbytes((tm, D), F32) + 4 * _nbytes((D, xd), BF16)
    return pl.pallas_call(
        functools.partial(_xattn_block_kernel, emit_norm=emit_norm),
        grid=(T // tm,),
        in_specs=in_specs,
        out_specs=out_specs,
        out_shape=out_shape,
        compiler_params=_params(("parallel",), vmem),
        name="cross_attention",
    )(*args)


def _gateup_kernel(x_ref, wg_ref, wu_ref, o_ref):
    x = x_ref[...]
    g = jnp.dot(x, wg_ref[...].astype(BF16), preferred_element_type=F32)
    u = jnp.dot(x, wu_ref[...].astype(BF16), preferred_element_type=F32)
    o_ref[...] = (g * jax.nn.sigmoid(g) * u).astype(o_ref.dtype)


def swiglu_gateup(x, wg, wu, *, tm=1024, tf=512):
    T, K = x.shape
    F = wg.shape[1]
    vmem = (_nbytes((tm, K), BF16) + 4 * _nbytes((K, tf), F32) + 2 * _nbytes((K, tf), BF16)
            + 6 * _nbytes((tm, tf), F32))
    return pl.pallas_call(
        _gateup_kernel,
        grid=(T // tm, F // tf),
        in_specs=[pl.BlockSpec((tm, K), lambda i, j: (i, 0), pipeline_mode=pl.Buffered(1)),
                  pl.BlockSpec((K, tf), lambda i, j: (0, j)),
                  pl.BlockSpec((K, tf), lambda i, j: (0, j))],
        out_specs=pl.BlockSpec((tm, tf), lambda i, j: (i, j)),
        out_shape=jax.ShapeDtypeStruct((T, F), BF16),
        compiler_params=_params(("parallel", "parallel"), vmem),
        name="swiglu_gateup",
    )(x, wg, wu)


def _router_kernel(x_ref, g_ref, w_ref, b_ref, idx_ref, gw_ref):
    x = x_ref[...]
    h = x * lax.rsqrt(jnp.mean(x * x, axis=-1, keepdims=True) + EPS) * g_ref[...]
    logits = jnp.dot(h, w_ref[...], preferred_element_type=F32, precision=lax.Precision.HIGHEST) + b_ref[...]
    lane = lax.broadcasted_iota(jnp.int32, logits.shape, 1)
    logits = jnp.where(lane < N_EXPERTS, logits, -jnp.inf)
    m1 = jnp.max(logits, axis=-1, keepdims=True)
    i1 = jnp.min(jnp.where(logits == m1, lane, LANES), axis=-1, keepdims=True)
    rest = jnp.where(lane == i1, -jnp.inf, logits)
    m2 = jnp.max(rest, axis=-1, keepdims=True)
    i2 = jnp.min(jnp.where(rest == m2, lane, LANES), axis=-1, keepdims=True)
    e = jnp.exp(m2 - m1)
    w1 = 1.0 / (1.0 + e)
    w2 = e / (1.0 + e)
    idx_ref[...] = jnp.where(lane == 0, i1, jnp.where(lane == 1, i2, 0))
    gw_ref[...] = jnp.where(lane == 0, w1, jnp.where(lane == 1, w2, 0.0))


def moe_router(x, g, w_router, b_router, *, tm=256):
    T, D = x.shape
    E = w_router.shape[1]
    w = jnp.pad(w_router, ((0, 0), (0, LANES - E)))
    b = jnp.pad(b_router, (0, LANES - E)).reshape(1, LANES)
    idx, gw = pl.pallas_call(
        _router_kernel,
        grid=(T // tm,),
        in_specs=[pl.BlockSpec((tm, D), lambda i: (i, 0)), pl.BlockSpec((1, D), lambda i: (0, 0)),
                  pl.BlockSpec((D, LANES), lambda i: (0, 0)), pl.BlockSpec((1, LANES), lambda i: (0, 0))],
        out_specs=[pl.BlockSpec((tm, LANES), lambda i: (i, 0)), pl.BlockSpec((tm, LANES), lambda i: (i, 0))],
        out_shape=[jax.ShapeDtypeStruct((T, LANES), jnp.int32), jax.ShapeDtypeStruct((T, LANES), F32)],
        compiler_params=_params(("parallel",), 8 * _nbytes((tm, D), F32)),
        name="moe_router",
    )(x, g.reshape(1, D), w, b)
    return idx[:, :TOP_K], gw[:, :TOP_K]


def _moe_dispatch(idx, tm):
    T = idx.shape[0]
    A = T * TOP_K
    e_flat = idx.reshape(A)
    onehot = (e_flat[:, None] == jnp.arange(N_EXPERTS, dtype=jnp.int32)[None, :]).astype(jnp.int32)
    csum = jnp.cumsum(onehot, axis=0)
    pos_in = jnp.sum(csum * onehot, axis=1) - 1
    counts = csum[-1]
    padded = ((counts + tm - 1) // tm) * tm
    gend = jnp.cumsum(padded)
    gstart = gend - padded
    dest = (jnp.sum(onehot * gstart[None, :], axis=1) + pos_in).astype(jnp.int32)
    n_rows = A + N_EXPERTS * tm
    row_tok = jnp.zeros((n_rows,), jnp.int32).at[dest].set(jnp.arange(A, dtype=jnp.int32) // TOP_K)
    nb = n_rows // tm
    n_used = (gend[-1] // tm).astype(jnp.int32)
    blk = jnp.arange(nb, dtype=jnp.int32)
    be = jnp.sum((blk[:, None] * tm >= gend[None, :]).astype(jnp.int32), axis=1)
    be = jnp.minimum(be, N_EXPERTS - 1)
    be = jnp.where(blk < n_used, be, be[jnp.maximum(n_used - 1, 0)]).astype(jnp.int32)
    return dest, row_tok, be, n_used.reshape(1)


def _moe_gather_kernel(tok_ref, nu_ref, x_hbm, g_ref, o_ref, buf, sem, *, rows):
    i = pl.program_id(0)
    n_used = nu_ref[0]

    def row_copy(slot, r, src_row):
        return pltpu.make_async_copy(x_hbm.at[pl.ds(src_row, 1)], buf.at[slot, pl.ds(r, 1)], sem.at[slot])

    def fetch(block, slot):
        def issue(r, c):
            row_copy(slot, r, tok_ref[block * rows + r]).start()
            return c

        lax.fori_loop(0, rows, issue, 0, unroll=8)

    @pl.when(i == 0)
    def _():
        fetch(0, 0)

    @pl.when(i + 1 < n_used)
    def _():
        fetch(i + 1, (i + 1) % 2)

    @pl.when(i < n_used)
    def _():
        slot = i % 2

        def drain(r, c):
            row_copy(slot, r, 0).wait()
            return c

        lax.fori_loop(0, rows, drain, 0, unroll=8)
        x = buf[slot]
        o_ref[...] = (x * lax.rsqrt(jnp.mean(x * x, axis=-1, keepdims=True) + EPS) * g_ref[...]).astype(o_ref.dtype)

    @pl.when(i >= n_used)
    def _():
        o_ref[...] = jnp.zeros(o_ref.shape, o_ref.dtype)


def moe_gather_norm(x, g, row_tok, n_used, *, tm, rows=128):
    T, D = x.shape
    n_rows = row_tok.shape[0]
    per = tm // rows
    return pl.pallas_call(
        functools.partial(_moe_gather_kernel, rows=rows),
        grid_spec=pltpu.PrefetchScalarGridSpec(
            num_scalar_prefetch=2,
            grid=(n_rows // rows,),
            in_specs=[pl.BlockSpec(memory_space=pl.ANY), pl.BlockSpec((1, D), lambda i, tok, nu: (0, 0))],
            out_specs=pl.BlockSpec((rows, D), lambda i, tok, nu: (i, 0)),
            scratch_shapes=[pltpu.VMEM((2, rows, D), F32), pltpu.SemaphoreType.DMA((2,))]),
        out_shape=jax.ShapeDtypeStruct((n_rows, D), BF16),
        compiler_params=_params(("arbitrary",), 10 * _nbytes((rows, D), F32)),
        name="moe_gather",
    )(row_tok, n_used * per, x, g.reshape(1, D))


def _moe_gateup_kernel(be_ref, nu_ref, x_ref, wg_ref, wu_ref, o_ref, wgb, wub):
    i = pl.program_id(1)
    changed = (i == 0) | (be_ref[i] != be_ref[jnp.maximum(i - 1, 0)])

    @pl.when(changed)
    def _():
        wgb[...] = wg_ref[...].astype(BF16)
        wub[...] = wu_ref[...].astype(BF16)

    @pl.when(i < nu_ref[0])
    def _():
        x = x_ref[...]
        g = jnp.dot(x, wgb[...], preferred_element_type=F32)
        u = jnp.dot(x, wub[...], preferred_element_type=F32)
        o_ref[...] = (g * jax.nn.sigmoid(g) * u).astype(o_ref.dtype)

    @pl.when(i >= nu_ref[0])
    def _():
        o_ref[...] = jnp.zeros(o_ref.shape, o_ref.dtype)


def moe_gateup(xs, wg, wu, be, n_used, *, tm, tf=512):
    n_rows, K = xs.shape
    F = wg.shape[2]
    nb = n_rows // tm
    xmap = lambda j, i, be, nu: (jnp.minimum(i, nu[0] - 1), 0)
    wmap = lambda j, i, be, nu: (be[i], 0, j)
    vmem = (2 * _nbytes((tm, K), BF16) + 4 * _nbytes((K, tf), F32) + 2 * _nbytes((K, tf), BF16)
            + 6 * _nbytes((tm, tf), F32))
    return pl.pallas_call(
        _moe_gateup_kernel,
        grid_spec=pltpu.PrefetchScalarGridSpec(
            num_scalar_prefetch=2,
            grid=(F // tf, nb),
            in_specs=[pl.BlockSpec((tm, K), xmap),
                      pl.BlockSpec((None, K, tf), wmap),
                      pl.BlockSpec((None, K, tf), wmap)],
            out_specs=pl.BlockSpec((tm, tf), lambda j, i, be, nu: (i, j)),
            scratch_shapes=[pltpu.VMEM((K, tf), BF16), pltpu.VMEM((K, tf), BF16)]),
        out_shape=jax.ShapeDtypeStruct((n_rows, F), BF16),
        compiler_params=_params(("arbitrary", "arbitrary"), vmem),
        name="moe_gateup",
    )(be, n_used, xs, wg, wu)


def _moe_down_kernel(be_ref, nu_ref, x_ref, w_ref, o_ref, wb):
    i = pl.program_id(1)
    changed = (i == 0) | (be_ref[i] != be_ref[jnp.maximum(i - 1, 0)])

    @pl.when(changed)
    def _():
        wb[...] = w_ref[...].astype(BF16)

    @pl.when(i < nu_ref[0])
    def _():
        o_ref[...] = jnp.dot(x_ref[...], wb[...], preferred_element_type=F32)

    @pl.when(i >= nu_ref[0])
    def _():
        o_ref[...] = jnp.zeros(o_ref.shape, o_ref.dtype)


def moe_down(hm, wd, be, n_used, *, tm, tn=512):
    n_rows, K = hm.shape
    N = wd.shape[2]
    nb = n_rows // tm
    xmap = lambda j, i, be, nu: (jnp.minimum(i, nu[0] - 1), 0)
    vmem = (2 * _nbytes((tm, K), BF16) + 2 * _nbytes((K, tn), F32) + _nbytes((K, tn), BF16)
            + 4 * _nbytes((tm, tn), F32))
    return pl.pallas_call(
        _moe_down_kernel,
        grid_spec=pltpu.PrefetchScalarGridSpec(
            num_scalar_prefetch=2,
            grid=(N // tn, nb),
            in_specs=[pl.BlockSpec((tm, K), xmap),
                      pl.BlockSpec((None, K, tn), lambda j, i, be, nu: (be[i], 0, j))],
            out_specs=pl.BlockSpec((tm, tn), lambda j, i, be, nu: (i, j)),
            scratch_shapes=[pltpu.VMEM((K, tn), BF16)]),
        out_shape=jax.ShapeDtypeStruct((n_rows, N), F32),
        compiler_params=_params(("arbitrary", "arbitrary"), vmem),
        name="moe_down",
    )(be, n_used, hm, wd)


def _moe_combine_kernel(pos_ref, x_ref, gw_ref, ys_hbm, o_ref, buf, sem, *, rows):
    i = pl.program_id(0)
    n_blocks = pl.num_programs(0)

    def row_copy(slot, k, r, src_row):
        return pltpu.make_async_copy(ys_hbm.at[pl.ds(src_row, 1)], buf.at[slot, k, pl.ds(r, 1)], sem.at[slot])

    def fetch(block, slot):
        def issue(r, c):
            a = (block * rows + r) * TOP_K
            for k in range(TOP_K):
                row_copy(slot, k, r, pos_ref[a + k]).start()
            return c

        lax.fori_loop(0, rows, issue, 0, unroll=4)

    @pl.when(i == 0)
    def _():
        fetch(0, 0)

    @pl.when(i + 1 < n_blocks)
    def _():
        fetch(i + 1, (i + 1) % 2)

    slot = i % 2

    def drain(r, c):
        for k in range(TOP_K):
            row_copy(slot, k, r, 0).wait()
        return c

    lax.fori_loop(0, rows, drain, 0, unroll=4)
    out = x_ref[...]
    for k in range(TOP_K):
        out = out + gw_ref[:, k:k + 1] * buf[slot, k]
    o_ref[...] = out


def moe_combine(x, ys, dest, gw, *, rows=128):
    T, D = x.shape
    return pl.pallas_call(
        functools.partial(_moe_combine_kernel, rows=rows),
        grid_spec=pltpu.PrefetchScalarGridSpec(
            num_scalar_prefetch=1,
            grid=(T // rows,),
            in_specs=[pl.BlockSpec((rows, D), lambda i, pos: (i, 0)),
                      pl.BlockSpec((rows, TOP_K), lambda i, pos: (i, 0)),
                      pl.BlockSpec(memory_space=pl.ANY)],
            out_specs=pl.BlockSpec((rows, D), lambda i, pos: (i, 0)),
            scratch_shapes=[pltpu.VMEM((2, TOP_K, rows, D), F32), pltpu.SemaphoreType.DMA((2,))]),
        out_shape=jax.ShapeDtypeStruct((T, D), F32),
        compiler_params=_params(("arbitrary",), 12 * _nbytes((rows, D), F32)),
        name="moe_combine",
    )(dest, x, gw, ys)


def moe_block(x, g, w_router, b_router, w_egate, w_eup, w_edown, *, tm=512):
    idx, gw = moe_router(x, g, w_router, b_router)
    dest, row_tok, be, n_used = _moe_dispatch(idx, tm)
    xs = moe_gather_norm(x, g, row_tok, n_used, tm=tm)
    hm = moe_gateup(xs, w_egate, w_eup, be, n_used, tm=tm)
    ys = moe_down(hm, w_edown, be, n_used, tm=tm)
    return moe_combine(x, ys, dest, gw)


def even_mixer_block(x, tabs_a, tabs_b, w_in, ga_q, ga_k, g_cq, w_uq, g_ckv, w_ukv, gb_q, gb_kn, gb_kr,
                     w_o, g_mix, *, batch, seq):
    a_width = A_HEADS * HEAD_DIM
    main_cols = 3 * a_width + Q_LORA + KV_LORA
    h = rmsnorm(x, g_mix)
    proj = matmul(h, w_in, n_out=main_cols, name="in_proj")
    w_kr = jnp.pad(w_in[:, main_cols:], ((0, 0), (0, LANES - ROPE_DIM)))
    krp = matmul(h, w_kr, name="in_proj_kr")
    o_a = mixer_a(proj, tabs_a, ga_q, ga_k, batch=batch, seq=seq)
    cq = rmsnorm(proj, g_cq, width=Q_LORA, col_block=3 * a_width // Q_LORA)
    ckv = rmsnorm(proj, g_ckv, width=KV_LORA, col_block=(3 * a_width + Q_LORA) // KV_LORA)
    w3 = w_uq.reshape(Q_LORA, B_HEADS, QK_DIM)
    w_uq_p = jnp.concatenate(
        [w3[:, :, :NOPE_DIM].reshape(Q_LORA, B_HEADS * NOPE_DIM),
         jnp.pad(w3[:, :, NOPE_DIM:], ((0, 0), (0, 0), (0, LANES - ROPE_DIM))).reshape(Q_LORA, B_HEADS * LANES)],
        axis=1)
    q_up = matmul(cq, w_uq_p, name="q_up")
    kv_up = matmul(ckv, w_ukv, name="kv_up")
    pad_r = lambda v: jnp.pad(v, (0, LANES - ROPE_DIM)).reshape(1, LANES)
    qt, kf, vb = mla_prep(q_up, kv_up, krp, tabs_b, gb_q[:NOPE_DIM].reshape(1, LANES), pad_r(gb_q[NOPE_DIM:]),
                          gb_kn.reshape(1, LANES), pad_r(gb_kr), batch=batch, seq=seq)
    o_b = mla_attention(qt, kf, vb, batch=batch, seq=seq)
    return matmul((o_a, o_b), w_o, res=x, name="mixer_out")


def kernel(x, mem, positions, g_mem, w_mem_kv, g_mem_k, g_mix, g_x, w_xq, g_xq, w_xo, g_ffn,
           w_in, ga_q, ga_k, g_cq, w_uq, g_ckv, w_ukv, gb_q, gb_kn, gb_kr, w_o_even,
           w_gate, w_up, w_down, w_qkv, w_o_odd, w_router, b_router, w_egate, w_eup, w_edown):
    B, S, D = x.shape
    M = mem.shape[1]
    depth = g_mix.shape[0]
    T = B * S
    xt = x.reshape(T, D)
    tabs_a = _rope_tables(positions, ROT_DIM)
    tabs_b = _rope_tables(positions, ROPE_DIM)
    kv = matmul(rmsnorm(mem.reshape(B * M, D), g_mem), w_mem_kv, tm=B * M, name="mem_kv")
    mk, mv = memkv_post(kv, g_mem_k)
    for layer in range(depth):
        i = layer // 2
        if layer % 2 == 0:
            xt = even_mixer_block(xt, tabs_a, tabs_b, w_in[i], ga_q[i], ga_k[i], g_cq[i], w_uq[i], g_ckv[i],
                                  w_ukv[i], gb_q[i], gb_kn[i], gb_kr[i], w_o_even[i], g_mix[layer],
                                  batch=B, seq=S)
        else:
            h = rmsnorm(xt, g_mix[layer])
            qkv = matmul(h, w_qkv[i], out_dtype=BF16, name="qkv_proj")
            o = stickbreak_attention(qkv, batch=B, seq=S, heads=D // HEAD_DIM)
            xt = matmul(o, w_o_odd[i], res=xt, name="mixer_out")
        xattn = functools.partial(cross_attention_block, xt, mk, mv, g_x[layer], w_xq[layer], g_xq[layer],
                                  w_xo[layer], seq=S, mem_len=M)
        if layer % 2 == 0:
            xt, h = xattn(g_ffn[layer])
            hm = swiglu_gateup(h, w_gate[i], w_up[i])
            xt = matmul(hm, w_down[i], res=xt, tn=1024, tk=2048, name="swiglu_down")
        else:
            xt = moe_block(xattn(), g_ffn[layer], w_router[i], b_router[i], w_egate[i], w_eup[i], w_edown[i])
    return xt.reshape(B, S, D)
```

```python
import functools

import jax
import jax.numpy as jnp
from jax import lax
from jax.experimental import pallas as pl
from jax.experimental.pallas import tpu as pltpu

F32 = jnp.float32
BF16 = jnp.bfloat16

HEAD_DIM = 128
ROT_DIM = HEAD_DIM // 4
ROPE_THETA = 500000.0
BLOCK = 128
NEG_INF = -1e30
EPS = 1e-6
DILATED_PAIRS = ((128, 1), (512, 4), (2048, 16))
A_HEADS = 16
B_HEADS = 16
Q_LORA = 1536
KV_LORA = 512
NOPE_DIM = 128
ROPE_DIM = 64
QK_DIM = NOPE_DIM + ROPE_DIM
X_HEADS = 4
N_EXPERTS = 8
TOP_K = 2
MIXER_A_GROUP = 8
ROW_DMA_UNROLL = 8
LOG2_E = 1.4426950408889634
MLA_Q_SCALE = QK_DIM ** -0.5 * LOG2_E

LANES = 128
V7X_VMEM_BYTES = 64 * 1024 * 1024
VMEM_CAP = V7X_VMEM_BYTES - 6 * 1024 * 1024


def _params(semantics, vmem_bytes):
    return pltpu.CompilerParams(dimension_semantics=semantics,
                                vmem_limit_bytes=int(min(max(vmem_bytes, 32 * 1024 * 1024), VMEM_CAP)))


def _nbytes(shape, dtype):
    n = 1
    for s in shape:
        n *= s
    return n * jnp.dtype(dtype).itemsize


def _rmsnorm_kernel(x_ref, g_ref, o_ref):
    x = x_ref[...].astype(F32)
    ms = jnp.mean(x * x, axis=-1, keepdims=True)
    o_ref[...] = (x * lax.rsqrt(ms + EPS) * g_ref[...]).astype(o_ref.dtype)


def rmsnorm(x, g, *, width=None, col_block=0, tm=256, out_dtype=BF16):
    T = x.shape[0]
    width = x.shape[1] if width is None else width
    tm = min(tm, T)
    return pl.pallas_call(
        _rmsnorm_kernel,
        grid=(T // tm,),
        in_specs=[pl.BlockSpec((tm, width), lambda i: (i, col_block)),
                  pl.BlockSpec((1, width), lambda i: (0, 0))],
        out_specs=pl.BlockSpec((tm, width), lambda i: (i, 0)),
        out_shape=jax.ShapeDtypeStruct((T, width), out_dtype),
        compiler_params=_params(("parallel",), 6 * _nbytes((tm, width), F32)),
        name="rmsnorm",
    )(x, g.reshape(1, width).astype(F32))


def _mm_kernel(*refs, nk, n_x, has_res):
    x_refs = refs[:n_x]
    w_ref = refs[n_x]
    rest = refs[n_x + 1:]
    res_ref = rest[0] if has_res else None
    o_ref = rest[1] if has_res else rest[0]
    acc_ref = rest[-1] if nk > 1 else None
    k = pl.program_id(2)
    w = w_ref[...].astype(BF16)

    def finish(part):
        out = part
        if has_res:
            out = out + res_ref[...]
        o_ref[...] = out.astype(o_ref.dtype)

    if n_x == 2:
        half = x_refs[0].shape[1]
        finish(jnp.dot(x_refs[0][...], w[:half], preferred_element_type=F32)
               + jnp.dot(x_refs[1][...], w[half:], preferred_element_type=F32))
    elif nk == 1:
        finish(jnp.dot(x_refs[0][...], w, preferred_element_type=F32))
    else:
        part = jnp.dot(x_refs[0][...], w, preferred_element_type=F32)

        @pl.when(k == 0)
        def _():
            acc_ref[...] = part

        @pl.when((k > 0) & (k < nk - 1))
        def _():
            acc_ref[...] += part

        @pl.when(k == nk - 1)
        def _():
            finish(acc_ref[...] + part)


def matmul(xs, w, *, n_out=None, res=None, out_dtype=F32, tm=1024, tn=512, tk=None, name="matmul"):
    if not isinstance(xs, (tuple, list)):
        xs = (xs,)
    n_x = len(xs)
    T = xs[0].shape[0]
    K = sum(x.shape[1] for x in xs)
    n_out = w.shape[1] if n_out is None else n_out
    tm = min(tm, T)
    tn = min(tn, n_out)
    if n_x == 2:
        assert tk is None and xs[1].shape[1] == xs[0].shape[1]
    tk = K if tk is None else tk
    nk = K // tk
    assert K == nk * tk and T % tm == 0 and n_out % tn == 0
    if n_x == 2:
        x_specs = [pl.BlockSpec((tm, K // 2), lambda i, j, k: (i, 0)) for _ in xs]
    else:
        x_specs = [pl.BlockSpec((tm, tk), lambda i, j, k: (i, k))]
    in_specs = x_specs + [pl.BlockSpec((tk, tn), lambda i, j, k: (k, j))]
    args = list(xs) + [w]
    if res is not None:
        in_specs.append(pl.BlockSpec((tm, tn), lambda i, j, k: (i, j)))
        args.append(res)
    scratch = [pltpu.VMEM((tm, tn), F32)] if nk > 1 else []
    vmem = (2 * _nbytes((tm, tk), xs[0].dtype) + 2 * _nbytes((tk, tn), F32) + _nbytes((tk, tn), BF16)
            + (2 * _nbytes((tm, tn), F32) if res is not None else 0)
            + 2 * _nbytes((tm, tn), out_dtype) + 2 * _nbytes((tm, tn), F32))
    return pl.pallas_call(
        functools.partial(_mm_kernel, nk=nk, n_x=n_x, has_res=res is not None),
        grid=(T // tm, n_out // tn, nk),
        in_specs=in_specs,
        out_specs=pl.BlockSpec((tm, tn), lambda i, j, k: (i, j)),
        out_shape=jax.ShapeDtypeStruct((T, n_out), out_dtype),
        scratch_shapes=scratch,
        compiler_params=_params(("parallel", "parallel", "arbitrary"), vmem),
        name=name,
    )(*args)


def _rope_tables(positions, dim):
    half = dim // 2
    inv_freq = ROPE_THETA ** (-jnp.arange(0, dim, 2, dtype=F32) / dim)
    ang = positions.astype(F32)[..., None] * inv_freq
    cos, sin = jnp.cos(ang), jnp.sin(ang)
    B, S = positions.shape
    ones = jnp.ones((B, S, LANES - dim), F32)
    zeros = jnp.zeros((B, S, LANES - dim), F32)
    zh = jnp.zeros((B, S, half), F32)
    c = jnp.concatenate([cos, cos, ones], axis=-1)
    sa = jnp.concatenate([-sin, zh, zeros], axis=-1)
    sb = jnp.concatenate([zh, sin, zeros], axis=-1)
    return [t.reshape(B * S, LANES) for t in (c, sa, sb)]


def _rope_lanes(y, c, sa, sb, half):
    return y * c + pltpu.roll(y, LANES - half, 1) * sa + pltpu.roll(y, half, 1) * sb


def _mixer_a_kernel(q_ref, k_ref, v_ref, c_ref, sa_ref, sb_ref, gq_ref, gk_ref, o_ref,
                    qs, ks, *state, seq):
    scale = HEAD_DIM ** -0.5 * LOG2_E
    c, sa, sb = c_ref[...], sa_ref[...], sb_ref[...]

    def prep(x, g):
        y = x * lax.rsqrt(jnp.mean(x * x, axis=-1, keepdims=True) + EPS) * g
        return _rope_lanes(y, c, sa, sb, ROT_DIM // 2)

    qs[...] = prep(q_ref[...], gq_ref[...])
    ks[...] = prep(k_ref[...], gk_ref[...])
    qi = lax.broadcasted_iota(jnp.int32, (BLOCK, 1), 0)
    kj = lax.broadcasted_iota(jnp.int32, (1, BLOCK), 1)
    trans_b = (((1,), (1,)), ((), ()))
    ones = jnp.ones((BLOCK, HEAD_DIM), BF16)

    for bi, (window, dil) in enumerate(DILATED_PAIRS):
        acc, m_sc, l_sc = state[3 * bi:3 * bi + 3]
        n_back = window // dil
        nb = seq // dil // BLOCK
        assert n_back <= BLOCK and nb * dil * BLOCK == seq
        cur_ok = (qi >= kj) & (qi - kj <= n_back)
        prev_ok = qi + BLOCK - kj <= n_back

        def rows_at(r, n, dil=dil):
            if dil == 1:
                return pl.ds(BLOCK * n, BLOCK)
            return pl.ds(r + dil * BLOCK * n, BLOCK, stride=dil)

        blocks = [(rows_at(r, n), rows_at(r, n - 1) if n > 0 else None) for r in range(dil) for n in range(nb)]
        for first in range(0, len(blocks), MIXER_A_GROUP):
            group = blocks[first:first + MIXER_A_GROUP]
            s_cur, s_prev = [], []
            for rows, prev in group:
                qb = qs[rows, :].astype(BF16)
                s_c = lax.dot_general(qb, ks[rows, :].astype(BF16), trans_b, preferred_element_type=F32) * scale
                s_cur.append(jnp.where(cur_ok, s_c, NEG_INF))
                if prev is None:
                    s_prev.append(None)
                else:
                    s_p = lax.dot_general(qb, ks[prev, :].astype(BF16), trans_b,
                                          preferred_element_type=F32) * scale
                    s_prev.append(jnp.where(prev_ok, s_p, NEG_INF))
            stats = []
            for s_c, s_p in zip(s_cur, s_prev):
                m = jnp.max(s_c, axis=-1, keepdims=True)
                if s_p is not None:
                    m = jnp.maximum(m, jnp.max(s_p, axis=-1, keepdims=True))
                p_c = jnp.exp2(s_c - m).astype(BF16)
                p_p = None if s_p is None else jnp.exp2(s_p - m).astype(BF16)
                stats.append((m, p_c, p_p))
            for (rows, prev), (m, p_c, p_p) in zip(group, stats):
                v_ext = jnp.concatenate([v_ref[rows, :].astype(BF16), ones], axis=1)
                o = jnp.dot(p_c, v_ext, preferred_element_type=F32)
                if p_p is not None:
                    v_ext = jnp.concatenate([v_ref[prev, :].astype(BF16), ones], axis=1)
                    o = o + jnp.dot(p_p, v_ext, preferred_element_type=F32)
                acc[rows, :] = o[:, :HEAD_DIM]
                l_sc[rows, :] = o[:, HEAD_DIM:]
                m_sc[rows, :] = jnp.broadcast_to(m, (BLOCK, HEAD_DIM))

    n_br = len(DILATED_PAIRS)
    m_all = state[1][...]
    for bi in range(1, n_br):
        m_all = jnp.maximum(m_all, state[3 * bi + 1][...])
    num = jnp.zeros(o_ref.shape, F32)
    den = jnp.zeros(o_ref.shape, F32)
    for bi in range(n_br):
        w = jnp.exp2(state[3 * bi + 1][...] - m_all)
        num = num + w * state[3 * bi][...]
        den = den + w * state[3 * bi + 2][...]
    o_ref[...] = (num / den).astype(o_ref.dtype)


def mixer_a(proj, tabs, gq, gk, *, batch, seq):
    T = batch * seq
    blk = (seq, HEAD_DIM)
    head = lambda off: pl.BlockSpec(blk, lambda b, h: (b, off + h))
    tab = pl.BlockSpec(blk, lambda b, h: (b, 0))
    gspec = pl.BlockSpec((1, HEAD_DIM), lambda b, h: (0, 0))
    return pl.pallas_call(
        functools.partial(_mixer_a_kernel, seq=seq),
        grid=(batch, A_HEADS),
        in_specs=[head(0), head(A_HEADS), head(2 * A_HEADS), tab, tab, tab, gspec, gspec],
        out_specs=pl.BlockSpec(blk, lambda b, h: (b, h)),
        out_shape=jax.ShapeDtypeStruct((T, A_HEADS * HEAD_DIM), BF16),
        scratch_shapes=[pltpu.VMEM(blk, F32) for _ in range(2 + 3 * len(DILATED_PAIRS))],
        compiler_params=_params(("parallel", "parallel"), 30 * _nbytes(blk, F32)),
        name="mixer_a",
    )(proj, proj, proj, *tabs, gq.reshape(1, HEAD_DIM), gk.reshape(1, HEAD_DIM))


def _mla_prep_kernel(qn_ref, qr_ref, kn_ref, v_ref, kr_ref, c_ref, sa_ref, sb_ref,
                     gqn_ref, gqr_ref, gkn_ref, gkr_ref, qt_ref, kf_ref, vb_ref):
    c, sa, sb = c_ref[...], sa_ref[...], sb_ref[...]
    half = ROPE_DIM // 2
    qn, qr = qn_ref[...], qr_ref[...]
    ss = jnp.sum(qn * qn, axis=-1, keepdims=True) + jnp.sum(qr * qr, axis=-1, keepdims=True)
    inv = lax.rsqrt(ss * (1.0 / QK_DIM) + EPS) * MLA_Q_SCALE
    qf = jnp.concatenate([qn * inv * gqn_ref[...], _rope_lanes(qr * inv * gqr_ref[...], c, sa, sb, half)], axis=1)
    qt_ref[...] = qf.T.astype(qt_ref.dtype)
    kn = kn_ref[...]
    kn = kn * lax.rsqrt(jnp.mean(kn * kn, axis=-1, keepdims=True) + EPS) * gkn_ref[...]
    kr = kr_ref[...]
    kr = kr * lax.rsqrt(jnp.sum(kr * kr, axis=-1, keepdims=True) * (1.0 / ROPE_DIM) + EPS) * gkr_ref[...]
    kf_ref[:, :NOPE_DIM] = kn.astype(kf_ref.dtype)
    kf_ref[:, NOPE_DIM:] = _rope_lanes(kr, c, sa, sb, half).astype(kf_ref.dtype)
    vb_ref[...] = v_ref[...].astype(vb_ref.dtype)


def mla_prep(q_up, kv_up, krp, tabs, gqn, gqr, gkn, gkr, *, batch, seq, tm=512):
    T = q_up.shape[0]
    H = B_HEADS
    per_seq = seq // tm
    blk = lambda f: pl.BlockSpec((tm, LANES), f)
    g = pl.BlockSpec((1, LANES), lambda i, h: (0, 0))
    return pl.pallas_call(
        _mla_prep_kernel,
        grid=(T // tm, H),
        in_specs=[blk(lambda i, h: (i, h)), blk(lambda i, h: (i, H + h)),
                  blk(lambda i, h: (i, 2 * h)), blk(lambda i, h: (i, 2 * h + 1)),
                  blk(lambda i, h: (i, 0)), blk(lambda i, h: (i, 0)), blk(lambda i, h: (i, 0)),
                  blk(lambda i, h: (i, 0)), g, g, g, g],
        out_specs=[pl.BlockSpec((None, None, 2 * LANES, tm), lambda i, h: (i // per_seq, h, 0, i % per_seq)),
                   pl.BlockSpec((tm, 2 * LANES), lambda i, h: (i, h)),
                   pl.BlockSpec((tm, LANES), lambda i, h: (i, h))],
        out_shape=[jax.ShapeDtypeStruct((batch, H, 2 * LANES, seq), BF16),
                   jax.ShapeDtypeStruct((T, H * 2 * LANES), BF16),
                   jax.ShapeDtypeStruct((T, H * LANES), BF16)],
        compiler_params=_params(("parallel", "parallel"), 40 * _nbytes((tm, LANES), F32)),
        name="mla_prep",
    )(q_up, q_up, kv_up, kv_up, krp, *tabs, gqn, gqr, gkn, gkr)


def _causal_attn_kernel(qt_ref, k_ref, v_ref, o_ref, s_sc, acc, *, tq, hg, dq, dv):
    qi = pl.program_id(2)
    key = lax.broadcasted_iota(jnp.int32, (tq, tq), 0)
    qry = lax.broadcasted_iota(jnp.int32, (tq, tq), 1)
    visible = key <= qry
    ones = jnp.ones((tq, dv), BF16)
    contract_rows = (((0,), (0,)), ((), ()))

    heads = range(hg)

    def scores(j, diagonal):
        rows = pl.ds(pl.multiple_of(j * tq, tq), tq)
        s = [jnp.dot(k_ref[rows, g * dq:(g + 1) * dq], qt_ref[g], preferred_element_type=F32) for g in heads]
        if diagonal:
            s = [jnp.where(visible, sg, NEG_INF) for sg in s]
        for g in heads:
            s_sc[g, j] = s[g]
        return s

    def col_max(j, ms):
        s = scores(j, False)
        return tuple(jnp.maximum(ms[g], jnp.max(s[g], axis=0, keepdims=True)) for g in heads)

    ms = lax.fori_loop(0, qi, col_max, tuple(jnp.max(sg, axis=0, keepdims=True) for sg in scores(qi, True)))

    def weighted(j):
        rows = pl.ds(pl.multiple_of(j * tq, tq), tq)
        p = [jnp.exp2(s_sc[g, j] - ms[g]).astype(BF16) for g in heads]
        return [lax.dot_general(p[g], jnp.concatenate([v_ref[rows, g * dv:(g + 1) * dv], ones], axis=1),
                                contract_rows, preferred_element_type=F32) for g in heads]

    for g, w in enumerate(weighted(qi)):
        acc[g] = w

    def below(j, carry):
        for g, w in enumerate(weighted(j)):
            acc[g] += w
        return carry

    lax.fori_loop(0, qi, below, 0)
    for g in heads:
        a = acc[g]
        o_ref[:, g * dv:(g + 1) * dv] = (a[:, :dv] / a[:, dv:]).astype(o_ref.dtype)


def mla_attention(qt, kf, vb, *, batch, seq, tq=512, hg=4):
    T = batch * seq
    nq = seq // tq
    dq = qt.shape[2]
    dv = vb.shape[1] // B_HEADS
    assert B_HEADS % hg == 0
    return pl.pallas_call(
        functools.partial(_causal_attn_kernel, tq=tq, hg=hg, dq=dq, dv=dv),
        grid=(batch, B_HEADS // hg, nq),
        in_specs=[pl.BlockSpec((None, hg, dq, tq), lambda b, h, i: (b, h, 0, i)),
                  pl.BlockSpec((seq, hg * dq), lambda b, h, i: (b, h)),
                  pl.BlockSpec((seq, hg * dv), lambda b, h, i: (b, h))],
        out_specs=pl.BlockSpec((tq, hg * dv), lambda b, h, i: (b * nq + i, h)),
        out_shape=jax.ShapeDtypeStruct((T, B_HEADS * dv), BF16),
        scratch_shapes=[pltpu.VMEM((hg, nq, tq, tq), F32), pltpu.VMEM((hg, tq, 2 * dv), F32)],
        compiler_params=_params(("parallel", "parallel", "parallel"),
                                _nbytes((hg, nq, tq, tq), F32) + 2 * _nbytes((seq, hg * (dq + dv)), BF16)
                                + 8 * hg * _nbytes((tq, tq), F32)),
        name="mla_attention",
    )(qt, kf, vb)


STICK_EXHAUSTED_LOG = -110.0


def _stickbreak_kernel(q_ref, k_ref, v_ref, o_ref, qt_sc, c_sc, acc, *, tq, hg, scale):
    qi = pl.program_id(2)
    c_sc[...] = jnp.zeros(c_sc.shape, F32)
    acc[...] = jnp.zeros(acc.shape, F32)
    for g in range(hg):
        qt_sc[g] = q_ref[:, g * HEAD_DIM:(g + 1) * HEAD_DIM].astype(F32).T.astype(BF16)
    key = lax.broadcasted_iota(jnp.int32, (tq, tq), 0)
    qry = lax.broadcasted_iota(jnp.int32, (tq, tq), 1)
    after = (qry > key).astype(BF16)
    earlier = key < qry
    contract_rows = (((0,), (0,)), ((), ()))

    def chunk(j, diagonal):
        rows = pl.ds(pl.multiple_of(j * tq, tq), tq)
        heads = range(hg)
        cols = [slice(g * HEAD_DIM, (g + 1) * HEAD_DIM) for g in heads]
        z = [jnp.dot(k_ref[rows, cols[g]], qt_sc[g], preferred_element_type=F32) * (scale * LOG2_E) for g in heads]
        log_beta, parts = [], []
        for g in heads:
            softplus = jnp.maximum(z[g], 0.0) + jnp.log2(1.0 + jnp.exp2(-jnp.abs(z[g])))
            log_keep = -softplus
            if diagonal:
                log_keep = jnp.where(earlier, log_keep, 0.0)
            hi = log_keep.astype(BF16)
            parts += [hi, (log_keep - hi.astype(F32)).astype(BF16)]
            log_beta.append(z[g] - softplus + c_sc[g])
            c_sc[g] += jnp.sum(log_keep, axis=0, keepdims=True)
        within = jnp.dot(after, jnp.concatenate(parts, axis=1), preferred_element_type=F32)
        for g in heads:
            later = within[:, 2 * g * tq:(2 * g + 1) * tq] + within[:, (2 * g + 1) * tq:(2 * g + 2) * tq]
            a = jnp.exp2(log_beta[g] + later)
            if diagonal:
                a = jnp.where(earlier, a, 0.0)
            acc[:, cols[g]] += lax.dot_general(a.astype(BF16), v_ref[rows, cols[g]], contract_rows,
                                               preferred_element_type=F32)

    def stick_left():
        return jnp.max(c_sc[...]) > STICK_EXHAUSTED_LOG * LOG2_E

    chunk(qi, True)

    def more(state):
        j, left = state
        return (j >= 0) & left

    def step(state):
        j, _ = state
        chunk(j, False)
        return j - 1, stick_left()

    lax.while_loop(more, step, (qi - 1, stick_left()))
    o_ref[...] = acc[...].astype(o_ref.dtype)


def stickbreak_attention(qkv, *, batch, seq, heads, tq=256, hg=4):
    T = batch * seq
    nq = seq // tq
    ng = heads // hg
    assert heads == ng * hg
    blk = hg * HEAD_DIM
    return pl.pallas_call(
        functools.partial(_stickbreak_kernel, tq=tq, hg=hg, scale=HEAD_DIM ** -0.5),
        grid=(batch, ng, nq),
        in_specs=[pl.BlockSpec((tq, blk), lambda b, h, i: (b * nq + i, h)),
                  pl.BlockSpec((seq, blk), lambda b, h, i: (b, ng + h)),
                  pl.BlockSpec((seq, blk), lambda b, h, i: (b, 2 * ng + h))],
        out_specs=pl.BlockSpec((tq, blk), lambda b, h, i: (b * nq + i, h)),
        out_shape=jax.ShapeDtypeStruct((T, heads * HEAD_DIM), BF16),
        scratch_shapes=[pltpu.VMEM((hg, HEAD_DIM, tq), BF16), pltpu.VMEM((hg, 1, tq), F32),
                        pltpu.VMEM((tq, blk), F32)],
        compiler_params=_params(("parallel", "parallel", "parallel"), 0),
        name="stickbreak",
    )(qkv, qkv, qkv)


def _memkv_kernel(kv_ref, g_ref, mk_ref, mv_ref):
    for h in range(X_HEADS):
        cols = slice(h * HEAD_DIM, (h + 1) * HEAD_DIM)
        k = kv_ref[:, cols]
        k = k * lax.rsqrt(jnp.mean(k * k, axis=-1, keepdims=True) + EPS) * g_ref[...]
        mk_ref[:, cols] = k.astype(mk_ref.dtype)
    mv_ref[...] = kv_ref[:, X_HEADS * HEAD_DIM:].astype(mv_ref.dtype)


def memkv_post(kv, g):
    M = kv.shape[0]
    xd = X_HEADS * HEAD_DIM
    return pl.pallas_call(
        _memkv_kernel,
        grid=(1,),
        in_specs=[pl.BlockSpec((M, 2 * xd), lambda i: (0, 0)), pl.BlockSpec((1, HEAD_DIM), lambda i: (0, 0))],
        out_specs=[pl.BlockSpec((M, xd), lambda i: (0, 0)), pl.BlockSpec((M, xd), lambda i: (0, 0))],
        out_shape=[jax.ShapeDtypeStruct((M, xd), BF16), jax.ShapeDtypeStruct((M, xd), BF16)],
        compiler_params=_params(("arbitrary",), 0),
        name="memkv_post",
    )(kv, g.reshape(1, HEAD_DIM))


def _xattn_block_kernel(x_ref, gx_ref, wq_ref, gq_ref, mk_ref, mv_ref, wo_ref, *rest, emit_norm):
    if emit_norm:
        gn_ref, o_ref, hn_ref = rest
    else:
        (o_ref,) = rest
    scale = HEAD_DIM ** -0.5
    x = x_ref[...]
    h = (x * lax.rsqrt(jnp.mean(x * x, axis=-1, keepdims=True) + EPS) * gx_ref[...]).astype(BF16)
    q_all = jnp.dot(h, wq_ref[...], preferred_element_type=F32)
    heads = []
    for hd in range(X_HEADS):
        cols = slice(hd * HEAD_DIM, (hd + 1) * HEAD_DIM)
        q = q_all[:, cols]
        q = (q * lax.rsqrt(jnp.mean(q * q, axis=-1, keepdims=True) + EPS) * gq_ref[...]).astype(BF16)
        s = lax.dot_general(q, mk_ref[:, cols], (((1,), (1,)), ((), ())), preferred_element_type=F32) * scale
        p = jnp.exp(s - jnp.max(s, axis=-1, keepdims=True))
        p = p / jnp.sum(p, axis=-1, keepdims=True)
        heads.append(jnp.dot(p.astype(BF16), mv_ref[:, cols], preferred_element_type=F32).astype(BF16))
    y = x + jnp.dot(jnp.concatenate(heads, axis=1), wo_ref[...], preferred_element_type=F32)
    o_ref[...] = y
    if emit_norm:
        hn_ref[...] = (y * lax.rsqrt(jnp.mean(y * y, axis=-1, keepdims=True) + EPS) * gn_ref[...]).astype(hn_ref.dtype)


def cross_attention_block(x, mk, mv, g_x, w_xq, g_xq, w_xo, g_next=None, *, seq, mem_len, tm=256):
    T, D = x.shape
    xd = X_HEADS * HEAD_DIM
    per_seq = seq // tm
    emit_norm = g_next is not None
    row = lambda w: pl.BlockSpec((1, w), lambda i: (0, 0))
    tok = pl.BlockSpec((tm, D), lambda i: (i, 0))
    in_specs = [tok, row(D), pl.BlockSpec((D, xd), lambda i: (0, 0)), row(HEAD_DIM),
                pl.BlockSpec((mem_len, xd), lambda i: (i // per_seq, 0)),
                pl.BlockSpec((mem_len, xd), lambda i: (i // per_seq, 0)),
                pl.BlockSpec((xd, D), lambda i: (0, 0))]
    args = [x, g_x.reshape(1, D), w_xq.astype(BF16), g_xq.reshape(1, HEAD_DIM), mk, mv, w_xo.astype(BF16)]
    out_specs, out_shape = tok, jax.ShapeDtypeStruct((T, D), F32)
    if emit_norm:
        in_specs.append(row(D))
        args.append(g_next.reshape(1, D))
        out_specs = [tok, tok]
        out_shape = [out_shape, jax.ShapeDtypeStruct((T, D), BF16)]
    vmem = 10 * _nbytes((tm, D), F32) + 4 * _nbytes((D, xd), BF16)
    return pl.pallas_call(
        functools.partial(_xattn_block_kernel, emit_norm=emit_norm),
        grid=(T // tm,),
        in_specs=in_specs,
        out_specs=out_specs,
        out_shape=out_shape,
        compiler_params=_params(("parallel",), vmem),
        name="cross_attention",
    )(*args)


def _gateup_kernel(x_ref, wg_ref, wu_ref, o_ref):
    x = x_ref[...]
    g = jnp.dot(x, wg_ref[...].astype(BF16), preferred_element_type=F32)
    u = jnp.dot(x, wu_ref[...].astype(BF16), preferred_element_type=F32)
    o_ref[...] = (g * jax.nn.sigmoid(g) * u).astype(o_ref.dtype)


def swiglu_gateup(x, wg, wu, *, tm=1024, tf=512):
    T, K = x.shape
    F = wg.shape[1]
    vmem = (_nbytes((tm, K), BF16) + 4 * _nbytes((K, tf), F32) + 2 * _nbytes((K, tf), BF16)
            + 6 * _nbytes((tm, tf), F32))
    return pl.pallas_call(
        _gateup_kernel,
        grid=(T // tm, F // tf),
        in_specs=[pl.BlockSpec((tm, K), lambda i, j: (i, 0), pipeline_mode=pl.Buffered(1)),
                  pl.BlockSpec((K, tf), lambda i, j: (0, j)),
                  pl.BlockSpec((K, tf), lambda i, j: (0, j))],
        out_specs=pl.BlockSpec((tm, tf), lambda i, j: (i, j)),
        out_shape=jax.ShapeDtypeStruct((T, F), BF16),
        compiler_params=_params(("parallel", "parallel"), vmem),
        name="swiglu_gateup",
    )(x, wg, wu)


def _router_kernel(x_ref, g_ref, w_ref, b_ref, idx_ref, gw_ref):
    x = x_ref[...]
    h = x * lax.rsqrt(jnp.mean(x * x, axis=-1, keepdims=True) + EPS) * g_ref[...]
    logits = jnp.dot(h, w_ref[...], preferred_element_type=F32, precision=lax.Precision.HIGHEST) + b_ref[...]
    lane = lax.broadcasted_iota(jnp.int32, logits.shape, 1)
    logits = jnp.where(lane < N_EXPERTS, logits, -jnp.inf)
    m1 = jnp.max(logits, axis=-1, keepdims=True)
    i1 = jnp.min(jnp.where(logits == m1, lane, LANES), axis=-1, keepdims=True)
    rest = jnp.where(lane == i1, -jnp.inf, logits)
    m2 = jnp.max(rest, axis=-1, keepdims=True)
    i2 = jnp.min(jnp.where(rest == m2, lane, LANES), axis=-1, keepdims=True)
    e = jnp.exp(m2 - m1)
    w1 = 1.0 / (1.0 + e)
    w2 = e / (1.0 + e)
    idx_ref[...] = jnp.where(lane == 0, i1, jnp.where(lane == 1, i2, 0))
    gw_ref[...] = jnp.where(lane == 0, w1, jnp.where(lane == 1, w2, 0.0))


def moe_router(x, g, w_router, b_router, *, tm=256):
    T, D = x.shape
    E = w_router.shape[1]
    w = jnp.pad(w_router, ((0, 0), (0, LANES - E)))
    b = jnp.pad(b_router, (0, LANES - E)).reshape(1, LANES)
    idx, gw = pl.pallas_call(
        _router_kernel,
        grid=(T // tm,),
        in_specs=[pl.BlockSpec((tm, D), lambda i: (i, 0)), pl.BlockSpec((1, D), lambda i: (0, 0)),
                  pl.BlockSpec((D, LANES), lambda i: (0, 0)), pl.BlockSpec((1, LANES), lambda i: (0, 0))],
        out_specs=[pl.BlockSpec((tm, LANES), lambda i: (i, 0)), pl.BlockSpec((tm, LANES), lambda i: (i, 0))],
        out_shape=[jax.ShapeDtypeStruct((T, LANES), jnp.int32), jax.ShapeDtypeStruct((T, LANES), F32)],
        compiler_params=_params(("parallel",), 8 * _nbytes((tm, D), F32)),
        name="moe_router",
    )(x, g.reshape(1, D), w, b)
    return idx[:, :TOP_K], gw[:, :TOP_K]


def _moe_dispatch(idx, tm):
    T = idx.shape[0]
    A = T * TOP_K
    e_flat = idx.reshape(A)
    onehot = (e_flat[:, None] == jnp.arange(N_EXPERTS, dtype=jnp.int32)[None, :]).astype(jnp.int32)
    csum = jnp.cumsum(onehot, axis=0)
    pos_in = jnp.sum(csum * onehot, axis=1) - 1
    counts = csum[-1]
    padded = ((counts + tm - 1) // tm) * tm
    gend = jnp.cumsum(padded)
    gstart = gend - padded
    dest = (jnp.sum(onehot * gstart[None, :], axis=1) + pos_in).astype(jnp.int32)
    n_rows = A + N_EXPERTS * tm
    row_tok = jnp.zeros((n_rows,), jnp.int32).at[dest].set(jnp.arange(A, dtype=jnp.int32) // TOP_K)
    nb = n_rows // tm
    n_used = (gend[-1] // tm).astype(jnp.int32)
    blk = jnp.arange(nb, dtype=jnp.int32)
    be = jnp.sum((blk[:, None] * tm >= gend[None, :]).astype(jnp.int32), axis=1)
    be = jnp.minimum(be, N_EXPERTS - 1)
    be = jnp.where(blk < n_used, be, be[jnp.maximum(n_used - 1, 0)]).astype(jnp.int32)
    return dest, row_tok, be, n_used.reshape(1)


def _moe_gather_kernel(tok_ref, nu_ref, x_hbm, g_ref, o_ref, buf, sem, *, rows):
    i = pl.program_id(0)
    n_used = nu_ref[0]

    def row_copy(slot, r, src_row):
        return pltpu.make_async_copy(x_hbm.at[pl.ds(src_row, 1)], buf.at[slot, pl.ds(r, 1)], sem.at[slot])

    def fetch(block, slot):
        def issue(group, c):
            for u in range(ROW_DMA_UNROLL):
                r = group * ROW_DMA_UNROLL + u
                row_copy(slot, r, tok_ref[block * rows + r]).start(priority=u % 2)
            return c

        lax.fori_loop(0, rows // ROW_DMA_UNROLL, issue, 0)

    @pl.when(i == 0)
    def _():
        fetch(0, 0)

    @pl.when(i + 1 < n_used)
    def _():
        fetch(i + 1, (i + 1) % 2)

    @pl.when(i < n_used)
    def _():
        slot = i % 2

        def drain(r, c):
            row_copy(slot, r, 0).wait()
            return c

        lax.fori_loop(0, rows, drain, 0, unroll=8)
        x = buf[slot]
        o_ref[...] = (x * lax.rsqrt(jnp.mean(x * x, axis=-1, keepdims=True) + EPS) * g_ref[...]).astype(o_ref.dtype)

    @pl.when(i >= n_used)
    def _():
        o_ref[...] = jnp.zeros(o_ref.shape, o_ref.dtype)


def moe_gather_norm(x, g, row_tok, n_used, *, tm, rows=128):
    T, D = x.shape
    n_rows = row_tok.shape[0]
    per = tm // rows
    return pl.pallas_call(
        functools.partial(_moe_gather_kernel, rows=rows),
        grid_spec=pltpu.PrefetchScalarGridSpec(
            num_scalar_prefetch=2,
            grid=(n_rows // rows,),
            in_specs=[pl.BlockSpec(memory_space=pl.ANY), pl.BlockSpec((1, D), lambda i, tok, nu: (0, 0))],
            out_specs=pl.BlockSpec((rows, D), lambda i, tok, nu: (i, 0)),
            scratch_shapes=[pltpu.VMEM((2, rows, D), F32), pltpu.SemaphoreType.DMA((2,))]),
        out_shape=jax.ShapeDtypeStruct((n_rows, D), BF16),
        compiler_params=_params(("arbitrary",), 10 * _nbytes((rows, D), F32)),
        name="moe_gather",
    )(row_tok, n_used * per, x, g.reshape(1, D))


def _moe_gateup_kernel(be_ref, nu_ref, x_ref, wg_ref, wu_ref, o_ref, wgb, wub):
    i = pl.program_id(1)
    changed = (i == 0) | (be_ref[i] != be_ref[jnp.maximum(i - 1, 0)])

    @pl.when(changed)
    def _():
        wgb[...] = wg_ref[...].astype(BF16)
        wub[...] = wu_ref[...].astype(BF16)

    @pl.when(i < nu_ref[0])
    def _():
        x = x_ref[...]
        g = jnp.dot(x, wgb[...], preferred_element_type=F32)
        u = jnp.dot(x, wub[...], preferred_element_type=F32)
        o_ref[...] = (g * jax.nn.sigmoid(g) * u).astype(o_ref.dtype)

    @pl.when(i >= nu_ref[0])
    def _():
        o_ref[...] = jnp.zeros(o_ref.shape, o_ref.dtype)


def moe_gateup(xs, wg, wu, be, n_used, *, tm, tf=512):
    n_rows, K = xs.shape
    F = wg.shape[2]
    nb = n_rows // tm
    xmap = lambda j, i, be, nu: (jnp.minimum(i, nu[0] - 1), 0)
    wmap = lambda j, i, be, nu: (be[i], 0, j)
    vmem = (2 * _nbytes((tm, K), BF16) + 4 * _nbytes((K, tf), F32) + 2 * _nbytes((K, tf), BF16)
            + 6 * _nbytes((tm, tf), F32))
    return pl.pallas_call(
        _moe_gateup_kernel,
        grid_spec=pltpu.PrefetchScalarGridSpec(
            num_scalar_prefetch=2,
            grid=(F // tf, nb),
            in_specs=[pl.BlockSpec((tm, K), xmap),
                      pl.BlockSpec((None, K, tf), wmap),
                      pl.BlockSpec((None, K, tf), wmap)],
            out_specs=pl.BlockSpec((tm, tf), lambda j, i, be, nu: (i, j)),
            scratch_shapes=[pltpu.VMEM((K, tf), BF16), pltpu.VMEM((K, tf), BF16)]),
        out_shape=jax.ShapeDtypeStruct((n_rows, F), BF16),
        compiler_params=_params(("arbitrary", "arbitrary"), vmem),
        name="moe_gateup",
    )(be, n_used, xs, wg, wu)


def _moe_down_kernel(be_ref, nu_ref, x_ref, w_ref, o_ref, wb):
    i = pl.program_id(1)
    changed = (i == 0) | (be_ref[i] != be_ref[jnp.maximum(i - 1, 0)])

    @pl.when(changed)
    def _():
        wb[...] = w_ref[...].astype(BF16)

    @pl.when(i < nu_ref[0])
    def _():
        o_ref[...] = jnp.dot(x_ref[...], wb[...], preferred_element_type=F32)

    @pl.when(i >= nu_ref[0])
    def _():
        o_ref[...] = jnp.zeros(o_ref.shape, o_ref.dtype)


def moe_down(hm, wd, be, n_used, *, tm, tn=512):
    n_rows, K = hm.shape
    N = wd.shape[2]
    nb = n_rows // tm
    xmap = lambda j, i, be, nu: (jnp.minimum(i, nu[0] - 1), 0)
    vmem = (2 * _nbytes((tm, K), BF16) + 2 * _nbytes((K, tn), F32) + _nbytes((K, tn), BF16)
            + 4 * _nbytes((tm, tn), F32))
    return pl.pallas_call(
        _moe_down_kernel,
        grid_spec=pltpu.PrefetchScalarGridSpec(
            num_scalar_prefetch=2,
            grid=(N // tn, nb),
            in_specs=[pl.BlockSpec((tm, K), xmap),
                      pl.BlockSpec((None, K, tn), lambda j, i, be, nu: (be[i], 0, j))],
            out_specs=pl.BlockSpec((tm, tn), lambda j, i, be, nu: (i, j)),
            scratch_shapes=[pltpu.VMEM((K, tn), BF16)]),
        out_shape=jax.ShapeDtypeStruct((n_rows, N), F32),
        compiler_params=_params(("arbitrary", "arbitrary"), vmem),
        name="moe_down",
    )(be, n_used, hm, wd)


def _moe_combine_kernel(pos_ref, x_ref, gw_ref, ys_hbm, o_ref, buf, sem, *, rows):
    i = pl.program_id(0)
    n_blocks = pl.num_programs(0)

    def row_copy(slot, k, r, src_row):
        return pltpu.make_async_copy(ys_hbm.at[pl.ds(src_row, 1)], buf.at[slot, k, pl.ds(r, 1)], sem.at[slot])

    def fetch(block, slot):
        def issue(group, c):
            for u in range(ROW_DMA_UNROLL):
                r = group * ROW_DMA_UNROLL + u
                a = (block * rows + r) * TOP_K
                for k in range(TOP_K):
                    row_copy(slot, k, r, pos_ref[a + k]).start(priority=k % 2)
            return c

        lax.fori_loop(0, rows // ROW_DMA_UNROLL, issue, 0)

    @pl.when(i == 0)
    def _():
        fetch(0, 0)

    @pl.when(i + 1 < n_blocks)
    def _():
        fetch(i + 1, (i + 1) % 2)

    slot = i % 2

    def drain(r, c):
        for k in range(TOP_K):
            row_copy(slot, k, r, 0).wait()
        return c

    lax.fori_loop(0, rows, drain, 0, unroll=4)
    out = x_ref[...]
    for k in range(TOP_K):
        out = out + gw_ref[:, k:k + 1] * buf[slot, k]
    o_ref[...] = out


def moe_combine(x, ys, dest, gw, *, rows=128):
    T, D = x.shape
    return pl.pallas_call(
        functools.partial(_moe_combine_kernel, rows=rows),
        grid_spec=pltpu.PrefetchScalarGridSpec(
            num_scalar_prefetch=1,
            grid=(T // rows,),
            in_specs=[pl.BlockSpec((rows, D), lambda i, pos: (i, 0)),
                      pl.BlockSpec((rows, TOP_K), lambda i, pos: (i, 0)),
                      pl.BlockSpec(memory_space=pl.ANY)],
            out_specs=pl.BlockSpec((rows, D), lambda i, pos: (i, 0)),
            scratch_shapes=[pltpu.VMEM((2, TOP_K, rows, D), F32), pltpu.SemaphoreType.DMA((2,))]),
        out_shape=jax.ShapeDtypeStruct((T, D), F32),
        compiler_params=_params(("arbitrary",), 12 * _nbytes((rows, D), F32)),
        name="moe_combine",
    )(dest, x, gw, ys)


def moe_block(x, g, w_router, b_router, w_egate, w_eup, w_edown, *, tm=512):
    idx, gw = moe_router(x, g, w_router, b_router)
    dest, row_tok, be, n_used = _moe_dispatch(idx, tm)
    xs = moe_gather_norm(x, g, row_tok, n_used, tm=tm)
    hm = moe_gateup(xs, w_egate, w_eup, be, n_used, tm=tm)
    ys = moe_down(hm, w_edown, be, n_used, tm=tm)
    return moe_combine(x, ys, dest, gw)


def even_mixer_block(x, tabs_a, tabs_b, w_in, ga_q, ga_k, g_cq, w_uq, g_ckv, w_ukv, gb_q, gb_kn, gb_kr,
                     w_o, g_mix, *, batch, seq):
    a_width = A_HEADS * HEAD_DIM
    main_cols = 3 * a_width + Q_LORA + KV_LORA
    h = rmsnorm(x, g_mix)
    proj = matmul(h, w_in, n_out=main_cols, name="in_proj")
    w_kr = jnp.pad(w_in[:, main_cols:], ((0, 0), (0, LANES - ROPE_DIM)))
    krp = matmul(h, w_kr, name="in_proj_kr")
    o_a = mixer_a(proj, tabs_a, ga_q, ga_k, batch=batch, seq=seq)
    cq = rmsnorm(proj, g_cq, width=Q_LORA, col_block=3 * a_width // Q_LORA)
    ckv = rmsnorm(proj, g_ckv, width=KV_LORA, col_block=(3 * a_width + Q_LORA) // KV_LORA)
    w3 = w_uq.reshape(Q_LORA, B_HEADS, QK_DIM)
    w_uq_p = jnp.concatenate(
        [w3[:, :, :NOPE_DIM].reshape(Q_LORA, B_HEADS * NOPE_DIM),
         jnp.pad(w3[:, :, NOPE_DIM:], ((0, 0), (0, 0), (0, LANES - ROPE_DIM))).reshape(Q_LORA, B_HEADS * LANES)],
        axis=1)
    q_up = matmul(cq, w_uq_p, name="q_up")
    kv_up = matmul(ckv, w_ukv, name="kv_up")
    pad_r = lambda v: jnp.pad(v, (0, LANES - ROPE_DIM)).reshape(1, LANES)
    qt, kf, vb = mla_prep(q_up, kv_up, krp, tabs_b, gb_q[:NOPE_DIM].reshape(1, LANES), pad_r(gb_q[NOPE_DIM:]),
                          gb_kn.reshape(1, LANES), pad_r(gb_kr), batch=batch, seq=seq)
    o_b = mla_attention(qt, kf, vb, batch=batch, seq=seq)
    return matmul((o_a, o_b), w_o, res=x, name="mixer_out")


def kernel(x, mem, positions, g_mem, w_mem_kv, g_mem_k, g_mix, g_x, w_xq, g_xq, w_xo, g_ffn,
           w_in, ga_q, ga_k, g_cq, w_uq, g_ckv, w_ukv, gb_q, gb_kn, gb_kr, w_o_even,
           w_gate, w_up, w_down, w_qkv, w_o_odd, w_router, b_router, w_egate, w_eup, w_edown):
    B, S, D = x.shape
    M = mem.shape[1]
    depth = g_mix.shape[0]
    T = B * S
    xt = x.reshape(T, D)
    tabs_a = _rope_tables(positions, ROT_DIM)
    tabs_b = _rope_tables(positions, ROPE_DIM)
    kv = matmul(rmsnorm(mem.reshape(B * M, D), g_mem), w_mem_kv, tm=B * M, name="mem_kv")
    mk, mv = memkv_post(kv, g_mem_k)
    for layer in range(depth):
        i = layer // 2
        if layer % 2 == 0:
            xt = even_mixer_block(xt, tabs_a, tabs_b, w_in[i], ga_q[i], ga_k[i], g_cq[i], w_uq[i], g_ckv[i],
                                  w_ukv[i], gb_q[i], gb_kn[i], gb_kr[i], w_o_even[i], g_mix[layer],
                                  batch=B, seq=S)
        else:
            h = rmsnorm(xt, g_mix[layer])
            qkv = matmul(h, w_qkv[i], out_dtype=BF16, name="qkv_proj")
            o = stickbreak_attention(qkv, batch=B, seq=S, heads=D // HEAD_DIM)
            xt = matmul(o, w_o_odd[i], res=xt, name="mixer_out")
        xattn = functools.partial(cross_attention_block, xt, mk, mv, g_x[layer], w_xq[layer], g_xq[layer],
                                  w_xo[layer], seq=S, mem_len=M)
        if layer % 2 == 0:
            xt, h = xattn(g_ffn[layer])
            hm = swiglu_gateup(h, w_gate[i], w_up[i])
            xt = matmul(hm, w_down[i], res=xt, tn=512, tk=w_down.shape[1] // 4, name="swiglu_down")
        else:
            xt = moe_block(xattn(), g_ffn[layer], w_router[i], b_router[i], w_egate[i], w_eup[i], w_edown[i])
    return xt.reshape(B, S, D)
```

```python
import functools

import jax
import jax.numpy as jnp
from jax import lax
from jax.experimental import pallas as pl
from jax.experimental.pallas import tpu as pltpu

F32 = jnp.float32
BF16 = jnp.bfloat16

HEAD_DIM = 128
ROT_DIM = HEAD_DIM // 4
ROPE_THETA = 500000.0
BLOCK = 128
NEG_INF = -1e30
EPS = 1e-6
DILATED_PAIRS = ((128, 1), (512, 4), (2048, 16))
A_HEADS = 16
B_HEADS = 16
Q_LORA = 1536
KV_LORA = 512
NOPE_DIM = 128
ROPE_DIM = 64
QK_DIM = NOPE_DIM + ROPE_DIM
X_HEADS = 4
N_EXPERTS = 8
TOP_K = 2
MIXER_A_GROUP = 8
LOG2_E = 1.4426950408889634
MLA_Q_SCALE = QK_DIM ** -0.5 * LOG2_E

LANES = 128
V7X_VMEM_BYTES = 64 * 1024 * 1024
VMEM_CAP = V7X_VMEM_BYTES - 6 * 1024 * 1024


def _params(semantics, vmem_bytes):
    return pltpu.CompilerParams(dimension_semantics=semantics,
                                vmem_limit_bytes=int(min(max(vmem_bytes, 32 * 1024 * 1024), VMEM_CAP)))


def _nbytes(shape, dtype):
    n = 1
    for s in shape:
        n *= s
    return n * jnp.dtype(dtype).itemsize


def _rmsnorm_kernel(x_ref, g_ref, o_ref):
    x = x_ref[...].astype(F32)
    ms = jnp.mean(x * x, axis=-1, keepdims=True)
    o_ref[...] = (x * lax.rsqrt(ms + EPS) * g_ref[...]).astype(o_ref.dtype)


def rmsnorm(x, g, *, width=None, col_block=0, tm=256, out_dtype=BF16):
    T = x.shape[0]
    width = x.shape[1] if width is None else width
    tm = min(tm, T)
    return pl.pallas_call(
        _rmsnorm_kernel,
        grid=(T // tm,),
        in_specs=[pl.BlockSpec((tm, width), lambda i: (i, col_block)),
                  pl.BlockSpec((1, width), lambda i: (0, 0))],
        out_specs=pl.BlockSpec((tm, width), lambda i: (i, 0)),
        out_shape=jax.ShapeDtypeStruct((T, width), out_dtype),
        compiler_params=_params(("parallel",), 6 * _nbytes((tm, width), F32)),
        name="rmsnorm",
    )(x, g.reshape(1, width).astype(F32))


def _mm_kernel(*refs, nk, n_x, has_res):
    x_refs = refs[:n_x]
    w_ref = refs[n_x]
    rest = refs[n_x + 1:]
    res_ref = rest[0] if has_res else None
    o_ref = rest[1] if has_res else rest[0]
    acc_ref = rest[-1] if nk > 1 else None
    k = pl.program_id(2)
    w = w_ref[...].astype(BF16)

    def finish(part):
        out = part
        if has_res:
            out = out + res_ref[...]
        o_ref[...] = out.astype(o_ref.dtype)

    if n_x == 2:
        half = x_refs[0].shape[1]
        finish(jnp.dot(x_refs[0][...], w[:half], preferred_element_type=F32)
               + jnp.dot(x_refs[1][...], w[half:], preferred_element_type=F32))
    elif nk == 1:
        finish(jnp.dot(x_refs[0][...], w, preferred_element_type=F32))
    else:
        part = jnp.dot(x_refs[0][...], w, preferred_element_type=F32)

        @pl.when(k == 0)
        def _():
            acc_ref[...] = part

        @pl.when((k > 0) & (k < nk - 1))
        def _():
            acc_ref[...] += part

        @pl.when(k == nk - 1)
        def _():
            finish(acc_ref[...] + part)


def matmul(xs, w, *, n_out=None, res=None, out_dtype=F32, tm=1024, tn=512, tk=None, single_buffer_x=False,
           name="matmul"):
    if not isinstance(xs, (tuple, list)):
        xs = (xs,)
    n_x = len(xs)
    T = xs[0].shape[0]
    K = sum(x.shape[1] for x in xs)
    n_out = w.shape[1] if n_out is None else n_out
    tm = min(tm, T)
    tn = min(tn, n_out)
    if n_x == 2:
        assert tk is None and xs[1].shape[1] == xs[0].shape[1]
    tk = K if tk is None else tk
    nk = K // tk
    assert K == nk * tk and T % tm == 0 and n_out % tn == 0
    x_buffers = 2
    if n_x == 2:
        x_specs = [pl.BlockSpec((tm, K // 2), lambda i, j, k: (i, 0)) for _ in xs]
    elif nk == 1 and single_buffer_x:
        x_buffers = 1
        x_specs = [pl.BlockSpec((tm, tk), lambda i, j, k: (i, k), pipeline_mode=pl.Buffered(1))]
    else:
        x_specs = [pl.BlockSpec((tm, tk), lambda i, j, k: (i, k))]
    in_specs = x_specs + [pl.BlockSpec((tk, tn), lambda i, j, k: (k, j))]
    args = list(xs) + [w]
    if res is not None:
        in_specs.append(pl.BlockSpec((tm, tn), lambda i, j, k: (i, j)))
        args.append(res)
    scratch = [pltpu.VMEM((tm, tn), F32)] if nk > 1 else []
    vmem = (x_buffers * _nbytes((tm, tk), xs[0].dtype) + 2 * _nbytes((tk, tn), F32) + _nbytes((tk, tn), BF16)
            + (2 * _nbytes((tm, tn), F32) if res is not None else 0)
            + 2 * _nbytes((tm, tn), out_dtype) + 2 * _nbytes((tm, tn), F32))
    return pl.pallas_call(
        functools.partial(_mm_kernel, nk=nk, n_x=n_x, has_res=res is not None),
        grid=(T // tm, n_out // tn, nk),
        in_specs=in_specs,
        out_specs=pl.BlockSpec((tm, tn), lambda i, j, k: (i, j)),
        out_shape=jax.ShapeDtypeStruct((T, n_out), out_dtype),
        scratch_shapes=scratch,
        compiler_params=_params(("parallel", "parallel", "arbitrary"), vmem),
        name=name,
    )(*args)


def _rope_tables(positions, dim):
    half = dim // 2
    inv_freq = ROPE_THETA ** (-jnp.arange(0, dim, 2, dtype=F32) / dim)
    ang = positions.astype(F32)[..., None] * inv_freq
    cos, sin = jnp.cos(ang), jnp.sin(ang)
    B, S = positions.shape
    ones = jnp.ones((B, S, LANES - dim), F32)
    zeros = jnp.zeros((B, S, LANES - dim), F32)
    zh = jnp.zeros((B, S, half), F32)
    c = jnp.concatenate([cos, cos, ones], axis=-1)
    sa = jnp.concatenate([-sin, zh, zeros], axis=-1)
    sb = jnp.concatenate([zh, sin, zeros], axis=-1)
    return [t.reshape(B * S, LANES) for t in (c, sa, sb)]


def _rope_lanes(y, c, sa, sb, half):
    return y * c + pltpu.roll(y, LANES - half, 1) * sa + pltpu.roll(y, half, 1) * sb


def _mixer_a_kernel(q_ref, k_ref, v_ref, c_ref, sa_ref, sb_ref, gq_ref, gk_ref, o_ref,
                    qs, ks, *state, seq):
    scale = HEAD_DIM ** -0.5 * LOG2_E
    c, sa, sb = c_ref[...], sa_ref[...], sb_ref[...]

    def prep(x, g):
        y = x * lax.rsqrt(jnp.mean(x * x, axis=-1, keepdims=True) + EPS) * g
        return _rope_lanes(y, c, sa, sb, ROT_DIM // 2)

    qs[...] = prep(q_ref[...], gq_ref[...])
    ks[...] = prep(k_ref[...], gk_ref[...])
    qi = lax.broadcasted_iota(jnp.int32, (BLOCK, 1), 0)
    kj = lax.broadcasted_iota(jnp.int32, (1, BLOCK), 1)
    trans_b = (((1,), (1,)), ((), ()))
    ones = jnp.ones((BLOCK, HEAD_DIM), BF16)

    for bi, (window, dil) in enumerate(DILATED_PAIRS):
        acc, m_sc, l_sc = state[3 * bi:3 * bi + 3]
        n_back = window // dil
        nb = seq // dil // BLOCK
        assert n_back <= BLOCK and nb * dil * BLOCK == seq
        cur_ok = (qi >= kj) & (qi - kj <= n_back)
        prev_ok = qi + BLOCK - kj <= n_back

        def rows_at(r, n, dil=dil):
            if dil == 1:
                return pl.ds(BLOCK * n, BLOCK)
            return pl.ds(r + dil * BLOCK * n, BLOCK, stride=dil)

        blocks = [(rows_at(r, n), rows_at(r, n - 1) if n > 0 else None) for r in range(dil) for n in range(nb)]
        for first in range(0, len(blocks), MIXER_A_GROUP):
            group = blocks[first:first + MIXER_A_GROUP]
            s_cur, s_prev = [], []
            for rows, prev in group:
                qb = qs[rows, :].astype(BF16)
                s_c = lax.dot_general(qb, ks[rows, :].astype(BF16), trans_b, preferred_element_type=F32) * scale
                s_cur.append(jnp.where(cur_ok, s_c, NEG_INF))
                if prev is None:
                    s_prev.append(None)
                else:
                    s_p = lax.dot_general(qb, ks[prev, :].astype(BF16), trans_b,
                                          preferred_element_type=F32) * scale
                    s_prev.append(jnp.where(prev_ok, s_p, NEG_INF))
            stats = []
            for s_c, s_p in zip(s_cur, s_prev):
                m = jnp.max(s_c, axis=-1, keepdims=True)
                if s_p is not None:
                    m = jnp.maximum(m, jnp.max(s_p, axis=-1, keepdims=True))
                p_c = jnp.exp2(s_c - m).astype(BF16)
                p_p = None if s_p is None else jnp.exp2(s_p - m).astype(BF16)
                stats.append((m, p_c, p_p))
            for (rows, prev), (m, p_c, p_p) in zip(group, stats):
                v_ext = jnp.concatenate([v_ref[rows, :].astype(BF16), ones], axis=1)
                o = jnp.dot(p_c, v_ext, preferred_element_type=F32)
                if p_p is not None:
                    v_ext = jnp.concatenate([v_ref[prev, :].astype(BF16), ones], axis=1)
                    o = o + jnp.dot(p_p, v_ext, preferred_element_type=F32)
                acc[rows, :] = o[:, :HEAD_DIM]
                l_sc[rows, :] = o[:, HEAD_DIM:]
                m_sc[rows, :] = jnp.broadcast_to(m, (BLOCK, HEAD_DIM))

    n_br = len(DILATED_PAIRS)
    m_all = state[1][...]
    for bi in range(1, n_br):
        m_all = jnp.maximum(m_all, state[3 * bi + 1][...])
    num = jnp.zeros(o_ref.shape, F32)
    den = jnp.zeros(o_ref.shape, F32)
    for bi in range(n_br):
        w = jnp.exp2(state[3 * bi + 1][...] - m_all)
        num = num + w * state[3 * bi][...]
        den = den + w * state[3 * bi + 2][...]
    o_ref[...] = (num / den).astype(o_ref.dtype)


def mixer_a(proj, tabs, gq, gk, *, batch, seq):
    T = batch * seq
    blk = (seq, HEAD_DIM)
    head = lambda off: pl.BlockSpec(blk, lambda b, h: (b, off + h))
    tab = pl.BlockSpec(blk, lambda b, h: (b, 0))
    gspec = pl.BlockSpec((1, HEAD_DIM), lambda b, h: (0, 0))
    return pl.pallas_call(
        functools.partial(_mixer_a_kernel, seq=seq),
        grid=(batch, A_HEADS),
        in_specs=[head(0), head(A_HEADS), head(2 * A_HEADS), tab, tab, tab, gspec, gspec],
        out_specs=pl.BlockSpec(blk, lambda b, h: (b, h)),
        out_shape=jax.ShapeDtypeStruct((T, A_HEADS * HEAD_DIM), BF16),
        scratch_shapes=[pltpu.VMEM(blk, F32) for _ in range(2 + 3 * len(DILATED_PAIRS))],
        compiler_params=_params(("parallel", "parallel"), 30 * _nbytes(blk, F32)),
        name="mixer_a",
    )(proj, proj, proj, *tabs, gq.reshape(1, HEAD_DIM), gk.reshape(1, HEAD_DIM))


def _mla_prep_kernel(qn_ref, qr_ref, kn_ref, v_ref, kr_ref, c_ref, sa_ref, sb_ref,
                     gqn_ref, gqr_ref, gkn_ref, gkr_ref, qt_ref, kf_ref, vb_ref):
    c, sa, sb = c_ref[...], sa_ref[...], sb_ref[...]
    half = ROPE_DIM // 2
    qn, qr = qn_ref[...], qr_ref[...]
    ss = jnp.sum(qn * qn, axis=-1, keepdims=True) + jnp.sum(qr * qr, axis=-1, keepdims=True)
    inv = lax.rsqrt(ss * (1.0 / QK_DIM) + EPS) * MLA_Q_SCALE
    qf = jnp.concatenate([qn * inv * gqn_ref[...], _rope_lanes(qr * inv * gqr_ref[...], c, sa, sb, half)], axis=1)
    qt_ref[...] = qf.T.astype(qt_ref.dtype)
    kn = kn_ref[...]
    kn = kn * lax.rsqrt(jnp.mean(kn * kn, axis=-1, keepdims=True) + EPS) * gkn_ref[...]
    kr = kr_ref[...]
    kr = kr * lax.rsqrt(jnp.sum(kr * kr, axis=-1, keepdims=True) * (1.0 / ROPE_DIM) + EPS) * gkr_ref[...]
    kf_ref[:, :NOPE_DIM] = kn.astype(kf_ref.dtype)
    kf_ref[:, NOPE_DIM:] = _rope_lanes(kr, c, sa, sb, half).astype(kf_ref.dtype)
    vb_ref[...] = v_ref[...].astype(vb_ref.dtype)


def mla_prep(q_up, kv_up, krp, tabs, gqn, gqr, gkn, gkr, *, batch, seq, tm=512):
    T = q_up.shape[0]
    H = B_HEADS
    per_seq = seq // tm
    blk = lambda f: pl.BlockSpec((tm, LANES), f)
    g = pl.BlockSpec((1, LANES), lambda i, h: (0, 0))
    return pl.pallas_call(
        _mla_prep_kernel,
        grid=(T // tm, H),
        in_specs=[blk(lambda i, h: (i, h)), blk(lambda i, h: (i, H + h)),
                  blk(lambda i, h: (i, 2 * h)), blk(lambda i, h: (i, 2 * h + 1)),
                  blk(lambda i, h: (i, 0)), blk(lambda i, h: (i, 0)), blk(lambda i, h: (i, 0)),
                  blk(lambda i, h: (i, 0)), g, g, g, g],
        out_specs=[pl.BlockSpec((None, None, 2 * LANES, tm), lambda i, h: (i // per_seq, h, 0, i % per_seq)),
                   pl.BlockSpec((tm, 2 * LANES), lambda i, h: (i, h)),
                   pl.BlockSpec((tm, LANES), lambda i, h: (i, h))],
        out_shape=[jax.ShapeDtypeStruct((batch, H, 2 * LANES, seq), BF16),
                   jax.ShapeDtypeStruct((T, H * 2 * LANES), BF16),
                   jax.ShapeDtypeStruct((T, H * LANES), BF16)],
        compiler_params=_params(("parallel", "parallel"), 40 * _nbytes((tm, LANES), F32)),
        name="mla_prep",
    )(q_up, q_up, kv_up, kv_up, krp, *tabs, gqn, gqr, gkn, gkr)


def _causal_attn_kernel(qt_ref, k_ref, v_ref, o_ref, s_sc, acc, *, tq, hg, dq, dv):
    qi = pl.program_id(2)
    key = lax.broadcasted_iota(jnp.int32, (tq, tq), 0)
    qry = lax.broadcasted_iota(jnp.int32, (tq, tq), 1)
    visible = key <= qry
    ones = jnp.ones((tq, dv), BF16)
    contract_rows = (((0,), (0,)), ((), ()))

    heads = range(hg)

    def scores(j, diagonal):
        rows = pl.ds(pl.multiple_of(j * tq, tq), tq)
        s = [jnp.dot(k_ref[rows, g * dq:(g + 1) * dq], qt_ref[g], preferred_element_type=F32) for g in heads]
        if diagonal:
            s = [jnp.where(visible, sg, NEG_INF) for sg in s]
        for g in heads:
            s_sc[g, j] = s[g]
        return s

    def col_max(j, ms):
        s = scores(j, False)
        return tuple(jnp.maximum(ms[g], jnp.max(s[g], axis=0, keepdims=True)) for g in heads)

    ms = lax.fori_loop(0, qi, col_max, tuple(jnp.max(sg, axis=0, keepdims=True) for sg in scores(qi, True)))

    def weighted(j):
        rows = pl.ds(pl.multiple_of(j * tq, tq), tq)
        p = [jnp.exp2(s_sc[g, j] - ms[g]).astype(BF16) for g in heads]
        return [lax.dot_general(p[g], jnp.concatenate([v_ref[rows, g * dv:(g + 1) * dv], ones], axis=1),
                                contract_rows, preferred_element_type=F32) for g in heads]

    for g, w in enumerate(weighted(qi)):
        acc[g] = w

    def below(j, carry):
        for g, w in enumerate(weighted(j)):
            acc[g] += w
        return carry

    lax.fori_loop(0, qi, below, 0)
    for g in heads:
        a = acc[g]
        o_ref[:, g * dv:(g + 1) * dv] = (a[:, :dv] / a[:, dv:]).astype(o_ref.dtype)


def mla_attention(qt, kf, vb, *, batch, seq, tq=512, hg=4):
    T = batch * seq
    nq = seq // tq
    dq = qt.shape[2]
    dv = vb.shape[1] // B_HEADS
    assert B_HEADS % hg == 0
    return pl.pallas_call(
        functools.partial(_causal_attn_kernel, tq=tq, hg=hg, dq=dq, dv=dv),
        grid=(batch, B_HEADS // hg, nq),
        in_specs=[pl.BlockSpec((None, hg, dq, tq), lambda b, h, i: (b, h, 0, i)),
                  pl.BlockSpec((seq, hg * dq), lambda b, h, i: (b, h)),
                  pl.BlockSpec((seq, hg * dv), lambda b, h, i: (b, h))],
        out_specs=pl.BlockSpec((tq, hg * dv), lambda b, h, i: (b * nq + i, h)),
        out_shape=jax.ShapeDtypeStruct((T, B_HEADS * dv), BF16),
        scratch_shapes=[pltpu.VMEM((hg, nq, tq, tq), F32), pltpu.VMEM((hg, tq, 2 * dv), F32)],
        compiler_params=_params(("parallel", "parallel", "parallel"),
                                _nbytes((hg, nq, tq, tq), F32) + 2 * _nbytes((seq, hg * (dq + dv)), BF16)
                                + 8 * hg * _nbytes((tq, tq), F32)),
        name="mla_attention",
    )(qt, kf, vb)


STICK_EXHAUSTED_LOG = -110.0


def _stickbreak_kernel(q_ref, k_ref, v_ref, o_ref, qt_sc, c_sc, acc, *, tq, hg, scale):
    qi = pl.program_id(2)
    c_sc[...] = jnp.zeros(c_sc.shape, F32)
    acc[...] = jnp.zeros(acc.shape, F32)
    for g in range(hg):
        qt_sc[g] = q_ref[:, g * HEAD_DIM:(g + 1) * HEAD_DIM].astype(F32).T.astype(BF16)
    key = lax.broadcasted_iota(jnp.int32, (tq, tq), 0)
    qry = lax.broadcasted_iota(jnp.int32, (tq, tq), 1)
    after = (qry > key).astype(BF16)
    earlier = key < qry
    contract_rows = (((0,), (0,)), ((), ()))

    def chunk(j, diagonal):
        rows = pl.ds(pl.multiple_of(j * tq, tq), tq)
        heads = range(hg)
        cols = [slice(g * HEAD_DIM, (g + 1) * HEAD_DIM) for g in heads]
        z = [jnp.dot(k_ref[rows, cols[g]], qt_sc[g], preferred_element_type=F32) * (scale * LOG2_E) for g in heads]
        log_beta, parts = [], []
        for g in heads:
            softplus = jnp.maximum(z[g], 0.0) + jnp.log2(1.0 + jnp.exp2(-jnp.abs(z[g])))
            log_keep = -softplus
            if diagonal:
                log_keep = jnp.where(earlier, log_keep, 0.0)
            hi = log_keep.astype(BF16)
            parts += [hi, (log_keep - hi.astype(F32)).astype(BF16)]
            log_beta.append(z[g] - softplus + c_sc[g])
            c_sc[g] += jnp.sum(log_keep, axis=0, keepdims=True)
        within = jnp.dot(after, jnp.concatenate(parts, axis=1), preferred_element_type=F32)
        for g in heads:
            later = within[:, 2 * g * tq:(2 * g + 1) * tq] + within[:, (2 * g + 1) * tq:(2 * g + 2) * tq]
            a = jnp.exp2(log_beta[g] + later)
            if diagonal:
                a = jnp.where(earlier, a, 0.0)
            acc[:, cols[g]] += lax.dot_general(a.astype(BF16), v_ref[rows, cols[g]], contract_rows,
                                               preferred_element_type=F32)

    def stick_left():
        return jnp.max(c_sc[...]) > STICK_EXHAUSTED_LOG * LOG2_E

    chunk(qi, True)

    def more(state):
        j, left = state
        return (j >= 0) & left

    def step(state):
        j, _ = state
        chunk(j, False)
        return j - 1, stick_left()

    lax.while_loop(more, step, (qi - 1, stick_left()))
    o_ref[...] = acc[...].astype(o_ref.dtype)


def stickbreak_attention(qkv, *, batch, seq, heads, tq=256, hg=4):
    T = batch * seq
    nq = seq // tq
    ng = heads // hg
    assert heads == ng * hg
    blk = hg * HEAD_DIM
    return pl.pallas_call(
        functools.partial(_stickbreak_kernel, tq=tq, hg=hg, scale=HEAD_DIM ** -0.5),
        grid=(batch, ng, nq),
        in_specs=[pl.BlockSpec((tq, blk), lambda b, h, i: (b * nq + i, h)),
                  pl.BlockSpec((seq, blk), lambda b, h, i: (b, ng + h)),
                  pl.BlockSpec((seq, blk), lambda b, h, i: (b, 2 * ng + h))],
        out_specs=pl.BlockSpec((tq, blk), lambda b, h, i: (b * nq + i, h)),
        out_shape=jax.ShapeDtypeStruct((T, heads * HEAD_DIM), BF16),
        scratch_shapes=[pltpu.VMEM((hg, HEAD_DIM, tq), BF16), pltpu.VMEM((hg, 1, tq), F32),
                        pltpu.VMEM((tq, blk), F32)],
        compiler_params=_params(("parallel", "parallel", "parallel"), 0),
        name="stickbreak",
    )(qkv, qkv, qkv)


def _memkv_kernel(kv_ref, g_ref, mk_ref, mv_ref):
    for h in range(X_HEADS):
        cols = slice(h * HEAD_DIM, (h + 1) * HEAD_DIM)
        k = kv_ref[:, cols]
        k = k * lax.rsqrt(jnp.mean(k * k, axis=-1, keepdims=True) + EPS) * g_ref[...]
        mk_ref[:, cols] = k.astype(mk_ref.dtype)
    mv_ref[...] = kv_ref[:, X_HEADS * HEAD_DIM:].astype(mv_ref.dtype)


def memkv_post(kv, g):
    M = kv.shape[0]
    xd = X_HEADS * HEAD_DIM
    return pl.pallas_call(
        _memkv_kernel,
        grid=(1,),
        in_specs=[pl.BlockSpec((M, 2 * xd), lambda i: (0, 0)), pl.BlockSpec((1, HEAD_DIM), lambda i: (0, 0))],
        out_specs=[pl.BlockSpec((M, xd), lambda i: (0, 0)), pl.BlockSpec((M, xd), lambda i: (0, 0))],
        out_shape=[jax.ShapeDtypeStruct((M, xd), BF16), jax.ShapeDtypeStruct((M, xd), BF16)],
        compiler_params=_params(("arbitrary",), 0),
        name="memkv_post",
    )(kv, g.reshape(1, HEAD_DIM))


def _xattn_block_kernel(x_ref, gx_ref, wq_ref, gq_ref, mk_ref, mv_ref, wo_ref, *rest, emit_norm):
    if emit_norm:
        gn_ref, o_ref, hn_ref = rest
    else:
        (o_ref,) = rest
    scale = HEAD_DIM ** -0.5
    x = x_ref[...]
    h = (x * lax.rsqrt(jnp.mean(x * x, axis=-1, keepdims=True) + EPS) * gx_ref[...]).astype(BF16)
    q_all = jnp.dot(h, wq_ref[...], preferred_element_type=F32)
    heads = []
    for hd in range(X_HEADS):
        cols = slice(hd * HEAD_DIM, (hd + 1) * HEAD_DIM)
        q = q_all[:, cols]
        q = (q * lax.rsqrt(jnp.mean(q * q, axis=-1, keepdims=True) + EPS) * gq_ref[...]).astype(BF16)
        s = lax.dot_general(q, mk_ref[:, cols], (((1,), (1,)), ((), ())), preferred_element_type=F32) * scale
        p = jnp.exp(s - jnp.max(s, axis=-1, keepdims=True))
        p = p / jnp.sum(p, axis=-1, keepdims=True)
        heads.append(jnp.dot(p.astype(BF16), mv_ref[:, cols], preferred_element_type=F32).astype(BF16))
    y = x + jnp.dot(jnp.concatenate(heads, axis=1), wo_ref[...], preferred_element_type=F32)
    o_ref[...] = y
    if emit_norm:
        hn_ref[...] = (y * lax.rsqrt(jnp.mean(y * y, axis=-1, keepdims=True) + EPS) * gn_ref[...]).astype(hn_ref.dtype)


def cross_attention_block(x, mk, mv, g_x, w_xq, g_xq, w_xo, g_next=None, *, seq, mem_len, tm=256):
    T, D = x.shape
    xd = X_HEADS * HEAD_DIM
    per_seq = seq // tm
    emit_norm = g_next is not None
    row = lambda w: pl.BlockSpec((1, w), lambda i: (0, 0))
    tok = pl.BlockSpec((tm, D), lambda i: (i, 0))
    in_specs = [tok, row(D), pl.BlockSpec((D, xd), lambda i: (0, 0)), row(HEAD_DIM),
                pl.BlockSpec((mem_len, xd), lambda i: (i // per_seq, 0)),
                pl.BlockSpec((mem_len, xd), lambda i: (i // per_seq, 0)),
                pl.BlockSpec((xd, D), lambda i: (0, 0))]
    args = [x, g_x.reshape(1, D), w_xq.astype(BF16), g_xq.reshape(1, HEAD_DIM), mk, mv, w_xo.astype(BF16)]
    out_specs, out_shape = tok, jax.ShapeDtypeStruct((T, D), F32)
    if emit_norm:
        in_specs.append(row(D))
        args.append(g_next.reshape(1, D))
        out_specs = [tok, tok]
        out_shape = [out_shape, jax.ShapeDtypeStruct((T, D), BF16)]
    vmem = 10 * _nbytes((tm, D), F32) + 4 * _nbytes((D, xd), BF16)
    return pl.pallas_call(
        functools.partial(_xattn_block_kernel, emit_norm=emit_norm),
        grid=(T // tm,),
        in_specs=in_specs,
        out_specs=out_specs,
        out_shape=out_shape,
        compiler_params=_params(("parallel",), vmem),
        name="cross_attention",
    )(*args)


def _gateup_kernel(x_ref, wg_ref, wu_ref, o_ref):
    x = x_ref[...]
    g = jnp.dot(x, wg_ref[...].astype(BF16), preferred_element_type=F32)
    u = jnp.dot(x, wu_ref[...].astype(BF16), preferred_element_type=F32)
    o_ref[...] = (g * jax.nn.sigmoid(g) * u).astype(o_ref.dtype)


def swiglu_gateup(x, wg, wu, *, tm=1024, tf=512):
    T, K = x.shape
    F = wg.shape[1]
    vmem = (_nbytes((tm, K), BF16) + 4 * _nbytes((K, tf), F32) + 2 * _nbytes((K, tf), BF16)
            + 6 * _nbytes((tm, tf), F32))
    return pl.pallas_call(
        _gateup_kernel,
        grid=(T // tm, F // tf),
        in_specs=[pl.BlockSpec((tm, K), lambda i, j: (i, 0), pipeline_mode=pl.Buffered(1)),
                  pl.BlockSpec((K, tf), lambda i, j: (0, j)),
                  pl.BlockSpec((K, tf), lambda i, j: (0, j))],
        out_specs=pl.BlockSpec((tm, tf), lambda i, j: (i, j)),
        out_shape=jax.ShapeDtypeStruct((T, F), BF16),
        compiler_params=_params(("parallel", "parallel"), vmem),
        name="swiglu_gateup",
    )(x, wg, wu)


def _router_kernel(x_ref, g_ref, w_ref, b_ref, idx_ref, gw_ref):
    x = x_ref[...]
    h = x * lax.rsqrt(jnp.mean(x * x, axis=-1, keepdims=True) + EPS) * g_ref[...]
    logits = jnp.dot(h, w_ref[...], preferred_element_type=F32, precision=lax.Precision.HIGHEST) + b_ref[...]
    lane = lax.broadcasted_iota(jnp.int32, logits.shape, 1)
    logits = jnp.where(lane < N_EXPERTS, logits, -jnp.inf)
    m1 = jnp.max(logits, axis=-1, keepdims=True)
    i1 = jnp.min(jnp.where(logits == m1, lane, LANES), axis=-1, keepdims=True)
    rest = jnp.where(lane == i1, -jnp.inf, logits)
    m2 = jnp.max(rest, axis=-1, keepdims=True)
    i2 = jnp.min(jnp.where(rest == m2, lane, LANES), axis=-1, keepdims=True)
    e = jnp.exp(m2 - m1)
    w1 = 1.0 / (1.0 + e)
    w2 = e / (1.0 + e)
    idx_ref[...] = jnp.where(lane == 0, i1, jnp.where(lane == 1, i2, 0))
    gw_ref[...] = jnp.where(lane == 0, w1, jnp.where(lane == 1, w2, 0.0))


def moe_router(x, g, w_router, b_router, *, tm=256):
    T, D = x.shape
    E = w_router.shape[1]
    w = jnp.pad(w_router, ((0, 0), (0, LANES - E)))
    b = jnp.pad(b_router, (0, LANES - E)).reshape(1, LANES)
    idx, gw = pl.pallas_call(
        _router_kernel,
        grid=(T // tm,),
        in_specs=[pl.BlockSpec((tm, D), lambda i: (i, 0)), pl.BlockSpec((1, D), lambda i: (0, 0)),
                  pl.BlockSpec((D, LANES), lambda i: (0, 0)), pl.BlockSpec((1, LANES), lambda i: (0, 0))],
        out_specs=[pl.BlockSpec((tm, LANES), lambda i: (i, 0)), pl.BlockSpec((tm, LANES), lambda i: (i, 0))],
        out_shape=[jax.ShapeDtypeStruct((T, LANES), jnp.int32), jax.ShapeDtypeStruct((T, LANES), F32)],
        compiler_params=_params(("parallel",), 8 * _nbytes((tm, D), F32)),
        name="moe_router",
    )(x, g.reshape(1, D), w, b)
    return idx[:, :TOP_K], gw[:, :TOP_K]


def _moe_dispatch(idx, tm):
    T = idx.shape[0]
    A = T * TOP_K
    e_flat = idx.reshape(A)
    onehot = (e_flat[:, None] == jnp.arange(N_EXPERTS, dtype=jnp.int32)[None, :]).astype(jnp.int32)
    csum = jnp.cumsum(onehot, axis=0)
    pos_in = jnp.sum(csum * onehot, axis=1) - 1
    counts = csum[-1]
    padded = ((counts + tm - 1) // tm) * tm
    gend = jnp.cumsum(padded)
    gstart = gend - padded
    dest = (jnp.sum(onehot * gstart[None, :], axis=1) + pos_in).astype(jnp.int32)
    n_rows = A + N_EXPERTS * tm
    row_tok = jnp.zeros((n_rows,), jnp.int32).at[dest].set(jnp.arange(A, dtype=jnp.int32) // TOP_K)
    nb = n_rows // tm
    n_used = (gend[-1] // tm).astype(jnp.int32)
    blk = jnp.arange(nb, dtype=jnp.int32)
    be = jnp.sum((blk[:, None] * tm >= gend[None, :]).astype(jnp.int32), axis=1)
    be = jnp.minimum(be, N_EXPERTS - 1)
    be = jnp.where(blk < n_used, be, be[jnp.maximum(n_used - 1, 0)]).astype(jnp.int32)
    return dest, row_tok, be, n_used.reshape(1)


def _moe_gather_kernel(tok_ref, nu_ref, x_hbm, g_ref, o_ref, buf, sem, *, rows):
    i = pl.program_id(0)
    n_used = nu_ref[0]

    def row_copy(slot, r, src_row):
        return pltpu.make_async_copy(x_hbm.at[pl.ds(src_row, 1)], buf.at[slot, pl.ds(r, 1)], sem.at[slot])

    def fetch(block, slot):
        def issue(r, c):
            row_copy(slot, r, tok_ref[block * rows + r]).start()
            return c

        lax.fori_loop(0, rows, issue, 0, unroll=8)

    @pl.when(i == 0)
    def _():
        fetch(0, 0)

    @pl.when(i + 1 < n_used)
    def _():
        fetch(i + 1, (i + 1) % 2)

    @pl.when(i < n_used)
    def _():
        slot = i % 2

        def drain(r, c):
            row_copy(slot, r, 0).wait()
            return c

        lax.fori_loop(0, rows, drain, 0, unroll=8)
        x = buf[slot]
        o_ref[...] = (x * lax.rsqrt(jnp.mean(x * x, axis=-1, keepdims=True) + EPS) * g_ref[...]).astype(o_ref.dtype)

    @pl.when(i >= n_used)
    def _():
        o_ref[...] = jnp.zeros(o_ref.shape, o_ref.dtype)


def moe_gather_norm(x, g, row_tok, n_used, *, tm, rows=128):
    T, D = x.shape
    n_rows = row_tok.shape[0]
    per = tm // rows
    return pl.pallas_call(
        functools.partial(_moe_gather_kernel, rows=rows),
        grid_spec=pltpu.PrefetchScalarGridSpec(
            num_scalar_prefetch=2,
            grid=(n_rows // rows,),
            in_specs=[pl.BlockSpec(memory_space=pl.ANY), pl.BlockSpec((1, D), lambda i, tok, nu: (0, 0))],
            out_specs=pl.BlockSpec((rows, D), lambda i, tok, nu: (i, 0)),
            scratch_shapes=[pltpu.VMEM((2, rows, D), F32), pltpu.SemaphoreType.DMA((2,))]),
        out_shape=jax.ShapeDtypeStruct((n_rows, D), BF16),
        compiler_params=_params(("arbitrary",), 10 * _nbytes((rows, D), F32)),
        name="moe_gather",
    )(row_tok, n_used * per, x, g.reshape(1, D))


def _moe_gateup_kernel(be_ref, nu_ref, x_ref, wg_ref, wu_ref, o_ref, wgb, wub):
    i = pl.program_id(1)
    changed = (i == 0) | (be_ref[i] != be_ref[jnp.maximum(i - 1, 0)])

    @pl.when(changed)
    def _():
        wgb[...] = wg_ref[...].astype(BF16)
        wub[...] = wu_ref[...].astype(BF16)

    @pl.when(i < nu_ref[0])
    def _():
        x = x_ref[...]
        g = jnp.dot(x, wgb[...], preferred_element_type=F32)
        u = jnp.dot(x, wub[...], preferred_element_type=F32)
        o_ref[...] = (g * jax.nn.sigmoid(g) * u).astype(o_ref.dtype)

    @pl.when(i >= nu_ref[0])
    def _():
        o_ref[...] = jnp.zeros(o_ref.shape, o_ref.dtype)


def moe_gateup(xs, wg, wu, be, n_used, *, tm, tf=512):
    n_rows, K = xs.shape
    F = wg.shape[2]
    nb = n_rows // tm
    xmap = lambda j, i, be, nu: (jnp.minimum(i, nu[0] - 1), 0)
    wmap = lambda j, i, be, nu: (be[i], 0, j)
    vmem = (2 * _nbytes((tm, K), BF16) + 4 * _nbytes((K, tf), F32) + 2 * _nbytes((K, tf), BF16)
            + 6 * _nbytes((tm, tf), F32))
    return pl.pallas_call(
        _moe_gateup_kernel,
        grid_spec=pltpu.PrefetchScalarGridSpec(
            num_scalar_prefetch=2,
            grid=(F // tf, nb),
            in_specs=[pl.BlockSpec((tm, K), xmap),
                      pl.BlockSpec((None, K, tf), wmap),
                      pl.BlockSpec((None, K, tf), wmap)],
            out_specs=pl.BlockSpec((tm, tf), lambda j, i, be, nu: (i, j)),
            scratch_shapes=[pltpu.VMEM((K, tf), BF16), pltpu.VMEM((K, tf), BF16)]),
        out_shape=jax.ShapeDtypeStruct((n_rows, F), BF16),
        compiler_params=_params(("arbitrary", "arbitrary"), vmem),
        name="moe_gateup",
    )(be, n_used, xs, wg, wu)


def _moe_down_kernel(be_ref, nu_ref, x_ref, w_ref, o_ref, wb):
    i = pl.program_id(1)
    changed = (i == 0) | (be_ref[i] != be_ref[jnp.maximum(i - 1, 0)])

    @pl.when(changed)
    def _():
        wb[...] = w_ref[...].astype(BF16)

    @pl.when(i < nu_ref[0])
    def _():
        o_ref[...] = jnp.dot(x_ref[...], wb[...], preferred_element_type=F32)

    @pl.when(i >= nu_ref[0])
    def _():
        o_ref[...] = jnp.zeros(o_ref.shape, o_ref.dtype)


def moe_down(hm, wd, be, n_used, *, tm, tn=1024):
    n_rows, K = hm.shape
    N = wd.shape[2]
    tn = min(tn, N)
    nb = n_rows // tm
    xmap = lambda j, i, be, nu: (jnp.minimum(i, nu[0] - 1), 0)
    vmem = (2 * _nbytes((tm, K), BF16) + 2 * _nbytes((K, tn), F32) + _nbytes((K, tn), BF16)
            + 4 * _nbytes((tm, tn), F32))
    return pl.pallas_call(
        _moe_down_kernel,
        grid_spec=pltpu.PrefetchScalarGridSpec(
            num_scalar_prefetch=2,
            grid=(N // tn, nb),
            in_specs=[pl.BlockSpec((tm, K), xmap),
                      pl.BlockSpec((None, K, tn), lambda j, i, be, nu: (be[i], 0, j))],
            out_specs=pl.BlockSpec((tm, tn), lambda j, i, be, nu: (i, j)),
            scratch_shapes=[pltpu.VMEM((K, tn), BF16)]),
        out_shape=jax.ShapeDtypeStruct((n_rows, N), F32),
        compiler_params=_params(("arbitrary", "arbitrary"), vmem),
        name="moe_down",
    )(be, n_used, hm, wd)


def _moe_combine_kernel(pos_ref, x_ref, gw_ref, ys_hbm, o_ref, buf, sem, *, rows):
    i = pl.program_id(0)
    n_blocks = pl.num_programs(0)

    def row_copy(slot, k, r, src_row):
        return pltpu.make_async_copy(ys_hbm.at[pl.ds(src_row, 1)], buf.at[slot, k, pl.ds(r, 1)], sem.at[slot])

    def fetch(block, slot):
        def issue(r, c):
            a = (block * rows + r) * TOP_K
            for k in range(TOP_K):
                row_copy(slot, k, r, pos_ref[a + k]).start()
            return c

        lax.fori_loop(0, rows, issue, 0, unroll=4)

    @pl.when(i == 0)
    def _():
        fetch(0, 0)

    @pl.when(i + 1 < n_blocks)
    def _():
        fetch(i + 1, (i + 1) % 2)

    slot = i % 2

    def drain(r, c):
        for k in range(TOP_K):
            row_copy(slot, k, r, 0).wait()
        return c

    lax.fori_loop(0, rows, drain, 0, unroll=4)
    out = x_ref[...]
    for k in range(TOP_K):
        out = out + gw_ref[:, k:k + 1] * buf[slot, k]
    o_ref[...] = out


def moe_combine(x, ys, dest, gw, *, rows=128):
    T, D = x.shape
    return pl.pallas_call(
        functools.partial(_moe_combine_kernel, rows=rows),
        grid_spec=pltpu.PrefetchScalarGridSpec(
            num_scalar_prefetch=1,
            grid=(T // rows,),
            in_specs=[pl.BlockSpec((rows, D), lambda i, pos: (i, 0)),
                      pl.BlockSpec((rows, TOP_K), lambda i, pos: (i, 0)),
                      pl.BlockSpec(memory_space=pl.ANY)],
            out_specs=pl.BlockSpec((rows, D), lambda i, pos: (i, 0)),
            scratch_shapes=[pltpu.VMEM((2, TOP_K, rows, D), F32), pltpu.SemaphoreType.DMA((2,))]),
        out_shape=jax.ShapeDtypeStruct((T, D), F32),
        compiler_params=_params(("arbitrary",), 12 * _nbytes((rows, D), F32)),
        name="moe_combine",
    )(dest, x, gw, ys)


def moe_block(x, g, w_router, b_router, w_egate, w_eup, w_edown, *, tm=512):
    idx, gw = moe_router(x, g, w_router, b_router)
    dest, row_tok, be, n_used = _moe_dispatch(idx, tm)
    xs = moe_gather_norm(x, g, row_tok, n_used, tm=tm)
    hm = moe_gateup(xs, w_egate, w_eup, be, n_used, tm=tm)
    ys = moe_down(hm, w_edown, be, n_used, tm=tm)
    return moe_combine(x, ys, dest, gw)


def even_mixer_block(x, tabs_a, tabs_b, w_in, ga_q, ga_k, g_cq, w_uq, g_ckv, w_ukv, gb_q, gb_kn, gb_kr,
                     w_o, g_mix, *, batch, seq):
    a_width = A_HEADS * HEAD_DIM
    main_cols = 3 * a_width + Q_LORA + KV_LORA
    h = rmsnorm(x, g_mix)
    proj = matmul(h, w_in, n_out=main_cols, tm=2048, single_buffer_x=True, name="in_proj")
    w_kr = jnp.pad(w_in[:, main_cols:], ((0, 0), (0, LANES - ROPE_DIM)))
    krp = matmul(h, w_kr, name="in_proj_kr")
    o_a = mixer_a(proj, tabs_a, ga_q, ga_k, batch=batch, seq=seq)
    cq = rmsnorm(proj, g_cq, width=Q_LORA, col_block=3 * a_width // Q_LORA)
    ckv = rmsnorm(proj, g_ckv, width=KV_LORA, col_block=(3 * a_width + Q_LORA) // KV_LORA)
    w3 = w_uq.reshape(Q_LORA, B_HEADS, QK_DIM)
    w_uq_p = jnp.concatenate(
        [w3[:, :, :NOPE_DIM].reshape(Q_LORA, B_HEADS * NOPE_DIM),
         jnp.pad(w3[:, :, NOPE_DIM:], ((0, 0), (0, 0), (0, LANES - ROPE_DIM))).reshape(Q_LORA, B_HEADS * LANES)],
        axis=1)
    q_up = matmul(cq, w_uq_p, name="q_up")
    kv_up = matmul(ckv, w_ukv, name="kv_up")
    pad_r = lambda v: jnp.pad(v, (0, LANES - ROPE_DIM)).reshape(1, LANES)
    qt, kf, vb = mla_prep(q_up, kv_up, krp, tabs_b, gb_q[:NOPE_DIM].reshape(1, LANES), pad_r(gb_q[NOPE_DIM:]),
                          gb_kn.reshape(1, LANES), pad_r(gb_kr), batch=batch, seq=seq)
    o_b = mla_attention(qt, kf, vb, batch=batch, seq=seq)
    return matmul((o_a, o_b), w_o, res=x, name="mixer_out")


def kernel(x, mem, positions, g_mem, w_mem_kv, g_mem_k, g_mix, g_x, w_xq, g_xq, w_xo, g_ffn,
           w_in, ga_q, ga_k, g_cq, w_uq, g_ckv, w_ukv, gb_q, gb_kn, gb_kr, w_o_even,
           w_gate, w_up, w_down, w_qkv, w_o_odd, w_router, b_router, w_egate, w_eup, w_edown):
    B, S, D = x.shape
    M = mem.shape[1]
    depth = g_mix.shape[0]
    T = B * S
    xt = x.reshape(T, D)
    tabs_a = _rope_tables(positions, ROT_DIM)
    tabs_b = _rope_tables(positions, ROPE_DIM)
    kv = matmul(rmsnorm(mem.reshape(B * M, D), g_mem), w_mem_kv, tm=B * M, name="mem_kv")
    mk, mv = memkv_post(kv, g_mem_k)
    for layer in range(depth):
        i = layer // 2
        if layer % 2 == 0:
            xt = even_mixer_block(xt, tabs_a, tabs_b, w_in[i], ga_q[i], ga_k[i], g_cq[i], w_uq[i], g_ckv[i],
                                  w_ukv[i], gb_q[i], gb_kn[i], gb_kr[i], w_o_even[i], g_mix[layer],
                                  batch=B, seq=S)
        else:
            h = rmsnorm(xt, g_mix[layer])
            qkv = matmul(h, w_qkv[i], out_dtype=BF16, tm=2048, single_buffer_x=True, name="qkv_proj")
            o = stickbreak_attention(qkv, batch=B, seq=S, heads=D // HEAD_DIM)
            xt = matmul(o, w_o_odd[i], res=xt, name="mixer_out")
        xattn = functools.partial(cross_attention_block, xt, mk, mv, g_x[layer], w_xq[layer], g_xq[layer],
                                  w_xo[layer], seq=S, mem_len=M)
        if layer % 2 == 0:
            xt, h = xattn(g_ffn[layer])
            hm = swiglu_gateup(h, w_gate[i], w_up[i])
            xt = matmul(hm, w_down[i], res=xt, tn=1024, tk=2048, name="swiglu_down")
        else:
            xt = moe_block(xattn(), g_ffn[layer], w_router[i], b_router[i], w_egate[i], w_eup[i], w_edown[i])
    return xt.reshape(B, S, D)
```

```python
import functools

import jax
import jax.numpy as jnp
from jax import lax
from jax.experimental import pallas as pl
from jax.experimental.pallas import tpu as pltpu

F32 = jnp.float32
BF16 = jnp.bfloat16

HEAD_DIM = 128
ROT_DIM = HEAD_DIM // 4
ROPE_THETA = 500000.0
BLOCK = 128
NEG_INF = -1e30
EPS = 1e-6
DILATED_PAIRS = ((128, 1), (512, 4), (2048, 16))
A_HEADS = 16
B_HEADS = 16
Q_LORA = 1536
KV_LORA = 512
NOPE_DIM = 128
ROPE_DIM = 64
QK_DIM = NOPE_DIM + ROPE_DIM
X_HEADS = 4
N_EXPERTS = 8
TOP_K = 2
MIXER_A_GROUP = 8
LOG2_E = 1.4426950408889634
MLA_Q_SCALE = QK_DIM ** -0.5 * LOG2_E

LANES = 128
V7X_VMEM_BYTES = 64 * 1024 * 1024
VMEM_CAP = V7X_VMEM_BYTES - 6 * 1024 * 1024


def _params(semantics, vmem_bytes):
    return pltpu.CompilerParams(dimension_semantics=semantics,
                                vmem_limit_bytes=int(min(max(vmem_bytes, 32 * 1024 * 1024), VMEM_CAP)))


def _nbytes(shape, dtype):
    n = 1
    for s in shape:
        n *= s
    return n * jnp.dtype(dtype).itemsize


def _rmsnorm_kernel(x_ref, g_ref, o_ref):
    x = x_ref[...].astype(F32)
    ms = jnp.mean(x * x, axis=-1, keepdims=True)
    o_ref[...] = (x * lax.rsqrt(ms + EPS) * g_ref[...]).astype(o_ref.dtype)


def rmsnorm(x, g, *, width=None, col_block=0, tm=256, out_dtype=BF16):
    T = x.shape[0]
    width = x.shape[1] if width is None else width
    tm = min(tm, T)
    return pl.pallas_call(
        _rmsnorm_kernel,
        grid=(T // tm,),
        in_specs=[pl.BlockSpec((tm, width), lambda i: (i, col_block)),
                  pl.BlockSpec((1, width), lambda i: (0, 0))],
        out_specs=pl.BlockSpec((tm, width), lambda i: (i, 0)),
        out_shape=jax.ShapeDtypeStruct((T, width), out_dtype),
        compiler_params=_params(("parallel",), 6 * _nbytes((tm, width), F32)),
        name="rmsnorm",
    )(x, g.reshape(1, width).astype(F32))


def _mm_kernel(*refs, nk, n_x, has_res):
    x_refs = refs[:n_x]
    w_ref = refs[n_x]
    rest = refs[n_x + 1:]
    res_ref = rest[0] if has_res else None
    o_ref = rest[1] if has_res else rest[0]
    acc_ref = rest[-1] if nk > 1 else None
    k = pl.program_id(2)
    w = w_ref[...].astype(BF16)

    def finish(part):
        out = part
        if has_res:
            out = out + res_ref[...]
        o_ref[...] = out.astype(o_ref.dtype)

    if n_x == 2:
        half = x_refs[0].shape[1]
        finish(jnp.dot(x_refs[0][...], w[:half], preferred_element_type=F32)
               + jnp.dot(x_refs[1][...], w[half:], preferred_element_type=F32))
    elif nk == 1:
        finish(jnp.dot(x_refs[0][...], w, preferred_element_type=F32))
    else:
        part = jnp.dot(x_refs[0][...], w, preferred_element_type=F32)

        @pl.when(k == 0)
        def _():
            acc_ref[...] = part

        @pl.when((k > 0) & (k < nk - 1))
        def _():
            acc_ref[...] += part

        @pl.when(k == nk - 1)
        def _():
            finish(acc_ref[...] + part)


def matmul(xs, w, *, n_out=None, res=None, out_dtype=F32, tm=1024, tn=512, tk=None, single_buffer_x=False,
           name="matmul"):
    if not isinstance(xs, (tuple, list)):
        xs = (xs,)
    n_x = len(xs)
    T = xs[0].shape[0]
    K = sum(x.shape[1] for x in xs)
    n_out = w.shape[1] if n_out is None else n_out
    tm = min(tm, T)
    tn = min(tn, n_out)
    if n_x == 2:
        assert tk is None and xs[1].shape[1] == xs[0].shape[1]
    tk = K if tk is None else tk
    nk = K // tk
    assert K == nk * tk and T % tm == 0 and n_out % tn == 0
    x_buffers = 2
    if n_x == 2:
        x_specs = [pl.BlockSpec((tm, K // 2), lambda i, j, k: (i, 0)) for _ in xs]
    elif nk == 1 and single_buffer_x:
        x_buffers = 1
        x_specs = [pl.BlockSpec((tm, tk), lambda i, j, k: (i, k), pipeline_mode=pl.Buffered(1))]
    else:
        x_specs = [pl.BlockSpec((tm, tk), lambda i, j, k: (i, k))]
    in_specs = x_specs + [pl.BlockSpec((tk, tn), lambda i, j, k: (k, j))]
    args = list(xs) + [w]
    if res is not None:
        in_specs.append(pl.BlockSpec((tm, tn), lambda i, j, k: (i, j)))
        args.append(res)
    scratch = [pltpu.VMEM((tm, tn), F32)] if nk > 1 else []
    vmem = (x_buffers * _nbytes((tm, tk), xs[0].dtype) + 2 * _nbytes((tk, tn), F32) + _nbytes((tk, tn), BF16)
            + (2 * _nbytes((tm, tn), F32) if res is not None else 0)
            + 2 * _nbytes((tm, tn), out_dtype) + 2 * _nbytes((tm, tn), F32))
    return pl.pallas_call(
        functools.partial(_mm_kernel, nk=nk, n_x=n_x, has_res=res is not None),
        grid=(T // tm, n_out // tn, nk),
        in_specs=in_specs,
        out_specs=pl.BlockSpec((tm, tn), lambda i, j, k: (i, j)),
        out_shape=jax.ShapeDtypeStruct((T, n_out), out_dtype),
        scratch_shapes=scratch,
        compiler_params=_params(("parallel", "parallel", "arbitrary"), vmem),
        name=name,
    )(*args)


def _rope_tables(positions, dim):
    half = dim // 2
    inv_freq = ROPE_THETA ** (-jnp.arange(0, dim, 2, dtype=F32) / dim)
    ang = positions.astype(F32)[..., None] * inv_freq
    cos, sin = jnp.cos(ang), jnp.sin(ang)
    B, S = positions.shape
    ones = jnp.ones((B, S, LANES - dim), F32)
    zeros = jnp.zeros((B, S, LANES - dim), F32)
    zh = jnp.zeros((B, S, half), F32)
    c = jnp.concatenate([cos, cos, ones], axis=-1)
    sa = jnp.concatenate([-sin, zh, zeros], axis=-1)
    sb = jnp.concatenate([zh, sin, zeros], axis=-1)
    return [t.reshape(B * S, LANES) for t in (c, sa, sb)]


def _rope_lanes(y, c, sa, sb, half):
    return y * c + pltpu.roll(y, LANES - half, 1) * sa + pltpu.roll(y, half, 1) * sb


def _mixer_a_kernel(q_ref, k_ref, v_ref, c_ref, sa_ref, sb_ref, gq_ref, gk_ref, o_ref,
                    qs, ks, *state, seq):
    scale = HEAD_DIM ** -0.5 * LOG2_E
    c, sa, sb = c_ref[...], sa_ref[...], sb_ref[...]

    def prep(x, g):
        y = x * lax.rsqrt(jnp.mean(x * x, axis=-1, keepdims=True) + EPS) * g
        return _rope_lanes(y, c, sa, sb, ROT_DIM // 2)

    qs[...] = prep(q_ref[...], gq_ref[...])
    ks[...] = prep(k_ref[...], gk_ref[...])
    qi = lax.broadcasted_iota(jnp.int32, (BLOCK, 1), 0)
    kj = lax.broadcasted_iota(jnp.int32, (1, BLOCK), 1)
    trans_b = (((1,), (1,)), ((), ()))
    ones = jnp.ones((BLOCK, HEAD_DIM), BF16)

    for bi, (window, dil) in enumerate(DILATED_PAIRS):
        acc, m_sc, l_sc = state[3 * bi:3 * bi + 3]
        n_back = window // dil
        nb = seq // dil // BLOCK
        assert n_back <= BLOCK and nb * dil * BLOCK == seq
        cur_ok = (qi >= kj) & (qi - kj <= n_back)
        prev_ok = qi + BLOCK - kj <= n_back

        def rows_at(r, n, dil=dil):
            if dil == 1:
                return pl.ds(BLOCK * n, BLOCK)
            return pl.ds(r + dil * BLOCK * n, BLOCK, stride=dil)

        blocks = [(rows_at(r, n), rows_at(r, n - 1) if n > 0 else None) for r in range(dil) for n in range(nb)]
        for first in range(0, len(blocks), MIXER_A_GROUP):
            group = blocks[first:first + MIXER_A_GROUP]
            s_cur, s_prev = [], []
            for rows, prev in group:
                qb = qs[rows, :].astype(BF16)
                s_c = lax.dot_general(qb, ks[rows, :].astype(BF16), trans_b, preferred_element_type=F32) * scale
                s_cur.append(jnp.where(cur_ok, s_c, NEG_INF))
                if prev is None:
                    s_prev.append(None)
                else:
                    s_p = lax.dot_general(qb, ks[prev, :].astype(BF16), trans_b,
                                          preferred_element_type=F32) * scale
                    s_prev.append(jnp.where(prev_ok, s_p, NEG_INF))
            stats = []
            for s_c, s_p in zip(s_cur, s_prev):
                m = jnp.max(s_c, axis=-1, keepdims=True)
                if s_p is not None:
                    m = jnp.maximum(m, jnp.max(s_p, axis=-1, keepdims=True))
                p_c = jnp.exp2(s_c - m).astype(BF16)
                p_p = None if s_p is None else jnp.exp2(s_p - m).astype(BF16)
                stats.append((m, p_c, p_p))
            for (rows, prev), (m, p_c, p_p) in zip(group, stats):
                v_ext = jnp.concatenate([v_ref[rows, :].astype(BF16), ones], axis=1)
                o = jnp.dot(p_c, v_ext, preferred_element_type=F32)
                if p_p is not None:
                    v_ext = jnp.concatenate([v_ref[prev, :].astype(BF16), ones], axis=1)
                    o = o + jnp.dot(p_p, v_ext, preferred_element_type=F32)
                acc[rows, :] = o[:, :HEAD_DIM]
                l_sc[rows, :] = o[:, HEAD_DIM:]
                m_sc[rows, :] = jnp.broadcast_to(m, (BLOCK, HEAD_DIM))

    n_br = len(DILATED_PAIRS)
    m_all = state[1][...]
    for bi in range(1, n_br):
        m_all = jnp.maximum(m_all, state[3 * bi + 1][...])
    num = jnp.zeros(o_ref.shape, F32)
    den = jnp.zeros(o_ref.shape, F32)
    for bi in range(n_br):
        w = jnp.exp2(state[3 * bi + 1][...] - m_all)
        num = num + w * state[3 * bi][...]
        den = den + w * state[3 * bi + 2][...]
    o_ref[...] = (num / den).astype(o_ref.dtype)


def mixer_a(proj, tabs, gq, gk, *, batch, seq):
    T = batch * seq
    blk = (seq, HEAD_DIM)
    head = lambda off: pl.BlockSpec(blk, lambda b, h: (b, off + h))
    tab = pl.BlockSpec(blk, lambda b, h: (b, 0))
    gspec = pl.BlockSpec((1, HEAD_DIM), lambda b, h: (0, 0))
    return pl.pallas_call(
        functools.partial(_mixer_a_kernel, seq=seq),
        grid=(batch, A_HEADS),
        in_specs=[head(0), head(A_HEADS), head(2 * A_HEADS), tab, tab, tab, gspec, gspec],
        out_specs=pl.BlockSpec(blk, lambda b, h: (b, h)),
        out_shape=jax.ShapeDtypeStruct((T, A_HEADS * HEAD_DIM), BF16),
        scratch_shapes=[pltpu.VMEM(blk, F32) for _ in range(2 + 3 * len(DILATED_PAIRS))],
        compiler_params=_params(("parallel", "parallel"), 30 * _nbytes(blk, F32)),
        name="mixer_a",
    )(proj, proj, proj, *tabs, gq.reshape(1, HEAD_DIM), gk.reshape(1, HEAD_DIM))


def _mla_prep_kernel(qn_ref, qr_ref, kn_ref, v_ref, kr_ref, c_ref, sa_ref, sb_ref,
                     gqn_ref, gqr_ref, gkn_ref, gkr_ref, qt_ref, kf_ref, vb_ref):
    c, sa, sb = c_ref[...], sa_ref[...], sb_ref[...]
    half = ROPE_DIM // 2
    qn, qr = qn_ref[...], qr_ref[...]
    ss = jnp.sum(qn * qn, axis=-1, keepdims=True) + jnp.sum(qr * qr, axis=-1, keepdims=True)
    inv = lax.rsqrt(ss * (1.0 / QK_DIM) + EPS) * MLA_Q_SCALE
    qf = jnp.concatenate([qn * inv * gqn_ref[...], _rope_lanes(qr * inv * gqr_ref[...], c, sa, sb, half)], axis=1)
    qt_ref[...] = qf.T.astype(qt_ref.dtype)
    kn = kn_ref[...]
    kn = kn * lax.rsqrt(jnp.mean(kn * kn, axis=-1, keepdims=True) + EPS) * gkn_ref[...]
    kr = kr_ref[...]
    kr = kr * lax.rsqrt(jnp.sum(kr * kr, axis=-1, keepdims=True) * (1.0 / ROPE_DIM) + EPS) * gkr_ref[...]
    kf_ref[:, :NOPE_DIM] = kn.astype(kf_ref.dtype)
    kf_ref[:, NOPE_DIM:] = _rope_lanes(kr, c, sa, sb, half).astype(kf_ref.dtype)
    vb_ref[...] = v_ref[...].astype(vb_ref.dtype)


def mla_prep(q_up, kv_up, krp, tabs, gqn, gqr, gkn, gkr, *, batch, seq, tm=512):
    T = q_up.shape[0]
    H = B_HEADS
    per_seq = seq // tm
    blk = lambda f: pl.BlockSpec((tm, LANES), f)
    g = pl.BlockSpec((1, LANES), lambda i, h: (0, 0))
    return pl.pallas_call(
        _mla_prep_kernel,
        grid=(T // tm, H),
        in_specs=[blk(lambda i, h: (i, h)), blk(lambda i, h: (i, H + h)),
                  blk(lambda i, h: (i, 2 * h)), blk(lambda i, h: (i, 2 * h + 1)),
                  blk(lambda i, h: (i, 0)), blk(lambda i, h: (i, 0)), blk(lambda i, h: (i, 0)),
                  blk(lambda i, h: (i, 0)), g, g, g, g],
        out_specs=[pl.BlockSpec((None, None, 2 * LANES, tm), lambda i, h: (i // per_seq, h, 0, i % per_seq)),
                   pl.BlockSpec((tm, 2 * LANES), lambda i, h: (i, h)),
                   pl.BlockSpec((tm, LANES), lambda i, h: (i, h))],
        out_shape=[jax.ShapeDtypeStruct((batch, H, 2 * LANES, seq), BF16),
                   jax.ShapeDtypeStruct((T, H * 2 * LANES), BF16),
                   jax.ShapeDtypeStruct((T, H * LANES), BF16)],
        compiler_params=_params(("parallel", "parallel"), 40 * _nbytes((tm, LANES), F32)),
        name="mla_prep",
    )(q_up, q_up, kv_up, kv_up, krp, *tabs, gqn, gqr, gkn, gkr)


def _causal_attn_kernel(qt_ref, k_ref, v_ref, o_ref, s_sc, acc, *, tq, hg, dq, dv):
    qi = pl.program_id(2)
    key = lax.broadcasted_iota(jnp.int32, (tq, tq), 0)
    qry = lax.broadcasted_iota(jnp.int32, (tq, tq), 1)
    visible = key <= qry
    ones = jnp.ones((tq, dv), BF16)
    contract_rows = (((0,), (0,)), ((), ()))

    heads = range(hg)

    def scores(j, diagonal):
        rows = pl.ds(pl.multiple_of(j * tq, tq), tq)
        s = [jnp.dot(k_ref[rows, g * dq:(g + 1) * dq], qt_ref[g], preferred_element_type=F32) for g in heads]
        if diagonal:
            s = [jnp.where(visible, sg, NEG_INF) for sg in s]
        for g in heads:
            s_sc[g, j] = s[g]
        return s

    def col_max(j, ms):
        s = scores(j, False)
        return tuple(jnp.maximum(ms[g], jnp.max(s[g], axis=0, keepdims=True)) for g in heads)

    ms = lax.fori_loop(0, qi, col_max, tuple(jnp.max(sg, axis=0, keepdims=True) for sg in scores(qi, True)))

    def weighted(j):
        rows = pl.ds(pl.multiple_of(j * tq, tq), tq)
        p = [jnp.exp2(s_sc[g, j] - ms[g]).astype(BF16) for g in heads]
        return [lax.dot_general(p[g], jnp.concatenate([v_ref[rows, g * dv:(g + 1) * dv], ones], axis=1),
                                contract_rows, preferred_element_type=F32) for g in heads]

    for g, w in enumerate(weighted(qi)):
        acc[g] = w

    def below(j, carry):
        for g, w in enumerate(weighted(j)):
            acc[g] += w
        return carry

    lax.fori_loop(0, qi, below, 0)
    for g in heads:
        a = acc[g]
        o_ref[:, g * dv:(g + 1) * dv] = (a[:, :dv] / a[:, dv:]).astype(o_ref.dtype)


def mla_attention(qt, kf, vb, *, batch, seq, tq=512, hg=4):
    T = batch * seq
    nq = seq // tq
    dq = qt.shape[2]
    dv = vb.shape[1] // B_HEADS
    assert B_HEADS % hg == 0
    return pl.pallas_call(
        functools.partial(_causal_attn_kernel, tq=tq, hg=hg, dq=dq, dv=dv),
        grid=(batch, B_HEADS // hg, nq),
        in_specs=[pl.BlockSpec((None, hg, dq, tq), lambda b, h, i: (b, h, 0, i)),
                  pl.BlockSpec((seq, hg * dq), lambda b, h, i: (b, h)),
                  pl.BlockSpec((seq, hg * dv), lambda b, h, i: (b, h))],
        out_specs=pl.BlockSpec((tq, hg * dv), lambda b, h, i: (b * nq + i, h)),
        out_shape=jax.ShapeDtypeStruct((T, B_HEADS * dv), BF16),
        scratch_shapes=[pltpu.VMEM((hg, nq, tq, tq), F32), pltpu.VMEM((hg, tq, 2 * dv), F32)],
        compiler_params=_params(("parallel", "parallel", "parallel"),
                                _nbytes((hg, nq, tq, tq), F32) + 2 * _nbytes((seq, hg * (dq + dv)), BF16)
                                + 8 * hg * _nbytes((tq, tq), F32)),
        name="mla_attention",
    )(qt, kf, vb)


STICK_EXHAUSTED_LOG = -110.0


def _stickbreak_kernel(q_ref, k_ref, v_ref, o_ref, qt_sc, c_sc, acc, *, tq, hg, scale):
    qi = pl.program_id(2)
    c_sc[...] = jnp.zeros(c_sc.shape, F32)
    acc[...] = jnp.zeros(acc.shape, F32)
    for g in range(hg):
        qt_sc[g] = q_ref[:, g * HEAD_DIM:(g + 1) * HEAD_DIM].astype(F32).T.astype(BF16)
    key = lax.broadcasted_iota(jnp.int32, (tq, tq), 0)
    qry = lax.broadcasted_iota(jnp.int32, (tq, tq), 1)
    after = (qry > key).astype(BF16)
    earlier = key < qry
    contract_rows = (((0,), (0,)), ((), ()))

    def chunk(j, diagonal):
        rows = pl.ds(pl.multiple_of(j * tq, tq), tq)
        heads = range(hg)
        cols = [slice(g * HEAD_DIM, (g + 1) * HEAD_DIM) for g in heads]
        z = [jnp.dot(k_ref[rows, cols[g]], qt_sc[g], preferred_element_type=F32) * (scale * LOG2_E) for g in heads]
        log_beta, parts = [], []
        for g in heads:
            softplus = jnp.maximum(z[g], 0.0) + jnp.log2(1.0 + jnp.exp2(-jnp.abs(z[g])))
            log_keep = -softplus
            if diagonal:
                log_keep = jnp.where(earlier, log_keep, 0.0)
            hi = log_keep.astype(BF16)
            parts += [hi, (log_keep - hi.astype(F32)).astype(BF16)]
            log_beta.append(z[g] - softplus + c_sc[g])
            c_sc[g] += jnp.sum(log_keep, axis=0, keepdims=True)
        within = jnp.dot(after, jnp.concatenate(parts, axis=1), preferred_element_type=F32)
        for g in heads:
            later = within[:, 2 * g * tq:(2 * g + 1) * tq] + within[:, (2 * g + 1) * tq:(2 * g + 2) * tq]
            a = jnp.exp2(log_beta[g] + later)
            if diagonal:
                a = jnp.where(earlier, a, 0.0)
            acc[:, cols[g]] += lax.dot_general(a.astype(BF16), v_ref[rows, cols[g]], contract_rows,
                                               preferred_element_type=F32)

    def stick_left():
        return jnp.max(c_sc[...]) > STICK_EXHAUSTED_LOG * LOG2_E

    chunk(qi, True)

    def more(state):
        j, left = state
        return (j >= 0) & left

    def step(state):
        j, _ = state
        chunk(j, False)
        return j - 1, stick_left()

    lax.while_loop(more, step, (qi - 1, stick_left()))
    o_ref[...] = acc[...].astype(o_ref.dtype)


def stickbreak_attention(qkv, *, batch, seq, heads, tq=256, hg=4):
    T = batch * seq
    nq = seq // tq
    ng = heads // hg
    assert heads == ng * hg
    blk = hg * HEAD_DIM
    return pl.pallas_call(
        functools.partial(_stickbreak_kernel, tq=tq, hg=hg, scale=HEAD_DIM ** -0.5),
        grid=(batch, ng, nq),
        in_specs=[pl.BlockSpec((tq, blk), lambda b, h, i: (b * nq + i, h)),
                  pl.BlockSpec((seq, blk), lambda b, h, i: (b, ng + h)),
                  pl.BlockSpec((seq, blk), lambda b, h, i: (b, 2 * ng + h))],
        out_specs=pl.BlockSpec((tq, blk), lambda b, h, i: (b * nq + i, h)),
        out_shape=jax.ShapeDtypeStruct((T, heads * HEAD_DIM), BF16),
        scratch_shapes=[pltpu.VMEM((hg, HEAD_DIM, tq), BF16), pltpu.VMEM((hg, 1, tq), F32),
                        pltpu.VMEM((tq, blk), F32)],
        compiler_params=_params(("parallel", "parallel", "parallel"), 0),
        name="stickbreak",
    )(qkv, qkv, qkv)


def _memkv_kernel(kv_ref, g_ref, mk_ref, mv_ref):
    for h in range(X_HEADS):
        cols = slice(h * HEAD_DIM, (h + 1) * HEAD_DIM)
        k = kv_ref[:, cols]
        k = k * lax.rsqrt(jnp.mean(k * k, axis=-1, keepdims=True) + EPS) * g_ref[...]
        mk_ref[:, cols] = k.astype(mk_ref.dtype)
    mv_ref[...] = kv_ref[:, X_HEADS * HEAD_DIM:].astype(mv_ref.dtype)


def memkv_post(kv, g):
    M = kv.shape[0]
    xd = X_HEADS * HEAD_DIM
    return pl.pallas_call(
        _memkv_kernel,
        grid=(1,),
        in_specs=[pl.BlockSpec((M, 2 * xd), lambda i: (0, 0)), pl.BlockSpec((1, HEAD_DIM), lambda i: (0, 0))],
        out_specs=[pl.BlockSpec((M, xd), lambda i: (0, 0)), pl.BlockSpec((M, xd), lambda i: (0, 0))],
        out_shape=[jax.ShapeDtypeStruct((M, xd), BF16), jax.ShapeDtypeStruct((M, xd), BF16)],
        compiler_params=_params(("arbitrary",), 0),
        name="memkv_post",
    )(kv, g.reshape(1, HEAD_DIM))


def _xattn_block_kernel(x_ref, gx_ref, wq_ref, gq_ref, mk_ref, mv_ref, wo_ref, *rest, emit_norm):
    if emit_norm:
        gn_ref, o_ref, hn_ref = rest
    else:
        (o_ref,) = rest
    scale = HEAD_DIM ** -0.5
    x = x_ref[...]
    h = (x * lax.rsqrt(jnp.mean(x * x, axis=-1, keepdims=True) + EPS) * gx_ref[...]).astype(BF16)
    q_all = jnp.dot(h, wq_ref[...], preferred_element_type=F32)
    heads = []
    for hd in range(X_HEADS):
        cols = slice(hd * HEAD_DIM, (hd + 1) * HEAD_DIM)
        q = q_all[:, cols]
        q = (q * lax.rsqrt(jnp.mean(q * q, axis=-1, keepdims=True) + EPS) * gq_ref[...]).astype(BF16)
        s = lax.dot_general(q, mk_ref[:, cols], (((1,), (1,)), ((), ())), preferred_element_type=F32) * scale
        p = jnp.exp(s - jnp.max(s, axis=-1, keepdims=True))
        p = p / jnp.sum(p, axis=-1, keepdims=True)
        heads.append(jnp.dot(p.astype(BF16), mv_ref[:, cols], preferred_element_type=F32).astype(BF16))
    y = x + jnp.dot(jnp.concatenate(heads, axis=1), wo_ref[...], preferred_element_type=F32)
    o_ref[...] = y
    if emit_norm:
        hn_ref[...] = (y * lax.rsqrt(jnp.mean(y * y, axis=-1, keepdims=True) + EPS) * gn_ref[...]).astype(hn_ref.dtype)


def cross_attention_block(x, mk, mv, g_x, w_xq, g_xq, w_xo, g_next=None, *, seq, mem_len, tm=256):
    T, D = x.shape
    xd = X_HEADS * HEAD_DIM
    per_seq = seq // tm
    emit_norm = g_next is not None
    row = lambda w: pl.BlockSpec((1, w), lambda i: (0, 0))
    tok = pl.BlockSpec((tm, D), lambda i: (i, 0))
    in_specs = [tok, row(D), pl.BlockSpec((D, xd), lambda i: (0, 0)), row(HEAD_DIM),
                pl.BlockSpec((mem_len, xd), lambda i: (i // per_seq, 0)),
                pl.BlockSpec((mem_len, xd), lambda i: (i // per_seq, 0)),
                pl.BlockSpec((xd, D), lambda i: (0, 0))]
    args = [x, g_x.reshape(1, D), w_xq.astype(BF16), g_xq.reshape(1, HEAD_DIM), mk, mv, w_xo.astype(BF16)]
    out_specs, out_shape = tok, jax.ShapeDtypeStruct((T, D), F32)
    if emit_norm:
        in_specs.append(row(D))
        args.append(g_next.reshape(1, D))
        out_specs = [tok, tok]
        out_shape = [out_shape, jax.ShapeDtypeStruct((T, D), BF16)]
    vmem = 10 * _nbytes((tm, D), F32) + 4 * _nbytes((D, xd), BF16)
    return pl.pallas_call(
        functools.partial(_xattn_block_kernel, emit_norm=emit_norm),
        grid=(T // tm,),
        in_specs=in_specs,
        out_specs=out_specs,
        out_shape=out_shape,
        compiler_params=_params(("parallel",), vmem),
        name="cross_attention",
    )(*args)


def _gateup_kernel(x_ref, wg_ref, wu_ref, o_ref):
    x = x_ref[...]
    g = jnp.dot(x, wg_ref[...].astype(BF16), preferred_element_type=F32)
    u = jnp.dot(x, wu_ref[...].astype(BF16), preferred_element_type=F32)
    o_ref[...] = (g * jax.nn.sigmoid(g) * u).astype(o_ref.dtype)


def swiglu_gateup(x, wg, wu, *, tm=1024, tf=512):
    T, K = x.shape
    F = wg.shape[1]
    vmem = (_nbytes((tm, K), BF16) + 4 * _nbytes((K, tf), F32) + 2 * _nbytes((K, tf), BF16)
            + 6 * _nbytes((tm, tf), F32))
    return pl.pallas_call(
        _gateup_kernel,
        grid=(T // tm, F // tf),
        in_specs=[pl.BlockSpec((tm, K), lambda i, j: (i, 0), pipeline_mode=pl.Buffered(1)),
                  pl.BlockSpec((K, tf), lambda i, j: (0, j)),
                  pl.BlockSpec((K, tf), lambda i, j: (0, j))],
        out_specs=pl.BlockSpec((tm, tf), lambda i, j: (i, j)),
        out_shape=jax.ShapeDtypeStruct((T, F), BF16),
        compiler_params=_params(("parallel", "parallel"), vmem),
        name="swiglu_gateup",
    )(x, wg, wu)


def _router_kernel(x_ref, g_ref, w_ref, b_ref, idx_ref, gw_ref):
    x = x_ref[...]
    h = x * lax.rsqrt(jnp.mean(x * x, axis=-1, keepdims=True) + EPS) * g_ref[...]
    logits = jnp.dot(h, w_ref[...], preferred_element_type=F32, precision=lax.Precision.HIGHEST) + b_ref[...]
    lane = lax.broadcasted_iota(jnp.int32, logits.shape, 1)
    logits = jnp.where(lane < N_EXPERTS, logits, -jnp.inf)
    m1 = jnp.max(logits, axis=-1, keepdims=True)
    i1 = jnp.min(jnp.where(logits == m1, lane, LANES), axis=-1, keepdims=True)
    rest = jnp.where(lane == i1, -jnp.inf, logits)
    m2 = jnp.max(rest, axis=-1, keepdims=True)
    i2 = jnp.min(jnp.where(rest == m2, lane, LANES), axis=-1, keepdims=True)
    e = jnp.exp(m2 - m1)
    w1 = 1.0 / (1.0 + e)
    w2 = e / (1.0 + e)
    idx_ref[...] = jnp.where(lane == 0, i1, jnp.where(lane == 1, i2, 0))
    gw_ref[...] = jnp.where(lane == 0, w1, jnp.where(lane == 1, w2, 0.0))


def moe_router(x, g, w_router, b_router, *, tm=256):
    T, D = x.shape
    E = w_router.shape[1]
    w = jnp.pad(w_router, ((0, 0), (0, LANES - E)))
    b = jnp.pad(b_router, (0, LANES - E)).reshape(1, LANES)
    idx, gw = pl.pallas_call(
        _router_kernel,
        grid=(T // tm,),
        in_specs=[pl.BlockSpec((tm, D), lambda i: (i, 0)), pl.BlockSpec((1, D), lambda i: (0, 0)),
                  pl.BlockSpec((D, LANES), lambda i: (0, 0)), pl.BlockSpec((1, LANES), lambda i: (0, 0))],
        out_specs=[pl.BlockSpec((tm, LANES), lambda i: (i, 0)), pl.BlockSpec((tm, LANES), lambda i: (i, 0))],
        out_shape=[jax.ShapeDtypeStruct((T, LANES), jnp.int32), jax.ShapeDtypeStruct((T, LANES), F32)],
        compiler_params=_params(("parallel",), 8 * _nbytes((tm, D), F32)),
        name="moe_router",
    )(x, g.reshape(1, D), w, b)
    return idx[:, :TOP_K], gw[:, :TOP_K]


def _moe_dispatch(idx, tm):
    T = idx.shape[0]
    A = T * TOP_K
    e_flat = idx.reshape(A)
    onehot = (e_flat[:, None] == jnp.arange(N_EXPERTS, dtype=jnp.int32)[None, :]).astype(jnp.int32)
    csum = jnp.cumsum(onehot, axis=0)
    pos_in = jnp.sum(csum * onehot, axis=1) - 1
    counts = csum[-1]
    padded = ((counts + tm - 1) // tm) * tm
    gend = jnp.cumsum(padded)
    gstart = gend - padded
    dest = (jnp.sum(onehot * gstart[None, :], axis=1) + pos_in).astype(jnp.int32)
    n_rows = A + N_EXPERTS * tm
    row_tok = jnp.zeros((n_rows,), jnp.int32).at[dest].set(jnp.arange(A, dtype=jnp.int32) // TOP_K)
    nb = n_rows // tm
    n_used = (gend[-1] // tm).astype(jnp.int32)
    blk = jnp.arange(nb, dtype=jnp.int32)
    be = jnp.sum((blk[:, None] * tm >= gend[None, :]).astype(jnp.int32), axis=1)
    be = jnp.minimum(be, N_EXPERTS - 1)
    be = jnp.where(blk < n_used, be, be[jnp.maximum(n_used - 1, 0)]).astype(jnp.int32)
    eid = jnp.arange(N_EXPERTS, dtype=jnp.int32)
    later = (counts[None, :] > 0) & (eid[None, :] > eid[:, None])
    nxt_of = jnp.min(jnp.where(later, eid[None, :], N_EXPERTS), axis=1)
    nxt_of = jnp.where(nxt_of == N_EXPERTS, -1, nxt_of).astype(jnp.int32)
    nxt = jnp.sum((be[:, None] == eid[None, :]) * nxt_of[None, :], axis=1).astype(jnp.int32)
    return dest, row_tok, be, n_used.reshape(1), nxt


def _moe_gather_kernel(tok_ref, nu_ref, x_hbm, g_ref, o_ref, buf, sem, *, rows):
    i = pl.program_id(0)
    n_used = nu_ref[0]

    def row_copy(slot, r, src_row):
        return pltpu.make_async_copy(x_hbm.at[pl.ds(src_row, 1)], buf.at[slot, pl.ds(r, 1)], sem.at[slot])

    def fetch(block, slot):
        def issue(r, c):
            row_copy(slot, r, tok_ref[block * rows + r]).start()
            return c

        lax.fori_loop(0, rows, issue, 0, unroll=8)

    @pl.when(i == 0)
    def _():
        fetch(0, 0)

    @pl.when(i + 1 < n_used)
    def _():
        fetch(i + 1, (i + 1) % 2)

    @pl.when(i < n_used)
    def _():
        slot = i % 2

        def drain(r, c):
            row_copy(slot, r, 0).wait()
            return c

        lax.fori_loop(0, rows, drain, 0, unroll=8)
        x = buf[slot]
        o_ref[...] = (x * lax.rsqrt(jnp.mean(x * x, axis=-1, keepdims=True) + EPS) * g_ref[...]).astype(o_ref.dtype)

    @pl.when(i >= n_used)
    def _():
        o_ref[...] = jnp.zeros(o_ref.shape, o_ref.dtype)


def moe_gather_norm(x, g, row_tok, n_used, *, tm, rows=128):
    T, D = x.shape
    n_rows = row_tok.shape[0]
    per = tm // rows
    return pl.pallas_call(
        functools.partial(_moe_gather_kernel, rows=rows),
        grid_spec=pltpu.PrefetchScalarGridSpec(
            num_scalar_prefetch=2,
            grid=(n_rows // rows,),
            in_specs=[pl.BlockSpec(memory_space=pl.ANY), pl.BlockSpec((1, D), lambda i, tok, nu: (0, 0))],
            out_specs=pl.BlockSpec((rows, D), lambda i, tok, nu: (i, 0)),
            scratch_shapes=[pltpu.VMEM((2, rows, D), F32), pltpu.SemaphoreType.DMA((2,))]),
        out_shape=jax.ShapeDtypeStruct((n_rows, D), BF16),
        compiler_params=_params(("arbitrary",), 10 * _nbytes((rows, D), F32)),
        name="moe_gather",
    )(row_tok, n_used * per, x, g.reshape(1, D))


def _stream_group_weights(be_ref, nu_ref, nxt_ref, w_hbms, stage, sem, w_bf16, *, tile, n_tiles):
    j = pl.program_id(0)
    i = pl.program_id(1)

    def copies(e, jj):
        cols = pl.ds(pl.multiple_of(jj * tile, tile), tile)
        return [pltpu.make_async_copy(w.at[e, :, cols], stage.at[n], sem.at[n]) for n, w in enumerate(w_hbms)]

    @pl.when((j == 0) & (i == 0))
    def _():
        for c in copies(be_ref[0], 0):
            c.start()

    first_of_group = (i < nu_ref[0]) & ((i == 0) | (be_ref[i] != be_ref[jnp.maximum(i - 1, 0)]))

    @pl.when(first_of_group)
    def _():
        for c in copies(be_ref[i], j):
            c.wait()
        for n, dst in enumerate(w_bf16):
            dst[...] = stage[n].astype(BF16)
        more_in_sweep = nxt_ref[i] >= 0
        e_next = jnp.where(more_in_sweep, nxt_ref[i], be_ref[0])
        j_next = jnp.where(more_in_sweep, j, j + 1)

        @pl.when(more_in_sweep | (j + 1 < n_tiles))
        def _():
            for c in copies(e_next, j_next):
                c.start()


def _moe_gateup_kernel(be_ref, nu_ref, nxt_ref, x_ref, wg_hbm, wu_hbm, o_ref, stage, sem, wgb, wub, *, tf, n_tiles):
    i = pl.program_id(1)
    _stream_group_weights(be_ref, nu_ref, nxt_ref, (wg_hbm, wu_hbm), stage, sem, (wgb, wub), tile=tf, n_tiles=n_tiles)

    @pl.when(i < nu_ref[0])
    def _():
        x = x_ref[...]
        g = jnp.dot(x, wgb[...], preferred_element_type=F32)
        u = jnp.dot(x, wub[...], preferred_element_type=F32)
        o_ref[...] = (g * jax.nn.sigmoid(g) * u).astype(o_ref.dtype)

    @pl.when(i >= nu_ref[0])
    def _():
        o_ref[...] = jnp.zeros(o_ref.shape, o_ref.dtype)


def moe_gateup(xs, wg, wu, be, n_used, nxt, *, tm, tf=512):
    n_rows, K = xs.shape
    F = wg.shape[2]
    tf = min(tf, F)
    nb = n_rows // tm
    xmap = lambda j, i, be, nu, nxt: (jnp.minimum(i, nu[0] - 1), 0)
    hbm = pl.BlockSpec(memory_space=pl.ANY)
    vmem = (2 * _nbytes((tm, K), BF16) + 2 * _nbytes((K, tf), F32) + 4 * _nbytes((K, tf), BF16)
            + 6 * _nbytes((tm, tf), F32))
    return pl.pallas_call(
        functools.partial(_moe_gateup_kernel, tf=tf, n_tiles=F // tf),
        grid_spec=pltpu.PrefetchScalarGridSpec(
            num_scalar_prefetch=3,
            grid=(F // tf, nb),
            in_specs=[pl.BlockSpec((tm, K), xmap), hbm, hbm],
            out_specs=pl.BlockSpec((tm, tf), lambda j, i, be, nu, nxt: (i, j)),
            scratch_shapes=[pltpu.VMEM((2, K, tf), F32), pltpu.SemaphoreType.DMA((2,)),
                            pltpu.VMEM((K, tf), BF16), pltpu.VMEM((K, tf), BF16)]),
        out_shape=jax.ShapeDtypeStruct((n_rows, F), BF16),
        compiler_params=_params(("arbitrary", "arbitrary"), vmem),
        name="moe_gateup",
    )(be, n_used, nxt, xs, wg, wu)


def _moe_down_kernel(be_ref, nu_ref, nxt_ref, x_ref, w_hbm, o_ref, stage, sem, wb, *, tn, n_tiles):
    i = pl.program_id(1)
    _stream_group_weights(be_ref, nu_ref, nxt_ref, (w_hbm,), stage, sem, (wb,), tile=tn, n_tiles=n_tiles)

    @pl.when(i < nu_ref[0])
    def _():
        o_ref[...] = jnp.dot(x_ref[...], wb[...], preferred_element_type=F32)

    @pl.when(i >= nu_ref[0])
    def _():
        o_ref[...] = jnp.zeros(o_ref.shape, o_ref.dtype)


def moe_down(hm, wd, be, n_used, nxt, *, tm, tn=1024):
    n_rows, K = hm.shape
    N = wd.shape[2]
    tn = min(tn, N)
    nb = n_rows // tm
    xmap = lambda j, i, be, nu, nxt: (jnp.minimum(i, nu[0] - 1), 0)
    vmem = (2 * _nbytes((tm, K), BF16) + _nbytes((K, tn), F32) + 2 * _nbytes((K, tn), BF16)
            + 4 * _nbytes((tm, tn), F32))
    return pl.pallas_call(
        functools.partial(_moe_down_kernel, tn=tn, n_tiles=N // tn),
        grid_spec=pltpu.PrefetchScalarGridSpec(
            num_scalar_prefetch=3,
            grid=(N // tn, nb),
            in_specs=[pl.BlockSpec((tm, K), xmap), pl.BlockSpec(memory_space=pl.ANY)],
            out_specs=pl.BlockSpec((tm, tn), lambda j, i, be, nu, nxt: (i, j)),
            scratch_shapes=[pltpu.VMEM((1, K, tn), F32), pltpu.SemaphoreType.DMA((1,)),
                            pltpu.VMEM((K, tn), BF16)]),
        out_shape=jax.ShapeDtypeStruct((n_rows, N), F32),
        compiler_params=_params(("arbitrary", "arbitrary"), vmem),
        name="moe_down",
    )(be, n_used, nxt, hm, wd)


def _moe_combine_kernel(pos_ref, x_ref, gw_ref, ys_hbm, o_ref, buf, sem, *, rows):
    i = pl.program_id(0)
    n_blocks = pl.num_programs(0)

    def row_copy(slot, k, r, src_row):
        return pltpu.make_async_copy(ys_hbm.at[pl.ds(src_row, 1)], buf.at[slot, k, pl.ds(r, 1)], sem.at[slot])

    def fetch(block, slot):
        def issue(r, c):
            a = (block * rows + r) * TOP_K
            for k in range(TOP_K):
                row_copy(slot, k, r, pos_ref[a + k]).start()
            return c

        lax.fori_loop(0, rows, issue, 0, unroll=4)

    @pl.when(i == 0)
    def _():
        fetch(0, 0)

    @pl.when(i + 1 < n_blocks)
    def _():
        fetch(i + 1, (i + 1) % 2)

    slot = i % 2

    def drain(r, c):
        for k in range(TOP_K):
            row_copy(slot, k, r, 0).wait()
        return c

    lax.fori_loop(0, rows, drain, 0, unroll=4)
    out = x_ref[...]
    for k in range(TOP_K):
        out = out + gw_ref[:, k:k + 1] * buf[slot, k]
    o_ref[...] = out


def moe_combine(x, ys, dest, gw, *, rows=128):
    T, D = x.shape
    return pl.pallas_call(
        functools.partial(_moe_combine_kernel, rows=rows),
        grid_spec=pltpu.PrefetchScalarGridSpec(
            num_scalar_prefetch=1,
            grid=(T // rows,),
            in_specs=[pl.BlockSpec((rows, D), lambda i, pos: (i, 0)),
                      pl.BlockSpec((rows, TOP_K), lambda i, pos: (i, 0)),
                      pl.BlockSpec(memory_space=pl.ANY)],
            out_specs=pl.BlockSpec((rows, D), lambda i, pos: (i, 0)),
            scratch_shapes=[pltpu.VMEM((2, TOP_K, rows, D), F32), pltpu.SemaphoreType.DMA((2,))]),
        out_shape=jax.ShapeDtypeStruct((T, D), F32),
        compiler_params=_params(("arbitrary",), 12 * _nbytes((rows, D), F32)),
        name="moe_combine",
    )(dest, x, gw, ys)


def moe_block(x, g, w_router, b_router, w_egate, w_eup, w_edown, *, tm=512):
    idx, gw = moe_router(x, g, w_router, b_router)
    dest, row_tok, be, n_used, nxt = _moe_dispatch(idx, tm)
    xs = moe_gather_norm(x, g, row_tok, n_used, tm=tm)
    hm = moe_gateup(xs, w_egate, w_eup, be, n_used, nxt, tm=tm)
    ys = moe_down(hm, w_edown, be, n_used, nxt, tm=tm)
    return moe_combine(x, ys, dest, gw)


def even_mixer_block(x, tabs_a, tabs_b, w_in, ga_q, ga_k, g_cq, w_uq, g_ckv, w_ukv, gb_q, gb_kn, gb_kr,
                     w_o, g_mix, *, batch, seq):
    a_width = A_HEADS * HEAD_DIM
    main_cols = 3 * a_width + Q_LORA + KV_LORA
    h = rmsnorm(x, g_mix)
    proj = matmul(h, w_in, n_out=main_cols, tm=2048, single_buffer_x=True, name="in_proj")
    w_kr = jnp.pad(w_in[:, main_cols:], ((0, 0), (0, LANES - ROPE_DIM)))
    krp = matmul(h, w_kr, name="in_proj_kr")
    o_a = mixer_a(proj, tabs_a, ga_q, ga_k, batch=batch, seq=seq)
    cq = rmsnorm(proj, g_cq, width=Q_LORA, col_block=3 * a_width // Q_LORA)
    ckv = rmsnorm(proj, g_ckv, width=KV_LORA, col_block=(3 * a_width + Q_LORA) // KV_LORA)
    w3 = w_uq.reshape(Q_LORA, B_HEADS, QK_DIM)
    w_uq_p = jnp.concatenate(
        [w3[:, :, :NOPE_DIM].reshape(Q_LORA, B_HEADS * NOPE_DIM),
         jnp.pad(w3[:, :, NOPE_DIM:], ((0, 0), (0, 0), (0, LANES - ROPE_DIM))).reshape(Q_LORA, B_HEADS * LANES)],
        axis=1)
    q_up = matmul(cq, w_uq_p, name="q_up")
    kv_up = matmul(ckv, w_ukv, name="kv_up")
    pad_r = lambda v: jnp.pad(v, (0, LANES - ROPE_DIM)).reshape(1, LANES)
    qt, kf, vb = mla_prep(q_up, kv_up, krp, tabs_b, gb_q[:NOPE_DIM].reshape(1, LANES), pad_r(gb_q[NOPE_DIM:]),
                          gb_kn.reshape(1, LANES), pad_r(gb_kr), batch=batch, seq=seq)
    o_b = mla_attention(qt, kf, vb, batch=batch, seq=seq)
    return matmul((o_a, o_b), w_o, res=x, name="mixer_out")


def kernel(x, mem, positions, g_mem, w_mem_kv, g_mem_k, g_mix, g_x, w_xq, g_xq, w_xo, g_ffn,
           w_in, ga_q, ga_k, g_cq, w_uq, g_ckv, w_ukv, gb_q, gb_kn, gb_kr, w_o_even,
           w_gate, w_up, w_down, w_qkv, w_o_odd, w_router, b_router, w_egate, w_eup, w_edown):
    B, S, D = x.shape
    M = mem.shape[1]
    depth = g_mix.shape[0]
    T = B * S
    xt = x.reshape(T, D)
    tabs_a = _rope_tables(positions, ROT_DIM)
    tabs_b = _rope_tables(positions, ROPE_DIM)
    kv = matmul(rmsnorm(mem.reshape(B * M, D), g_mem), w_mem_kv, tm=B * M, name="mem_kv")
    mk, mv = memkv_post(kv, g_mem_k)
    for layer in range(depth):
        i = layer // 2
        if layer % 2 == 0:
            xt = even_mixer_block(xt, tabs_a, tabs_b, w_in[i], ga_q[i], ga_k[i], g_cq[i], w_uq[i], g_ckv[i],
                                  w_ukv[i], gb_q[i], gb_kn[i], gb_kr[i], w_o_even[i], g_mix[layer],
                                  batch=B, seq=S)
        else:
            h = rmsnorm(xt, g_mix[layer])
            qkv = matmul(h, w_qkv[i], out_dtype=BF16, tm=2048, single_buffer_x=True, name="qkv_proj")
            o = stickbreak_attention(qkv, batch=B, seq=S, heads=D // HEAD_DIM)
            xt = matmul(o, w_o_odd[i], res=xt, name="mixer_out")
        xattn = functools.partial(cross_attention_block, xt, mk, mv, g_x[layer], w_xq[layer], g_xq[layer],
                                  w_xo[layer], seq=S, mem_len=M)
        if layer % 2 == 0:
            xt, h = xattn(g_ffn[layer])
            hm = swiglu_gateup(h, w_gate[i], w_up[i])
            xt = matmul(hm, w_down[i], res=xt, tn=1024, tk=2048, name="swiglu_down")
        else:
            xt = moe_block(xattn(), g_ffn[layer], w_router[i], b_router[i], w_egate[i], w_eup[i], w_edown[i])
    return xt.reshape(B, S, D)
```

```python
import functools

import jax
import jax.numpy as jnp
from jax import lax
from jax.experimental import pallas as pl
from jax.experimental.pallas import tpu as pltpu

F32 = jnp.float32
BF16 = jnp.bfloat16

HEAD_DIM = 128
ROT_DIM = HEAD_DIM // 4
ROPE_THETA = 500000.0
BLOCK = 128
NEG_INF = -1e30
EPS = 1e-6
DILATED_PAIRS = ((128, 1), (512, 4), (2048, 16))
A_HEADS = 16
B_HEADS = 16
Q_LORA = 1536
KV_LORA = 512
NOPE_DIM = 128
ROPE_DIM = 64
QK_DIM = NOPE_DIM + ROPE_DIM
X_HEADS = 4
N_EXPERTS = 8
TOP_K = 2
MIXER_A_GROUP = 8
LOG2_E = 1.4426950408889634
MLA_Q_SCALE = QK_DIM ** -0.5 * LOG2_E

LANES = 128
V7X_VMEM_BYTES = 64 * 1024 * 1024
VMEM_CAP = V7X_VMEM_BYTES - 6 * 1024 * 1024


def _params(semantics, vmem_bytes):
    return pltpu.CompilerParams(dimension_semantics=semantics,
                                vmem_limit_bytes=int(min(max(vmem_bytes, 32 * 1024 * 1024), VMEM_CAP)))


def _nbytes(shape, dtype):
    n = 1
    for s in shape:
        n *= s
    return n * jnp.dtype(dtype).itemsize


def _rmsnorm_kernel(x_ref, g_ref, o_ref):
    x = x_ref[...].astype(F32)
    ms = jnp.mean(x * x, axis=-1, keepdims=True)
    o_ref[...] = (x * lax.rsqrt(ms + EPS) * g_ref[...]).astype(o_ref.dtype)


def rmsnorm(x, g, *, width=None, col_block=0, tm=256, out_dtype=BF16):
    T = x.shape[0]
    width = x.shape[1] if width is None else width
    tm = min(tm, T)
    return pl.pallas_call(
        _rmsnorm_kernel,
        grid=(T // tm,),
        in_specs=[pl.BlockSpec((tm, width), lambda i: (i, col_block)),
                  pl.BlockSpec((1, width), lambda i: (0, 0))],
        out_specs=pl.BlockSpec((tm, width), lambda i: (i, 0)),
        out_shape=jax.ShapeDtypeStruct((T, width), out_dtype),
        compiler_params=_params(("parallel",), 6 * _nbytes((tm, width), F32)),
        name="rmsnorm",
    )(x, g.reshape(1, width).astype(F32))


def _mm_kernel(*refs, nk, n_x, has_res):
    x_refs = refs[:n_x]
    w_ref = refs[n_x]
    rest = refs[n_x + 1:]
    res_ref = rest[0] if has_res else None
    o_ref = rest[1] if has_res else rest[0]
    acc_ref = rest[-1] if nk > 1 else None
    k = pl.program_id(2)
    w = w_ref[...].astype(BF16)

    def finish(part):
        out = part
        if has_res:
            out = out + res_ref[...]
        o_ref[...] = out.astype(o_ref.dtype)

    if n_x == 2:
        half = x_refs[0].shape[1]
        finish(jnp.dot(x_refs[0][...], w[:half], preferred_element_type=F32)
               + jnp.dot(x_refs[1][...], w[half:], preferred_element_type=F32))
    elif nk == 1:
        finish(jnp.dot(x_refs[0][...], w, preferred_element_type=F32))
    else:
        part = jnp.dot(x_refs[0][...], w, preferred_element_type=F32)

        @pl.when(k == 0)
        def _():
            acc_ref[...] = part

        @pl.when((k > 0) & (k < nk - 1))
        def _():
            acc_ref[...] += part

        @pl.when(k == nk - 1)
        def _():
            finish(acc_ref[...] + part)


def matmul(xs, w, *, n_out=None, res=None, out_dtype=F32, tm=1024, tn=512, tk=None, single_buffer_x=False,
           name="matmul"):
    if not isinstance(xs, (tuple, list)):
        xs = (xs,)
    n_x = len(xs)
    T = xs[0].shape[0]
    K = sum(x.shape[1] for x in xs)
    n_out = w.shape[1] if n_out is None else n_out
    tm = min(tm, T)
    tn = min(tn, n_out)
    if n_x == 2:
        assert tk is None and xs[1].shape[1] == xs[0].shape[1]
    tk = K if tk is None else tk
    nk = K // tk
    assert K == nk * tk and T % tm == 0 and n_out % tn == 0
    x_buffers = 2
    if n_x == 2:
        x_specs = [pl.BlockSpec((tm, K // 2), lambda i, j, k: (i, 0)) for _ in xs]
    elif nk == 1 and single_buffer_x:
        x_buffers = 1
        x_specs = [pl.BlockSpec((tm, tk), lambda i, j, k: (i, k), pipeline_mode=pl.Buffered(1))]
    else:
        x_specs = [pl.BlockSpec((tm, tk), lambda i, j, k: (i, k))]
    in_specs = x_specs + [pl.BlockSpec((tk, tn), lambda i, j, k: (k, j))]
    args = list(xs) + [w]
    if res is not None:
        in_specs.append(pl.BlockSpec((tm, tn), lambda i, j, k: (i, j)))
        args.append(res)
    scratch = [pltpu.VMEM((tm, tn), F32)] if nk > 1 else []
    vmem = (x_buffers * _nbytes((tm, tk), xs[0].dtype) + 2 * _nbytes((tk, tn), F32) + _nbytes((tk, tn), BF16)
            + (2 * _nbytes((tm, tn), F32) if res is not None else 0)
            + 2 * _nbytes((tm, tn), out_dtype) + 2 * _nbytes((tm, tn), F32))
    return pl.pallas_call(
        functools.partial(_mm_kernel, nk=nk, n_x=n_x, has_res=res is not None),
        grid=(T // tm, n_out // tn, nk),
        in_specs=in_specs,
        out_specs=pl.BlockSpec((tm, tn), lambda i, j, k: (i, j)),
        out_shape=jax.ShapeDtypeStruct((T, n_out), out_dtype),
        scratch_shapes=scratch,
        compiler_params=_params(("parallel", "parallel", "arbitrary"), vmem),
        name=name,
    )(*args)


def _rope_tables(positions, dim):
    half = dim // 2
    inv_freq = ROPE_THETA ** (-jnp.arange(0, dim, 2, dtype=F32) / dim)
    ang = positions.astype(F32)[..., None] * inv_freq
    cos, sin = jnp.cos(ang), jnp.sin(ang)
    B, S = positions.shape
    ones = jnp.ones((B, S, LANES - dim), F32)
    zeros = jnp.zeros((B, S, LANES - dim), F32)
    zh = jnp.zeros((B, S, half), F32)
    c = jnp.concatenate([cos, cos, ones], axis=-1)
    sa = jnp.concatenate([-sin, zh, zeros], axis=-1)
    sb = jnp.concatenate([zh, sin, zeros], axis=-1)
    return [t.reshape(B * S, LANES) for t in (c, sa, sb)]


def _rope_lanes(y, c, sa, sb, half):
    return y * c + pltpu.roll(y, LANES - half, 1) * sa + pltpu.roll(y, half, 1) * sb


def _mixer_a_kernel(q_ref, k_ref, v_ref, c_ref, sa_ref, sb_ref, gq_ref, gk_ref, o_ref,
                    qs, ks, *state, seq):
    scale = HEAD_DIM ** -0.5 * LOG2_E
    c, sa, sb = c_ref[...], sa_ref[...], sb_ref[...]

    def prep(x, g):
        y = x * lax.rsqrt(jnp.mean(x * x, axis=-1, keepdims=True) + EPS) * g
        return _rope_lanes(y, c, sa, sb, ROT_DIM // 2)

    qs[...] = prep(q_ref[...], gq_ref[...])
    ks[...] = prep(k_ref[...], gk_ref[...])
    qi = lax.broadcasted_iota(jnp.int32, (BLOCK, 1), 0)
    kj = lax.broadcasted_iota(jnp.int32, (1, BLOCK), 1)
    trans_b = (((1,), (1,)), ((), ()))
    ones = jnp.ones((BLOCK, HEAD_DIM), BF16)

    for bi, (window, dil) in enumerate(DILATED_PAIRS):
        acc, m_sc, l_sc = state[3 * bi:3 * bi + 3]
        n_back = window // dil
        nb = seq // dil // BLOCK
        assert n_back <= BLOCK and nb * dil * BLOCK == seq
        cur_ok = (qi >= kj) & (qi - kj <= n_back)
        prev_ok = qi + BLOCK - kj <= n_back

        def rows_at(r, n, dil=dil):
            if dil == 1:
                return pl.ds(BLOCK * n, BLOCK)
            return pl.ds(r + dil * BLOCK * n, BLOCK, stride=dil)

        blocks = [(rows_at(r, n), rows_at(r, n - 1) if n > 0 else None) for r in range(dil) for n in range(nb)]
        for first in range(0, len(blocks), MIXER_A_GROUP):
            group = blocks[first:first + MIXER_A_GROUP]
            s_cur, s_prev = [], []
            for rows, prev in group:
                qb = qs[rows, :].astype(BF16)
                s_c = lax.dot_general(qb, ks[rows, :].astype(BF16), trans_b, preferred_element_type=F32) * scale
                s_cur.append(jnp.where(cur_ok, s_c, NEG_INF))
                if prev is None:
                    s_prev.append(None)
                else:
                    s_p = lax.dot_general(qb, ks[prev, :].astype(BF16), trans_b,
                                          preferred_element_type=F32) * scale
                    s_prev.append(jnp.where(prev_ok, s_p, NEG_INF))
            stats = []
            for s_c, s_p in zip(s_cur, s_prev):
                m = jnp.max(s_c, axis=-1, keepdims=True)
                if s_p is not None:
                    m = jnp.maximum(m, jnp.max(s_p, axis=-1, keepdims=True))
                p_c = jnp.exp2(s_c - m).astype(BF16)
                p_p = None if s_p is None else jnp.exp2(s_p - m).astype(BF16)
                stats.append((m, p_c, p_p))
            for (rows, prev), (m, p_c, p_p) in zip(group, stats):
                v_ext = jnp.concatenate([v_ref[rows, :].astype(BF16), ones], axis=1)
                o = jnp.dot(p_c, v_ext, preferred_element_type=F32)
                if p_p is not None:
                    v_ext = jnp.concatenate([v_ref[prev, :].astype(BF16), ones], axis=1)
                    o = o + jnp.dot(p_p, v_ext, preferred_element_type=F32)
                acc[rows, :] = o[:, :HEAD_DIM]
                l_sc[rows, :] = o[:, HEAD_DIM:]
                m_sc[rows, :] = jnp.broadcast_to(m, (BLOCK, HEAD_DIM))

    n_br = len(DILATED_PAIRS)
    m_all = state[1][...]
    for bi in range(1, n_br):
        m_all = jnp.maximum(m_all, state[3 * bi + 1][...])
    num = jnp.zeros(o_ref.shape, F32)
    den = jnp.zeros(o_ref.shape, F32)
    for bi in range(n_br):
        w = jnp.exp2(state[3 * bi + 1][...] - m_all)
        num = num + w * state[3 * bi][...]
        den = den + w * state[3 * bi + 2][...]
    o_ref[...] = (num / den).astype(o_ref.dtype)


def mixer_a(proj, tabs, gq, gk, *, batch, seq):
    T = batch * seq
    blk = (seq, HEAD_DIM)
    head = lambda off: pl.BlockSpec(blk, lambda b, h: (b, off + h))
    tab = pl.BlockSpec(blk, lambda b, h: (b, 0))
    gspec = pl.BlockSpec((1, HEAD_DIM), lambda b, h: (0, 0))
    return pl.pallas_call(
        functools.partial(_mixer_a_kernel, seq=seq),
        grid=(batch, A_HEADS),
        in_specs=[head(0), head(A_HEADS), head(2 * A_HEADS), tab, tab, tab, gspec, gspec],
        out_specs=pl.BlockSpec(blk, lambda b, h: (b, h)),
        out_shape=jax.ShapeDtypeStruct((T, A_HEADS * HEAD_DIM), BF16),
        scratch_shapes=[pltpu.VMEM(blk, F32) for _ in range(2 + 3 * len(DILATED_PAIRS))],
        compiler_params=_params(("parallel", "parallel"), 30 * _nbytes(blk, F32)),
        name="mixer_a",
    )(proj, proj, proj, *tabs, gq.reshape(1, HEAD_DIM), gk.reshape(1, HEAD_DIM))


def _mla_prep_kernel(qn_ref, qr_ref, kn_ref, v_ref, kr_ref, c_ref, sa_ref, sb_ref,
                     gqn_ref, gqr_ref, gkn_ref, gkr_ref, qt_ref, kf_ref, vb_ref, kr_sc):
    c, sa, sb = c_ref[...], sa_ref[...], sb_ref[...]
    half = ROPE_DIM // 2

    @pl.when(pl.program_id(1) == 0)
    def _():
        kr = kr_ref[...]
        kr = kr * lax.rsqrt(jnp.sum(kr * kr, axis=-1, keepdims=True) * (1.0 / ROPE_DIM) + EPS) * gkr_ref[...]
        kr_sc[...] = _rope_lanes(kr, c, sa, sb, half).astype(kr_sc.dtype)

    qn, qr = qn_ref[...], qr_ref[...]
    ss = jnp.sum(qn * qn, axis=-1, keepdims=True) + jnp.sum(qr * qr, axis=-1, keepdims=True)
    inv = lax.rsqrt(ss * (1.0 / QK_DIM) + EPS) * MLA_Q_SCALE
    qf = jnp.concatenate([qn * inv * gqn_ref[...], _rope_lanes(qr * inv * gqr_ref[...], c, sa, sb, half)], axis=1)
    qt_ref[...] = qf.astype(qt_ref.dtype).T
    kn = kn_ref[...]
    kn = kn * lax.rsqrt(jnp.mean(kn * kn, axis=-1, keepdims=True) + EPS) * gkn_ref[...]
    kf_ref[:, :NOPE_DIM] = kn.astype(kf_ref.dtype)
    kf_ref[:, NOPE_DIM:] = kr_sc[...]
    vb_ref[...] = v_ref[...].astype(vb_ref.dtype)


def mla_prep(q_up, kv_up, krp, tabs, gqn, gqr, gkn, gkr, *, batch, seq, tm=512):
    T = q_up.shape[0]
    H = B_HEADS
    per_seq = seq // tm
    blk = lambda f: pl.BlockSpec((tm, LANES), f)
    g = pl.BlockSpec((1, LANES), lambda i, h: (0, 0))
    return pl.pallas_call(
        _mla_prep_kernel,
        grid=(T // tm, H),
        in_specs=[blk(lambda i, h: (i, h)), blk(lambda i, h: (i, H + h)),
                  blk(lambda i, h: (i, 2 * h)), blk(lambda i, h: (i, 2 * h + 1)),
                  blk(lambda i, h: (i, 0)), blk(lambda i, h: (i, 0)), blk(lambda i, h: (i, 0)),
                  blk(lambda i, h: (i, 0)), g, g, g, g],
        out_specs=[pl.BlockSpec((None, None, 2 * LANES, tm), lambda i, h: (i // per_seq, h, 0, i % per_seq)),
                   pl.BlockSpec((tm, 2 * LANES), lambda i, h: (i, h)),
                   pl.BlockSpec((tm, LANES), lambda i, h: (i, h))],
        out_shape=[jax.ShapeDtypeStruct((batch, H, 2 * LANES, seq), BF16),
                   jax.ShapeDtypeStruct((T, H * 2 * LANES), BF16),
                   jax.ShapeDtypeStruct((T, H * LANES), BF16)],
        scratch_shapes=[pltpu.VMEM((tm, LANES), BF16)],
        compiler_params=_params(("parallel", "arbitrary"), 40 * _nbytes((tm, LANES), F32)),
        name="mla_prep",
    )(q_up, q_up, kv_up, kv_up, krp, *tabs, gqn, gqr, gkn, gkr)


def _causal_attn_kernel(qt_ref, k_ref, v_ref, o_ref, s_sc, acc, *, tq, hg, dq, dv):
    qi = pl.program_id(2)
    key = lax.broadcasted_iota(jnp.int32, (tq, tq), 0)
    qry = lax.broadcasted_iota(jnp.int32, (tq, tq), 1)
    visible = key <= qry
    ones = jnp.ones((tq, dv), BF16)
    contract_rows = (((0,), (0,)), ((), ()))

    heads = range(hg)

    def scores(j, diagonal):
        rows = pl.ds(pl.multiple_of(j * tq, tq), tq)
        s = [jnp.dot(k_ref[rows, g * dq:(g + 1) * dq], qt_ref[g], preferred_element_type=F32) for g in heads]
        if diagonal:
            s = [jnp.where(visible, sg, NEG_INF) for sg in s]
        for g in heads:
            s_sc[g, j] = s[g]
        return s

    def col_max(j, ms):
        s = scores(j, False)
        return tuple(jnp.maximum(ms[g], jnp.max(s[g], axis=0, keepdims=True)) for g in heads)

    ms = lax.fori_loop(0, qi, col_max, tuple(jnp.max(sg, axis=0, keepdims=True) for sg in scores(qi, True)))

    def weighted(j):
        rows = pl.ds(pl.multiple_of(j * tq, tq), tq)
        p = [jnp.exp2(s_sc[g, j] - ms[g]).astype(BF16) for g in heads]
        return [lax.dot_general(p[g], jnp.concatenate([v_ref[rows, g * dv:(g + 1) * dv], ones], axis=1),
                                contract_rows, preferred_element_type=F32) for g in heads]

    for g, w in enumerate(weighted(qi)):
        acc[g] = w

    def below(j, carry):
        for g, w in enumerate(weighted(j)):
            acc[g] += w
        return carry

    lax.fori_loop(0, qi, below, 0)
    for g in heads:
        a = acc[g]
        o_ref[:, g * dv:(g + 1) * dv] = (a[:, :dv] / a[:, dv:]).astype(o_ref.dtype)


def mla_attention(qt, kf, vb, *, batch, seq, tq=512, hg=4):
    T = batch * seq
    nq = seq // tq
    dq = qt.shape[2]
    dv = vb.shape[1] // B_HEADS
    assert B_HEADS % hg == 0
    return pl.pallas_call(
        functools.partial(_causal_attn_kernel, tq=tq, hg=hg, dq=dq, dv=dv),
        grid=(batch, B_HEADS // hg, nq),
        in_specs=[pl.BlockSpec((None, hg, dq, tq), lambda b, h, i: (b, h, 0, i)),
                  pl.BlockSpec((seq, hg * dq), lambda b, h, i: (b, h)),
                  pl.BlockSpec((seq, hg * dv), lambda b, h, i: (b, h))],
        out_specs=pl.BlockSpec((tq, hg * dv), lambda b, h, i: (b * nq + i, h)),
        out_shape=jax.ShapeDtypeStruct((T, B_HEADS * dv), BF16),
        scratch_shapes=[pltpu.VMEM((hg, nq, tq, tq), F32), pltpu.VMEM((hg, tq, 2 * dv), F32)],
        compiler_params=_params(("parallel", "parallel", "parallel"),
                                _nbytes((hg, nq, tq, tq), F32) + 2 * _nbytes((seq, hg * (dq + dv)), BF16)
                                + 8 * hg * _nbytes((tq, tq), F32)),
        name="mla_attention",
    )(qt, kf, vb)


STICK_EXHAUSTED_LOG = -110.0


def _stickbreak_kernel(q_ref, k_ref, v_ref, o_ref, qt_sc, c_sc, acc, *, tq, hg, scale):
    qi = pl.program_id(2)
    c_sc[...] = jnp.zeros(c_sc.shape, F32)
    acc[...] = jnp.zeros(acc.shape, F32)
    for g in range(hg):
        qt_sc[g] = q_ref[:, g * HEAD_DIM:(g + 1) * HEAD_DIM].T
    key = lax.broadcasted_iota(jnp.int32, (tq, tq), 0)
    qry = lax.broadcasted_iota(jnp.int32, (tq, tq), 1)
    after = (qry > key).astype(BF16)
    earlier = key < qry
    contract_rows = (((0,), (0,)), ((), ()))

    def chunk(j, diagonal):
        rows = pl.ds(pl.multiple_of(j * tq, tq), tq)
        heads = range(hg)
        cols = [slice(g * HEAD_DIM, (g + 1) * HEAD_DIM) for g in heads]
        z = [jnp.dot(k_ref[rows, cols[g]], qt_sc[g], preferred_element_type=F32) * (scale * LOG2_E) for g in heads]
        log_beta, parts = [], []
        for g in heads:
            softplus = jnp.maximum(z[g], 0.0) + jnp.log2(1.0 + jnp.exp2(-jnp.abs(z[g])))
            log_keep = -softplus
            if diagonal:
                log_keep = jnp.where(earlier, log_keep, 0.0)
            parts.append(log_keep.astype(BF16))
            log_beta.append(z[g] - softplus + c_sc[g])
            c_sc[g] += jnp.sum(log_keep, axis=0, keepdims=True)
        within = jnp.dot(after, jnp.concatenate(parts, axis=1), preferred_element_type=F32)
        for g in heads:
            a = jnp.exp2(log_beta[g] + within[:, g * tq:(g + 1) * tq])
            if diagonal:
                a = jnp.where(earlier, a, 0.0)
            acc[:, cols[g]] += lax.dot_general(a.astype(BF16), v_ref[rows, cols[g]], contract_rows,
                                               preferred_element_type=F32)

    def stick_left():
        return jnp.max(c_sc[...]) > STICK_EXHAUSTED_LOG * LOG2_E

    chunk(qi, True)

    def more(state):
        j, left = state
        return (j >= 0) & left

    def step(state):
        j, _ = state
        chunk(j, False)
        return j - 1, stick_left()

    lax.while_loop(more, step, (qi - 1, stick_left()))
    o_ref[...] = acc[...].astype(o_ref.dtype)


def stickbreak_attention(qkv, *, batch, seq, heads, tq=256, hg=4):
    T = batch * seq
    nq = seq // tq
    ng = heads // hg
    assert heads == ng * hg
    blk = hg * HEAD_DIM
    return pl.pallas_call(
        functools.partial(_stickbreak_kernel, tq=tq, hg=hg, scale=HEAD_DIM ** -0.5),
        grid=(batch, ng, nq),
        in_specs=[pl.BlockSpec((tq, blk), lambda b, h, i: (b * nq + i, h)),
                  pl.BlockSpec((seq, blk), lambda b, h, i: (b, ng + h)),
                  pl.BlockSpec((seq, blk), lambda b, h, i: (b, 2 * ng + h))],
        out_specs=pl.BlockSpec((tq, blk), lambda b, h, i: (b * nq + i, h)),
        out_shape=jax.ShapeDtypeStruct((T, heads * HEAD_DIM), BF16),
        scratch_shapes=[pltpu.VMEM((hg, HEAD_DIM, tq), BF16), pltpu.VMEM((hg, 1, tq), F32),
                        pltpu.VMEM((tq, blk), F32)],
        compiler_params=_params(("parallel", "parallel", "parallel"), 0),
        name="stickbreak",
    )(qkv, qkv, qkv)


def _memkv_kernel(kv_ref, g_ref, mk_ref, mv_ref):
    for h in range(X_HEADS):
        cols = slice(h * HEAD_DIM, (h + 1) * HEAD_DIM)
        k = kv_ref[:, cols]
        k = k * lax.rsqrt(jnp.mean(k * k, axis=-1, keepdims=True) + EPS) * g_ref[...]
        mk_ref[:, cols] = k.astype(mk_ref.dtype)
    mv_ref[...] = kv_ref[:, X_HEADS * HEAD_DIM:].astype(mv_ref.dtype)


def memkv_post(kv, g):
    M = kv.shape[0]
    xd = X_HEADS * HEAD_DIM
    return pl.pallas_call(
        _memkv_kernel,
        grid=(1,),
        in_specs=[pl.BlockSpec((M, 2 * xd), lambda i: (0, 0)), pl.BlockSpec((1, HEAD_DIM), lambda i: (0, 0))],
        out_specs=[pl.BlockSpec((M, xd), lambda i: (0, 0)), pl.BlockSpec((M, xd), lambda i: (0, 0))],
        out_shape=[jax.ShapeDtypeStruct((M, xd), BF16), jax.ShapeDtypeStruct((M, xd), BF16)],
        compiler_params=_params(("arbitrary",), 0),
        name="memkv_post",
    )(kv, g.reshape(1, HEAD_DIM))


def _xattn_block_kernel(x_ref, gx_ref, wq_ref, gq_ref, mk_ref, mv_ref, wo_ref, *rest, emit_norm):
    if emit_norm:
        gn_ref, o_ref, hn_ref = rest
    else:
        (o_ref,) = rest
    scale = HEAD_DIM ** -0.5
    x = x_ref[...]
    h = (x * lax.rsqrt(jnp.mean(x * x, axis=-1, keepdims=True) + EPS) * gx_ref[...]).astype(BF16)
    q_all = jnp.dot(h, wq_ref[...], preferred_element_type=F32)
    heads = []
    for hd in range(X_HEADS):
        cols = slice(hd * HEAD_DIM, (hd + 1) * HEAD_DIM)
        q = q_all[:, cols]
        q = (q * lax.rsqrt(jnp.mean(q * q, axis=-1, keepdims=True) + EPS) * gq_ref[...]).astype(BF16)
        s = lax.dot_general(q, mk_ref[:, cols], (((1,), (1,)), ((), ())), preferred_element_type=F32) * scale
        p = jnp.exp(s - jnp.max(s, axis=-1, keepdims=True))
        p = p / jnp.sum(p, axis=-1, keepdims=True)
        heads.append(jnp.dot(p.astype(BF16), mv_ref[:, cols], preferred_element_type=F32).astype(BF16))
    y = x + jnp.dot(jnp.concatenate(heads, axis=1), wo_ref[...], preferred_element_type=F32)
    o_ref[...] = y
    if emit_norm:
        hn_ref[...] = (y * lax.rsqrt(jnp.mean(y * y, axis=-1, keepdims=True) + EPS) * gn_ref[...]).astype(hn_ref.dtype)


def cross_attention_block(x, mk, mv, g_x, w_xq, g_xq, w_xo, g_next=None, *, seq, mem_len, tm=256):
    T, D = x.shape
    xd = X_HEADS * HEAD_DIM
    per_seq = seq // tm
    emit_norm = g_next is not None
    row = lambda w: pl.BlockSpec((1, w), lambda i: (0, 0))
    tok = pl.BlockSpec((tm, D), lambda i: (i, 0))
    in_specs = [tok, row(D), pl.BlockSpec((D, xd), lambda i: (0, 0)), row(HEAD_DIM),
                pl.BlockSpec((mem_len, xd), lambda i: (i // per_seq, 0)),
                pl.BlockSpec((mem_len, xd), lambda i: (i // per_seq, 0)),
                pl.BlockSpec((xd, D), lambda i: (0, 0))]
    args = [x, g_x.reshape(1, D), w_xq.astype(BF16), g_xq.reshape(1, HEAD_DIM), mk, mv, w_xo.astype(BF16)]
    out_specs, out_shape = tok, jax.ShapeDtypeStruct((T, D), F32)
    if emit_norm:
        in_specs.append(row(D))
        args.append(g_next.reshape(1, D))
        out_specs = [tok, tok]
        out_shape = [out_shape, jax.ShapeDtypeStruct((T, D), BF16)]
    vmem = 10 * _nbytes((tm, D), F32) + 4 * _nbytes((D, xd), BF16)
    return pl.pallas_call(
        functools.partial(_xattn_block_kernel, emit_norm=emit_norm),
        grid=(T // tm,),
        in_specs=in_specs,
        out_specs=out_specs,
        out_shape=out_shape,
        compiler_params=_params(("parallel",), vmem),
        name="cross_attention",
    )(*args)


def _gateup_kernel(x_ref, wg_ref, wu_ref, o_ref):
    x = x_ref[...]
    g = jnp.dot(x, wg_ref[...].astype(BF16), preferred_element_type=F32)
    u = jnp.dot(x, wu_ref[...].astype(BF16), preferred_element_type=F32)
    o_ref[...] = (g * jax.nn.sigmoid(g) * u).astype(o_ref.dtype)


def swiglu_gateup(x, wg, wu, *, tm=2048, tf=256):
    T, K = x.shape
    F = wg.shape[1]
    vmem = (_nbytes((tm, K), BF16) + 4 * _nbytes((K, tf), F32) + 2 * _nbytes((K, tf), BF16)
            + 6 * _nbytes((tm, tf), F32))
    return pl.pallas_call(
        _gateup_kernel,
        grid=(T // tm, F // tf),
        in_specs=[pl.BlockSpec((tm, K), lambda i, j: (i, 0), pipeline_mode=pl.Buffered(1)),
                  pl.BlockSpec((K, tf), lambda i, j: (0, j)),
                  pl.BlockSpec((K, tf), lambda i, j: (0, j))],
        out_specs=pl.BlockSpec((tm, tf), lambda i, j: (i, j)),
        out_shape=jax.ShapeDtypeStruct((T, F), BF16),
        compiler_params=_params(("parallel", "parallel"), vmem),
        name="swiglu_gateup",
    )(x, wg, wu)


def _router_kernel(x_ref, g_ref, w_ref, b_ref, idx_ref, gw_ref):
    x = x_ref[...]
    h = x * lax.rsqrt(jnp.mean(x * x, axis=-1, keepdims=True) + EPS) * g_ref[...]
    logits = jnp.dot(h, w_ref[...], preferred_element_type=F32, precision=lax.Precision.HIGHEST) + b_ref[...]
    lane = lax.broadcasted_iota(jnp.int32, logits.shape, 1)
    logits = jnp.where(lane < N_EXPERTS, logits, -jnp.inf)
    m1 = jnp.max(logits, axis=-1, keepdims=True)
    i1 = jnp.min(jnp.where(logits == m1, lane, LANES), axis=-1, keepdims=True)
    rest = jnp.where(lane == i1, -jnp.inf, logits)
    m2 = jnp.max(rest, axis=-1, keepdims=True)
    i2 = jnp.min(jnp.where(rest == m2, lane, LANES), axis=-1, keepdims=True)
    e = jnp.exp(m2 - m1)
    w1 = 1.0 / (1.0 + e)
    w2 = e / (1.0 + e)
    idx_ref[...] = jnp.where(lane == 0, i1, jnp.where(lane == 1, i2, 0))
    gw_ref[...] = jnp.where(lane == 0, w1, jnp.where(lane == 1, w2, 0.0))


def moe_router(x, g, w_router, b_router, *, tm=256):
    T, D = x.shape
    E = w_router.shape[1]
    w = jnp.pad(w_router, ((0, 0), (0, LANES - E)))
    b = jnp.pad(b_router, (0, LANES - E)).reshape(1, LANES)
    idx, gw = pl.pallas_call(
        _router_kernel,
        grid=(T // tm,),
        in_specs=[pl.BlockSpec((tm, D), lambda i: (i, 0)), pl.BlockSpec((1, D), lambda i: (0, 0)),
                  pl.BlockSpec((D, LANES), lambda i: (0, 0)), pl.BlockSpec((1, LANES), lambda i: (0, 0))],
        out_specs=[pl.BlockSpec((tm, LANES), lambda i: (i, 0)), pl.BlockSpec((tm, LANES), lambda i: (i, 0))],
        out_shape=[jax.ShapeDtypeStruct((T, LANES), jnp.int32), jax.ShapeDtypeStruct((T, LANES), F32)],
        compiler_params=_params(("parallel",), 8 * _nbytes((tm, D), F32)),
        name="moe_router",
    )(x, g.reshape(1, D), w, b)
    return idx[:, :TOP_K], gw[:, :TOP_K]


def _moe_dispatch(idx, tm):
    T = idx.shape[0]
    A = T * TOP_K
    e_flat = idx.reshape(A)
    onehot = (e_flat[:, None] == jnp.arange(N_EXPERTS, dtype=jnp.int32)[None, :]).astype(jnp.int32)
    csum = jnp.cumsum(onehot, axis=0)
    pos_in = jnp.sum(csum * onehot, axis=1) - 1
    counts = csum[-1]
    padded = ((counts + tm - 1) // tm) * tm
    gend = jnp.cumsum(padded)
    gstart = gend - padded
    dest = (jnp.sum(onehot * gstart[None, :], axis=1) + pos_in).astype(jnp.int32)
    n_rows = A + N_EXPERTS * tm
    row_tok = jnp.zeros((n_rows,), jnp.int32).at[dest].set(jnp.arange(A, dtype=jnp.int32) // TOP_K)
    nb = n_rows // tm
    n_used = (gend[-1] // tm).astype(jnp.int32)
    blk = jnp.arange(nb, dtype=jnp.int32)
    be = jnp.sum((blk[:, None] * tm >= gend[None, :]).astype(jnp.int32), axis=1)
    be = jnp.minimum(be, N_EXPERTS - 1)
    be = jnp.where(blk < n_used, be, be[jnp.maximum(n_used - 1, 0)]).astype(jnp.int32)
    eid = jnp.arange(N_EXPERTS, dtype=jnp.int32)
    later = (counts[None, :] > 0) & (eid[None, :] > eid[:, None])
    nxt_of = jnp.min(jnp.where(later, eid[None, :], N_EXPERTS), axis=1)
    nxt_of = jnp.where(nxt_of == N_EXPERTS, -1, nxt_of).astype(jnp.int32)
    nxt = jnp.sum((be[:, None] == eid[None, :]) * nxt_of[None, :], axis=1).astype(jnp.int32)
    return dest, row_tok, be, n_used.reshape(1), nxt


def _moe_gather_kernel(tok_ref, nu_ref, x_hbm, g_ref, o_ref, buf, sem, *, rows):
    i = pl.program_id(0)
    n_used = nu_ref[0]

    def row_copy(slot, r, src_row):
        return pltpu.make_async_copy(x_hbm.at[pl.ds(src_row, 1)], buf.at[slot, pl.ds(r, 1)], sem.at[slot])

    def fetch(block, slot):
        def issue(r, c):
            row_copy(slot, r, tok_ref[block * rows + r]).start()
            return c

        lax.fori_loop(0, rows, issue, 0, unroll=8)

    @pl.when(i == 0)
    def _():
        fetch(0, 0)

    @pl.when(i + 1 < n_used)
    def _():
        fetch(i + 1, (i + 1) % 2)

    @pl.when(i < n_used)
    def _():
        slot = i % 2

        def drain(r, c):
            row_copy(slot, r, 0).wait()
            return c

        lax.fori_loop(0, rows, drain, 0, unroll=8)
        x = buf[slot]
        o_ref[...] = (x * lax.rsqrt(jnp.mean(x * x, axis=-1, keepdims=True) + EPS) * g_ref[...]).astype(o_ref.dtype)

    @pl.when(i >= n_used)
    def _():
        o_ref[...] = jnp.zeros(o_ref.shape, o_ref.dtype)


def moe_gather_norm(x, g, row_tok, n_used, *, tm, rows=128):
    T, D = x.shape
    n_rows = row_tok.shape[0]
    per = tm // rows
    return pl.pallas_call(
        functools.partial(_moe_gather_kernel, rows=rows),
        grid_spec=pltpu.PrefetchScalarGridSpec(
            num_scalar_prefetch=2,
            grid=(n_rows // rows,),
            in_specs=[pl.BlockSpec(memory_space=pl.ANY), pl.BlockSpec((1, D), lambda i, tok, nu: (0, 0))],
            out_specs=pl.BlockSpec((rows, D), lambda i, tok, nu: (i, 0)),
            scratch_shapes=[pltpu.VMEM((2, rows, D), F32), pltpu.SemaphoreType.DMA((2,))]),
        out_shape=jax.ShapeDtypeStruct((n_rows, D), BF16),
        compiler_params=_params(("arbitrary",), 10 * _nbytes((rows, D), F32)),
        name="moe_gather",
    )(row_tok, n_used * per, x, g.reshape(1, D))


def _stream_group_weights(be_ref, nu_ref, nxt_ref, w_hbms, stage, sem, w_bf16, *, tile, n_tiles):
    j = pl.program_id(0)
    i = pl.program_id(1)

    def copies(e, jj):
        cols = pl.ds(pl.multiple_of(jj * tile, tile), tile)
        return [pltpu.make_async_copy(w.at[e, :, cols], stage.at[n], sem.at[n]) for n, w in enumerate(w_hbms)]

    @pl.when((j == 0) & (i == 0))
    def _():
        for c in copies(be_ref[0], 0):
            c.start()

    first_of_group = (i < nu_ref[0]) & ((i == 0) | (be_ref[i] != be_ref[jnp.maximum(i - 1, 0)]))

    @pl.when(first_of_group)
    def _():
        for c in copies(be_ref[i], j):
            c.wait()
        for n, dst in enumerate(w_bf16):
            dst[...] = stage[n].astype(BF16)
        more_in_sweep = nxt_ref[i] >= 0
        e_next = jnp.where(more_in_sweep, nxt_ref[i], be_ref[0])
        j_next = jnp.where(more_in_sweep, j, j + 1)

        @pl.when(more_in_sweep | (j + 1 < n_tiles))
        def _():
            for c in copies(e_next, j_next):
                c.start()


def _moe_gateup_kernel(be_ref, nu_ref, nxt_ref, x_ref, wg_hbm, wu_hbm, o_ref, stage, sem, wgb, wub, *, tf, n_tiles):
    i = pl.program_id(1)
    _stream_group_weights(be_ref, nu_ref, nxt_ref, (wg_hbm, wu_hbm), stage, sem, (wgb, wub), tile=tf, n_tiles=n_tiles)

    @pl.when(i < nu_ref[0])
    def _():
        x = x_ref[...]
        g = jnp.dot(x, wgb[...], preferred_element_type=F32)
        u = jnp.dot(x, wub[...], preferred_element_type=F32)
        o_ref[...] = (g * jax.nn.sigmoid(g) * u).astype(o_ref.dtype)

    @pl.when(i >= nu_ref[0])
    def _():
        o_ref[...] = jnp.zeros(o_ref.shape, o_ref.dtype)


def moe_gateup(xs, wg, wu, be, n_used, nxt, *, tm, tf=512):
    n_rows, K = xs.shape
    F = wg.shape[2]
    tf = min(tf, F)
    nb = n_rows // tm
    xmap = lambda j, i, be, nu, nxt: (jnp.minimum(i, nu[0] - 1), 0)
    hbm = pl.BlockSpec(memory_space=pl.ANY)
    vmem = (2 * _nbytes((tm, K), BF16) + 2 * _nbytes((K, tf), F32) + 4 * _nbytes((K, tf), BF16)
            + 6 * _nbytes((tm, tf), F32))
    return pl.pallas_call(
        functools.partial(_moe_gateup_kernel, tf=tf, n_tiles=F // tf),
        grid_spec=pltpu.PrefetchScalarGridSpec(
            num_scalar_prefetch=3,
            grid=(F // tf, nb),
            in_specs=[pl.BlockSpec((tm, K), xmap), hbm, hbm],
            out_specs=pl.BlockSpec((tm, tf), lambda j, i, be, nu, nxt: (i, j)),
            scratch_shapes=[pltpu.VMEM((2, K, tf), F32), pltpu.SemaphoreType.DMA((2,)),
                            pltpu.VMEM((K, tf), BF16), pltpu.VMEM((K, tf), BF16)]),
        out_shape=jax.ShapeDtypeStruct((n_rows, F), BF16),
        compiler_params=_params(("arbitrary", "arbitrary"), vmem),
        name="moe_gateup",
    )(be, n_used, nxt, xs, wg, wu)


def _moe_down_kernel(be_ref, nu_ref, nxt_ref, x_ref, w_hbm, o_ref, stage, sem, wb, *, tn, n_tiles):
    i = pl.program_id(1)
    _stream_group_weights(be_ref, nu_ref, nxt_ref, (w_hbm,), stage, sem, (wb,), tile=tn, n_tiles=n_tiles)

    @pl.when(i < nu_ref[0])
    def _():
        o_ref[...] = jnp.dot(x_ref[...], wb[...], preferred_element_type=F32)

    @pl.when(i >= nu_ref[0])
    def _():
        o_ref[...] = jnp.zeros(o_ref.shape, o_ref.dtype)


def moe_down(hm, wd, be, n_used, nxt, *, tm, tn=1024):
    n_rows, K = hm.shape
    N = wd.shape[2]
    tn = min(tn, N)
    nb = n_rows // tm
    xmap = lambda j, i, be, nu, nxt: (jnp.minimum(i, nu[0] - 1), 0)
    vmem = (2 * _nbytes((tm, K), BF16) + _nbytes((K, tn), F32) + 2 * _nbytes((K, tn), BF16)
            + 4 * _nbytes((tm, tn), F32))
    return pl.pallas_call(
        functools.partial(_moe_down_kernel, tn=tn, n_tiles=N // tn),
        grid_spec=pltpu.PrefetchScalarGridSpec(
            num_scalar_prefetch=3,
            grid=(N // tn, nb),
            in_specs=[pl.BlockSpec((tm, K), xmap), pl.BlockSpec(memory_space=pl.ANY)],
            out_specs=pl.BlockSpec((tm, tn), lambda j, i, be, nu, nxt: (i, j)),
            scratch_shapes=[pltpu.VMEM((1, K, tn), F32), pltpu.SemaphoreType.DMA((1,)),
                            pltpu.VMEM((K, tn), BF16)]),
        out_shape=jax.ShapeDtypeStruct((n_rows, N), F32),
        compiler_params=_params(("arbitrary", "arbitrary"), vmem),
        name="moe_down",
    )(be, n_used, nxt, hm, wd)


def _moe_combine_kernel(pos_ref, x_ref, gw_ref, ys_hbm, o_ref, buf, sem, *, rows):
    i = pl.program_id(0)
    n_blocks = pl.num_programs(0)

    def row_copy(slot, k, r, src_row):
        return pltpu.make_async_copy(ys_hbm.at[pl.ds(src_row, 1)], buf.at[slot, k, pl.ds(r, 1)], sem.at[slot])

    def fetch(block, slot):
        def issue(r, c):
            a = (block * rows + r) * TOP_K
            for k in range(TOP_K):
                row_copy(slot, k, r, pos_ref[a + k]).start()
            return c

        lax.fori_loop(0, rows, issue, 0, unroll=4)

    @pl.when(i == 0)
    def _():
        fetch(0, 0)

    @pl.when(i + 1 < n_blocks)
    def _():
        fetch(i + 1, (i + 1) % 2)

    slot = i % 2

    def drain(r, c):
        for k in range(TOP_K):
            row_copy(slot, k, r, 0).wait()
        return c

    lax.fori_loop(0, rows, drain, 0, unroll=4)
    out = x_ref[...]
    for k in range(TOP_K):
        out = out + gw_ref[:, k:k + 1] * buf[slot, k]
    o_ref[...] = out


def moe_combine(x, ys, dest, gw, *, rows=128):
    T, D = x.shape
    return pl.pallas_call(
        functools.partial(_moe_combine_kernel, rows=rows),
        grid_spec=pltpu.PrefetchScalarGridSpec(
            num_scalar_prefetch=1,
            grid=(T // rows,),
            in_specs=[pl.BlockSpec((rows, D), lambda i, pos: (i, 0)),
                      pl.BlockSpec((rows, TOP_K), lambda i, pos: (i, 0)),
                      pl.BlockSpec(memory_space=pl.ANY)],
            out_specs=pl.BlockSpec((rows, D), lambda i, pos: (i, 0)),
            scratch_shapes=[pltpu.VMEM((2, TOP_K, rows, D), F32), pltpu.SemaphoreType.DMA((2,))]),
        out_shape=jax.ShapeDtypeStruct((T, D), F32),
        compiler_params=_params(("arbitrary",), 12 * _nbytes((rows, D), F32)),
        name="moe_combine",
    )(dest, x, gw, ys)


def moe_block(x, g, w_router, b_router, w_egate, w_eup, w_edown, *, tm=512):
    idx, gw = moe_router(x, g, w_router, b_router)
    dest, row_tok, be, n_used, nxt = _moe_dispatch(idx, tm)
    xs = moe_gather_norm(x, g, row_tok, n_used, tm=tm)
    hm = moe_gateup(xs, w_egate, w_eup, be, n_used, nxt, tm=tm)
    ys = moe_down(hm, w_edown, be, n_used, nxt, tm=tm)
    return moe_combine(x, ys, dest, gw)


def even_mixer_block(x, tabs_a, tabs_b, w_in, ga_q, ga_k, g_cq, w_uq, g_ckv, w_ukv, gb_q, gb_kn, gb_kr,
                     w_o, g_mix, *, batch, seq):
    a_width = A_HEADS * HEAD_DIM
    main_cols = 3 * a_width + Q_LORA + KV_LORA
    h = rmsnorm(x, g_mix)
    proj = matmul(h, w_in, n_out=main_cols, tm=2048, single_buffer_x=True, name="in_proj")
    w_kr = jnp.pad(w_in[:, main_cols:], ((0, 0), (0, LANES - ROPE_DIM)))
    krp = matmul(h, w_kr, name="in_proj_kr")
    o_a = mixer_a(proj, tabs_a, ga_q, ga_k, batch=batch, seq=seq)
    cq = rmsnorm(proj, g_cq, width=Q_LORA, col_block=3 * a_width // Q_LORA)
    ckv = rmsnorm(proj, g_ckv, width=KV_LORA, col_block=(3 * a_width + Q_LORA) // KV_LORA)
    w3 = w_uq.reshape(Q_LORA, B_HEADS, QK_DIM)
    w_uq_p = jnp.concatenate(
        [w3[:, :, :NOPE_DIM].reshape(Q_LORA, B_HEADS * NOPE_DIM),
         jnp.pad(w3[:, :, NOPE_DIM:], ((0, 0), (0, 0), (0, LANES - ROPE_DIM))).reshape(Q_LORA, B_HEADS * LANES)],
        axis=1)
    q_up = matmul(cq, w_uq_p, name="q_up")
    kv_up = matmul(ckv, w_ukv, name="kv_up")
    pad_r = lambda v: jnp.pad(v, (0, LANES - ROPE_DIM)).reshape(1, LANES)
    qt, kf, vb = mla_prep(q_up, kv_up, krp, tabs_b, gb_q[:NOPE_DIM].reshape(1, LANES), pad_r(gb_q[NOPE_DIM:]),
                          gb_kn.reshape(1, LANES), pad_r(gb_kr), batch=batch, seq=seq)
    o_b = mla_attention(qt, kf, vb, batch=batch, seq=seq)
    return matmul((o_a, o_b), w_o, res=x, name="mixer_out")


def kernel(x, mem, positions, g_mem, w_mem_kv, g_mem_k, g_mix, g_x, w_xq, g_xq, w_xo, g_ffn,
           w_in, ga_q, ga_k, g_cq, w_uq, g_ckv, w_ukv, gb_q, gb_kn, gb_kr, w_o_even,
           w_gate, w_up, w_down, w_qkv, w_o_odd, w_router, b_router, w_egate, w_eup, w_edown):
    B, S, D = x.shape
    M = mem.shape[1]
    depth = g_mix.shape[0]
    T = B * S
    xt = x.reshape(T, D)
    tabs_a = _rope_tables(positions, ROT_DIM)
    tabs_b = _rope_tables(positions, ROPE_DIM)
    kv = matmul(rmsnorm(mem.reshape(B * M, D), g_mem), w_mem_kv, tm=B * M, name="mem_kv")
    mk, mv = memkv_post(kv, g_mem_k)
    for layer in range(depth):
        i = layer // 2
        if layer % 2 == 0:
            xt = even_mixer_block(xt, tabs_a, tabs_b, w_in[i], ga_q[i], ga_k[i], g_cq[i], w_uq[i], g_ckv[i],
                                  w_ukv[i], gb_q[i], gb_kn[i], gb_kr[i], w_o_even[i], g_mix[layer],
                                  batch=B, seq=S)
        else:
            h = rmsnorm(xt, g_mix[layer])
            qkv = matmul(h, w_qkv[i], out_dtype=BF16, tm=2048, single_buffer_x=True, name="qkv_proj")
            o = stickbreak_attention(qkv, batch=B, seq=S, heads=D // HEAD_DIM)
            xt = matmul(o, w_o_odd[i], res=xt, name="mixer_out")
        xattn = functools.partial(cross_attention_block, xt, mk, mv, g_x[layer], w_xq[layer], g_xq[layer],
                                  w_xo[layer], seq=S, mem_len=M)
        if layer % 2 == 0:
            xt, h = xattn(g_ffn[layer])
            hm = swiglu_gateup(h, w_gate[i], w_up[i])
            xt = matmul(hm, w_down[i], res=xt, tn=1024, tk=2048, name="swiglu_down")
        else:
            xt = moe_block(xattn(), g_ffn[layer], w_router[i], b_router[i], w_egate[i], w_eup[i], w_edown[i])
    return xt.reshape(B, S, D)
```

```python
import functools

import jax
import jax.numpy as jnp
from jax import lax
from jax.experimental import pallas as pl
from jax.experimental.pallas import tpu as pltpu

F32 = jnp.float32
BF16 = jnp.bfloat16

HEAD_DIM = 128
ROT_DIM = HEAD_DIM // 4
ROPE_THETA = 500000.0
BLOCK = 128
NEG_INF = -1e30
EPS = 1e-6
DILATED_PAIRS = ((128, 1), (512, 4), (2048, 16))
A_HEADS = 16
B_HEADS = 16
Q_LORA = 1536
KV_LORA = 512
NOPE_DIM = 128
ROPE_DIM = 64
QK_DIM = NOPE_DIM + ROPE_DIM
X_HEADS = 4
N_EXPERTS = 8
TOP_K = 2
MIXER_A_GROUP = 8
MOE_ROW_STEP = 128
LOG2_E = 1.4426950408889634
MLA_Q_SCALE = QK_DIM ** -0.5 * LOG2_E

LANES = 128
V7X_VMEM_BYTES = 64 * 1024 * 1024
VMEM_CAP = V7X_VMEM_BYTES - 6 * 1024 * 1024


def _params(semantics, vmem_bytes):
    return pltpu.CompilerParams(dimension_semantics=semantics,
                                vmem_limit_bytes=int(min(max(vmem_bytes, 32 * 1024 * 1024), VMEM_CAP)))


def _nbytes(shape, dtype):
    n = 1
    for s in shape:
        n *= s
    return n * jnp.dtype(dtype).itemsize


def _rmsnorm_kernel(x_ref, g_ref, o_ref):
    x = x_ref[...].astype(F32)
    ms = jnp.mean(x * x, axis=-1, keepdims=True)
    o_ref[...] = (x * lax.rsqrt(ms + EPS) * g_ref[...]).astype(o_ref.dtype)


def rmsnorm(x, g, *, width=None, col_block=0, tm=256, out_dtype=BF16):
    T = x.shape[0]
    width = x.shape[1] if width is None else width
    tm = min(tm, T)
    return pl.pallas_call(
        _rmsnorm_kernel,
        grid=(T // tm,),
        in_specs=[pl.BlockSpec((tm, width), lambda i: (i, col_block)),
                  pl.BlockSpec((1, width), lambda i: (0, 0))],
        out_specs=pl.BlockSpec((tm, width), lambda i: (i, 0)),
        out_shape=jax.ShapeDtypeStruct((T, width), out_dtype),
        compiler_params=_params(("parallel",), 6 * _nbytes((tm, width), F32)),
        name="rmsnorm",
    )(x, g.reshape(1, width).astype(F32))


def _mm_kernel(*refs, nk, n_x, has_res):
    x_refs = refs[:n_x]
    w_ref = refs[n_x]
    rest = refs[n_x + 1:]
    res_ref = rest[0] if has_res else None
    o_ref = rest[1] if has_res else rest[0]
    acc_ref = rest[-1] if nk > 1 else None
    k = pl.program_id(2)
    w = w_ref[...].astype(BF16)

    def finish(part):
        out = part
        if has_res:
            out = out + res_ref[...]
        o_ref[...] = out.astype(o_ref.dtype)

    if n_x == 2:
        half = x_refs[0].shape[1]
        finish(jnp.dot(x_refs[0][...], w[:half], preferred_element_type=F32)
               + jnp.dot(x_refs[1][...], w[half:], preferred_element_type=F32))
    elif nk == 1:
        finish(jnp.dot(x_refs[0][...], w, preferred_element_type=F32))
    else:
        part = jnp.dot(x_refs[0][...], w, preferred_element_type=F32)

        @pl.when(k == 0)
        def _():
            acc_ref[...] = part

        @pl.when((k > 0) & (k < nk - 1))
        def _():
            acc_ref[...] += part

        @pl.when(k == nk - 1)
        def _():
            finish(acc_ref[...] + part)


def matmul(xs, w, *, n_out=None, res=None, out_dtype=F32, tm=1024, tn=512, tk=None, single_buffer_x=False,
           name="matmul"):
    if not isinstance(xs, (tuple, list)):
        xs = (xs,)
    n_x = len(xs)
    T = xs[0].shape[0]
    K = sum(x.shape[1] for x in xs)
    n_out = w.shape[1] if n_out is None else n_out
    tm = min(tm, T)
    tn = min(tn, n_out)
    if n_x == 2:
        assert tk is None and xs[1].shape[1] == xs[0].shape[1]
    tk = K if tk is None else tk
    nk = K // tk
    assert K == nk * tk and T % tm == 0 and n_out % tn == 0
    x_buffers = 2
    if n_x == 2:
        x_specs = [pl.BlockSpec((tm, K // 2), lambda i, j, k: (i, 0)) for _ in xs]
    elif nk == 1 and single_buffer_x:
        x_buffers = 1
        x_specs = [pl.BlockSpec((tm, tk), lambda i, j, k: (i, k), pipeline_mode=pl.Buffered(1))]
    else:
        x_specs = [pl.BlockSpec((tm, tk), lambda i, j, k: (i, k))]
    in_specs = x_specs + [pl.BlockSpec((tk, tn), lambda i, j, k: (k, j))]
    args = list(xs) + [w]
    if res is not None:
        in_specs.append(pl.BlockSpec((tm, tn), lambda i, j, k: (i, j)))
        args.append(res)
    scratch = [pltpu.VMEM((tm, tn), F32)] if nk > 1 else []
    vmem = (x_buffers * _nbytes((tm, tk), xs[0].dtype) + 2 * _nbytes((tk, tn), F32) + _nbytes((tk, tn), BF16)
            + (2 * _nbytes((tm, tn), F32) if res is not None else 0)
            + 2 * _nbytes((tm, tn), out_dtype) + 2 * _nbytes((tm, tn), F32))
    return pl.pallas_call(
        functools.partial(_mm_kernel, nk=nk, n_x=n_x, has_res=res is not None),
        grid=(T // tm, n_out // tn, nk),
        in_specs=in_specs,
        out_specs=pl.BlockSpec((tm, tn), lambda i, j, k: (i, j)),
        out_shape=jax.ShapeDtypeStruct((T, n_out), out_dtype),
        scratch_shapes=scratch,
        compiler_params=_params(("parallel", "parallel", "arbitrary"), vmem),
        name=name,
    )(*args)


def _rope_tables(positions, dim):
    half = dim // 2
    inv_freq = ROPE_THETA ** (-jnp.arange(0, dim, 2, dtype=F32) / dim)
    ang = positions.astype(F32)[..., None] * inv_freq
    cos, sin = jnp.cos(ang), jnp.sin(ang)
    B, S = positions.shape
    ones = jnp.ones((B, S, LANES - dim), F32)
    zeros = jnp.zeros((B, S, LANES - dim), F32)
    zh = jnp.zeros((B, S, half), F32)
    c = jnp.concatenate([cos, cos, ones], axis=-1)
    sa = jnp.concatenate([-sin, zh, zeros], axis=-1)
    sb = jnp.concatenate([zh, sin, zeros], axis=-1)
    return [t.reshape(B * S, LANES) for t in (c, sa, sb)]


def _rope_lanes(y, c, sa, sb, half):
    return y * c + pltpu.roll(y, LANES - half, 1) * sa + pltpu.roll(y, half, 1) * sb


def _mixer_a_kernel(q_ref, k_ref, v_ref, c_ref, sa_ref, sb_ref, gq_ref, gk_ref, o_ref,
                    qs, ks, *state, seq):
    scale = HEAD_DIM ** -0.5 * LOG2_E
    c, sa, sb = c_ref[...], sa_ref[...], sb_ref[...]

    def prep(x, g):
        y = x * lax.rsqrt(jnp.mean(x * x, axis=-1, keepdims=True) + EPS) * g
        return _rope_lanes(y, c, sa, sb, ROT_DIM // 2)

    qs[...] = prep(q_ref[...], gq_ref[...])
    ks[...] = prep(k_ref[...], gk_ref[...])
    qi = lax.broadcasted_iota(jnp.int32, (BLOCK, 1), 0)
    kj = lax.broadcasted_iota(jnp.int32, (1, BLOCK), 1)
    trans_b = (((1,), (1,)), ((), ()))
    ones = jnp.ones((BLOCK, HEAD_DIM), BF16)

    for bi, (window, dil) in enumerate(DILATED_PAIRS):
        acc, m_sc, l_sc = state[3 * bi:3 * bi + 3]
        n_back = window // dil
        nb = seq // dil // BLOCK
        assert n_back <= BLOCK and nb * dil * BLOCK == seq
        cur_ok = (qi >= kj) & (qi - kj <= n_back)
        prev_ok = qi + BLOCK - kj <= n_back

        def rows_at(r, n, dil=dil):
            if dil == 1:
                return pl.ds(BLOCK * n, BLOCK)
            return pl.ds(r + dil * BLOCK * n, BLOCK, stride=dil)

        blocks = [(rows_at(r, n), rows_at(r, n - 1) if n > 0 else None) for r in range(dil) for n in range(nb)]
        for first in range(0, len(blocks), MIXER_A_GROUP):
            group = blocks[first:first + MIXER_A_GROUP]
            s_cur, s_prev = [], []
            for rows, prev in group:
                qb = qs[rows, :].astype(BF16)
                s_c = lax.dot_general(qb, ks[rows, :].astype(BF16), trans_b, preferred_element_type=F32) * scale
                s_cur.append(jnp.where(cur_ok, s_c, NEG_INF))
                if prev is None:
                    s_prev.append(None)
                else:
                    s_p = lax.dot_general(qb, ks[prev, :].astype(BF16), trans_b,
                                          preferred_element_type=F32) * scale
                    s_prev.append(jnp.where(prev_ok, s_p, NEG_INF))
            stats = []
            for s_c, s_p in zip(s_cur, s_prev):
                m = jnp.max(s_c, axis=-1, keepdims=True)
                if s_p is not None:
                    m = jnp.maximum(m, jnp.max(s_p, axis=-1, keepdims=True))
                p_c = jnp.exp2(s_c - m).astype(BF16)
                p_p = None if s_p is None else jnp.exp2(s_p - m).astype(BF16)
                stats.append((m, p_c, p_p))
            for (rows, prev), (m, p_c, p_p) in zip(group, stats):
                v_ext = jnp.concatenate([v_ref[rows, :].astype(BF16), ones], axis=1)
                o = jnp.dot(p_c, v_ext, preferred_element_type=F32)
                if p_p is not None:
                    v_ext = jnp.concatenate([v_ref[prev, :].astype(BF16), ones], axis=1)
                    o = o + jnp.dot(p_p, v_ext, preferred_element_type=F32)
                acc[rows, :] = o[:, :HEAD_DIM]
                l_sc[rows, :] = o[:, HEAD_DIM:]
                m_sc[rows, :] = jnp.broadcast_to(m, (BLOCK, HEAD_DIM))

    n_br = len(DILATED_PAIRS)
    m_all = state[1][...]
    for bi in range(1, n_br):
        m_all = jnp.maximum(m_all, state[3 * bi + 1][...])
    num = jnp.zeros(o_ref.shape, F32)
    den = jnp.zeros(o_ref.shape, F32)
    for bi in range(n_br):
        w = jnp.exp2(state[3 * bi + 1][...] - m_all)
        num = num + w * state[3 * bi][...]
        den = den + w * state[3 * bi + 2][...]
    o_ref[...] = (num / den).astype(o_ref.dtype)


def mixer_a(proj, tabs, gq, gk, *, batch, seq):
    T = batch * seq
    blk = (seq, HEAD_DIM)
    head = lambda off: pl.BlockSpec(blk, lambda b, h: (b, off + h))
    tab = pl.BlockSpec(blk, lambda b, h: (b, 0))
    gspec = pl.BlockSpec((1, HEAD_DIM), lambda b, h: (0, 0))
    return pl.pallas_call(
        functools.partial(_mixer_a_kernel, seq=seq),
        grid=(batch, A_HEADS),
        in_specs=[head(0), head(A_HEADS), head(2 * A_HEADS), tab, tab, tab, gspec, gspec],
        out_specs=pl.BlockSpec(blk, lambda b, h: (b, h)),
        out_shape=jax.ShapeDtypeStruct((T, A_HEADS * HEAD_DIM), BF16),
        scratch_shapes=[pltpu.VMEM(blk, F32) for _ in range(2 + 3 * len(DILATED_PAIRS))],
        compiler_params=_params(("parallel", "parallel"), 30 * _nbytes(blk, F32)),
        name="mixer_a",
    )(proj, proj, proj, *tabs, gq.reshape(1, HEAD_DIM), gk.reshape(1, HEAD_DIM))


def _mla_prep_kernel(qn_ref, qr_ref, kn_ref, v_ref, kr_ref, c_ref, sa_ref, sb_ref,
                     gqn_ref, gqr_ref, gkn_ref, gkr_ref, qt_ref, kf_ref, vb_ref, kr_sc):
    c, sa, sb = c_ref[...], sa_ref[...], sb_ref[...]
    half = ROPE_DIM // 2

    @pl.when(pl.program_id(1) == 0)
    def _():
        kr = kr_ref[...]
        kr = kr * lax.rsqrt(jnp.sum(kr * kr, axis=-1, keepdims=True) * (1.0 / ROPE_DIM) + EPS) * gkr_ref[...]
        kr_sc[...] = _rope_lanes(kr, c, sa, sb, half).astype(kr_sc.dtype)

    qn, qr = qn_ref[...], qr_ref[...]
    ss = jnp.sum(qn * qn, axis=-1, keepdims=True) + jnp.sum(qr * qr, axis=-1, keepdims=True)
    inv = lax.rsqrt(ss * (1.0 / QK_DIM) + EPS) * MLA_Q_SCALE
    qf = jnp.concatenate([qn * inv * gqn_ref[...], _rope_lanes(qr * inv * gqr_ref[...], c, sa, sb, half)], axis=1)
    qt_ref[...] = qf.astype(qt_ref.dtype).T
    kn = kn_ref[...]
    kn = kn * lax.rsqrt(jnp.mean(kn * kn, axis=-1, keepdims=True) + EPS) * gkn_ref[...]
    kf_ref[:, :NOPE_DIM] = kn.astype(kf_ref.dtype)
    kf_ref[:, NOPE_DIM:] = kr_sc[...]
    vb_ref[...] = v_ref[...].astype(vb_ref.dtype)


def mla_prep(q_up, kv_up, krp, tabs, gqn, gqr, gkn, gkr, *, batch, seq, tm=512):
    T = q_up.shape[0]
    H = B_HEADS
    per_seq = seq // tm
    blk = lambda f: pl.BlockSpec((tm, LANES), f)
    g = pl.BlockSpec((1, LANES), lambda i, h: (0, 0))
    return pl.pallas_call(
        _mla_prep_kernel,
        grid=(T // tm, H),
        in_specs=[blk(lambda i, h: (i, h)), blk(lambda i, h: (i, H + h)),
                  blk(lambda i, h: (i, 2 * h)), blk(lambda i, h: (i, 2 * h + 1)),
                  blk(lambda i, h: (i, 0)), blk(lambda i, h: (i, 0)), blk(lambda i, h: (i, 0)),
                  blk(lambda i, h: (i, 0)), g, g, g, g],
        out_specs=[pl.BlockSpec((None, None, 2 * LANES, tm), lambda i, h: (i // per_seq, h, 0, i % per_seq)),
                   pl.BlockSpec((tm, 2 * LANES), lambda i, h: (i, h)),
                   pl.BlockSpec((tm, LANES), lambda i, h: (i, h))],
        out_shape=[jax.ShapeDtypeStruct((batch, H, 2 * LANES, seq), BF16),
                   jax.ShapeDtypeStruct((T, H * 2 * LANES), BF16),
                   jax.ShapeDtypeStruct((T, H * LANES), BF16)],
        scratch_shapes=[pltpu.VMEM((tm, LANES), BF16)],
        compiler_params=_params(("parallel", "arbitrary"), 40 * _nbytes((tm, LANES), F32)),
        name="mla_prep",
    )(q_up, q_up, kv_up, kv_up, krp, *tabs, gqn, gqr, gkn, gkr)


def _causal_attn_kernel(qt_ref, k_ref, v_ref, o_ref, s_sc, acc, *, tq, hg, dq, dv):
    qi = pl.program_id(2)
    key = lax.broadcasted_iota(jnp.int32, (tq, tq), 0)
    qry = lax.broadcasted_iota(jnp.int32, (tq, tq), 1)
    visible = key <= qry
    ones = jnp.ones((tq, dv), BF16)
    contract_rows = (((0,), (0,)), ((), ()))

    heads = range(hg)

    def scores(j, diagonal):
        rows = pl.ds(pl.multiple_of(j * tq, tq), tq)
        s = [jnp.dot(k_ref[rows, g * dq:(g + 1) * dq], qt_ref[g], preferred_element_type=F32) for g in heads]
        if diagonal:
            s = [jnp.where(visible, sg, NEG_INF) for sg in s]
        for g in heads:
            s_sc[g, j] = s[g]
        return s

    def col_max(j, ms):
        s = scores(j, False)
        return tuple(jnp.maximum(ms[g], jnp.max(s[g], axis=0, keepdims=True)) for g in heads)

    ms = lax.fori_loop(0, qi, col_max, tuple(jnp.max(sg, axis=0, keepdims=True) for sg in scores(qi, True)))

    def weighted(j):
        rows = pl.ds(pl.multiple_of(j * tq, tq), tq)
        p = [jnp.exp2(s_sc[g, j] - ms[g]).astype(BF16) for g in heads]
        return [lax.dot_general(p[g], jnp.concatenate([v_ref[rows, g * dv:(g + 1) * dv], ones], axis=1),
                                contract_rows, preferred_element_type=F32) for g in heads]

    for g, w in enumerate(weighted(qi)):
        acc[g] = w

    def below(j, carry):
        for g, w in enumerate(weighted(j)):
            acc[g] += w
        return carry

    lax.fori_loop(0, qi, below, 0)
    for g in heads:
        a = acc[g]
        o_ref[:, g * dv:(g + 1) * dv] = (a[:, :dv] / a[:, dv:]).astype(o_ref.dtype)


def mla_attention(qt, kf, vb, *, batch, seq, tq=512, hg=4):
    T = batch * seq
    nq = seq // tq
    dq = qt.shape[2]
    dv = vb.shape[1] // B_HEADS
    assert B_HEADS % hg == 0
    return pl.pallas_call(
        functools.partial(_causal_attn_kernel, tq=tq, hg=hg, dq=dq, dv=dv),
        grid=(batch, B_HEADS // hg, nq),
        in_specs=[pl.BlockSpec((None, hg, dq, tq), lambda b, h, i: (b, h, 0, i)),
                  pl.BlockSpec((seq, hg * dq), lambda b, h, i: (b, h)),
                  pl.BlockSpec((seq, hg * dv), lambda b, h, i: (b, h))],
        out_specs=pl.BlockSpec((tq, hg * dv), lambda b, h, i: (b * nq + i, h)),
        out_shape=jax.ShapeDtypeStruct((T, B_HEADS * dv), BF16),
        scratch_shapes=[pltpu.VMEM((hg, nq, tq, tq), F32), pltpu.VMEM((hg, tq, 2 * dv), F32)],
        compiler_params=_params(("parallel", "parallel", "parallel"),
                                _nbytes((hg, nq, tq, tq), F32) + 2 * _nbytes((seq, hg * (dq + dv)), BF16)
                                + 8 * hg * _nbytes((tq, tq), F32)),
        name="mla_attention",
    )(qt, kf, vb)


STICK_EXHAUSTED_LOG = -110.0


def _stickbreak_kernel(q_ref, k_ref, v_ref, o_ref, qt_sc, c_sc, acc, *, tq, hg, scale):
    qi = pl.program_id(2)
    c_sc[...] = jnp.zeros(c_sc.shape, F32)
    acc[...] = jnp.zeros(acc.shape, F32)
    for g in range(hg):
        qt_sc[g] = q_ref[:, g * HEAD_DIM:(g + 1) * HEAD_DIM].T
    key = lax.broadcasted_iota(jnp.int32, (tq, tq), 0)
    qry = lax.broadcasted_iota(jnp.int32, (tq, tq), 1)
    after = (qry > key).astype(BF16)
    earlier = key < qry
    contract_rows = (((0,), (0,)), ((), ()))

    def chunk(j, diagonal):
        rows = pl.ds(pl.multiple_of(j * tq, tq), tq)
        heads = range(hg)
        cols = [slice(g * HEAD_DIM, (g + 1) * HEAD_DIM) for g in heads]
        z = [jnp.dot(k_ref[rows, cols[g]], qt_sc[g], preferred_element_type=F32) * (scale * LOG2_E) for g in heads]
        log_beta, parts = [], []
        for g in heads:
            softplus = jnp.maximum(z[g], 0.0) + jnp.log2(1.0 + jnp.exp2(-jnp.abs(z[g])))
            log_keep = -softplus
            if diagonal:
                log_keep = jnp.where(earlier, log_keep, 0.0)
            parts.append(log_keep.astype(BF16))
            log_beta.append(z[g] - softplus + c_sc[g])
            c_sc[g] += jnp.sum(log_keep, axis=0, keepdims=True)
        within = jnp.dot(after, jnp.concatenate(parts, axis=1), preferred_element_type=F32)
        for g in heads:
            a = jnp.exp2(log_beta[g] + within[:, g * tq:(g + 1) * tq])
            if diagonal:
                a = jnp.where(earlier, a, 0.0)
            acc[:, cols[g]] += lax.dot_general(a.astype(BF16), v_ref[rows, cols[g]], contract_rows,
                                               preferred_element_type=F32)

    def stick_left():
        return jnp.max(c_sc[...]) > STICK_EXHAUSTED_LOG * LOG2_E

    chunk(qi, True)

    def more(state):
        j, left = state
        return (j >= 0) & left

    def step(state):
        j, _ = state
        chunk(j, False)
        return j - 1, stick_left()

    lax.while_loop(more, step, (qi - 1, stick_left()))
    o_ref[...] = acc[...].astype(o_ref.dtype)


def stickbreak_attention(qkv, *, batch, seq, heads, tq=256, hg=4):
    T = batch * seq
    nq = seq // tq
    ng = heads // hg
    assert heads == ng * hg
    blk = hg * HEAD_DIM
    return pl.pallas_call(
        functools.partial(_stickbreak_kernel, tq=tq, hg=hg, scale=HEAD_DIM ** -0.5),
        grid=(batch, ng, nq),
        in_specs=[pl.BlockSpec((tq, blk), lambda b, h, i: (b * nq + i, h)),
                  pl.BlockSpec((seq, blk), lambda b, h, i: (b, ng + h)),
                  pl.BlockSpec((seq, blk), lambda b, h, i: (b, 2 * ng + h))],
        out_specs=pl.BlockSpec((tq, blk), lambda b, h, i: (b * nq + i, h)),
        out_shape=jax.ShapeDtypeStruct((T, heads * HEAD_DIM), BF16),
        scratch_shapes=[pltpu.VMEM((hg, HEAD_DIM, tq), BF16), pltpu.VMEM((hg, 1, tq), F32),
                        pltpu.VMEM((tq, blk), F32)],
        compiler_params=_params(("parallel", "parallel", "parallel"), 0),
        name="stickbreak",
    )(qkv, qkv, qkv)


def _memkv_kernel(kv_ref, g_ref, mk_ref, mv_ref):
    for h in range(X_HEADS):
        cols = slice(h * HEAD_DIM, (h + 1) * HEAD_DIM)
        k = kv_ref[:, cols]
        k = k * lax.rsqrt(jnp.mean(k * k, axis=-1, keepdims=True) + EPS) * g_ref[...]
        mk_ref[:, cols] = k.astype(mk_ref.dtype)
    mv_ref[...] = kv_ref[:, X_HEADS * HEAD_DIM:].astype(mv_ref.dtype)


def memkv_post(kv, g):
    M = kv.shape[0]
    xd = X_HEADS * HEAD_DIM
    return pl.pallas_call(
        _memkv_kernel,
        grid=(1,),
        in_specs=[pl.BlockSpec((M, 2 * xd), lambda i: (0, 0)), pl.BlockSpec((1, HEAD_DIM), lambda i: (0, 0))],
        out_specs=[pl.BlockSpec((M, xd), lambda i: (0, 0)), pl.BlockSpec((M, xd), lambda i: (0, 0))],
        out_shape=[jax.ShapeDtypeStruct((M, xd), BF16), jax.ShapeDtypeStruct((M, xd), BF16)],
        compiler_params=_params(("arbitrary",), 0),
        name="memkv_post",
    )(kv, g.reshape(1, HEAD_DIM))


def _xattn_block_kernel(x_ref, gx_ref, wq_ref, gq_ref, mk_ref, mv_ref, wo_ref, *rest, emit_norm):
    if emit_norm:
        gn_ref, o_ref, hn_ref = rest
    else:
        (o_ref,) = rest
    scale = HEAD_DIM ** -0.5
    x = x_ref[...]
    h = (x * lax.rsqrt(jnp.mean(x * x, axis=-1, keepdims=True) + EPS) * gx_ref[...]).astype(BF16)
    q_all = jnp.dot(h, wq_ref[...], preferred_element_type=F32)
    heads = []
    for hd in range(X_HEADS):
        cols = slice(hd * HEAD_DIM, (hd + 1) * HEAD_DIM)
        q = q_all[:, cols]
        q = (q * lax.rsqrt(jnp.mean(q * q, axis=-1, keepdims=True) + EPS) * gq_ref[...]).astype(BF16)
        s = lax.dot_general(q, mk_ref[:, cols], (((1,), (1,)), ((), ())), preferred_element_type=F32) * scale
        p = jnp.exp(s - jnp.max(s, axis=-1, keepdims=True))
        p = p / jnp.sum(p, axis=-1, keepdims=True)
        heads.append(jnp.dot(p.astype(BF16), mv_ref[:, cols], preferred_element_type=F32).astype(BF16))
    y = x + jnp.dot(jnp.concatenate(heads, axis=1), wo_ref[...], preferred_element_type=F32)
    o_ref[...] = y
    if emit_norm:
        hn_ref[...] = (y * lax.rsqrt(jnp.mean(y * y, axis=-1, keepdims=True) + EPS) * gn_ref[...]).astype(hn_ref.dtype)


def cross_attention_block(x, mk, mv, g_x, w_xq, g_xq, w_xo, g_next=None, *, seq, mem_len, tm=256):
    T, D = x.shape
    xd = X_HEADS * HEAD_DIM
    per_seq = seq // tm
    emit_norm = g_next is not None
    row = lambda w: pl.BlockSpec((1, w), lambda i: (0, 0))
    tok = pl.BlockSpec((tm, D), lambda i: (i, 0))
    in_specs = [tok, row(D), pl.BlockSpec((D, xd), lambda i: (0, 0)), row(HEAD_DIM),
                pl.BlockSpec((mem_len, xd), lambda i: (i // per_seq, 0)),
                pl.BlockSpec((mem_len, xd), lambda i: (i // per_seq, 0)),
                pl.BlockSpec((xd, D), lambda i: (0, 0))]
    args = [x, g_x.reshape(1, D), w_xq.astype(BF16), g_xq.reshape(1, HEAD_DIM), mk, mv, w_xo.astype(BF16)]
    out_specs, out_shape = tok, jax.ShapeDtypeStruct((T, D), F32)
    if emit_norm:
        in_specs.append(row(D))
        args.append(g_next.reshape(1, D))
        out_specs = [tok, tok]
        out_shape = [out_shape, jax.ShapeDtypeStruct((T, D), BF16)]
    vmem = 10 * _nbytes((tm, D), F32) + 4 * _nbytes((D, xd), BF16)
    return pl.pallas_call(
        functools.partial(_xattn_block_kernel, emit_norm=emit_norm),
        grid=(T // tm,),
        in_specs=in_specs,
        out_specs=out_specs,
        out_shape=out_shape,
        compiler_params=_params(("parallel",), vmem),
        name="cross_attention",
    )(*args)


def _gateup_kernel(x_ref, wg_ref, wu_ref, o_ref):
    x = x_ref[...]
    g = jnp.dot(x, wg_ref[...].astype(BF16), preferred_element_type=F32)
    u = jnp.dot(x, wu_ref[...].astype(BF16), preferred_element_type=F32)
    o_ref[...] = (g * jax.nn.sigmoid(g) * u).astype(o_ref.dtype)


def swiglu_gateup(x, wg, wu, *, tm=2048, tf=256):
    T, K = x.shape
    F = wg.shape[1]
    vmem = (_nbytes((tm, K), BF16) + 4 * _nbytes((K, tf), F32) + 2 * _nbytes((K, tf), BF16)
            + 6 * _nbytes((tm, tf), F32))
    return pl.pallas_call(
        _gateup_kernel,
        grid=(T // tm, F // tf),
        in_specs=[pl.BlockSpec((tm, K), lambda i, j: (i, 0), pipeline_mode=pl.Buffered(1)),
                  pl.BlockSpec((K, tf), lambda i, j: (0, j)),
                  pl.BlockSpec((K, tf), lambda i, j: (0, j))],
        out_specs=pl.BlockSpec((tm, tf), lambda i, j: (i, j)),
        out_shape=jax.ShapeDtypeStruct((T, F), BF16),
        compiler_params=_params(("parallel", "parallel"), vmem),
        name="swiglu_gateup",
    )(x, wg, wu)


def _router_kernel(x_ref, g_ref, w_ref, b_ref, idx_ref, gw_ref):
    x = x_ref[...]
    h = x * lax.rsqrt(jnp.mean(x * x, axis=-1, keepdims=True) + EPS) * g_ref[...]
    logits = jnp.dot(h, w_ref[...], preferred_element_type=F32, precision=lax.Precision.HIGHEST) + b_ref[...]
    lane = lax.broadcasted_iota(jnp.int32, logits.shape, 1)
    logits = jnp.where(lane < N_EXPERTS, logits, -jnp.inf)
    m1 = jnp.max(logits, axis=-1, keepdims=True)
    i1 = jnp.min(jnp.where(logits == m1, lane, LANES), axis=-1, keepdims=True)
    rest = jnp.where(lane == i1, -jnp.inf, logits)
    m2 = jnp.max(rest, axis=-1, keepdims=True)
    i2 = jnp.min(jnp.where(rest == m2, lane, LANES), axis=-1, keepdims=True)
    e = jnp.exp(m2 - m1)
    w1 = 1.0 / (1.0 + e)
    w2 = e / (1.0 + e)
    idx_ref[...] = jnp.where(lane == 0, i1, jnp.where(lane == 1, i2, 0))
    gw_ref[...] = jnp.where(lane == 0, w1, jnp.where(lane == 1, w2, 0.0))


def moe_router(x, g, w_router, b_router, *, tm=256):
    T, D = x.shape
    E = w_router.shape[1]
    w = jnp.pad(w_router, ((0, 0), (0, LANES - E)))
    b = jnp.pad(b_router, (0, LANES - E)).reshape(1, LANES)
    idx, gw = pl.pallas_call(
        _router_kernel,
        grid=(T // tm,),
        in_specs=[pl.BlockSpec((tm, D), lambda i: (i, 0)), pl.BlockSpec((1, D), lambda i: (0, 0)),
                  pl.BlockSpec((D, LANES), lambda i: (0, 0)), pl.BlockSpec((1, LANES), lambda i: (0, 0))],
        out_specs=[pl.BlockSpec((tm, LANES), lambda i: (i, 0)), pl.BlockSpec((tm, LANES), lambda i: (i, 0))],
        out_shape=[jax.ShapeDtypeStruct((T, LANES), jnp.int32), jax.ShapeDtypeStruct((T, LANES), F32)],
        compiler_params=_params(("parallel",), 8 * _nbytes((tm, D), F32)),
        name="moe_router",
    )(x, g.reshape(1, D), w, b)
    return idx[:, :TOP_K], gw[:, :TOP_K]


def _moe_dispatch(idx, tm):
    T = idx.shape[0]
    A = T * TOP_K
    e_flat = idx.reshape(A)
    onehot = (e_flat[:, None] == jnp.arange(N_EXPERTS, dtype=jnp.int32)[None, :]).astype(jnp.int32)
    csum = jnp.cumsum(onehot, axis=0)
    pos_in = jnp.sum(csum * onehot, axis=1) - 1
    counts = csum[-1]
    padded = ((counts + tm - 1) // tm) * tm
    gend = jnp.cumsum(padded)
    gstart = gend - padded
    dest = (jnp.sum(onehot * gstart[None, :], axis=1) + pos_in).astype(jnp.int32)
    n_rows = A + N_EXPERTS * tm
    row_tok = jnp.zeros((n_rows,), jnp.int32).at[dest].set(jnp.arange(A, dtype=jnp.int32) // TOP_K)
    nb = n_rows // tm
    n_used = (gend[-1] // tm).astype(jnp.int32)
    blk = jnp.arange(nb, dtype=jnp.int32)
    be = jnp.sum((blk[:, None] * tm >= gend[None, :]).astype(jnp.int32), axis=1)
    be = jnp.minimum(be, N_EXPERTS - 1)
    be = jnp.where(blk < n_used, be, be[jnp.maximum(n_used - 1, 0)]).astype(jnp.int32)
    eid = jnp.arange(N_EXPERTS, dtype=jnp.int32)
    later = (counts[None, :] > 0) & (eid[None, :] > eid[:, None])
    nxt_of = jnp.min(jnp.where(later, eid[None, :], N_EXPERTS), axis=1)
    nxt_of = jnp.where(nxt_of == N_EXPERTS, -1, nxt_of).astype(jnp.int32)
    mine = be[:, None] == eid[None, :]
    nxt = jnp.sum(mine * nxt_of[None, :], axis=1).astype(jnp.int32)
    group_end_row = jnp.sum(mine * (gstart + counts)[None, :], axis=1)
    nv = jnp.where(blk < n_used, jnp.clip(group_end_row - blk * tm, 0, tm), 0).astype(jnp.int32)
    return dest, row_tok, be, n_used.reshape(1), nxt, nv


def _moe_gather_kernel(tok_ref, nu_ref, x_hbm, g_ref, o_ref, buf, sem, *, rows):
    i = pl.program_id(0)
    n_used = nu_ref[0]

    def row_copy(slot, r, src_row):
        return pltpu.make_async_copy(x_hbm.at[pl.ds(src_row, 1)], buf.at[slot, pl.ds(r, 1)], sem.at[slot])

    def fetch(block, slot):
        def issue(r, c):
            row_copy(slot, r, tok_ref[block * rows + r]).start()
            return c

        lax.fori_loop(0, rows, issue, 0, unroll=8)

    @pl.when(i == 0)
    def _():
        fetch(0, 0)

    @pl.when(i + 1 < n_used)
    def _():
        fetch(i + 1, (i + 1) % 2)

    @pl.when(i < n_used)
    def _():
        slot = i % 2

        def drain(r, c):
            row_copy(slot, r, 0).wait()
            return c

        lax.fori_loop(0, rows, drain, 0, unroll=8)
        x = buf[slot]
        o_ref[...] = (x * lax.rsqrt(jnp.mean(x * x, axis=-1, keepdims=True) + EPS) * g_ref[...]).astype(o_ref.dtype)

    @pl.when(i >= n_used)
    def _():
        o_ref[...] = jnp.zeros(o_ref.shape, o_ref.dtype)


def moe_gather_norm(x, g, row_tok, n_used, *, tm, rows=128):
    T, D = x.shape
    n_rows = row_tok.shape[0]
    per = tm // rows
    return pl.pallas_call(
        functools.partial(_moe_gather_kernel, rows=rows),
        grid_spec=pltpu.PrefetchScalarGridSpec(
            num_scalar_prefetch=2,
            grid=(n_rows // rows,),
            in_specs=[pl.BlockSpec(memory_space=pl.ANY), pl.BlockSpec((1, D), lambda i, tok, nu: (0, 0))],
            out_specs=pl.BlockSpec((rows, D), lambda i, tok, nu: (i, 0)),
            scratch_shapes=[pltpu.VMEM((2, rows, D), F32), pltpu.SemaphoreType.DMA((2,))]),
        out_shape=jax.ShapeDtypeStruct((n_rows, D), BF16),
        compiler_params=_params(("arbitrary",), 10 * _nbytes((rows, D), F32)),
        name="moe_gather",
    )(row_tok, n_used * per, x, g.reshape(1, D))


def _stream_group_weights(be_ref, nu_ref, nxt_ref, w_hbms, stage, sem, w_bf16, *, tile, n_tiles):
    j = pl.program_id(0)
    i = pl.program_id(1)

    def copies(e, jj):
        cols = pl.ds(pl.multiple_of(jj * tile, tile), tile)
        return [pltpu.make_async_copy(w.at[e, :, cols], stage.at[n], sem.at[n]) for n, w in enumerate(w_hbms)]

    @pl.when((j == 0) & (i == 0))
    def _():
        for c in copies(be_ref[0], 0):
            c.start()

    first_of_group = (i < nu_ref[0]) & ((i == 0) | (be_ref[i] != be_ref[jnp.maximum(i - 1, 0)]))

    @pl.when(first_of_group)
    def _():
        for c in copies(be_ref[i], j):
            c.wait()
        for n, dst in enumerate(w_bf16):
            dst[...] = stage[n].astype(BF16)
        more_in_sweep = nxt_ref[i] >= 0
        e_next = jnp.where(more_in_sweep, nxt_ref[i], be_ref[0])
        j_next = jnp.where(more_in_sweep, j, j + 1)

        @pl.when(more_in_sweep | (j + 1 < n_tiles))
        def _():
            for c in copies(e_next, j_next):
                c.start()


def _for_valid_rows(nv, o_ref, compute):
    tm = o_ref.shape[0]
    for rows in range(MOE_ROW_STEP, tm + 1, MOE_ROW_STEP):
        @pl.when((nv > rows - MOE_ROW_STEP) & (nv <= rows))
        def _(rows=rows):
            o_ref[:rows, :] = compute(rows).astype(o_ref.dtype)
            if rows < tm:
                o_ref[rows:, :] = jnp.zeros((tm - rows, o_ref.shape[1]), o_ref.dtype)

    @pl.when(nv == 0)
    def _():
        o_ref[...] = jnp.zeros(o_ref.shape, o_ref.dtype)


def _moe_gateup_kernel(be_ref, nu_ref, nxt_ref, nv_ref, x_ref, wg_hbm, wu_hbm, o_ref, stage, sem, wgb, wub,
                       *, tf, n_tiles):
    _stream_group_weights(be_ref, nu_ref, nxt_ref, (wg_hbm, wu_hbm), stage, sem, (wgb, wub), tile=tf, n_tiles=n_tiles)

    def compute(rows):
        x = x_ref[:rows, :]
        g = jnp.dot(x, wgb[...], preferred_element_type=F32)
        u = jnp.dot(x, wub[...], preferred_element_type=F32)
        return g * jax.nn.sigmoid(g) * u

    _for_valid_rows(nv_ref[pl.program_id(1)], o_ref, compute)


def moe_gateup(xs, wg, wu, be, n_used, nxt, nv, *, tm, tf=512):
    n_rows, K = xs.shape
    F = wg.shape[2]
    tf = min(tf, F)
    nb = n_rows // tm
    xmap = lambda j, i, be, nu, *_: (jnp.minimum(i, nu[0] - 1), 0)
    hbm = pl.BlockSpec(memory_space=pl.ANY)
    vmem = (2 * _nbytes((tm, K), BF16) + 2 * _nbytes((K, tf), F32) + 4 * _nbytes((K, tf), BF16)
            + 6 * _nbytes((tm, tf), F32))
    return pl.pallas_call(
        functools.partial(_moe_gateup_kernel, tf=tf, n_tiles=F // tf),
        grid_spec=pltpu.PrefetchScalarGridSpec(
            num_scalar_prefetch=4,
            grid=(F // tf, nb),
            in_specs=[pl.BlockSpec((tm, K), xmap), hbm, hbm],
            out_specs=pl.BlockSpec((tm, tf), lambda j, i, *_: (i, j)),
            scratch_shapes=[pltpu.VMEM((2, K, tf), F32), pltpu.SemaphoreType.DMA((2,)),
                            pltpu.VMEM((K, tf), BF16), pltpu.VMEM((K, tf), BF16)]),
        out_shape=jax.ShapeDtypeStruct((n_rows, F), BF16),
        compiler_params=_params(("arbitrary", "arbitrary"), vmem),
        name="moe_gateup",
    )(be, n_used, nxt, nv, xs, wg, wu)


def _moe_down_kernel(be_ref, nu_ref, nxt_ref, nv_ref, x_ref, w_hbm, o_ref, stage, sem, wb, *, tn, n_tiles):
    _stream_group_weights(be_ref, nu_ref, nxt_ref, (w_hbm,), stage, sem, (wb,), tile=tn, n_tiles=n_tiles)
    _for_valid_rows(nv_ref[pl.program_id(1)], o_ref,
                    lambda rows: jnp.dot(x_ref[:rows, :], wb[...], preferred_element_type=F32))


def moe_down(hm, wd, be, n_used, nxt, nv, *, tm, tn=1024):
    n_rows, K = hm.shape
    N = wd.shape[2]
    tn = min(tn, N)
    nb = n_rows // tm
    xmap = lambda j, i, be, nu, *_: (jnp.minimum(i, nu[0] - 1), 0)
    vmem = (2 * _nbytes((tm, K), BF16) + _nbytes((K, tn), F32) + 2 * _nbytes((K, tn), BF16)
            + 4 * _nbytes((tm, tn), F32))
    return pl.pallas_call(
        functools.partial(_moe_down_kernel, tn=tn, n_tiles=N // tn),
        grid_spec=pltpu.PrefetchScalarGridSpec(
            num_scalar_prefetch=4,
            grid=(N // tn, nb),
            in_specs=[pl.BlockSpec((tm, K), xmap), pl.BlockSpec(memory_space=pl.ANY)],
            out_specs=pl.BlockSpec((tm, tn), lambda j, i, *_: (i, j)),
            scratch_shapes=[pltpu.VMEM((1, K, tn), F32), pltpu.SemaphoreType.DMA((1,)),
                            pltpu.VMEM((K, tn), BF16)]),
        out_shape=jax.ShapeDtypeStruct((n_rows, N), F32),
        compiler_params=_params(("arbitrary", "arbitrary"), vmem),
        name="moe_down",
    )(be, n_used, nxt, nv, hm, wd)


def _moe_combine_kernel(pos_ref, x_ref, gw_ref, ys_hbm, o_ref, buf, sem, *, rows):
    i = pl.program_id(0)
    n_blocks = pl.num_programs(0)

    def row_copy(slot, k, r, src_row):
        return pltpu.make_async_copy(ys_hbm.at[pl.ds(src_row, 1)], buf.at[slot, k, pl.ds(r, 1)], sem.at[slot])

    def fetch(block, slot):
        def issue(r, c):
            a = (block * rows + r) * TOP_K
            for k in range(TOP_K):
                row_copy(slot, k, r, pos_ref[a + k]).start()
            return c

        lax.fori_loop(0, rows, issue, 0, unroll=4)

    @pl.when(i == 0)
    def _():
        fetch(0, 0)

    @pl.when(i + 1 < n_blocks)
    def _():
        fetch(i + 1, (i + 1) % 2)

    slot = i % 2

    def drain(r, c):
        for k in range(TOP_K):
            row_copy(slot, k, r, 0).wait()
        return c

    lax.fori_loop(0, rows, drain, 0, unroll=4)
    out = x_ref[...]
    for k in range(TOP_K):
        out = out + gw_ref[:, k:k + 1] * buf[slot, k]
    o_ref[...] = out


def moe_combine(x, ys, dest, gw, *, rows=128):
    T, D = x.shape
    return pl.pallas_call(
        functools.partial(_moe_combine_kernel, rows=rows),
        grid_spec=pltpu.PrefetchScalarGridSpec(
            num_scalar_prefetch=1,
            grid=(T // rows,),
            in_specs=[pl.BlockSpec((rows, D), lambda i, pos: (i, 0)),
                      pl.BlockSpec((rows, TOP_K), lambda i, pos: (i, 0)),
                      pl.BlockSpec(memory_space=pl.ANY)],
            out_specs=pl.BlockSpec((rows, D), lambda i, pos: (i, 0)),
            scratch_shapes=[pltpu.VMEM((2, TOP_K, rows, D), F32), pltpu.SemaphoreType.DMA((2,))]),
        out_shape=jax.ShapeDtypeStruct((T, D), F32),
        compiler_params=_params(("arbitrary",), 12 * _nbytes((rows, D), F32)),
        name="moe_combine",
    )(dest, x, gw, ys)


def moe_block(x, g, w_router, b_router, w_egate, w_eup, w_edown, *, tm=512):
    idx, gw = moe_router(x, g, w_router, b_router)
    dest, row_tok, be, n_used, nxt, nv = _moe_dispatch(idx, tm)
    xs = moe_gather_norm(x, g, row_tok, n_used, tm=tm)
    hm = moe_gateup(xs, w_egate, w_eup, be, n_used, nxt, nv, tm=tm)
    ys = moe_down(hm, w_edown, be, n_used, nxt, nv, tm=tm)
    return moe_combine(x, ys, dest, gw)


def even_mixer_block(x, tabs_a, tabs_b, w_in, ga_q, ga_k, g_cq, w_uq, g_ckv, w_ukv, gb_q, gb_kn, gb_kr,
                     w_o, g_mix, *, batch, seq):
    a_width = A_HEADS * HEAD_DIM
    main_cols = 3 * a_width + Q_LORA + KV_LORA
    h = rmsnorm(x, g_mix)
    proj = matmul(h, w_in, n_out=main_cols, tm=2048, single_buffer_x=True, name="in_proj")
    w_kr = jnp.pad(w_in[:, main_cols:], ((0, 0), (0, LANES - ROPE_DIM)))
    krp = matmul(h, w_kr, name="in_proj_kr")
    o_a = mixer_a(proj, tabs_a, ga_q, ga_k, batch=batch, seq=seq)
    cq = rmsnorm(proj, g_cq, width=Q_LORA, col_block=3 * a_width // Q_LORA)
    ckv = rmsnorm(proj, g_ckv, width=KV_LORA, col_block=(3 * a_width + Q_LORA) // KV_LORA)
    w3 = w_uq.reshape(Q_LORA, B_HEADS, QK_DIM)
    w_uq_p = jnp.concatenate(
        [w3[:, :, :NOPE_DIM].reshape(Q_LORA, B_HEADS * NOPE_DIM),
         jnp.pad(w3[:, :, NOPE_DIM:], ((0, 0), (0, 0), (0, LANES - ROPE_DIM))).reshape(Q_LORA, B_HEADS * LANES)],
        axis=1)
    q_up = matmul(cq, w_uq_p, name="q_up")
    kv_up = matmul(ckv, w_ukv, name="kv_up")
    pad_r = lambda v: jnp.pad(v, (0, LANES - ROPE_DIM)).reshape(1, LANES)
    qt, kf, vb = mla_prep(q_up, kv_up, krp, tabs_b, gb_q[:NOPE_DIM].reshape(1, LANES), pad_r(gb_q[NOPE_DIM:]),
                          gb_kn.reshape(1, LANES), pad_r(gb_kr), batch=batch, seq=seq)
    o_b = mla_attention(qt, kf, vb, batch=batch, seq=seq)
    return matmul((o_a, o_b), w_o, res=x, name="mixer_out")


def kernel(x, mem, positions, g_mem, w_mem_kv, g_mem_k, g_mix, g_x, w_xq, g_xq, w_xo, g_ffn,
           w_in, ga_q, ga_k, g_cq, w_uq, g_ckv, w_ukv, gb_q, gb_kn, gb_kr, w_o_even,
           w_gate, w_up, w_down, w_qkv, w_o_odd, w_router, b_router, w_egate, w_eup, w_edown):
    B, S, D = x.shape
    M = mem.shape[1]
    depth = g_mix.shape[0]
    T = B * S
    xt = x.reshape(T, D)
    tabs_a = _rope_tables(positions, ROT_DIM)
    tabs_b = _rope_tables(positions, ROPE_DIM)
    kv = matmul(rmsnorm(mem.reshape(B * M, D), g_mem), w_mem_kv, tm=B * M, name="mem_kv")
    mk, mv = memkv_post(kv, g_mem_k)
    for layer in range(depth):
        i = layer // 2
        if layer % 2 == 0:
            xt = even_mixer_block(xt, tabs_a, tabs_b, w_in[i], ga_q[i], ga_k[i], g_cq[i], w_uq[i], g_ckv[i],
                                  w_ukv[i], gb_q[i], gb_kn[i], gb_kr[i], w_o_even[i], g_mix[layer],
                                  batch=B, seq=S)
        else:
            h = rmsnorm(xt, g_mix[layer])
            qkv = matmul(h, w_qkv[i], out_dtype=BF16, tm=2048, single_buffer_x=True, name="qkv_proj")
            o = stickbreak_attention(qkv, batch=B, seq=S, heads=D // HEAD_DIM)
            xt = matmul(o, w_o_odd[i], res=xt, name="mixer_out")
        xattn = functools.partial(cross_attention_block, xt, mk, mv, g_x[layer], w_xq[layer], g_xq[layer],
                                  w_xo[layer], seq=S, mem_len=M)
        if layer % 2 == 0:
            xt, h = xattn(g_ffn[layer])
            hm = swiglu_gateup(h, w_gate[i], w_up[i])
            xt = matmul(hm, w_down[i], res=xt, tn=1024, tk=2048, name="swiglu_down")
        else:
            xt = moe_block(xattn(), g_ffn[layer], w_router[i], b_router[i], w_egate[i], w_eup[i], w_edown[i])
    return xt.reshape(B, S, D)
```

```python
import functools

import jax
import jax.numpy as jnp
from jax import lax
from jax.experimental import pallas as pl
from jax.experimental.pallas import tpu as pltpu

F32 = jnp.float32
BF16 = jnp.bfloat16

HEAD_DIM = 128
ROT_DIM = HEAD_DIM // 4
ROPE_THETA = 500000.0
BLOCK = 128
NEG_INF = -1e30
EPS = 1e-6
DILATED_PAIRS = ((128, 1), (512, 4), (2048, 16))
A_HEADS = 16
B_HEADS = 16
Q_LORA = 1536
KV_LORA = 512
NOPE_DIM = 128
ROPE_DIM = 64
QK_DIM = NOPE_DIM + ROPE_DIM
X_HEADS = 4
N_EXPERTS = 8
TOP_K = 2
MIXER_A_GROUP = 8
MOE_ROW_STEP = 128
LOG2_E = 1.4426950408889634
MLA_Q_SCALE = QK_DIM ** -0.5 * LOG2_E

LANES = 128
V7X_VMEM_BYTES = 64 * 1024 * 1024
VMEM_CAP = V7X_VMEM_BYTES - 6 * 1024 * 1024


def _params(semantics, vmem_bytes):
    return pltpu.CompilerParams(dimension_semantics=semantics,
                                vmem_limit_bytes=int(min(max(vmem_bytes, 32 * 1024 * 1024), VMEM_CAP)))


def _nbytes(shape, dtype):
    n = 1
    for s in shape:
        n *= s
    return n * jnp.dtype(dtype).itemsize


def _rmsnorm_kernel(x_ref, g_ref, o_ref):
    x = x_ref[...].astype(F32)
    ms = jnp.mean(x * x, axis=-1, keepdims=True)
    o_ref[...] = (x * lax.rsqrt(ms + EPS) * g_ref[...]).astype(o_ref.dtype)


def rmsnorm(x, g, *, width=None, col_block=0, tm=256, out_dtype=BF16):
    T = x.shape[0]
    width = x.shape[1] if width is None else width
    tm = min(tm, T)
    return pl.pallas_call(
        _rmsnorm_kernel,
        grid=(T // tm,),
        in_specs=[pl.BlockSpec((tm, width), lambda i: (i, col_block)),
                  pl.BlockSpec((1, width), lambda i: (0, 0))],
        out_specs=pl.BlockSpec((tm, width), lambda i: (i, 0)),
        out_shape=jax.ShapeDtypeStruct((T, width), out_dtype),
        compiler_params=_params(("parallel",), 6 * _nbytes((tm, width), F32)),
        name="rmsnorm",
    )(x, g.reshape(1, width).astype(F32))


def _mm_kernel(*refs, nk, n_x, has_res, w_transposed):
    x_refs = refs[:n_x]
    w_ref = refs[n_x]
    rest = refs[n_x + 1:]
    res_ref = rest[0] if has_res else None
    o_ref = rest[1] if has_res else rest[0]
    acc_ref = rest[-1] if nk > 1 else None
    k = pl.program_id(2)
    w = w_ref[...].astype(BF16)

    def finish(part):
        out = part
        if has_res:
            out = out + res_ref[...]
        o_ref[...] = out.astype(o_ref.dtype)

    if n_x == 2:
        half = x_refs[0].shape[1]
        finish(jnp.dot(x_refs[0][...], w[:half], preferred_element_type=F32)
               + jnp.dot(x_refs[1][...], w[half:], preferred_element_type=F32))
    elif nk == 1 and w_transposed:
        finish(lax.dot_general(x_refs[0][...], w, (((1,), (1,)), ((), ())), preferred_element_type=F32))
    elif nk == 1:
        finish(jnp.dot(x_refs[0][...], w, preferred_element_type=F32))
    else:
        part = jnp.dot(x_refs[0][...], w, preferred_element_type=F32)

        @pl.when(k == 0)
        def _():
            acc_ref[...] = part

        @pl.when((k > 0) & (k < nk - 1))
        def _():
            acc_ref[...] += part

        @pl.when(k == nk - 1)
        def _():
            finish(acc_ref[...] + part)


def matmul(xs, w, *, n_out=None, res=None, out_dtype=F32, tm=1024, tn=512, tk=None, single_buffer_x=False,
           w_transposed=False, name="matmul"):
    if not isinstance(xs, (tuple, list)):
        xs = (xs,)
    n_x = len(xs)
    T = xs[0].shape[0]
    K = sum(x.shape[1] for x in xs)
    n_out = (w.shape[0] if w_transposed else w.shape[1]) if n_out is None else n_out
    tm = min(tm, T)
    tn = min(tn, n_out)
    if n_x == 2:
        assert tk is None and xs[1].shape[1] == xs[0].shape[1]
    tk = K if tk is None else tk
    nk = K // tk
    assert K == nk * tk and T % tm == 0 and n_out % tn == 0
    x_buffers = 2
    if n_x == 2:
        x_specs = [pl.BlockSpec((tm, K // 2), lambda i, j, k: (i, 0)) for _ in xs]
    elif nk == 1 and single_buffer_x:
        x_buffers = 1
        x_specs = [pl.BlockSpec((tm, tk), lambda i, j, k: (i, k), pipeline_mode=pl.Buffered(1))]
    else:
        x_specs = [pl.BlockSpec((tm, tk), lambda i, j, k: (i, k))]
    if w_transposed:
        assert n_x == 1 and nk == 1
        in_specs = x_specs + [pl.BlockSpec((tn, tk), lambda i, j, k: (j, k))]
    else:
        in_specs = x_specs + [pl.BlockSpec((tk, tn), lambda i, j, k: (k, j))]
    args = list(xs) + [w]
    if res is not None:
        in_specs.append(pl.BlockSpec((tm, tn), lambda i, j, k: (i, j)))
        args.append(res)
    scratch = [pltpu.VMEM((tm, tn), F32)] if nk > 1 else []
    vmem = (x_buffers * _nbytes((tm, tk), xs[0].dtype) + 2 * _nbytes((tk, tn), F32) + _nbytes((tk, tn), BF16)
            + (2 * _nbytes((tm, tn), F32) if res is not None else 0)
            + 2 * _nbytes((tm, tn), out_dtype) + 2 * _nbytes((tm, tn), F32))
    return pl.pallas_call(
        functools.partial(_mm_kernel, nk=nk, n_x=n_x, has_res=res is not None, w_transposed=w_transposed),
        grid=(T // tm, n_out // tn, nk),
        in_specs=in_specs,
        out_specs=pl.BlockSpec((tm, tn), lambda i, j, k: (i, j)),
        out_shape=jax.ShapeDtypeStruct((T, n_out), out_dtype),
        scratch_shapes=scratch,
        compiler_params=_params(("parallel", "parallel", "arbitrary"), vmem),
        name=name,
    )(*args)


def _rope_tables(positions, dim):
    half = dim // 2
    inv_freq = ROPE_THETA ** (-jnp.arange(0, dim, 2, dtype=F32) / dim)
    ang = positions.astype(F32)[..., None] * inv_freq
    cos, sin = jnp.cos(ang), jnp.sin(ang)
    B, S = positions.shape
    ones = jnp.ones((B, S, LANES - dim), F32)
    zeros = jnp.zeros((B, S, LANES - dim), F32)
    zh = jnp.zeros((B, S, half), F32)
    c = jnp.concatenate([cos, cos, ones], axis=-1)
    sa = jnp.concatenate([-sin, zh, zeros], axis=-1)
    sb = jnp.concatenate([zh, sin, zeros], axis=-1)
    return [t.reshape(B * S, LANES) for t in (c, sa, sb)]


def _rope_lanes(y, c, sa, sb, half):
    return y * c + pltpu.roll(y, LANES - half, 1) * sa + pltpu.roll(y, half, 1) * sb


def _mixer_a_kernel(q_ref, k_ref, v_ref, c_ref, sa_ref, sb_ref, gq_ref, gk_ref, o_ref,
                    qs, ks, *state, seq):
    scale = HEAD_DIM ** -0.5 * LOG2_E
    c, sa, sb = c_ref[...], sa_ref[...], sb_ref[...]

    def prep(x, g):
        y = x * lax.rsqrt(jnp.mean(x * x, axis=-1, keepdims=True) + EPS) * g
        return _rope_lanes(y, c, sa, sb, ROT_DIM // 2)

    qs[...] = prep(q_ref[...], gq_ref[...])
    ks[...] = prep(k_ref[...], gk_ref[...])
    qi = lax.broadcasted_iota(jnp.int32, (BLOCK, 1), 0)
    kj = lax.broadcasted_iota(jnp.int32, (1, BLOCK), 1)
    trans_b = (((1,), (1,)), ((), ()))
    ones = jnp.ones((BLOCK, HEAD_DIM), BF16)

    for bi, (window, dil) in enumerate(DILATED_PAIRS):
        acc, m_sc, l_sc = state[3 * bi:3 * bi + 3]
        n_back = window // dil
        nb = seq // dil // BLOCK
        assert n_back <= BLOCK and nb * dil * BLOCK == seq
        cur_ok = (qi >= kj) & (qi - kj <= n_back)
        prev_ok = qi + BLOCK - kj <= n_back

        def rows_at(r, n, dil=dil):
            if dil == 1:
                return pl.ds(BLOCK * n, BLOCK)
            return pl.ds(r + dil * BLOCK * n, BLOCK, stride=dil)

        blocks = [(rows_at(r, n), rows_at(r, n - 1) if n > 0 else None) for r in range(dil) for n in range(nb)]
        for first in range(0, len(blocks), MIXER_A_GROUP):
            group = blocks[first:first + MIXER_A_GROUP]
            s_cur, s_prev = [], []
            for rows, prev in group:
                qb = qs[rows, :].astype(BF16)
                s_c = lax.dot_general(qb, ks[rows, :].astype(BF16), trans_b, preferred_element_type=F32) * scale
                s_cur.append(jnp.where(cur_ok, s_c, NEG_INF))
                if prev is None:
                    s_prev.append(None)
                else:
                    s_p = lax.dot_general(qb, ks[prev, :].astype(BF16), trans_b,
                                          preferred_element_type=F32) * scale
                    s_prev.append(jnp.where(prev_ok, s_p, NEG_INF))
            stats = []
            for s_c, s_p in zip(s_cur, s_prev):
                m = jnp.max(s_c, axis=-1, keepdims=True)
                if s_p is not None:
                    m = jnp.maximum(m, jnp.max(s_p, axis=-1, keepdims=True))
                p_c = jnp.exp2(s_c - m).astype(BF16)
                p_p = None if s_p is None else jnp.exp2(s_p - m).astype(BF16)
                stats.append((m, p_c, p_p))
            for (rows, prev), (m, p_c, p_p) in zip(group, stats):
                v_ext = jnp.concatenate([v_ref[rows, :].astype(BF16), ones], axis=1)
                o = jnp.dot(p_c, v_ext, preferred_element_type=F32)
                if p_p is not None:
                    v_ext = jnp.concatenate([v_ref[prev, :].astype(BF16), ones], axis=1)
                    o = o + jnp.dot(p_p, v_ext, preferred_element_type=F32)
                acc[rows, :] = o[:, :HEAD_DIM]
                l_sc[rows, :] = o[:, HEAD_DIM:]
                m_sc[rows, :] = jnp.broadcast_to(m, (BLOCK, HEAD_DIM))

    n_br = len(DILATED_PAIRS)
    m_all = state[1][...]
    for bi in range(1, n_br):
        m_all = jnp.maximum(m_all, state[3 * bi + 1][...])
    num = jnp.zeros(o_ref.shape, F32)
    den = jnp.zeros(o_ref.shape, F32)
    for bi in range(n_br):
        w = jnp.exp2(state[3 * bi + 1][...] - m_all)
        num = num + w * state[3 * bi][...]
        den = den + w * state[3 * bi + 2][...]
    o_ref[...] = (num / den).astype(o_ref.dtype)


def mixer_a(proj, tabs, gq, gk, *, batch, seq):
    T = batch * seq
    blk = (seq, HEAD_DIM)
    head = lambda off: pl.BlockSpec(blk, lambda b, h: (b, off + h))
    tab = pl.BlockSpec(blk, lambda b, h: (b, 0))
    gspec = pl.BlockSpec((1, HEAD_DIM), lambda b, h: (0, 0))
    return pl.pallas_call(
        functools.partial(_mixer_a_kernel, seq=seq),
        grid=(batch, A_HEADS),
        in_specs=[head(0), head(A_HEADS), head(2 * A_HEADS), tab, tab, tab, gspec, gspec],
        out_specs=pl.BlockSpec(blk, lambda b, h: (b, h)),
        out_shape=jax.ShapeDtypeStruct((T, A_HEADS * HEAD_DIM), BF16),
        scratch_shapes=[pltpu.VMEM(blk, F32) for _ in range(2 + 3 * len(DILATED_PAIRS))],
        compiler_params=_params(("parallel", "parallel"), 30 * _nbytes(blk, F32)),
        name="mixer_a",
    )(proj, proj, proj, *tabs, gq.reshape(1, HEAD_DIM), gk.reshape(1, HEAD_DIM))


def _mla_prep_kernel(qn_ref, qr_ref, kn_ref, v_ref, kr_ref, c_ref, sa_ref, sb_ref,
                     gqn_ref, gqr_ref, gkn_ref, gkr_ref, qt_ref, kf_ref, vb_ref, kr_sc):
    c, sa, sb = c_ref[...], sa_ref[...], sb_ref[...]
    half = ROPE_DIM // 2

    @pl.when(pl.program_id(1) == 0)
    def _():
        kr = kr_ref[...]
        kr = kr * lax.rsqrt(jnp.sum(kr * kr, axis=-1, keepdims=True) * (1.0 / ROPE_DIM) + EPS) * gkr_ref[...]
        kr_sc[...] = _rope_lanes(kr, c, sa, sb, half).astype(kr_sc.dtype)

    qn, qr = qn_ref[...], qr_ref[...]
    ss = jnp.sum(qn * qn, axis=-1, keepdims=True) + jnp.sum(qr * qr, axis=-1, keepdims=True)
    inv = lax.rsqrt(ss * (1.0 / QK_DIM) + EPS) * MLA_Q_SCALE
    qf = jnp.concatenate([qn * inv * gqn_ref[...], _rope_lanes(qr * inv * gqr_ref[...], c, sa, sb, half)], axis=1)
    qt_ref[...] = qf.astype(qt_ref.dtype).T
    kn = kn_ref[...]
    kn = kn * lax.rsqrt(jnp.mean(kn * kn, axis=-1, keepdims=True) + EPS) * gkn_ref[...]
    kf_ref[:, :NOPE_DIM] = kn.astype(kf_ref.dtype)
    kf_ref[:, NOPE_DIM:] = kr_sc[...]
    vb_ref[...] = v_ref[...].astype(vb_ref.dtype)


def mla_prep(q_up, kv_up, krp, tabs, gqn, gqr, gkn, gkr, *, batch, seq, tm=512):
    T = q_up.shape[0]
    H = B_HEADS
    per_seq = seq // tm
    blk = lambda f: pl.BlockSpec((tm, LANES), f)
    g = pl.BlockSpec((1, LANES), lambda i, h: (0, 0))
    return pl.pallas_call(
        _mla_prep_kernel,
        grid=(T // tm, H),
        in_specs=[blk(lambda i, h: (i, h)), blk(lambda i, h: (i, H + h)),
                  blk(lambda i, h: (i, 2 * h)), blk(lambda i, h: (i, 2 * h + 1)),
                  blk(lambda i, h: (i, 0)), blk(lambda i, h: (i, 0)), blk(lambda i, h: (i, 0)),
                  blk(lambda i, h: (i, 0)), g, g, g, g],
        out_specs=[pl.BlockSpec((None, None, 2 * LANES, tm), lambda i, h: (i // per_seq, h, 0, i % per_seq)),
                   pl.BlockSpec((tm, 2 * LANES), lambda i, h: (i, h)),
                   pl.BlockSpec((tm, LANES), lambda i, h: (i, h))],
        out_shape=[jax.ShapeDtypeStruct((batch, H, 2 * LANES, seq), BF16),
                   jax.ShapeDtypeStruct((T, H * 2 * LANES), BF16),
                   jax.ShapeDtypeStruct((T, H * LANES), BF16)],
        scratch_shapes=[pltpu.VMEM((tm, LANES), BF16)],
        compiler_params=_params(("parallel", "arbitrary"), 40 * _nbytes((tm, LANES), F32)),
        name="mla_prep",
    )(q_up, q_up, kv_up, kv_up, krp, *tabs, gqn, gqr, gkn, gkr)


def _causal_attn_kernel(qt_ref, k_ref, v_ref, o_ref, s_sc, acc, *, tq, hg, dq, dv):
    qi = pl.program_id(2)
    key = lax.broadcasted_iota(jnp.int32, (tq, tq), 0)
    qry = lax.broadcasted_iota(jnp.int32, (tq, tq), 1)
    visible = key <= qry
    ones = jnp.ones((tq, dv), BF16)
    contract_rows = (((0,), (0,)), ((), ()))

    heads = range(hg)

    def scores(j, diagonal):
        rows = pl.ds(pl.multiple_of(j * tq, tq), tq)
        s = [jnp.dot(k_ref[rows, g * dq:(g + 1) * dq], qt_ref[g], preferred_element_type=F32) for g in heads]
        if diagonal:
            s = [jnp.where(visible, sg, NEG_INF) for sg in s]
        for g in heads:
            s_sc[g, j] = s[g]
        return s

    def col_max(j, ms):
        s = scores(j, False)
        return tuple(jnp.maximum(ms[g], jnp.max(s[g], axis=0, keepdims=True)) for g in heads)

    ms = lax.fori_loop(0, qi, col_max, tuple(jnp.max(sg, axis=0, keepdims=True) for sg in scores(qi, True)))

    def weighted(j):
        rows = pl.ds(pl.multiple_of(j * tq, tq), tq)
        p = [jnp.exp2(s_sc[g, j] - ms[g]).astype(BF16) for g in heads]
        return [lax.dot_general(p[g], jnp.concatenate([v_ref[rows, g * dv:(g + 1) * dv], ones], axis=1),
                                contract_rows, preferred_element_type=F32) for g in heads]

    for g, w in enumerate(weighted(qi)):
        acc[g] = w

    def below(j, carry):
        for g, w in enumerate(weighted(j)):
            acc[g] += w
        return carry

    lax.fori_loop(0, qi, below, 0)
    for g in heads:
        a = acc[g]
        o_ref[:, g * dv:(g + 1) * dv] = (a[:, :dv] / a[:, dv:]).astype(o_ref.dtype)


def mla_attention(qt, kf, vb, *, batch, seq, tq=512, hg=4):
    T = batch * seq
    nq = seq // tq
    dq = qt.shape[2]
    dv = vb.shape[1] // B_HEADS
    assert B_HEADS % hg == 0
    return pl.pallas_call(
        functools.partial(_causal_attn_kernel, tq=tq, hg=hg, dq=dq, dv=dv),
        grid=(batch, B_HEADS // hg, nq),
        in_specs=[pl.BlockSpec((None, hg, dq, tq), lambda b, h, i: (b, h, 0, i)),
                  pl.BlockSpec((seq, hg * dq), lambda b, h, i: (b, h)),
                  pl.BlockSpec((seq, hg * dv), lambda b, h, i: (b, h))],
        out_specs=pl.BlockSpec((tq, hg * dv), lambda b, h, i: (b * nq + i, h)),
        out_shape=jax.ShapeDtypeStruct((T, B_HEADS * dv), BF16),
        scratch_shapes=[pltpu.VMEM((hg, nq, tq, tq), F32), pltpu.VMEM((hg, tq, 2 * dv), F32)],
        compiler_params=_params(("parallel", "parallel", "parallel"),
                                _nbytes((hg, nq, tq, tq), F32) + 2 * _nbytes((seq, hg * (dq + dv)), BF16)
                                + 8 * hg * _nbytes((tq, tq), F32)),
        name="mla_attention",
    )(qt, kf, vb)


STICK_EXHAUSTED_LOG = -110.0


def _stickbreak_kernel(q_ref, k_ref, v_ref, o_ref, qt_sc, c_sc, acc, *, tq, hg, scale):
    qi = pl.program_id(2)
    c_sc[...] = jnp.zeros(c_sc.shape, F32)
    acc[...] = jnp.zeros(acc.shape, F32)
    for g in range(hg):
        qt_sc[g] = q_ref[:, g * HEAD_DIM:(g + 1) * HEAD_DIM].T
    key = lax.broadcasted_iota(jnp.int32, (tq, tq), 0)
    qry = lax.broadcasted_iota(jnp.int32, (tq, tq), 1)
    after = (qry > key).astype(BF16)
    earlier = key < qry
    contract_rows = (((0,), (0,)), ((), ()))

    def chunk(j, diagonal):
        rows = pl.ds(pl.multiple_of(j * tq, tq), tq)
        heads = range(hg)
        cols = [slice(g * HEAD_DIM, (g + 1) * HEAD_DIM) for g in heads]
        z = [jnp.dot(k_ref[rows, cols[g]], qt_sc[g], preferred_element_type=F32) * (scale * LOG2_E) for g in heads]
        log_beta, parts = [], []
        for g in heads:
            softplus = jnp.maximum(z[g], 0.0) + jnp.log2(1.0 + jnp.exp2(-jnp.abs(z[g])))
            log_keep = -softplus
            if diagonal:
                log_keep = jnp.where(earlier, log_keep, 0.0)
            parts.append(log_keep.astype(BF16))
            log_beta.append(z[g] - softplus + c_sc[g])
            c_sc[g] += jnp.sum(log_keep, axis=0, keepdims=True)
        within = jnp.dot(after, jnp.concatenate(parts, axis=1), preferred_element_type=F32)
        for g in heads:
            a = jnp.exp2(log_beta[g] + within[:, g * tq:(g + 1) * tq])
            if diagonal:
                a = jnp.where(earlier, a, 0.0)
            acc[:, cols[g]] += lax.dot_general(a.astype(BF16), v_ref[rows, cols[g]], contract_rows,
                                               preferred_element_type=F32)

    def stick_left():
        return jnp.max(c_sc[...]) > STICK_EXHAUSTED_LOG * LOG2_E

    chunk(qi, True)

    def more(state):
        j, left = state
        return (j >= 0) & left

    def step(state):
        j, _ = state
        chunk(j, False)
        return j - 1, stick_left()

    lax.while_loop(more, step, (qi - 1, stick_left()))
    o_ref[...] = acc[...].astype(o_ref.dtype)


def stickbreak_attention(qkv, *, batch, seq, heads, tq=256, hg=4):
    T = batch * seq
    nq = seq // tq
    ng = heads // hg
    assert heads == ng * hg
    blk = hg * HEAD_DIM
    return pl.pallas_call(
        functools.partial(_stickbreak_kernel, tq=tq, hg=hg, scale=HEAD_DIM ** -0.5),
        grid=(batch, ng, nq),
        in_specs=[pl.BlockSpec((tq, blk), lambda b, h, i: (b * nq + i, h)),
                  pl.BlockSpec((seq, blk), lambda b, h, i: (b, ng + h)),
                  pl.BlockSpec((seq, blk), lambda b, h, i: (b, 2 * ng + h))],
        out_specs=pl.BlockSpec((tq, blk), lambda b, h, i: (b * nq + i, h)),
        out_shape=jax.ShapeDtypeStruct((T, heads * HEAD_DIM), BF16),
        scratch_shapes=[pltpu.VMEM((hg, HEAD_DIM, tq), BF16), pltpu.VMEM((hg, 1, tq), F32),
                        pltpu.VMEM((tq, blk), F32)],
        compiler_params=_params(("parallel", "parallel", "parallel"), 0),
        name="stickbreak",
    )(qkv, qkv, qkv)


def _memkv_kernel(kv_ref, g_ref, mk_ref, mv_ref):
    for h in range(X_HEADS):
        cols = slice(h * HEAD_DIM, (h + 1) * HEAD_DIM)
        k = kv_ref[:, cols]
        k = k * lax.rsqrt(jnp.mean(k * k, axis=-1, keepdims=True) + EPS) * g_ref[...]
        mk_ref[:, cols] = k.astype(mk_ref.dtype)
    mv_ref[...] = kv_ref[:, X_HEADS * HEAD_DIM:].astype(mv_ref.dtype)


def memkv_post(kv, g):
    M = kv.shape[0]
    xd = X_HEADS * HEAD_DIM
    return pl.pallas_call(
        _memkv_kernel,
        grid=(1,),
        in_specs=[pl.BlockSpec((M, 2 * xd), lambda i: (0, 0)), pl.BlockSpec((1, HEAD_DIM), lambda i: (0, 0))],
        out_specs=[pl.BlockSpec((M, xd), lambda i: (0, 0)), pl.BlockSpec((M, xd), lambda i: (0, 0))],
        out_shape=[jax.ShapeDtypeStruct((M, xd), BF16), jax.ShapeDtypeStruct((M, xd), BF16)],
        compiler_params=_params(("arbitrary",), 0),
        name="memkv_post",
    )(kv, g.reshape(1, HEAD_DIM))


def _xattn_block_kernel(x_ref, gx_ref, wq_ref, gq_ref, mk_ref, mv_ref, wo_ref, *rest, emit_norm):
    if emit_norm:
        gn_ref, o_ref, hn_ref = rest
    else:
        (o_ref,) = rest
    scale = HEAD_DIM ** -0.5
    x = x_ref[...]
    h = (x * lax.rsqrt(jnp.mean(x * x, axis=-1, keepdims=True) + EPS) * gx_ref[...]).astype(BF16)
    q_all = jnp.dot(h, wq_ref[...], preferred_element_type=F32)
    heads = []
    for hd in range(X_HEADS):
        cols = slice(hd * HEAD_DIM, (hd + 1) * HEAD_DIM)
        q = q_all[:, cols]
        q = (q * lax.rsqrt(jnp.mean(q * q, axis=-1, keepdims=True) + EPS) * gq_ref[...]).astype(BF16)
        s = lax.dot_general(q, mk_ref[:, cols], (((1,), (1,)), ((), ())), preferred_element_type=F32) * scale
        p = jnp.exp(s - jnp.max(s, axis=-1, keepdims=True))
        p = p / jnp.sum(p, axis=-1, keepdims=True)
        heads.append(jnp.dot(p.astype(BF16), mv_ref[:, cols], preferred_element_type=F32).astype(BF16))
    y = x + jnp.dot(jnp.concatenate(heads, axis=1), wo_ref[...], preferred_element_type=F32)
    o_ref[...] = y
    if emit_norm:
        hn_ref[...] = (y * lax.rsqrt(jnp.mean(y * y, axis=-1, keepdims=True) + EPS) * gn_ref[...]).astype(hn_ref.dtype)


def cross_attention_block(x, mk, mv, g_x, w_xq, g_xq, w_xo, g_next=None, *, seq, mem_len, tm=256):
    T, D = x.shape
    xd = X_HEADS * HEAD_DIM
    per_seq = seq // tm
    emit_norm = g_next is not None
    row = lambda w: pl.BlockSpec((1, w), lambda i: (0, 0))
    tok = pl.BlockSpec((tm, D), lambda i: (i, 0))
    in_specs = [tok, row(D), pl.BlockSpec((D, xd), lambda i: (0, 0)), row(HEAD_DIM),
                pl.BlockSpec((mem_len, xd), lambda i: (i // per_seq, 0)),
                pl.BlockSpec((mem_len, xd), lambda i: (i // per_seq, 0)),
                pl.BlockSpec((xd, D), lambda i: (0, 0))]
    args = [x, g_x.reshape(1, D), w_xq.astype(BF16), g_xq.reshape(1, HEAD_DIM), mk, mv, w_xo.astype(BF16)]
    out_specs, out_shape = tok, jax.ShapeDtypeStruct((T, D), F32)
    if emit_norm:
        in_specs.append(row(D))
        args.append(g_next.reshape(1, D))
        out_specs = [tok, tok]
        out_shape = [out_shape, jax.ShapeDtypeStruct((T, D), BF16)]
    vmem = 10 * _nbytes((tm, D), F32) + 4 * _nbytes((D, xd), BF16)
    return pl.pallas_call(
        functools.partial(_xattn_block_kernel, emit_norm=emit_norm),
        grid=(T // tm,),
        in_specs=in_specs,
        out_specs=out_specs,
        out_shape=out_shape,
        compiler_params=_params(("parallel",), vmem),
        name="cross_attention",
    )(*args)


def _gateup_kernel(x_ref, wg_ref, wu_ref, o_ref):
    x = x_ref[...]
    g = jnp.dot(x, wg_ref[...].astype(BF16), preferred_element_type=F32)
    u = jnp.dot(x, wu_ref[...].astype(BF16), preferred_element_type=F32)
    o_ref[...] = (g * jax.nn.sigmoid(g) * u).astype(o_ref.dtype)


def swiglu_gateup(x, wg, wu, *, tm=2048, tf=256):
    T, K = x.shape
    F = wg.shape[1]
    vmem = (_nbytes((tm, K), BF16) + 4 * _nbytes((K, tf), F32) + 2 * _nbytes((K, tf), BF16)
            + 6 * _nbytes((tm, tf), F32))
    return pl.pallas_call(
        _gateup_kernel,
        grid=(T // tm, F // tf),
        in_specs=[pl.BlockSpec((tm, K), lambda i, j: (i, 0), pipeline_mode=pl.Buffered(1)),
                  pl.BlockSpec((K, tf), lambda i, j: (0, j)),
                  pl.BlockSpec((K, tf), lambda i, j: (0, j))],
        out_specs=pl.BlockSpec((tm, tf), lambda i, j: (i, j)),
        out_shape=jax.ShapeDtypeStruct((T, F), BF16),
        compiler_params=_params(("parallel", "parallel"), vmem),
        name="swiglu_gateup",
    )(x, wg, wu)


def _router_kernel(x_ref, g_ref, w_ref, b_ref, idx_ref, gw_ref):
    x = x_ref[...]
    h = x * lax.rsqrt(jnp.mean(x * x, axis=-1, keepdims=True) + EPS) * g_ref[...]
    logits = jnp.dot(h, w_ref[...], preferred_element_type=F32, precision=lax.Precision.HIGHEST) + b_ref[...]
    lane = lax.broadcasted_iota(jnp.int32, logits.shape, 1)
    logits = jnp.where(lane < N_EXPERTS, logits, -jnp.inf)
    m1 = jnp.max(logits, axis=-1, keepdims=True)
    i1 = jnp.min(jnp.where(logits == m1, lane, LANES), axis=-1, keepdims=True)
    rest = jnp.where(lane == i1, -jnp.inf, logits)
    m2 = jnp.max(rest, axis=-1, keepdims=True)
    i2 = jnp.min(jnp.where(rest == m2, lane, LANES), axis=-1, keepdims=True)
    e = jnp.exp(m2 - m1)
    w1 = 1.0 / (1.0 + e)
    w2 = e / (1.0 + e)
    idx_ref[...] = jnp.where(lane == 0, i1, jnp.where(lane == 1, i2, 0))
    gw_ref[...] = jnp.where(lane == 0, w1, jnp.where(lane == 1, w2, 0.0))


def moe_router(x, g, w_router, b_router, *, tm=256):
    T, D = x.shape
    E = w_router.shape[1]
    w = jnp.pad(w_router, ((0, 0), (0, LANES - E)))
    b = jnp.pad(b_router, (0, LANES - E)).reshape(1, LANES)
    idx, gw = pl.pallas_call(
        _router_kernel,
        grid=(T // tm,),
        in_specs=[pl.BlockSpec((tm, D), lambda i: (i, 0)), pl.BlockSpec((1, D), lambda i: (0, 0)),
                  pl.BlockSpec((D, LANES), lambda i: (0, 0)), pl.BlockSpec((1, LANES), lambda i: (0, 0))],
        out_specs=[pl.BlockSpec((tm, LANES), lambda i: (i, 0)), pl.BlockSpec((tm, LANES), lambda i: (i, 0))],
        out_shape=[jax.ShapeDtypeStruct((T, LANES), jnp.int32), jax.ShapeDtypeStruct((T, LANES), F32)],
        compiler_params=_params(("parallel",), 8 * _nbytes((tm, D), F32)),
        name="moe_router",
    )(x, g.reshape(1, D), w, b)
    return idx[:, :TOP_K], gw[:, :TOP_K]


def _moe_dispatch(idx, tm):
    T = idx.shape[0]
    A = T * TOP_K
    e_flat = idx.reshape(A)
    onehot = (e_flat[:, None] == jnp.arange(N_EXPERTS, dtype=jnp.int32)[None, :]).astype(jnp.int32)
    csum = jnp.cumsum(onehot, axis=0)
    pos_in = jnp.sum(csum * onehot, axis=1) - 1
    counts = csum[-1]
    padded = ((counts + tm - 1) // tm) * tm
    gend = jnp.cumsum(padded)
    gstart = gend - padded
    dest = (jnp.sum(onehot * gstart[None, :], axis=1) + pos_in).astype(jnp.int32)
    n_rows = A + N_EXPERTS * tm
    row_tok = jnp.zeros((n_rows,), jnp.int32).at[dest].set(jnp.arange(A, dtype=jnp.int32) // TOP_K)
    nb = n_rows // tm
    n_used = (gend[-1] // tm).astype(jnp.int32)
    blk = jnp.arange(nb, dtype=jnp.int32)
    be = jnp.sum((blk[:, None] * tm >= gend[None, :]).astype(jnp.int32), axis=1)
    be = jnp.minimum(be, N_EXPERTS - 1)
    be = jnp.where(blk < n_used, be, be[jnp.maximum(n_used - 1, 0)]).astype(jnp.int32)
    eid = jnp.arange(N_EXPERTS, dtype=jnp.int32)
    later = (counts[None, :] > 0) & (eid[None, :] > eid[:, None])
    nxt_of = jnp.min(jnp.where(later, eid[None, :], N_EXPERTS), axis=1)
    nxt_of = jnp.where(nxt_of == N_EXPERTS, -1, nxt_of).astype(jnp.int32)
    mine = be[:, None] == eid[None, :]
    nxt = jnp.sum(mine * nxt_of[None, :], axis=1).astype(jnp.int32)
    group_end_row = jnp.sum(mine * (gstart + counts)[None, :], axis=1)
    nv = jnp.where(blk < n_used, jnp.clip(group_end_row - blk * tm, 0, tm), 0).astype(jnp.int32)
    return dest, row_tok, be, n_used.reshape(1), nxt, nv


def _moe_gather_kernel(tok_ref, nu_ref, x_hbm, g_ref, o_ref, buf, sem, *, rows):
    i = pl.program_id(0)
    n_used = nu_ref[0]

    def row_copy(slot, r, src_row):
        return pltpu.make_async_copy(x_hbm.at[pl.ds(src_row, 1)], buf.at[slot, pl.ds(r, 1)], sem.at[slot])

    def fetch(block, slot):
        def issue(r, c):
            row_copy(slot, r, tok_ref[block * rows + r]).start()
            return c

        lax.fori_loop(0, rows, issue, 0, unroll=8)

    @pl.when(i == 0)
    def _():
        fetch(0, 0)

    @pl.when(i + 1 < n_used)
    def _():
        fetch(i + 1, (i + 1) % 2)

    @pl.when(i < n_used)
    def _():
        slot = i % 2

        def drain(r, c):
            row_copy(slot, r, 0).wait()
            return c

        lax.fori_loop(0, rows, drain, 0, unroll=8)
        x = buf[slot]
        o_ref[...] = (x * lax.rsqrt(jnp.mean(x * x, axis=-1, keepdims=True) + EPS) * g_ref[...]).astype(o_ref.dtype)

    @pl.when(i >= n_used)
    def _():
        o_ref[...] = jnp.zeros(o_ref.shape, o_ref.dtype)


def moe_gather_norm(x, g, row_tok, n_used, *, tm, rows=128):
    T, D = x.shape
    n_rows = row_tok.shape[0]
    per = tm // rows
    return pl.pallas_call(
        functools.partial(_moe_gather_kernel, rows=rows),
        grid_spec=pltpu.PrefetchScalarGridSpec(
            num_scalar_prefetch=2,
            grid=(n_rows // rows,),
            in_specs=[pl.BlockSpec(memory_space=pl.ANY), pl.BlockSpec((1, D), lambda i, tok, nu: (0, 0))],
            out_specs=pl.BlockSpec((rows, D), lambda i, tok, nu: (i, 0)),
            scratch_shapes=[pltpu.VMEM((2, rows, D), F32), pltpu.SemaphoreType.DMA((2,))]),
        out_shape=jax.ShapeDtypeStruct((n_rows, D), BF16),
        compiler_params=_params(("arbitrary",), 10 * _nbytes((rows, D), F32)),
        name="moe_gather",
    )(row_tok, n_used * per, x, g.reshape(1, D))


def _stream_group_weights(be_ref, nu_ref, nxt_ref, w_hbms, stage, sem, w_bf16, *, tile, n_tiles):
    j = pl.program_id(0)
    i = pl.program_id(1)

    def copies(e, jj):
        cols = pl.ds(pl.multiple_of(jj * tile, tile), tile)
        return [pltpu.make_async_copy(w.at[e, :, cols], stage.at[n], sem.at[n]) for n, w in enumerate(w_hbms)]

    @pl.when((j == 0) & (i == 0))
    def _():
        for c in copies(be_ref[0], 0):
            c.start()

    first_of_group = (i < nu_ref[0]) & ((i == 0) | (be_ref[i] != be_ref[jnp.maximum(i - 1, 0)]))

    @pl.when(first_of_group)
    def _():
        for c in copies(be_ref[i], j):
            c.wait()
        for n, dst in enumerate(w_bf16):
            dst[...] = stage[n].astype(BF16)
        more_in_sweep = nxt_ref[i] >= 0
        e_next = jnp.where(more_in_sweep, nxt_ref[i], be_ref[0])
        j_next = jnp.where(more_in_sweep, j, j + 1)

        @pl.when(more_in_sweep | (j + 1 < n_tiles))
        def _():
            for c in copies(e_next, j_next):
                c.start()


def _for_valid_rows(nv, o_ref, compute):
    tm = o_ref.shape[0]
    for rows in range(MOE_ROW_STEP, tm + 1, MOE_ROW_STEP):
        @pl.when((nv > rows - MOE_ROW_STEP) & (nv <= rows))
        def _(rows=rows):
            o_ref[:rows, :] = compute(rows).astype(o_ref.dtype)
            if rows < tm:
                o_ref[rows:, :] = jnp.zeros((tm - rows, o_ref.shape[1]), o_ref.dtype)

    @pl.when(nv == 0)
    def _():
        o_ref[...] = jnp.zeros(o_ref.shape, o_ref.dtype)


def _moe_gateup_kernel(be_ref, nu_ref, nxt_ref, nv_ref, x_ref, wg_hbm, wu_hbm, o_ref, stage, sem, wgb, wub,
                       *, tf, n_tiles):
    _stream_group_weights(be_ref, nu_ref, nxt_ref, (wg_hbm, wu_hbm), stage, sem, (wgb, wub), tile=tf, n_tiles=n_tiles)

    def compute(rows):
        x = x_ref[:rows, :]
        g = jnp.dot(x, wgb[...], preferred_element_type=F32)
        u = jnp.dot(x, wub[...], preferred_element_type=F32)
        return g * jax.nn.sigmoid(g) * u

    _for_valid_rows(nv_ref[pl.program_id(1)], o_ref, compute)


def moe_gateup(xs, wg, wu, be, n_used, nxt, nv, *, tm, tf=512):
    n_rows, K = xs.shape
    F = wg.shape[2]
    tf = min(tf, F)
    nb = n_rows // tm
    xmap = lambda j, i, be, nu, *_: (jnp.minimum(i, nu[0] - 1), 0)
    hbm = pl.BlockSpec(memory_space=pl.ANY)
    vmem = (2 * _nbytes((tm, K), BF16) + 2 * _nbytes((K, tf), F32) + 4 * _nbytes((K, tf), BF16)
            + 6 * _nbytes((tm, tf), F32))
    return pl.pallas_call(
        functools.partial(_moe_gateup_kernel, tf=tf, n_tiles=F // tf),
        grid_spec=pltpu.PrefetchScalarGridSpec(
            num_scalar_prefetch=4,
            grid=(F // tf, nb),
            in_specs=[pl.BlockSpec((tm, K), xmap), hbm, hbm],
            out_specs=pl.BlockSpec((tm, tf), lambda j, i, *_: (i, j)),
            scratch_shapes=[pltpu.VMEM((2, K, tf), F32), pltpu.SemaphoreType.DMA((2,)),
                            pltpu.VMEM((K, tf), BF16), pltpu.VMEM((K, tf), BF16)]),
        out_shape=jax.ShapeDtypeStruct((n_rows, F), BF16),
        compiler_params=_params(("arbitrary", "arbitrary"), vmem),
        name="moe_gateup",
    )(be, n_used, nxt, nv, xs, wg, wu)


def _moe_down_kernel(be_ref, nu_ref, nxt_ref, nv_ref, x_ref, w_hbm, o_ref, stage, sem, wb, *, tn, n_tiles):
    _stream_group_weights(be_ref, nu_ref, nxt_ref, (w_hbm,), stage, sem, (wb,), tile=tn, n_tiles=n_tiles)
    _for_valid_rows(nv_ref[pl.program_id(1)], o_ref,
                    lambda rows: jnp.dot(x_ref[:rows, :], wb[...], preferred_element_type=F32))


def moe_down(hm, wd, be, n_used, nxt, nv, *, tm, tn=1024):
    n_rows, K = hm.shape
    N = wd.shape[2]
    tn = min(tn, N)
    nb = n_rows // tm
    xmap = lambda j, i, be, nu, *_: (jnp.minimum(i, nu[0] - 1), 0)
    vmem = (2 * _nbytes((tm, K), BF16) + _nbytes((K, tn), F32) + 2 * _nbytes((K, tn), BF16)
            + 4 * _nbytes((tm, tn), F32))
    return pl.pallas_call(
        functools.partial(_moe_down_kernel, tn=tn, n_tiles=N // tn),
        grid_spec=pltpu.PrefetchScalarGridSpec(
            num_scalar_prefetch=4,
            grid=(N // tn, nb),
            in_specs=[pl.BlockSpec((tm, K), xmap), pl.BlockSpec(memory_space=pl.ANY)],
            out_specs=pl.BlockSpec((tm, tn), lambda j, i, *_: (i, j)),
            scratch_shapes=[pltpu.VMEM((1, K, tn), F32), pltpu.SemaphoreType.DMA((1,)),
                            pltpu.VMEM((K, tn), BF16)]),
        out_shape=jax.ShapeDtypeStruct((n_rows, N), F32),
        compiler_params=_params(("arbitrary", "arbitrary"), vmem),
        name="moe_down",
    )(be, n_used, nxt, nv, hm, wd)


def _moe_combine_kernel(pos_ref, x_ref, gw_ref, ys_hbm, o_ref, buf, sem, *, rows):
    i = pl.program_id(0)
    n_blocks = pl.num_programs(0)

    def row_copy(slot, k, r, src_row):
        return pltpu.make_async_copy(ys_hbm.at[pl.ds(src_row, 1)], buf.at[slot, k, pl.ds(r, 1)], sem.at[slot])

    def fetch(block, slot):
        def issue(r, c):
            a = (block * rows + r) * TOP_K
            for k in range(TOP_K):
                row_copy(slot, k, r, pos_ref[a + k]).start()
            return c

        lax.fori_loop(0, rows, issue, 0, unroll=4)

    @pl.when(i == 0)
    def _():
        fetch(0, 0)

    @pl.when(i + 1 < n_blocks)
    def _():
        fetch(i + 1, (i + 1) % 2)

    slot = i % 2

    def drain(r, c):
        for k in range(TOP_K):
            row_copy(slot, k, r, 0).wait()
        return c

    lax.fori_loop(0, rows, drain, 0, unroll=4)
    out = x_ref[...]
    for k in range(TOP_K):
        out = out + gw_ref[:, k:k + 1] * buf[slot, k]
    o_ref[...] = out


def moe_combine(x, ys, dest, gw, *, rows=128):
    T, D = x.shape
    return pl.pallas_call(
        functools.partial(_moe_combine_kernel, rows=rows),
        grid_spec=pltpu.PrefetchScalarGridSpec(
            num_scalar_prefetch=1,
            grid=(T // rows,),
            in_specs=[pl.BlockSpec((rows, D), lambda i, pos: (i, 0)),
                      pl.BlockSpec((rows, TOP_K), lambda i, pos: (i, 0)),
                      pl.BlockSpec(memory_space=pl.ANY)],
            out_specs=pl.BlockSpec((rows, D), lambda i, pos: (i, 0)),
            scratch_shapes=[pltpu.VMEM((2, TOP_K, rows, D), F32), pltpu.SemaphoreType.DMA((2,))]),
        out_shape=jax.ShapeDtypeStruct((T, D), F32),
        compiler_params=_params(("arbitrary",), 12 * _nbytes((rows, D), F32)),
        name="moe_combine",
    )(dest, x, gw, ys)


def moe_block(x, g, w_router, b_router, w_egate, w_eup, w_edown, *, tm=512):
    idx, gw = moe_router(x, g, w_router, b_router)
    dest, row_tok, be, n_used, nxt, nv = _moe_dispatch(idx, tm)
    xs = moe_gather_norm(x, g, row_tok, n_used, tm=tm)
    hm = moe_gateup(xs, w_egate, w_eup, be, n_used, nxt, nv, tm=tm)
    ys = moe_down(hm, w_edown, be, n_used, nxt, nv, tm=tm)
    return moe_combine(x, ys, dest, gw)


def even_mixer_block(x, tabs_a, tabs_b, w_in, ga_q, ga_k, g_cq, w_uq, g_ckv, w_ukv, gb_q, gb_kn, gb_kr,
                     w_o, g_mix, *, batch, seq):
    a_width = A_HEADS * HEAD_DIM
    main_cols = 3 * a_width + Q_LORA + KV_LORA
    h = rmsnorm(x, g_mix)
    w_in_t = jnp.swapaxes(w_in, 0, 1)
    proj = matmul(h, w_in_t, n_out=main_cols, tm=2048, single_buffer_x=True, w_transposed=True, name="in_proj")
    w_kr = jnp.pad(w_in_t[main_cols:, :], ((0, LANES - ROPE_DIM), (0, 0)))
    krp = matmul(h, w_kr, w_transposed=True, name="in_proj_kr")
    o_a = mixer_a(proj, tabs_a, ga_q, ga_k, batch=batch, seq=seq)
    cq = rmsnorm(proj, g_cq, width=Q_LORA, col_block=3 * a_width // Q_LORA)
    ckv = rmsnorm(proj, g_ckv, width=KV_LORA, col_block=(3 * a_width + Q_LORA) // KV_LORA)
    w3 = w_uq.reshape(Q_LORA, B_HEADS, QK_DIM)
    w_uq_p = jnp.concatenate(
        [w3[:, :, :NOPE_DIM].reshape(Q_LORA, B_HEADS * NOPE_DIM),
         jnp.pad(w3[:, :, NOPE_DIM:], ((0, 0), (0, 0), (0, LANES - ROPE_DIM))).reshape(Q_LORA, B_HEADS * LANES)],
        axis=1)
    q_up = matmul(cq, w_uq_p, name="q_up")
    kv_up = matmul(ckv, w_ukv, name="kv_up")
    pad_r = lambda v: jnp.pad(v, (0, LANES - ROPE_DIM)).reshape(1, LANES)
    qt, kf, vb = mla_prep(q_up, kv_up, krp, tabs_b, gb_q[:NOPE_DIM].reshape(1, LANES), pad_r(gb_q[NOPE_DIM:]),
                          gb_kn.reshape(1, LANES), pad_r(gb_kr), batch=batch, seq=seq)
    o_b = mla_attention(qt, kf, vb, batch=batch, seq=seq)
    return matmul((o_a, o_b), w_o, res=x, name="mixer_out")


def kernel(x, mem, positions, g_mem, w_mem_kv, g_mem_k, g_mix, g_x, w_xq, g_xq, w_xo, g_ffn,
           w_in, ga_q, ga_k, g_cq, w_uq, g_ckv, w_ukv, gb_q, gb_kn, gb_kr, w_o_even,
           w_gate, w_up, w_down, w_qkv, w_o_odd, w_router, b_router, w_egate, w_eup, w_edown):
    B, S, D = x.shape
    M = mem.shape[1]
    depth = g_mix.shape[0]
    T = B * S
    xt = x.reshape(T, D)
    tabs_a = _rope_tables(positions, ROT_DIM)
    tabs_b = _rope_tables(positions, ROPE_DIM)
    kv = matmul(rmsnorm(mem.reshape(B * M, D), g_mem), w_mem_kv, tm=B * M, name="mem_kv")
    mk, mv = memkv_post(kv, g_mem_k)
    for layer in range(depth):
        i = layer // 2
        if layer % 2 == 0:
            xt = even_mixer_block(xt, tabs_a, tabs_b, w_in[i], ga_q[i], ga_k[i], g_cq[i], w_uq[i], g_ckv[i],
                                  w_ukv[i], gb_q[i], gb_kn[i], gb_kr[i], w_o_even[i], g_mix[layer],
                                  batch=B, seq=S)
        else:
            h = rmsnorm(xt, g_mix[layer])
            qkv = matmul(h, w_qkv[i], out_dtype=BF16, tm=2048, single_buffer_x=True, name="qkv_proj")
            o = stickbreak_attention(qkv, batch=B, seq=S, heads=D // HEAD_DIM)
            xt = matmul(o, w_o_odd[i], res=xt, name="mixer_out")
        xattn = functools.partial(cross_attention_block, xt, mk, mv, g_x[layer], w_xq[layer], g_xq[layer],
                                  w_xo[layer], seq=S, mem_len=M)
        if layer % 2 == 0:
            xt, h = xattn(g_ffn[layer])
            hm = swiglu_gateup(h, w_gate[i], w_up[i])
            xt = matmul(hm, w_down[i], res=xt, tn=1024, tk=2048, name="swiglu_down")
        else:
            xt = moe_block(xattn(), g_ffn[layer], w_router[i], b_router[i], w_egate[i], w_eup[i], w_edown[i])
    return xt.reshape(B, S, D)
```

```python
import functools

import jax
import jax.numpy as jnp
from jax import lax
from jax.experimental import pallas as pl
from jax.experimental.pallas import tpu as pltpu

F32 = jnp.float32
BF16 = jnp.bfloat16

HEAD_DIM = 128
ROT_DIM = HEAD_DIM // 4
ROPE_THETA = 500000.0
BLOCK = 128
NEG_INF = -1e30
EPS = 1e-6
DILATED_PAIRS = ((128, 1), (512, 4), (2048, 16))
A_HEADS = 16
B_HEADS = 16
Q_LORA = 1536
KV_LORA = 512
NOPE_DIM = 128
ROPE_DIM = 64
QK_DIM = NOPE_DIM + ROPE_DIM
X_HEADS = 4
N_EXPERTS = 8
TOP_K = 2
MIXER_A_GROUP = 8
MOE_ROW_STEP = 128
LOG2_E = 1.4426950408889634
MLA_Q_SCALE = QK_DIM ** -0.5 * LOG2_E

LANES = 128
V7X_VMEM_BYTES = 64 * 1024 * 1024
VMEM_CAP = V7X_VMEM_BYTES - 6 * 1024 * 1024


def _params(semantics, vmem_bytes):
    return pltpu.CompilerParams(dimension_semantics=semantics,
                                vmem_limit_bytes=int(min(max(vmem_bytes, 32 * 1024 * 1024), VMEM_CAP)))


def _nbytes(shape, dtype):
    n = 1
    for s in shape:
        n *= s
    return n * jnp.dtype(dtype).itemsize


def _rmsnorm_kernel(x_ref, g_ref, o_ref):
    x = x_ref[...]
    ms = jnp.mean(x * x, axis=-1, keepdims=True)
    o_ref[...] = (x * lax.rsqrt(ms + EPS) * g_ref[...]).astype(o_ref.dtype)


def rmsnorm(x, g, *, tm=256):
    T, width = x.shape
    tm = min(tm, T)
    return pl.pallas_call(
        _rmsnorm_kernel,
        grid=(T // tm,),
        in_specs=[pl.BlockSpec((tm, width), lambda i: (i, 0)),
                  pl.BlockSpec((1, width), lambda i: (0, 0))],
        out_specs=pl.BlockSpec((tm, width), lambda i: (i, 0)),
        out_shape=jax.ShapeDtypeStruct((T, width), BF16),
        compiler_params=_params(("parallel",), 6 * _nbytes((tm, width), F32)),
        name="rmsnorm",
    )(x, g.reshape(1, width))


def _mm_kernel(*refs, nk, n_x, has_res, w_transposed):
    x_refs = refs[:n_x]
    w_ref = refs[n_x]
    rest = refs[n_x + 1:]
    res_ref = rest[0] if has_res else None
    o_ref = rest[1] if has_res else rest[0]
    acc_ref = rest[-1] if nk > 1 else None
    k = pl.program_id(2)
    w = w_ref[...].astype(BF16)

    def finish(part):
        out = part
        if has_res:
            out = out + res_ref[...]
        o_ref[...] = out.astype(o_ref.dtype)

    if n_x == 2:
        half = x_refs[0].shape[1]
        finish(jnp.dot(x_refs[0][...], w[:half], preferred_element_type=F32)
               + jnp.dot(x_refs[1][...], w[half:], preferred_element_type=F32))
    elif nk == 1 and w_transposed:
        finish(lax.dot_general(x_refs[0][...], w, (((1,), (1,)), ((), ())), preferred_element_type=F32))
    elif nk == 1:
        finish(jnp.dot(x_refs[0][...], w, preferred_element_type=F32))
    else:
        part = jnp.dot(x_refs[0][...], w, preferred_element_type=F32)

        @pl.when(k == 0)
        def _():
            acc_ref[...] = part

        @pl.when((k > 0) & (k < nk - 1))
        def _():
            acc_ref[...] += part

        @pl.when(k == nk - 1)
        def _():
            finish(acc_ref[...] + part)


def matmul(xs, w, *, n_out=None, res=None, out_dtype=F32, tm=1024, tn=512, tk=None, single_buffer_x=False,
           w_transposed=False, name="matmul"):
    if not isinstance(xs, (tuple, list)):
        xs = (xs,)
    n_x = len(xs)
    T = xs[0].shape[0]
    K = sum(x.shape[1] for x in xs)
    n_out = (w.shape[0] if w_transposed else w.shape[1]) if n_out is None else n_out
    tm = min(tm, T)
    tn = min(tn, n_out)
    if n_x == 2:
        assert tk is None and xs[1].shape[1] == xs[0].shape[1]
    tk = K if tk is None else tk
    nk = K // tk
    assert K == nk * tk and T % tm == 0 and n_out % tn == 0
    x_buffers = 2
    if n_x == 2:
        x_specs = [pl.BlockSpec((tm, K // 2), lambda i, j, k: (i, 0)) for _ in xs]
    elif nk == 1 and single_buffer_x:
        x_buffers = 1
        x_specs = [pl.BlockSpec((tm, tk), lambda i, j, k: (i, k), pipeline_mode=pl.Buffered(1))]
    else:
        x_specs = [pl.BlockSpec((tm, tk), lambda i, j, k: (i, k))]
    if w_transposed:
        assert n_x == 1 and nk == 1
        in_specs = x_specs + [pl.BlockSpec((tn, tk), lambda i, j, k: (j, k))]
    else:
        in_specs = x_specs + [pl.BlockSpec((tk, tn), lambda i, j, k: (k, j))]
    args = list(xs) + [w]
    if res is not None:
        in_specs.append(pl.BlockSpec((tm, tn), lambda i, j, k: (i, j)))
        args.append(res)
    scratch = [pltpu.VMEM((tm, tn), F32)] if nk > 1 else []
    vmem = (x_buffers * _nbytes((tm, tk), xs[0].dtype) + 2 * _nbytes((tk, tn), F32) + _nbytes((tk, tn), BF16)
            + (2 * _nbytes((tm, tn), F32) if res is not None else 0)
            + 2 * _nbytes((tm, tn), out_dtype) + 2 * _nbytes((tm, tn), F32))
    return pl.pallas_call(
        functools.partial(_mm_kernel, nk=nk, n_x=n_x, has_res=res is not None, w_transposed=w_transposed),
        grid=(T // tm, n_out // tn, nk),
        in_specs=in_specs,
        out_specs=pl.BlockSpec((tm, tn), lambda i, j, k: (i, j)),
        out_shape=jax.ShapeDtypeStruct((T, n_out), out_dtype),
        scratch_shapes=scratch,
        compiler_params=_params(("parallel", "parallel", "arbitrary"), vmem),
        name=name,
    )(*args)


def _norm_mm_kernel(x_ref, g_ref, w_ref, o_ref, xn):
    @pl.when(pl.program_id(1) == 0)
    def _():
        x = x_ref[...]
        xn[...] = (x * lax.rsqrt(jnp.mean(x * x, axis=-1, keepdims=True) + EPS) * g_ref[...]).astype(xn.dtype)

    o_ref[...] = jnp.dot(xn[...], w_ref[...].astype(BF16), preferred_element_type=F32)


def norm_matmul(x, g, w, *, width, col_block, tm=1024, tn=1024, name="norm_matmul"):
    T = x.shape[0]
    N = w.shape[1]
    tm, tn = min(tm, T), min(tn, N)
    vmem = (2 * _nbytes((tm, width), F32) + _nbytes((tm, width), BF16) + 2 * _nbytes((width, tn), F32)
            + _nbytes((width, tn), BF16) + 4 * _nbytes((tm, tn), F32))
    return pl.pallas_call(
        _norm_mm_kernel,
        grid=(T // tm, N // tn),
        in_specs=[pl.BlockSpec((tm, width), lambda i, j: (i, col_block)),
                  pl.BlockSpec((1, width), lambda i, j: (0, 0)),
                  pl.BlockSpec((width, tn), lambda i, j: (0, j))],
        out_specs=pl.BlockSpec((tm, tn), lambda i, j: (i, j)),
        out_shape=jax.ShapeDtypeStruct((T, N), F32),
        scratch_shapes=[pltpu.VMEM((tm, width), BF16)],
        compiler_params=_params(("parallel", "arbitrary"), vmem),
        name=name,
    )(x, g.reshape(1, width), w)


def _rope_tables(positions, dim):
    half = dim // 2
    inv_freq = ROPE_THETA ** (-jnp.arange(0, dim, 2, dtype=F32) / dim)
    ang = positions.astype(F32)[..., None] * inv_freq
    cos, sin = jnp.cos(ang), jnp.sin(ang)
    B, S = positions.shape
    ones = jnp.ones((B, S, LANES - dim), F32)
    zeros = jnp.zeros((B, S, LANES - dim), F32)
    zh = jnp.zeros((B, S, half), F32)
    c = jnp.concatenate([cos, cos, ones], axis=-1)
    sa = jnp.concatenate([-sin, zh, zeros], axis=-1)
    sb = jnp.concatenate([zh, sin, zeros], axis=-1)
    return [t.reshape(B * S, LANES) for t in (c, sa, sb)]


def _rope_lanes(y, c, sa, sb, half):
    return y * c + pltpu.roll(y, LANES - half, 1) * sa + pltpu.roll(y, half, 1) * sb


def _mixer_a_kernel(q_ref, k_ref, v_ref, c_ref, sa_ref, sb_ref, gq_ref, gk_ref, o_ref,
                    qs, ks, *state, seq):
    scale = HEAD_DIM ** -0.5 * LOG2_E
    c, sa, sb = c_ref[...], sa_ref[...], sb_ref[...]

    def prep(x, g):
        y = x * lax.rsqrt(jnp.mean(x * x, axis=-1, keepdims=True) + EPS) * g
        return _rope_lanes(y, c, sa, sb, ROT_DIM // 2)

    qs[...] = prep(q_ref[...], gq_ref[...])
    ks[...] = prep(k_ref[...], gk_ref[...])
    qi = lax.broadcasted_iota(jnp.int32, (BLOCK, 1), 0)
    kj = lax.broadcasted_iota(jnp.int32, (1, BLOCK), 1)
    trans_b = (((1,), (1,)), ((), ()))
    ones = jnp.ones((BLOCK, HEAD_DIM), BF16)

    for bi, (window, dil) in enumerate(DILATED_PAIRS):
        acc, m_sc, l_sc = state[3 * bi:3 * bi + 3]
        n_back = window // dil
        nb = seq // dil // BLOCK
        assert n_back <= BLOCK and nb * dil * BLOCK == seq
        cur_ok = (qi >= kj) & (qi - kj <= n_back)
        prev_ok = qi + BLOCK - kj <= n_back

        def rows_at(r, n, dil=dil):
            if dil == 1:
                return pl.ds(BLOCK * n, BLOCK)
            return pl.ds(r + dil * BLOCK * n, BLOCK, stride=dil)

        blocks = [(rows_at(r, n), rows_at(r, n - 1) if n > 0 else None) for r in range(dil) for n in range(nb)]
        for first in range(0, len(blocks), MIXER_A_GROUP):
            group = blocks[first:first + MIXER_A_GROUP]
            s_cur, s_prev = [], []
            for rows, prev in group:
                qb = qs[rows, :].astype(BF16)
                s_c = lax.dot_general(qb, ks[rows, :].astype(BF16), trans_b, preferred_element_type=F32) * scale
                s_cur.append(jnp.where(cur_ok, s_c, NEG_INF))
                if prev is None:
                    s_prev.append(None)
                else:
                    s_p = lax.dot_general(qb, ks[prev, :].astype(BF16), trans_b,
                                          preferred_element_type=F32) * scale
                    s_prev.append(jnp.where(prev_ok, s_p, NEG_INF))
            stats = []
            for s_c, s_p in zip(s_cur, s_prev):
                m = jnp.max(s_c, axis=-1, keepdims=True)
                if s_p is not None:
                    m = jnp.maximum(m, jnp.max(s_p, axis=-1, keepdims=True))
                p_c = jnp.exp2(s_c - m).astype(BF16)
                p_p = None if s_p is None else jnp.exp2(s_p - m).astype(BF16)
                stats.append((m, p_c, p_p))
            for (rows, prev), (m, p_c, p_p) in zip(group, stats):
                v_ext = jnp.concatenate([v_ref[rows, :].astype(BF16), ones], axis=1)
                o = jnp.dot(p_c, v_ext, preferred_element_type=F32)
                if p_p is not None:
                    v_ext = jnp.concatenate([v_ref[prev, :].astype(BF16), ones], axis=1)
                    o = o + jnp.dot(p_p, v_ext, preferred_element_type=F32)
                acc[rows, :] = o[:, :HEAD_DIM]
                l_sc[rows, :] = o[:, HEAD_DIM:]
                m_sc[rows, :] = jnp.broadcast_to(m, (BLOCK, HEAD_DIM))

    n_br = len(DILATED_PAIRS)
    m_all = state[1][...]
    for bi in range(1, n_br):
        m_all = jnp.maximum(m_all, state[3 * bi + 1][...])
    num = jnp.zeros(o_ref.shape, F32)
    den = jnp.zeros(o_ref.shape, F32)
    for bi in range(n_br):
        w = jnp.exp2(state[3 * bi + 1][...] - m_all)
        num = num + w * state[3 * bi][...]
        den = den + w * state[3 * bi + 2][...]
    o_ref[...] = (num / den).astype(o_ref.dtype)


def mixer_a(proj, tabs, gq, gk, *, batch, seq):
    T = batch * seq
    blk = (seq, HEAD_DIM)
    head = lambda off: pl.BlockSpec(blk, lambda b, h: (b, off + h))
    tab = pl.BlockSpec(blk, lambda b, h: (b, 0))
    gspec = pl.BlockSpec((1, HEAD_DIM), lambda b, h: (0, 0))
    return pl.pallas_call(
        functools.partial(_mixer_a_kernel, seq=seq),
        grid=(batch, A_HEADS),
        in_specs=[head(0), head(A_HEADS), head(2 * A_HEADS), tab, tab, tab, gspec, gspec],
        out_specs=pl.BlockSpec(blk, lambda b, h: (b, h)),
        out_shape=jax.ShapeDtypeStruct((T, A_HEADS * HEAD_DIM), BF16),
        scratch_shapes=[pltpu.VMEM(blk, F32) for _ in range(2 + 3 * len(DILATED_PAIRS))],
        compiler_params=_params(("parallel", "parallel"), 30 * _nbytes(blk, F32)),
        name="mixer_a",
    )(proj, proj, proj, *tabs, gq.reshape(1, HEAD_DIM), gk.reshape(1, HEAD_DIM))


def _mla_prep_kernel(qn_ref, qr_ref, kn_ref, v_ref, kr_ref, c_ref, sa_ref, sb_ref,
                     gqn_ref, gqr_ref, gkn_ref, gkr_ref, qt_ref, kf_ref, vb_ref, kr_sc):
    c, sa, sb = c_ref[...], sa_ref[...], sb_ref[...]
    half = ROPE_DIM // 2

    @pl.when(pl.program_id(1) == 0)
    def _():
        kr = kr_ref[...]
        kr = kr * lax.rsqrt(jnp.sum(kr * kr, axis=-1, keepdims=True) * (1.0 / ROPE_DIM) + EPS) * gkr_ref[...]
        kr_sc[...] = _rope_lanes(kr, c, sa, sb, half).astype(kr_sc.dtype)

    qn, qr = qn_ref[...], qr_ref[...]
    ss = jnp.sum(qn * qn, axis=-1, keepdims=True) + jnp.sum(qr * qr, axis=-1, keepdims=True)
    inv = lax.rsqrt(ss * (1.0 / QK_DIM) + EPS) * MLA_Q_SCALE
    qf = jnp.concatenate([qn * inv * gqn_ref[...], _rope_lanes(qr * inv * gqr_ref[...], c, sa, sb, half)], axis=1)
    qt_ref[...] = qf.astype(qt_ref.dtype).T
    kn = kn_ref[...]
    kn = kn * lax.rsqrt(jnp.mean(kn * kn, axis=-1, keepdims=True) + EPS) * gkn_ref[...]
    kf_ref[:, :NOPE_DIM] = kn.astype(kf_ref.dtype)
    kf_ref[:, NOPE_DIM:] = kr_sc[...]
    vb_ref[...] = v_ref[...].astype(vb_ref.dtype)


def mla_prep(q_up, kv_up, krp, tabs, gqn, gqr, gkn, gkr, *, batch, seq, tm=512):
    T = q_up.shape[0]
    H = B_HEADS
    per_seq = seq // tm
    blk = lambda f: pl.BlockSpec((tm, LANES), f)
    g = pl.BlockSpec((1, LANES), lambda i, h: (0, 0))
    return pl.pallas_call(
        _mla_prep_kernel,
        grid=(T // tm, H),
        in_specs=[blk(lambda i, h: (i, h)), blk(lambda i, h: (i, H + h)),
                  blk(lambda i, h: (i, 2 * h)), blk(lambda i, h: (i, 2 * h + 1)),
                  blk(lambda i, h: (i, 0)), blk(lambda i, h: (i, 0)), blk(lambda i, h: (i, 0)),
                  blk(lambda i, h: (i, 0)), g, g, g, g],
        out_specs=[pl.BlockSpec((None, None, 2 * LANES, tm), lambda i, h: (i // per_seq, h, 0, i % per_seq)),
                   pl.BlockSpec((tm, 2 * LANES), lambda i, h: (i, h)),
                   pl.BlockSpec((tm, LANES), lambda i, h: (i, h))],
        out_shape=[jax.ShapeDtypeStruct((batch, H, 2 * LANES, seq), BF16),
                   jax.ShapeDtypeStruct((T, H * 2 * LANES), BF16),
                   jax.ShapeDtypeStruct((T, H * LANES), BF16)],
        scratch_shapes=[pltpu.VMEM((tm, LANES), BF16)],
        compiler_params=_params(("parallel", "arbitrary"), 40 * _nbytes((tm, LANES), F32)),
        name="mla_prep",
    )(q_up, q_up, kv_up, kv_up, krp, *tabs, gqn, gqr, gkn, gkr)


def _causal_attn_kernel(qt_ref, k_ref, v_ref, o_ref, s_sc, acc, *, tq, hg, dq, dv):
    qi = pl.program_id(2)
    key = lax.broadcasted_iota(jnp.int32, (tq, tq), 0)
    qry = lax.broadcasted_iota(jnp.int32, (tq, tq), 1)
    visible = key <= qry
    ones = jnp.ones((tq, dv), BF16)
    contract_rows = (((0,), (0,)), ((), ()))

    heads = range(hg)

    def scores(j, diagonal):
        rows = pl.ds(pl.multiple_of(j * tq, tq), tq)
        s = [jnp.dot(k_ref[rows, g * dq:(g + 1) * dq], qt_ref[g], preferred_element_type=F32) for g in heads]
        if diagonal:
            s = [jnp.where(visible, sg, NEG_INF) for sg in s]
        for g in heads:
            s_sc[g, j] = s[g]
        return s

    def col_max(j, ms):
        s = scores(j, False)
        return tuple(jnp.maximum(ms[g], jnp.max(s[g], axis=0, keepdims=True)) for g in heads)

    ms = lax.fori_loop(0, qi, col_max, tuple(jnp.max(sg, axis=0, keepdims=True) for sg in scores(qi, True)))

    def weighted(j):
        rows = pl.ds(pl.multiple_of(j * tq, tq), tq)
        p = [jnp.exp2(s_sc[g, j] - ms[g]).astype(BF16) for g in heads]
        return [lax.dot_general(p[g], jnp.concatenate([v_ref[rows, g * dv:(g + 1) * dv], ones], axis=1),
                                contract_rows, preferred_element_type=F32) for g in heads]

    for g, w in enumerate(weighted(qi)):
        acc[g] = w

    def below(j, carry):
        for g, w in enumerate(weighted(j)):
            acc[g] += w
        return carry

    lax.fori_loop(0, qi, below, 0)
    for g in heads:
        a = acc[g]
        o_ref[:, g * dv:(g + 1) * dv] = (a[:, :dv] / a[:, dv:]).astype(o_ref.dtype)


def mla_attention(qt, kf, vb, *, batch, seq, tq=512, hg=4):
    T = batch * seq
    nq = seq // tq
    dq = qt.shape[2]
    dv = vb.shape[1] // B_HEADS
    assert B_HEADS % hg == 0
    return pl.pallas_call(
        functools.partial(_causal_attn_kernel, tq=tq, hg=hg, dq=dq, dv=dv),
        grid=(batch, B_HEADS // hg, nq),
        in_specs=[pl.BlockSpec((None, hg, dq, tq), lambda b, h, i: (b, h, 0, i)),
                  pl.BlockSpec((seq, hg * dq), lambda b, h, i: (b, h)),
                  pl.BlockSpec((seq, hg * dv), lambda b, h, i: (b, h))],
        out_specs=pl.BlockSpec((tq, hg * dv), lambda b, h, i: (b * nq + i, h)),
        out_shape=jax.ShapeDtypeStruct((T, B_HEADS * dv), BF16),
        scratch_shapes=[pltpu.VMEM((hg, nq, tq, tq), F32), pltpu.VMEM((hg, tq, 2 * dv), F32)],
        compiler_params=_params(("parallel", "parallel", "parallel"),
                                _nbytes((hg, nq, tq, tq), F32) + 2 * _nbytes((seq, hg * (dq + dv)), BF16)
                                + 8 * hg * _nbytes((tq, tq), F32)),
        name="mla_attention",
    )(qt, kf, vb)


STICK_EXHAUSTED_LOG = -110.0


def _stickbreak_kernel(q_ref, k_ref, v_ref, o_ref, qt_sc, c_sc, acc, *, tq, hg, scale):
    qi = pl.program_id(2)
    c_sc[...] = jnp.zeros(c_sc.shape, F32)
    acc[...] = jnp.zeros(acc.shape, F32)
    for g in range(hg):
        qt_sc[g] = q_ref[:, g * HEAD_DIM:(g + 1) * HEAD_DIM].T
    key = lax.broadcasted_iota(jnp.int32, (tq, tq), 0)
    qry = lax.broadcasted_iota(jnp.int32, (tq, tq), 1)
    after = (qry > key).astype(BF16)
    earlier = key < qry
    contract_rows = (((0,), (0,)), ((), ()))

    def chunk(j, diagonal):
        rows = pl.ds(pl.multiple_of(j * tq, tq), tq)
        heads = range(hg)
        cols = [slice(g * HEAD_DIM, (g + 1) * HEAD_DIM) for g in heads]
        z = [jnp.dot(k_ref[rows, cols[g]], qt_sc[g], preferred_element_type=F32) * (scale * LOG2_E) for g in heads]
        log_beta, parts = [], []
        for g in heads:
            softplus = jnp.maximum(z[g], 0.0) + jnp.log2(1.0 + jnp.exp2(-jnp.abs(z[g])))
            log_keep = -softplus
            if diagonal:
                log_keep = jnp.where(earlier, log_keep, 0.0)
            parts.append(log_keep.astype(BF16))
            log_beta.append(z[g] - softplus + c_sc[g])
            c_sc[g] += jnp.sum(log_keep, axis=0, keepdims=True)
        within = jnp.dot(after, jnp.concatenate(parts, axis=1), preferred_element_type=F32)
        for g in heads:
            a = jnp.exp2(log_beta[g] + within[:, g * tq:(g + 1) * tq])
            if diagonal:
                a = jnp.where(earlier, a, 0.0)
            acc[:, cols[g]] += lax.dot_general(a.astype(BF16), v_ref[rows, cols[g]], contract_rows,
                                               preferred_element_type=F32)

    def stick_left():
        return jnp.max(c_sc[...]) > STICK_EXHAUSTED_LOG * LOG2_E

    chunk(qi, True)

    def more(state):
        j, left = state
        return (j >= 0) & left

    def step(state):
        j, _ = state
        chunk(j, False)
        return j - 1, stick_left()

    lax.while_loop(more, step, (qi - 1, stick_left()))
    o_ref[...] = acc[...].astype(o_ref.dtype)


def stickbreak_attention(qkv, *, batch, seq, heads, tq=256, hg=4):
    T = batch * seq
    nq = seq // tq
    ng = heads // hg
    assert heads == ng * hg
    blk = hg * HEAD_DIM
    return pl.pallas_call(
        functools.partial(_stickbreak_kernel, tq=tq, hg=hg, scale=HEAD_DIM ** -0.5),
        grid=(batch, ng, nq),
        in_specs=[pl.BlockSpec((tq, blk), lambda b, h, i: (b * nq + i, h)),
                  pl.BlockSpec((seq, blk), lambda b, h, i: (b, ng + h)),
                  pl.BlockSpec((seq, blk), lambda b, h, i: (b, 2 * ng + h))],
        out_specs=pl.BlockSpec((tq, blk), lambda b, h, i: (b * nq + i, h)),
        out_shape=jax.ShapeDtypeStruct((T, heads * HEAD_DIM), BF16),
        scratch_shapes=[pltpu.VMEM((hg, HEAD_DIM, tq), BF16), pltpu.VMEM((hg, 1, tq), F32),
                        pltpu.VMEM((tq, blk), F32)],
        compiler_params=_params(("parallel", "parallel", "parallel"), 0),
        name="stickbreak",
    )(qkv, qkv, qkv)


def _memkv_kernel(kv_ref, g_ref, mk_ref, mv_ref):
    for h in range(X_HEADS):
        cols = slice(h * HEAD_DIM, (h + 1) * HEAD_DIM)
        k = kv_ref[:, cols]
        k = k * lax.rsqrt(jnp.mean(k * k, axis=-1, keepdims=True) + EPS) * g_ref[...]
        mk_ref[:, cols] = k.astype(mk_ref.dtype)
    mv_ref[...] = kv_ref[:, X_HEADS * HEAD_DIM:].astype(mv_ref.dtype)


def memkv_post(kv, g):
    M = kv.shape[0]
    xd = X_HEADS * HEAD_DIM
    return pl.pallas_call(
        _memkv_kernel,
        grid=(1,),
        in_specs=[pl.BlockSpec((M, 2 * xd), lambda i: (0, 0)), pl.BlockSpec((1, HEAD_DIM), lambda i: (0, 0))],
        out_specs=[pl.BlockSpec((M, xd), lambda i: (0, 0)), pl.BlockSpec((M, xd), lambda i: (0, 0))],
        out_shape=[jax.ShapeDtypeStruct((M, xd), BF16), jax.ShapeDtypeStruct((M, xd), BF16)],
        compiler_params=_params(("arbitrary",), 0),
        name="memkv_post",
    )(kv, g.reshape(1, HEAD_DIM))


def _xattn_block_kernel(x_ref, gx_ref, wq_ref, gq_ref, mk_ref, mv_ref, wo_ref, *rest, emit_norm):
    if emit_norm:
        gn_ref, o_ref, hn_ref = rest
    else:
        (o_ref,) = rest
    scale = HEAD_DIM ** -0.5
    x = x_ref[...]
    h = (x * lax.rsqrt(jnp.mean(x * x, axis=-1, keepdims=True) + EPS) * gx_ref[...]).astype(BF16)
    q_all = jnp.dot(h, wq_ref[...], preferred_element_type=F32)
    heads = []
    for hd in range(X_HEADS):
        cols = slice(hd * HEAD_DIM, (hd + 1) * HEAD_DIM)
        q = q_all[:, cols]
        q = (q * lax.rsqrt(jnp.mean(q * q, axis=-1, keepdims=True) + EPS) * gq_ref[...]).astype(BF16)
        s = lax.dot_general(q, mk_ref[:, cols], (((1,), (1,)), ((), ())), preferred_element_type=F32) * scale
        p = jnp.exp(s - jnp.max(s, axis=-1, keepdims=True))
        p = p / jnp.sum(p, axis=-1, keepdims=True)
        heads.append(jnp.dot(p.astype(BF16), mv_ref[:, cols], preferred_element_type=F32).astype(BF16))
    y = x + jnp.dot(jnp.concatenate(heads, axis=1), wo_ref[...], preferred_element_type=F32)
    o_ref[...] = y
    if emit_norm:
        hn_ref[...] = (y * lax.rsqrt(jnp.mean(y * y, axis=-1, keepdims=True) + EPS) * gn_ref[...]).astype(hn_ref.dtype)


def cross_attention_block(x, mk, mv, g_x, w_xq, g_xq, w_xo, g_next=None, *, seq, mem_len, tm=256):
    T, D = x.shape
    xd = X_HEADS * HEAD_DIM
    per_seq = seq // tm
    emit_norm = g_next is not None
    row = lambda w: pl.BlockSpec((1, w), lambda i: (0, 0))
    tok = pl.BlockSpec((tm, D), lambda i: (i, 0))
    in_specs = [tok, row(D), pl.BlockSpec((D, xd), lambda i: (0, 0)), row(HEAD_DIM),
                pl.BlockSpec((mem_len, xd), lambda i: (i // per_seq, 0)),
                pl.BlockSpec((mem_len, xd), lambda i: (i // per_seq, 0)),
                pl.BlockSpec((xd, D), lambda i: (0, 0))]
    args = [x, g_x.reshape(1, D), w_xq.astype(BF16), g_xq.reshape(1, HEAD_DIM), mk, mv, w_xo.astype(BF16)]
    out_specs, out_shape = tok, jax.ShapeDtypeStruct((T, D), F32)
    if emit_norm:
        in_specs.append(row(D))
        args.append(g_next.reshape(1, D))
        out_specs = [tok, tok]
        out_shape = [out_shape, jax.ShapeDtypeStruct((T, D), BF16)]
    vmem = 10 * _nbytes((tm, D), F32) + 4 * _nbytes((D, xd), BF16)
    return pl.pallas_call(
        functools.partial(_xattn_block_kernel, emit_norm=emit_norm),
        grid=(T // tm,),
        in_specs=in_specs,
        out_specs=out_specs,
        out_shape=out_shape,
        compiler_params=_params(("parallel",), vmem),
        name="cross_attention",
    )(*args)


def _gateup_kernel(x_ref, wg_ref, wu_ref, o_ref):
    x = x_ref[...]
    g = jnp.dot(x, wg_ref[...].astype(BF16), preferred_element_type=F32)
    u = jnp.dot(x, wu_ref[...].astype(BF16), preferred_element_type=F32)
    o_ref[...] = (g * jax.nn.sigmoid(g) * u).astype(o_ref.dtype)


def swiglu_gateup(x, wg, wu, *, tm=2048, tf=256):
    T, K = x.shape
    F = wg.shape[1]
    vmem = (_nbytes((tm, K), BF16) + 4 * _nbytes((K, tf), F32) + 2 * _nbytes((K, tf), BF16)
            + 6 * _nbytes((tm, tf), F32))
    return pl.pallas_call(
        _gateup_kernel,
        grid=(T // tm, F // tf),
        in_specs=[pl.BlockSpec((tm, K), lambda i, j: (i, 0), pipeline_mode=pl.Buffered(1)),
                  pl.BlockSpec((K, tf), lambda i, j: (0, j)),
                  pl.BlockSpec((K, tf), lambda i, j: (0, j))],
        out_specs=pl.BlockSpec((tm, tf), lambda i, j: (i, j)),
        out_shape=jax.ShapeDtypeStruct((T, F), BF16),
        compiler_params=_params(("parallel", "parallel"), vmem),
        name="swiglu_gateup",
    )(x, wg, wu)


def _router_kernel(x_ref, g_ref, w_ref, b_ref, idx_ref, gw_ref):
    x = x_ref[...]
    h = x * lax.rsqrt(jnp.mean(x * x, axis=-1, keepdims=True) + EPS) * g_ref[...]
    logits = jnp.dot(h, w_ref[...], preferred_element_type=F32, precision=lax.Precision.HIGHEST) + b_ref[...]
    lane = lax.broadcasted_iota(jnp.int32, logits.shape, 1)
    logits = jnp.where(lane < N_EXPERTS, logits, -jnp.inf)
    m1 = jnp.max(logits, axis=-1, keepdims=True)
    i1 = jnp.min(jnp.where(logits == m1, lane, LANES), axis=-1, keepdims=True)
    rest = jnp.where(lane == i1, -jnp.inf, logits)
    m2 = jnp.max(rest, axis=-1, keepdims=True)
    i2 = jnp.min(jnp.where(rest == m2, lane, LANES), axis=-1, keepdims=True)
    e = jnp.exp(m2 - m1)
    w1 = 1.0 / (1.0 + e)
    w2 = e / (1.0 + e)
    idx_ref[...] = jnp.where(lane == 0, i1, jnp.where(lane == 1, i2, 0))
    gw_ref[...] = jnp.where(lane == 0, w1, jnp.where(lane == 1, w2, 0.0))


def moe_router(x, g, w_router, b_router, *, tm=256):
    T, D = x.shape
    E = w_router.shape[1]
    w = jnp.pad(w_router, ((0, 0), (0, LANES - E)))
    b = jnp.pad(b_router, (0, LANES - E)).reshape(1, LANES)
    idx, gw = pl.pallas_call(
        _router_kernel,
        grid=(T // tm,),
        in_specs=[pl.BlockSpec((tm, D), lambda i: (i, 0)), pl.BlockSpec((1, D), lambda i: (0, 0)),
                  pl.BlockSpec((D, LANES), lambda i: (0, 0)), pl.BlockSpec((1, LANES), lambda i: (0, 0))],
        out_specs=[pl.BlockSpec((tm, LANES), lambda i: (i, 0)), pl.BlockSpec((tm, LANES), lambda i: (i, 0))],
        out_shape=[jax.ShapeDtypeStruct((T, LANES), jnp.int32), jax.ShapeDtypeStruct((T, LANES), F32)],
        compiler_params=_params(("parallel",), 8 * _nbytes((tm, D), F32)),
        name="moe_router",
    )(x, g.reshape(1, D), w, b)
    return idx[:, :TOP_K], gw[:, :TOP_K]


def _moe_dispatch(idx, tm):
    T = idx.shape[0]
    A = T * TOP_K
    e_flat = idx.reshape(A)
    onehot = (e_flat[:, None] == jnp.arange(N_EXPERTS, dtype=jnp.int32)[None, :]).astype(jnp.int32)
    csum = jnp.cumsum(onehot, axis=0)
    pos_in = jnp.sum(csum * onehot, axis=1) - 1
    counts = csum[-1]
    padded = ((counts + tm - 1) // tm) * tm
    gend = jnp.cumsum(padded)
    gstart = gend - padded
    dest = (jnp.sum(onehot * gstart[None, :], axis=1) + pos_in).astype(jnp.int32)
    n_rows = A + N_EXPERTS * tm
    row_tok = jnp.zeros((n_rows,), jnp.int32).at[dest].set(jnp.arange(A, dtype=jnp.int32) // TOP_K)
    nb = n_rows // tm
    n_used = (gend[-1] // tm).astype(jnp.int32)
    blk = jnp.arange(nb, dtype=jnp.int32)
    be = jnp.sum((blk[:, None] * tm >= gend[None, :]).astype(jnp.int32), axis=1)
    be = jnp.minimum(be, N_EXPERTS - 1)
    be = jnp.where(blk < n_used, be, be[jnp.maximum(n_used - 1, 0)]).astype(jnp.int32)
    eid = jnp.arange(N_EXPERTS, dtype=jnp.int32)
    later = (counts[None, :] > 0) & (eid[None, :] > eid[:, None])
    nxt_of = jnp.min(jnp.where(later, eid[None, :], N_EXPERTS), axis=1)
    nxt_of = jnp.where(nxt_of == N_EXPERTS, -1, nxt_of).astype(jnp.int32)
    mine = be[:, None] == eid[None, :]
    nxt = jnp.sum(mine * nxt_of[None, :], axis=1).astype(jnp.int32)
    group_end_row = jnp.sum(mine * (gstart + counts)[None, :], axis=1)
    nv = jnp.where(blk < n_used, jnp.clip(group_end_row - blk * tm, 0, tm), 0).astype(jnp.int32)
    return dest, row_tok, be, n_used.reshape(1), nxt, nv


def _moe_gather_kernel(tok_ref, nu_ref, x_hbm, g_ref, o_ref, buf, sem, *, rows):
    i = pl.program_id(0)
    n_used = nu_ref[0]

    def row_copy(slot, r, src_row):
        return pltpu.make_async_copy(x_hbm.at[pl.ds(src_row, 1)], buf.at[slot, pl.ds(r, 1)], sem.at[slot])

    def fetch(block, slot):
        def issue(r, c):
            row_copy(slot, r, tok_ref[block * rows + r]).start()
            return c

        lax.fori_loop(0, rows, issue, 0, unroll=8)

    @pl.when(i == 0)
    def _():
        fetch(0, 0)

    @pl.when(i + 1 < n_used)
    def _():
        fetch(i + 1, (i + 1) % 2)

    @pl.when(i < n_used)
    def _():
        slot = i % 2

        def drain(r, c):
            row_copy(slot, r, 0).wait()
            return c

        lax.fori_loop(0, rows, drain, 0, unroll=8)
        x = buf[slot]
        o_ref[...] = (x * lax.rsqrt(jnp.mean(x * x, axis=-1, keepdims=True) + EPS) * g_ref[...]).astype(o_ref.dtype)

    @pl.when(i >= n_used)
    def _():
        o_ref[...] = jnp.zeros(o_ref.shape, o_ref.dtype)


def moe_gather_norm(x, g, row_tok, n_used, *, tm, rows=128):
    T, D = x.shape
    n_rows = row_tok.shape[0]
    per = tm // rows
    return pl.pallas_call(
        functools.partial(_moe_gather_kernel, rows=rows),
        grid_spec=pltpu.PrefetchScalarGridSpec(
            num_scalar_prefetch=2,
            grid=(n_rows // rows,),
            in_specs=[pl.BlockSpec(memory_space=pl.ANY), pl.BlockSpec((1, D), lambda i, tok, nu: (0, 0))],
            out_specs=pl.BlockSpec((rows, D), lambda i, tok, nu: (i, 0)),
            scratch_shapes=[pltpu.VMEM((2, rows, D), F32), pltpu.SemaphoreType.DMA((2,))]),
        out_shape=jax.ShapeDtypeStruct((n_rows, D), BF16),
        compiler_params=_params(("arbitrary",), 10 * _nbytes((rows, D), F32)),
        name="moe_gather",
    )(row_tok, n_used * per, x, g.reshape(1, D))


def _stream_group_weights(be_ref, nu_ref, nxt_ref, w_hbms, stage, sem, w_bf16, *, tile, n_tiles):
    j = pl.program_id(0)
    i = pl.program_id(1)

    def copies(e, jj):
        cols = pl.ds(pl.multiple_of(jj * tile, tile), tile)
        return [pltpu.make_async_copy(w.at[e, :, cols], stage.at[n], sem.at[n]) for n, w in enumerate(w_hbms)]

    @pl.when((j == 0) & (i == 0))
    def _():
        for c in copies(be_ref[0], 0):
            c.start()

    first_of_group = (i < nu_ref[0]) & ((i == 0) | (be_ref[i] != be_ref[jnp.maximum(i - 1, 0)]))

    @pl.when(first_of_group)
    def _():
        for c in copies(be_ref[i], j):
            c.wait()
        for n, dst in enumerate(w_bf16):
            dst[...] = stage[n].astype(BF16)
        more_in_sweep = nxt_ref[i] >= 0
        e_next = jnp.where(more_in_sweep, nxt_ref[i], be_ref[0])
        j_next = jnp.where(more_in_sweep, j, j + 1)

        @pl.when(more_in_sweep | (j + 1 < n_tiles))
        def _():
            for c in copies(e_next, j_next):
                c.start()


def _for_valid_rows(nv, o_ref, compute):
    tm = o_ref.shape[0]
    for rows in range(MOE_ROW_STEP, tm + 1, MOE_ROW_STEP):
        @pl.when((nv > rows - MOE_ROW_STEP) & (nv <= rows))
        def _(rows=rows):
            o_ref[:rows, :] = compute(rows).astype(o_ref.dtype)
            if rows < tm:
                o_ref[rows:, :] = jnp.zeros((tm - rows, o_ref.shape[1]), o_ref.dtype)

    @pl.when(nv == 0)
    def _():
        o_ref[...] = jnp.zeros(o_ref.shape, o_ref.dtype)


def _moe_gateup_kernel(be_ref, nu_ref, nxt_ref, nv_ref, x_ref, wg_hbm, wu_hbm, o_ref, stage, sem, wgb, wub,
                       *, tf, n_tiles):
    _stream_group_weights(be_ref, nu_ref, nxt_ref, (wg_hbm, wu_hbm), stage, sem, (wgb, wub), tile=tf, n_tiles=n_tiles)

    def compute(rows):
        x = x_ref[:rows, :]
        g = jnp.dot(x, wgb[...], preferred_element_type=F32)
        u = jnp.dot(x, wub[...], preferred_element_type=F32)
        return g * jax.nn.sigmoid(g) * u

    _for_valid_rows(nv_ref[pl.program_id(1)], o_ref, compute)


def moe_gateup(xs, wg, wu, be, n_used, nxt, nv, *, tm, tf=512):
    n_rows, K = xs.shape
    F = wg.shape[2]
    tf = min(tf, F)
    nb = n_rows // tm
    xmap = lambda j, i, be, nu, *_: (jnp.minimum(i, nu[0] - 1), 0)
    hbm = pl.BlockSpec(memory_space=pl.ANY)
    vmem = (2 * _nbytes((tm, K), BF16) + 2 * _nbytes((K, tf), F32) + 4 * _nbytes((K, tf), BF16)
            + 6 * _nbytes((tm, tf), F32))
    return pl.pallas_call(
        functools.partial(_moe_gateup_kernel, tf=tf, n_tiles=F // tf),
        grid_spec=pltpu.PrefetchScalarGridSpec(
            num_scalar_prefetch=4,
            grid=(F // tf, nb),
            in_specs=[pl.BlockSpec((tm, K), xmap), hbm, hbm],
            out_specs=pl.BlockSpec((tm, tf), lambda j, i, *_: (i, j)),
            scratch_shapes=[pltpu.VMEM((2, K, tf), F32), pltpu.SemaphoreType.DMA((2,)),
                            pltpu.VMEM((K, tf), BF16), pltpu.VMEM((K, tf), BF16)]),
        out_shape=jax.ShapeDtypeStruct((n_rows, F), BF16),
        compiler_params=_params(("arbitrary", "arbitrary"), vmem),
        name="moe_gateup",
    )(be, n_used, nxt, nv, xs, wg, wu)


def _moe_down_kernel(be_ref, nu_ref, nxt_ref, nv_ref, x_ref, w_hbm, o_ref, stage, sem, wb, *, tn, n_tiles):
    _stream_group_weights(be_ref, nu_ref, nxt_ref, (w_hbm,), stage, sem, (wb,), tile=tn, n_tiles=n_tiles)
    _for_valid_rows(nv_ref[pl.program_id(1)], o_ref,
                    lambda rows: jnp.dot(x_ref[:rows, :], wb[...], preferred_element_type=F32))


def moe_down(hm, wd, be, n_used, nxt, nv, *, tm, tn=1024):
    n_rows, K = hm.shape
    N = wd.shape[2]
    tn = min(tn, N)
    nb = n_rows // tm
    xmap = lambda j, i, be, nu, *_: (jnp.minimum(i, nu[0] - 1), 0)
    vmem = (2 * _nbytes((tm, K), BF16) + _nbytes((K, tn), F32) + 2 * _nbytes((K, tn), BF16)
            + 4 * _nbytes((tm, tn), F32))
    return pl.pallas_call(
        functools.partial(_moe_down_kernel, tn=tn, n_tiles=N // tn),
        grid_spec=pltpu.PrefetchScalarGridSpec(
            num_scalar_prefetch=4,
            grid=(N // tn, nb),
            in_specs=[pl.BlockSpec((tm, K), xmap), pl.BlockSpec(memory_space=pl.ANY)],
            out_specs=pl.BlockSpec((tm, tn), lambda j, i, *_: (i, j)),
            scratch_shapes=[pltpu.VMEM((1, K, tn), F32), pltpu.SemaphoreType.DMA((1,)),
                            pltpu.VMEM((K, tn), BF16)]),
        out_shape=jax.ShapeDtypeStruct((n_rows, N), F32),
        compiler_params=_params(("arbitrary", "arbitrary"), vmem),
        name="moe_down",
    )(be, n_used, nxt, nv, hm, wd)


def _moe_combine_kernel(pos_ref, x_ref, gw_ref, ys_hbm, o_ref, buf, sem, *, rows):
    i = pl.program_id(0)
    n_blocks = pl.num_programs(0)

    def row_copy(slot, k, r, src_row):
        return pltpu.make_async_copy(ys_hbm.at[pl.ds(src_row, 1)], buf.at[slot, k, pl.ds(r, 1)], sem.at[slot])

    def fetch(block, slot):
        def issue(r, c):
            a = (block * rows + r) * TOP_K
            for k in range(TOP_K):
                row_copy(slot, k, r, pos_ref[a + k]).start()
            return c

        lax.fori_loop(0, rows, issue, 0, unroll=4)

    @pl.when(i == 0)
    def _():
        fetch(0, 0)

    @pl.when(i + 1 < n_blocks)
    def _():
        fetch(i + 1, (i + 1) % 2)

    slot = i % 2

    def drain(r, c):
        for k in range(TOP_K):
            row_copy(slot, k, r, 0).wait()
        return c

    lax.fori_loop(0, rows, drain, 0, unroll=4)
    out = x_ref[...]
    for k in range(TOP_K):
        out = out + gw_ref[:, k:k + 1] * buf[slot, k]
    o_ref[...] = out


def moe_combine(x, ys, dest, gw, *, rows=128):
    T, D = x.shape
    return pl.pallas_call(
        functools.partial(_moe_combine_kernel, rows=rows),
        grid_spec=pltpu.PrefetchScalarGridSpec(
            num_scalar_prefetch=1,
            grid=(T // rows,),
            in_specs=[pl.BlockSpec((rows, D), lambda i, pos: (i, 0)),
                      pl.BlockSpec((rows, TOP_K), lambda i, pos: (i, 0)),
                      pl.BlockSpec(memory_space=pl.ANY)],
            out_specs=pl.BlockSpec((rows, D), lambda i, pos: (i, 0)),
            scratch_shapes=[pltpu.VMEM((2, TOP_K, rows, D), F32), pltpu.SemaphoreType.DMA((2,))]),
        out_shape=jax.ShapeDtypeStruct((T, D), F32),
        compiler_params=_params(("arbitrary",), 12 * _nbytes((rows, D), F32)),
        name="moe_combine",
    )(dest, x, gw, ys)


def moe_block(x, g, w_router, b_router, w_egate, w_eup, w_edown, *, tm=512):
    idx, gw = moe_router(x, g, w_router, b_router)
    dest, row_tok, be, n_used, nxt, nv = _moe_dispatch(idx, tm)
    xs = moe_gather_norm(x, g, row_tok, n_used, tm=tm)
    hm = moe_gateup(xs, w_egate, w_eup, be, n_used, nxt, nv, tm=tm)
    ys = moe_down(hm, w_edown, be, n_used, nxt, nv, tm=tm)
    return moe_combine(x, ys, dest, gw)


def even_mixer_block(x, tabs_a, tabs_b, w_in, ga_q, ga_k, g_cq, w_uq, g_ckv, w_ukv, gb_q, gb_kn, gb_kr,
                     w_o, g_mix, *, batch, seq):
    a_width = A_HEADS * HEAD_DIM
    main_cols = 3 * a_width + Q_LORA + KV_LORA
    h = rmsnorm(x, g_mix)
    w_in_t = jnp.swapaxes(w_in, 0, 1)
    proj = matmul(h, w_in_t, n_out=main_cols, tm=2048, single_buffer_x=True, w_transposed=True, name="in_proj")
    w_kr = jnp.pad(w_in_t[main_cols:, :], ((0, LANES - ROPE_DIM), (0, 0)))
    krp = matmul(h, w_kr, w_transposed=True, name="in_proj_kr")
    o_a = mixer_a(proj, tabs_a, ga_q, ga_k, batch=batch, seq=seq)
    w3 = w_uq.reshape(Q_LORA, B_HEADS, QK_DIM)
    w_uq_p = jnp.concatenate(
        [w3[:, :, :NOPE_DIM].reshape(Q_LORA, B_HEADS * NOPE_DIM),
         jnp.pad(w3[:, :, NOPE_DIM:], ((0, 0), (0, 0), (0, LANES - ROPE_DIM))).reshape(Q_LORA, B_HEADS * LANES)],
        axis=1)
    q_up = norm_matmul(proj, g_cq, w_uq_p, width=Q_LORA, col_block=3 * a_width // Q_LORA, name="q_up")
    kv_up = norm_matmul(proj, g_ckv, w_ukv, width=KV_LORA, col_block=(3 * a_width + Q_LORA) // KV_LORA,
                        name="kv_up")
    pad_r = lambda v: jnp.pad(v, (0, LANES - ROPE_DIM)).reshape(1, LANES)
    qt, kf, vb = mla_prep(q_up, kv_up, krp, tabs_b, gb_q[:NOPE_DIM].reshape(1, LANES), pad_r(gb_q[NOPE_DIM:]),
                          gb_kn.reshape(1, LANES), pad_r(gb_kr), batch=batch, seq=seq)
    o_b = mla_attention(qt, kf, vb, batch=batch, seq=seq)
    return matmul((o_a, o_b), w_o, res=x, name="mixer_out")


def kernel(x, mem, positions, g_mem, w_mem_kv, g_mem_k, g_mix, g_x, w_xq, g_xq, w_xo, g_ffn,
           w_in, ga_q, ga_k, g_cq, w_uq, g_ckv, w_ukv, gb_q, gb_kn, gb_kr, w_o_even,
           w_gate, w_up, w_down, w_qkv, w_o_odd, w_router, b_router, w_egate, w_eup, w_edown):
    B, S, D = x.shape
    M = mem.shape[1]
    depth = g_mix.shape[0]
    T = B * S
    xt = x.reshape(T, D)
    tabs_a = _rope_tables(positions, ROT_DIM)
    tabs_b = _rope_tables(positions, ROPE_DIM)
    kv = matmul(rmsnorm(mem.reshape(B * M, D), g_mem), w_mem_kv, tm=B * M, name="mem_kv")
    mk, mv = memkv_post(kv, g_mem_k)
    for layer in range(depth):
        i = layer // 2
        if layer % 2 == 0:
            xt = even_mixer_block(xt, tabs_a, tabs_b, w_in[i], ga_q[i], ga_k[i], g_cq[i], w_uq[i], g_ckv[i],
                                  w_ukv[i], gb_q[i], gb_kn[i], gb_kr[i], w_o_even[i], g_mix[layer],
                                  batch=B, seq=S)
        else:
            h = rmsnorm(xt, g_mix[layer])
            qkv = matmul(h, w_qkv[i], out_dtype=BF16, tm=2048, single_buffer_x=True, name="qkv_proj")
            o = stickbreak_attention(qkv, batch=B, seq=S, heads=D // HEAD_DIM)
            xt = matmul(o, w_o_odd[i], res=xt, name="mixer_out")
        xattn = functools.partial(cross_attention_block, xt, mk, mv, g_x[layer], w_xq[layer], g_xq[layer],
                                  w_xo[layer], seq=S, mem_len=M)
        if layer % 2 == 0:
            xt, h = xattn(g_ffn[layer])
            hm = swiglu_gateup(h, w_gate[i], w_up[i])
            xt = matmul(hm, w_down[i], res=xt, tn=1024, tk=2048, name="swiglu_down")
        else:
            xt = moe_block(xattn(), g_ffn[layer], w_router[i], b_router[i], w_egate[i], w_eup[i], w_edown[i])
    return xt.reshape(B, S, D)
```

```python
import functools

import jax
import jax.numpy as jnp
from jax import lax
from jax.experimental import pallas as pl
from jax.experimental.pallas import tpu as pltpu

F32 = jnp.float32
BF16 = jnp.bfloat16

HEAD_DIM = 128
ROT_DIM = HEAD_DIM // 4
ROPE_THETA = 500000.0
BLOCK = 128
NEG_INF = -1e30
EPS = 1e-6
DILATED_PAIRS = ((128, 1), (512, 4), (2048, 16))
A_HEADS = 16
B_HEADS = 16
Q_LORA = 1536
KV_LORA = 512
NOPE_DIM = 128
ROPE_DIM = 64
QK_DIM = NOPE_DIM + ROPE_DIM
X_HEADS = 4
N_EXPERTS = 8
TOP_K = 2
MIXER_A_GROUP = 8
MOE_ROW_STEP = 128
LOG2_E = 1.4426950408889634
MLA_Q_SCALE = QK_DIM ** -0.5 * LOG2_E

LANES = 128
V7X_VMEM_BYTES = 64 * 1024 * 1024
VMEM_CAP = V7X_VMEM_BYTES - 6 * 1024 * 1024


def _params(semantics, vmem_bytes):
    return pltpu.CompilerParams(dimension_semantics=semantics,
                                vmem_limit_bytes=int(min(max(vmem_bytes, 32 * 1024 * 1024), VMEM_CAP)))


def _nbytes(shape, dtype):
    n = 1
    for s in shape:
        n *= s
    return n * jnp.dtype(dtype).itemsize


def _rmsnorm_kernel(x_ref, g_ref, o_ref):
    x = x_ref[...]
    ms = jnp.mean(x * x, axis=-1, keepdims=True)
    o_ref[...] = (x * lax.rsqrt(ms + EPS) * g_ref[...]).astype(o_ref.dtype)


def rmsnorm(x, g, *, tm=256):
    T, width = x.shape
    tm = min(tm, T)
    return pl.pallas_call(
        _rmsnorm_kernel,
        grid=(T // tm,),
        in_specs=[pl.BlockSpec((tm, width), lambda i: (i, 0)),
                  pl.BlockSpec((1, width), lambda i: (0, 0))],
        out_specs=pl.BlockSpec((tm, width), lambda i: (i, 0)),
        out_shape=jax.ShapeDtypeStruct((T, width), BF16),
        compiler_params=_params(("parallel",), 6 * _nbytes((tm, width), F32)),
        name="rmsnorm",
    )(x, g.reshape(1, width))


def _mm_kernel(*refs, nk, n_x, has_res, w_transposed):
    x_refs = refs[:n_x]
    w_ref = refs[n_x]
    rest = refs[n_x + 1:]
    res_ref = rest[0] if has_res else None
    o_ref = rest[1] if has_res else rest[0]
    acc_ref = rest[-1] if nk > 1 else None
    k = pl.program_id(2)
    w = w_ref[...].astype(BF16)

    def finish(part):
        out = part
        if has_res:
            out = out + res_ref[...]
        o_ref[...] = out.astype(o_ref.dtype)

    if n_x == 2:
        half = x_refs[0].shape[1]
        finish(jnp.dot(x_refs[0][...], w[:half], preferred_element_type=F32)
               + jnp.dot(x_refs[1][...], w[half:], preferred_element_type=F32))
    elif nk == 1 and w_transposed:
        finish(lax.dot_general(x_refs[0][...], w, (((1,), (1,)), ((), ())), preferred_element_type=F32))
    elif nk == 1:
        finish(jnp.dot(x_refs[0][...], w, preferred_element_type=F32))
    else:
        part = jnp.dot(x_refs[0][...], w, preferred_element_type=F32)

        @pl.when(k == 0)
        def _():
            acc_ref[...] = part

        @pl.when((k > 0) & (k < nk - 1))
        def _():
            acc_ref[...] += part

        @pl.when(k == nk - 1)
        def _():
            finish(acc_ref[...] + part)


def matmul(xs, w, *, n_out=None, res=None, out_dtype=F32, tm=1024, tn=512, tk=None, single_buffer_x=False,
           w_transposed=False, name="matmul"):
    if not isinstance(xs, (tuple, list)):
        xs = (xs,)
    n_x = len(xs)
    T = xs[0].shape[0]
    K = sum(x.shape[1] for x in xs)
    n_out = (w.shape[0] if w_transposed else w.shape[1]) if n_out is None else n_out
    tm = min(tm, T)
    tn = min(tn, n_out)
    if n_x == 2:
        assert tk is None and xs[1].shape[1] == xs[0].shape[1]
    tk = K if tk is None else tk
    nk = K // tk
    assert K == nk * tk and T % tm == 0 and n_out % tn == 0
    x_buffers = 2
    if n_x == 2:
        x_specs = [pl.BlockSpec((tm, K // 2), lambda i, j, k: (i, 0)) for _ in xs]
    elif nk == 1 and single_buffer_x:
        x_buffers = 1
        x_specs = [pl.BlockSpec((tm, tk), lambda i, j, k: (i, k), pipeline_mode=pl.Buffered(1))]
    else:
        x_specs = [pl.BlockSpec((tm, tk), lambda i, j, k: (i, k))]
    if w_transposed:
        assert n_x == 1 and nk == 1
        in_specs = x_specs + [pl.BlockSpec((tn, tk), lambda i, j, k: (j, k))]
    else:
        in_specs = x_specs + [pl.BlockSpec((tk, tn), lambda i, j, k: (k, j))]
    args = list(xs) + [w]
    if res is not None:
        in_specs.append(pl.BlockSpec((tm, tn), lambda i, j, k: (i, j)))
        args.append(res)
    scratch = [pltpu.VMEM((tm, tn), F32)] if nk > 1 else []
    vmem = (x_buffers * _nbytes((tm, tk), xs[0].dtype) + 2 * _nbytes((tk, tn), F32) + _nbytes((tk, tn), BF16)
            + (2 * _nbytes((tm, tn), F32) if res is not None else 0)
            + 2 * _nbytes((tm, tn), out_dtype) + 2 * _nbytes((tm, tn), F32))
    return pl.pallas_call(
        functools.partial(_mm_kernel, nk=nk, n_x=n_x, has_res=res is not None, w_transposed=w_transposed),
        grid=(T // tm, n_out // tn, nk),
        in_specs=in_specs,
        out_specs=pl.BlockSpec((tm, tn), lambda i, j, k: (i, j)),
        out_shape=jax.ShapeDtypeStruct((T, n_out), out_dtype),
        scratch_shapes=scratch,
        compiler_params=_params(("parallel", "parallel", "arbitrary"), vmem),
        name=name,
    )(*args)


def _norm_mm_kernel(x_ref, g_ref, w_ref, o_ref, xn):
    @pl.when(pl.program_id(1) == 0)
    def _():
        x = x_ref[...]
        xn[...] = (x * lax.rsqrt(jnp.mean(x * x, axis=-1, keepdims=True) + EPS) * g_ref[...]).astype(xn.dtype)

    o_ref[...] = jnp.dot(xn[...], w_ref[...].astype(BF16), preferred_element_type=F32)


def norm_matmul(x, g, w, *, width, col_block, tm=1024, tn=1024, name="norm_matmul"):
    T = x.shape[0]
    N = w.shape[1]
    tm, tn = min(tm, T), min(tn, N)
    vmem = (2 * _nbytes((tm, width), F32) + _nbytes((tm, width), BF16) + 2 * _nbytes((width, tn), F32)
            + _nbytes((width, tn), BF16) + 4 * _nbytes((tm, tn), F32))
    return pl.pallas_call(
        _norm_mm_kernel,
        grid=(T // tm, N // tn),
        in_specs=[pl.BlockSpec((tm, width), lambda i, j: (i, col_block)),
                  pl.BlockSpec((1, width), lambda i, j: (0, 0)),
                  pl.BlockSpec((width, tn), lambda i, j: (0, j))],
        out_specs=pl.BlockSpec((tm, tn), lambda i, j: (i, j)),
        out_shape=jax.ShapeDtypeStruct((T, N), F32),
        scratch_shapes=[pltpu.VMEM((tm, width), BF16)],
        compiler_params=_params(("parallel", "arbitrary"), vmem),
        name=name,
    )(x, g.reshape(1, width), w)


def _rope_tables(positions, dim):
    half = dim // 2
    inv_freq = ROPE_THETA ** (-jnp.arange(0, dim, 2, dtype=F32) / dim)
    ang = positions.astype(F32)[..., None] * inv_freq
    cos, sin = jnp.cos(ang), jnp.sin(ang)
    B, S = positions.shape
    ones = jnp.ones((B, S, LANES - dim), F32)
    zeros = jnp.zeros((B, S, LANES - dim), F32)
    zh = jnp.zeros((B, S, half), F32)
    c = jnp.concatenate([cos, cos, ones], axis=-1)
    sa = jnp.concatenate([-sin, zh, zeros], axis=-1)
    sb = jnp.concatenate([zh, sin, zeros], axis=-1)
    return [t.reshape(B * S, LANES) for t in (c, sa, sb)]


def _rope_lanes(y, c, sa, sb, half):
    return y * c + pltpu.roll(y, LANES - half, 1) * sa + pltpu.roll(y, half, 1) * sb


def _mixer_a_kernel(q_ref, k_ref, v_ref, c_ref, sa_ref, sb_ref, gq_ref, gk_ref, o_ref,
                    qs, ks, *state, seq):
    scale = HEAD_DIM ** -0.5 * LOG2_E
    c, sa, sb = c_ref[...], sa_ref[...], sb_ref[...]

    def prep(x, g):
        y = x * lax.rsqrt(jnp.mean(x * x, axis=-1, keepdims=True) + EPS) * g
        return _rope_lanes(y, c, sa, sb, ROT_DIM // 2)

    qs[...] = prep(q_ref[...], gq_ref[...])
    ks[...] = prep(k_ref[...], gk_ref[...])
    qi = lax.broadcasted_iota(jnp.int32, (BLOCK, 1), 0)
    kj = lax.broadcasted_iota(jnp.int32, (1, BLOCK), 1)
    trans_b = (((1,), (1,)), ((), ()))
    ones = jnp.ones((BLOCK, HEAD_DIM), BF16)

    for bi, (window, dil) in enumerate(DILATED_PAIRS):
        acc, m_sc, l_sc = state[3 * bi:3 * bi + 3]
        n_back = window // dil
        nb = seq // dil // BLOCK
        assert n_back <= BLOCK and nb * dil * BLOCK == seq
        cur_ok = (qi >= kj) & (qi - kj <= n_back)
        prev_ok = qi + BLOCK - kj <= n_back

        def rows_at(r, n, dil=dil):
            if dil == 1:
                return pl.ds(BLOCK * n, BLOCK)
            return pl.ds(r + dil * BLOCK * n, BLOCK, stride=dil)

        blocks = [(rows_at(r, n), rows_at(r, n - 1) if n > 0 else None) for r in range(dil) for n in range(nb)]
        for first in range(0, len(blocks), MIXER_A_GROUP):
            group = blocks[first:first + MIXER_A_GROUP]
            s_cur, s_prev = [], []
            for rows, prev in group:
                qb = qs[rows, :].astype(BF16)
                s_c = lax.dot_general(qb, ks[rows, :].astype(BF16), trans_b, preferred_element_type=F32) * scale
                s_cur.append(jnp.where(cur_ok, s_c, NEG_INF))
                if prev is None:
                    s_prev.append(None)
                else:
                    s_p = lax.dot_general(qb, ks[prev, :].astype(BF16), trans_b,
                                          preferred_element_type=F32) * scale
                    s_prev.append(jnp.where(prev_ok, s_p, NEG_INF))
            stats = []
            for s_c, s_p in zip(s_cur, s_prev):
                m = jnp.max(s_c, axis=-1, keepdims=True)
                if s_p is not None:
                    m = jnp.maximum(m, jnp.max(s_p, axis=-1, keepdims=True))
                p_c = jnp.exp2(s_c - m).astype(BF16)
                p_p = None if s_p is None else jnp.exp2(s_p - m).astype(BF16)
                stats.append((m, p_c, p_p))
            for (rows, prev), (m, p_c, p_p) in zip(group, stats):
                v_ext = jnp.concatenate([v_ref[rows, :].astype(BF16), ones], axis=1)
                o = jnp.dot(p_c, v_ext, preferred_element_type=F32)
                if p_p is not None:
                    v_ext = jnp.concatenate([v_ref[prev, :].astype(BF16), ones], axis=1)
                    o = o + jnp.dot(p_p, v_ext, preferred_element_type=F32)
                acc[rows, :] = o[:, :HEAD_DIM]
                l_sc[rows, :] = o[:, HEAD_DIM:]
                m_sc[rows, :] = jnp.broadcast_to(m, (BLOCK, HEAD_DIM))

    n_br = len(DILATED_PAIRS)
    m_all = state[1][...]
    for bi in range(1, n_br):
        m_all = jnp.maximum(m_all, state[3 * bi + 1][...])
    num = jnp.zeros(o_ref.shape, F32)
    den = jnp.zeros(o_ref.shape, F32)
    for bi in range(n_br):
        w = jnp.exp2(state[3 * bi + 1][...] - m_all)
        num = num + w * state[3 * bi][...]
        den = den + w * state[3 * bi + 2][...]
    o_ref[...] = (num / den).astype(o_ref.dtype)


def mixer_a(proj, tabs, gq, gk, *, batch, seq):
    T = batch * seq
    blk = (seq, HEAD_DIM)
    head = lambda off: pl.BlockSpec(blk, lambda b, h: (b, off + h))
    tab = pl.BlockSpec(blk, lambda b, h: (b, 0))
    gspec = pl.BlockSpec((1, HEAD_DIM), lambda b, h: (0, 0))
    return pl.pallas_call(
        functools.partial(_mixer_a_kernel, seq=seq),
        grid=(batch, A_HEADS),
        in_specs=[head(0), head(A_HEADS), head(2 * A_HEADS), tab, tab, tab, gspec, gspec],
        out_specs=pl.BlockSpec(blk, lambda b, h: (b, h)),
        out_shape=jax.ShapeDtypeStruct((T, A_HEADS * HEAD_DIM), BF16),
        scratch_shapes=[pltpu.VMEM(blk, F32) for _ in range(2 + 3 * len(DILATED_PAIRS))],
        compiler_params=_params(("parallel", "parallel"), 30 * _nbytes(blk, F32)),
        name="mixer_a",
    )(proj, proj, proj, *tabs, gq.reshape(1, HEAD_DIM), gk.reshape(1, HEAD_DIM))


def _mla_prep_kernel(qn_ref, qr_ref, kn_ref, v_ref, kr_ref, c_ref, sa_ref, sb_ref,
                     gqn_ref, gqr_ref, gkn_ref, gkr_ref, qt_ref, kf_ref, vb_ref, kr_sc):
    c, sa, sb = c_ref[...], sa_ref[...], sb_ref[...]
    half = ROPE_DIM // 2

    @pl.when(pl.program_id(1) == 0)
    def _():
        kr = kr_ref[...]
        kr = kr * lax.rsqrt(jnp.sum(kr * kr, axis=-1, keepdims=True) * (1.0 / ROPE_DIM) + EPS) * gkr_ref[...]
        kr_sc[...] = _rope_lanes(kr, c, sa, sb, half).astype(kr_sc.dtype)

    qn, qr = qn_ref[...], qr_ref[...]
    ss = jnp.sum(qn * qn, axis=-1, keepdims=True) + jnp.sum(qr * qr, axis=-1, keepdims=True)
    inv = lax.rsqrt(ss * (1.0 / QK_DIM) + EPS) * MLA_Q_SCALE
    qf = jnp.concatenate([qn * inv * gqn_ref[...], _rope_lanes(qr * inv * gqr_ref[...], c, sa, sb, half)], axis=1)
    qt_ref[...] = qf.astype(qt_ref.dtype).T
    kn = kn_ref[...]
    kn = kn * lax.rsqrt(jnp.mean(kn * kn, axis=-1, keepdims=True) + EPS) * gkn_ref[...]
    kf_ref[:, :NOPE_DIM] = kn.astype(kf_ref.dtype)
    kf_ref[:, NOPE_DIM:] = kr_sc[...]
    vb_ref[...] = v_ref[...].astype(vb_ref.dtype)


def mla_prep(q_up, kv_up, krp, tabs, gqn, gqr, gkn, gkr, *, batch, seq, tm=1024):
    T = q_up.shape[0]
    H = B_HEADS
    per_seq = seq // tm
    blk = lambda f: pl.BlockSpec((tm, LANES), f)
    g = pl.BlockSpec((1, LANES), lambda i, h: (0, 0))
    return pl.pallas_call(
        _mla_prep_kernel,
        grid=(T // tm, H),
        in_specs=[blk(lambda i, h: (i, h)), blk(lambda i, h: (i, H + h)),
                  blk(lambda i, h: (i, 2 * h)), blk(lambda i, h: (i, 2 * h + 1)),
                  blk(lambda i, h: (i, 0)), blk(lambda i, h: (i, 0)), blk(lambda i, h: (i, 0)),
                  blk(lambda i, h: (i, 0)), g, g, g, g],
        out_specs=[pl.BlockSpec((None, None, 2 * LANES, tm), lambda i, h: (i // per_seq, h, 0, i % per_seq)),
                   pl.BlockSpec((tm, 2 * LANES), lambda i, h: (i, h)),
                   pl.BlockSpec((tm, LANES), lambda i, h: (i, h))],
        out_shape=[jax.ShapeDtypeStruct((batch, H, 2 * LANES, seq), BF16),
                   jax.ShapeDtypeStruct((T, H * 2 * LANES), BF16),
                   jax.ShapeDtypeStruct((T, H * LANES), BF16)],
        scratch_shapes=[pltpu.VMEM((tm, LANES), BF16)],
        compiler_params=_params(("parallel", "arbitrary"), 40 * _nbytes((tm, LANES), F32)),
        name="mla_prep",
    )(q_up, q_up, kv_up, kv_up, krp, *tabs, gqn, gqr, gkn, gkr)


def _causal_attn_kernel(qt_ref, k_ref, v_ref, o_ref, s_sc, acc, *, tq, hg, dq, dv):
    qi = pl.program_id(2)
    key = lax.broadcasted_iota(jnp.int32, (tq, tq), 0)
    qry = lax.broadcasted_iota(jnp.int32, (tq, tq), 1)
    visible = key <= qry
    ones = jnp.ones((tq, dv), BF16)
    contract_rows = (((0,), (0,)), ((), ()))

    heads = range(hg)

    def scores(j, diagonal):
        rows = pl.ds(pl.multiple_of(j * tq, tq), tq)
        s = [jnp.dot(k_ref[rows, g * dq:(g + 1) * dq], qt_ref[g], preferred_element_type=F32) for g in heads]
        if diagonal:
            s = [jnp.where(visible, sg, NEG_INF) for sg in s]
        for g in heads:
            s_sc[g, j] = s[g]
        return s

    def col_max(j, ms):
        s = scores(j, False)
        return tuple(jnp.maximum(ms[g], jnp.max(s[g], axis=0, keepdims=True)) for g in heads)

    ms = lax.fori_loop(0, qi, col_max, tuple(jnp.max(sg, axis=0, keepdims=True) for sg in scores(qi, True)))

    def weighted(j):
        rows = pl.ds(pl.multiple_of(j * tq, tq), tq)
        p = [jnp.exp2(s_sc[g, j] - ms[g]).astype(BF16) for g in heads]
        return [lax.dot_general(p[g], jnp.concatenate([v_ref[rows, g * dv:(g + 1) * dv], ones], axis=1),
                                contract_rows, preferred_element_type=F32) for g in heads]

    for g, w in enumerate(weighted(qi)):
        acc[g] = w

    def below(j, carry):
        for g, w in enumerate(weighted(j)):
            acc[g] += w
        return carry

    lax.fori_loop(0, qi, below, 0)
    for g in heads:
        a = acc[g]
        o_ref[:, g * dv:(g + 1) * dv] = (a[:, :dv] / a[:, dv:]).astype(o_ref.dtype)


def mla_attention(qt, kf, vb, *, batch, seq, tq=512, hg=4):
    T = batch * seq
    nq = seq // tq
    dq = qt.shape[2]
    dv = vb.shape[1] // B_HEADS
    assert B_HEADS % hg == 0
    return pl.pallas_call(
        functools.partial(_causal_attn_kernel, tq=tq, hg=hg, dq=dq, dv=dv),
        grid=(batch, B_HEADS // hg, nq),
        in_specs=[pl.BlockSpec((None, hg, dq, tq), lambda b, h, i: (b, h, 0, i)),
                  pl.BlockSpec((seq, hg * dq), lambda b, h, i: (b, h)),
                  pl.BlockSpec((seq, hg * dv), lambda b, h, i: (b, h))],
        out_specs=pl.BlockSpec((tq, hg * dv), lambda b, h, i: (b * nq + i, h)),
        out_shape=jax.ShapeDtypeStruct((T, B_HEADS * dv), BF16),
        scratch_shapes=[pltpu.VMEM((hg, nq, tq, tq), F32), pltpu.VMEM((hg, tq, 2 * dv), F32)],
        compiler_params=_params(("parallel", "parallel", "parallel"),
                                _nbytes((hg, nq, tq, tq), F32) + 2 * _nbytes((seq, hg * (dq + dv)), BF16)
                                + 8 * hg * _nbytes((tq, tq), F32)),
        name="mla_attention",
    )(qt, kf, vb)


STICK_EXHAUSTED_LOG = -110.0


def _stickbreak_kernel(q_ref, k_ref, v_ref, o_ref, qt_sc, c_sc, acc, *, tq, hg, scale):
    qi = pl.program_id(2)
    c_sc[...] = jnp.zeros(c_sc.shape, F32)
    acc[...] = jnp.zeros(acc.shape, F32)
    for g in range(hg):
        qt_sc[g] = q_ref[:, g * HEAD_DIM:(g + 1) * HEAD_DIM].T
    key = lax.broadcasted_iota(jnp.int32, (tq, tq), 0)
    qry = lax.broadcasted_iota(jnp.int32, (tq, tq), 1)
    after = (qry > key).astype(BF16)
    earlier = key < qry
    contract_rows = (((0,), (0,)), ((), ()))

    def chunk(j, diagonal):
        rows = pl.ds(pl.multiple_of(j * tq, tq), tq)
        heads = range(hg)
        cols = [slice(g * HEAD_DIM, (g + 1) * HEAD_DIM) for g in heads]
        z = [jnp.dot(k_ref[rows, cols[g]], qt_sc[g], preferred_element_type=F32) * (scale * LOG2_E) for g in heads]
        log_beta, parts = [], []
        for g in heads:
            softplus = jnp.maximum(z[g], 0.0) + jnp.log2(1.0 + jnp.exp2(-jnp.abs(z[g])))
            log_keep = -softplus
            if diagonal:
                log_keep = jnp.where(earlier, log_keep, 0.0)
            parts.append(log_keep.astype(BF16))
            log_beta.append(z[g] - softplus + c_sc[g])
            c_sc[g] += jnp.sum(log_keep, axis=0, keepdims=True)
        within = jnp.dot(after, jnp.concatenate(parts, axis=1), preferred_element_type=F32)
        for g in heads:
            a = jnp.exp2(log_beta[g] + within[:, g * tq:(g + 1) * tq])
            if diagonal:
                a = jnp.where(earlier, a, 0.0)
            acc[:, cols[g]] += lax.dot_general(a.astype(BF16), v_ref[rows, cols[g]], contract_rows,
                                               preferred_element_type=F32)

    def stick_left():
        return jnp.max(c_sc[...]) > STICK_EXHAUSTED_LOG * LOG2_E

    chunk(qi, True)

    def more(state):
        j, left = state
        return (j >= 0) & left

    def step(state):
        j, _ = state
        chunk(j, False)
        return j - 1, stick_left()

    lax.while_loop(more, step, (qi - 1, stick_left()))
    o_ref[...] = acc[...].astype(o_ref.dtype)


def stickbreak_attention(qkv, *, batch, seq, heads, tq=256, hg=8):
    T = batch * seq
    nq = seq // tq
    ng = heads // hg
    assert heads == ng * hg
    blk = hg * HEAD_DIM
    return pl.pallas_call(
        functools.partial(_stickbreak_kernel, tq=tq, hg=hg, scale=HEAD_DIM ** -0.5),
        grid=(batch, ng, nq),
        in_specs=[pl.BlockSpec((tq, blk), lambda b, h, i: (b * nq + i, h)),
                  pl.BlockSpec((seq, blk), lambda b, h, i: (b, ng + h)),
                  pl.BlockSpec((seq, blk), lambda b, h, i: (b, 2 * ng + h))],
        out_specs=pl.BlockSpec((tq, blk), lambda b, h, i: (b * nq + i, h)),
        out_shape=jax.ShapeDtypeStruct((T, heads * HEAD_DIM), BF16),
        scratch_shapes=[pltpu.VMEM((hg, HEAD_DIM, tq), BF16), pltpu.VMEM((hg, 1, tq), F32),
                        pltpu.VMEM((tq, blk), F32)],
        compiler_params=_params(("parallel", "parallel", "parallel"), 0),
        name="stickbreak",
    )(qkv, qkv, qkv)


def _memkv_kernel(kv_ref, g_ref, mk_ref, mv_ref):
    for h in range(X_HEADS):
        cols = slice(h * HEAD_DIM, (h + 1) * HEAD_DIM)
        k = kv_ref[:, cols]
        k = k * lax.rsqrt(jnp.mean(k * k, axis=-1, keepdims=True) + EPS) * g_ref[...]
        mk_ref[:, cols] = k.astype(mk_ref.dtype)
    mv_ref[...] = kv_ref[:, X_HEADS * HEAD_DIM:].astype(mv_ref.dtype)


def memkv_post(kv, g):
    M = kv.shape[0]
    xd = X_HEADS * HEAD_DIM
    return pl.pallas_call(
        _memkv_kernel,
        grid=(1,),
        in_specs=[pl.BlockSpec((M, 2 * xd), lambda i: (0, 0)), pl.BlockSpec((1, HEAD_DIM), lambda i: (0, 0))],
        out_specs=[pl.BlockSpec((M, xd), lambda i: (0, 0)), pl.BlockSpec((M, xd), lambda i: (0, 0))],
        out_shape=[jax.ShapeDtypeStruct((M, xd), BF16), jax.ShapeDtypeStruct((M, xd), BF16)],
        compiler_params=_params(("arbitrary",), 0),
        name="memkv_post",
    )(kv, g.reshape(1, HEAD_DIM))


def _xattn_block_kernel(x_ref, gx_ref, wq_ref, gq_ref, mk_ref, mv_ref, wo_ref, *rest, emit_norm):
    if emit_norm:
        gn_ref, o_ref, hn_ref = rest
    else:
        (o_ref,) = rest
    scale = HEAD_DIM ** -0.5
    x = x_ref[...]
    h = (x * lax.rsqrt(jnp.mean(x * x, axis=-1, keepdims=True) + EPS) * gx_ref[...]).astype(BF16)
    q_all = jnp.dot(h, wq_ref[...], preferred_element_type=F32)
    heads = []
    for hd in range(X_HEADS):
        cols = slice(hd * HEAD_DIM, (hd + 1) * HEAD_DIM)
        q = q_all[:, cols]
        q = (q * lax.rsqrt(jnp.mean(q * q, axis=-1, keepdims=True) + EPS) * gq_ref[...]).astype(BF16)
        s = lax.dot_general(q, mk_ref[:, cols], (((1,), (1,)), ((), ())), preferred_element_type=F32) * scale
        p = jnp.exp(s - jnp.max(s, axis=-1, keepdims=True))
        p = p / jnp.sum(p, axis=-1, keepdims=True)
        heads.append(jnp.dot(p.astype(BF16), mv_ref[:, cols], preferred_element_type=F32).astype(BF16))
    y = x + jnp.dot(jnp.concatenate(heads, axis=1), wo_ref[...], preferred_element_type=F32)
    o_ref[...] = y
    if emit_norm:
        hn_ref[...] = (y * lax.rsqrt(jnp.mean(y * y, axis=-1, keepdims=True) + EPS) * gn_ref[...]).astype(hn_ref.dtype)


def cross_attention_block(x, mk, mv, g_x, w_xq, g_xq, w_xo, g_next=None, *, seq, mem_len, tm=256):
    T, D = x.shape
    xd = X_HEADS * HEAD_DIM
    per_seq = seq // tm
    emit_norm = g_next is not None
    row = lambda w: pl.BlockSpec((1, w), lambda i: (0, 0))
    tok = pl.BlockSpec((tm, D), lambda i: (i, 0))
    in_specs = [tok, row(D), pl.BlockSpec((D, xd), lambda i: (0, 0)), row(HEAD_DIM),
                pl.BlockSpec((mem_len, xd), lambda i: (i // per_seq, 0)),
                pl.BlockSpec((mem_len, xd), lambda i: (i // per_seq, 0)),
                pl.BlockSpec((xd, D), lambda i: (0, 0))]
    args = [x, g_x.reshape(1, D), w_xq.astype(BF16), g_xq.reshape(1, HEAD_DIM), mk, mv, w_xo.astype(BF16)]
    out_specs, out_shape = tok, jax.ShapeDtypeStruct((T, D), F32)
    if emit_norm:
        in_specs.append(row(D))
        args.append(g_next.reshape(1, D))
        out_specs = [tok, tok]
        out_shape = [out_shape, jax.ShapeDtypeStruct((T, D), BF16)]
    vmem = 10 * _nbytes((tm, D), F32) + 4 * _nbytes((D, xd), BF16)
    return pl.pallas_call(
        functools.partial(_xattn_block_kernel, emit_norm=emit_norm),
        grid=(T // tm,),
        in_specs=in_specs,
        out_specs=out_specs,
        out_shape=out_shape,
        compiler_params=_params(("parallel",), vmem),
        name="cross_attention",
    )(*args)


def _gateup_kernel(x_ref, wg_ref, wu_ref, o_ref):
    x = x_ref[...]
    g = jnp.dot(x, wg_ref[...].astype(BF16), preferred_element_type=F32)
    u = jnp.dot(x, wu_ref[...].astype(BF16), preferred_element_type=F32)
    o_ref[...] = (g * jax.nn.sigmoid(g) * u).astype(o_ref.dtype)


def swiglu_gateup(x, wg, wu, *, tm=2048, tf=256):
    T, K = x.shape
    F = wg.shape[1]
    vmem = (_nbytes((tm, K), BF16) + 4 * _nbytes((K, tf), F32) + 2 * _nbytes((K, tf), BF16)
            + 6 * _nbytes((tm, tf), F32))
    return pl.pallas_call(
        _gateup_kernel,
        grid=(T // tm, F // tf),
        in_specs=[pl.BlockSpec((tm, K), lambda i, j: (i, 0), pipeline_mode=pl.Buffered(1)),
                  pl.BlockSpec((K, tf), lambda i, j: (0, j)),
                  pl.BlockSpec((K, tf), lambda i, j: (0, j))],
        out_specs=pl.BlockSpec((tm, tf), lambda i, j: (i, j)),
        out_shape=jax.ShapeDtypeStruct((T, F), BF16),
        compiler_params=_params(("parallel", "parallel"), vmem),
        name="swiglu_gateup",
    )(x, wg, wu)


def _router_kernel(x_ref, g_ref, w_ref, b_ref, idx_ref, gw_ref):
    x = x_ref[...]
    h = x * lax.rsqrt(jnp.mean(x * x, axis=-1, keepdims=True) + EPS) * g_ref[...]
    logits = jnp.dot(h, w_ref[...], preferred_element_type=F32, precision=lax.Precision.HIGHEST) + b_ref[...]
    lane = lax.broadcasted_iota(jnp.int32, logits.shape, 1)
    logits = jnp.where(lane < N_EXPERTS, logits, -jnp.inf)
    m1 = jnp.max(logits, axis=-1, keepdims=True)
    i1 = jnp.min(jnp.where(logits == m1, lane, LANES), axis=-1, keepdims=True)
    rest = jnp.where(lane == i1, -jnp.inf, logits)
    m2 = jnp.max(rest, axis=-1, keepdims=True)
    i2 = jnp.min(jnp.where(rest == m2, lane, LANES), axis=-1, keepdims=True)
    e = jnp.exp(m2 - m1)
    w1 = 1.0 / (1.0 + e)
    w2 = e / (1.0 + e)
    idx_ref[...] = jnp.where(lane == 0, i1, jnp.where(lane == 1, i2, 0))
    gw_ref[...] = jnp.where(lane == 0, w1, jnp.where(lane == 1, w2, 0.0))


def moe_router(x, g, w_router, b_router, *, tm=256):
    T, D = x.shape
    E = w_router.shape[1]
    w = jnp.pad(w_router, ((0, 0), (0, LANES - E)))
    b = jnp.pad(b_router, (0, LANES - E)).reshape(1, LANES)
    idx, gw = pl.pallas_call(
        _router_kernel,
        grid=(T // tm,),
        in_specs=[pl.BlockSpec((tm, D), lambda i: (i, 0)), pl.BlockSpec((1, D), lambda i: (0, 0)),
                  pl.BlockSpec((D, LANES), lambda i: (0, 0)), pl.BlockSpec((1, LANES), lambda i: (0, 0))],
        out_specs=[pl.BlockSpec((tm, LANES), lambda i: (i, 0)), pl.BlockSpec((tm, LANES), lambda i: (i, 0))],
        out_shape=[jax.ShapeDtypeStruct((T, LANES), jnp.int32), jax.ShapeDtypeStruct((T, LANES), F32)],
        compiler_params=_params(("parallel",), 8 * _nbytes((tm, D), F32)),
        name="moe_router",
    )(x, g.reshape(1, D), w, b)
    return idx[:, :TOP_K], gw[:, :TOP_K]


def _moe_dispatch(idx, tm):
    T = idx.shape[0]
    A = T * TOP_K
    e_flat = idx.reshape(A)
    onehot = (e_flat[:, None] == jnp.arange(N_EXPERTS, dtype=jnp.int32)[None, :]).astype(jnp.int32)
    csum = jnp.cumsum(onehot, axis=0)
    pos_in = jnp.sum(csum * onehot, axis=1) - 1
    counts = csum[-1]
    padded = ((counts + tm - 1) // tm) * tm
    gend = jnp.cumsum(padded)
    gstart = gend - padded
    dest = (jnp.sum(onehot * gstart[None, :], axis=1) + pos_in).astype(jnp.int32)
    n_rows = A + N_EXPERTS * tm
    row_tok = jnp.zeros((n_rows,), jnp.int32).at[dest].set(jnp.arange(A, dtype=jnp.int32) // TOP_K)
    nb = n_rows // tm
    n_used = (gend[-1] // tm).astype(jnp.int32)
    blk = jnp.arange(nb, dtype=jnp.int32)
    be = jnp.sum((blk[:, None] * tm >= gend[None, :]).astype(jnp.int32), axis=1)
    be = jnp.minimum(be, N_EXPERTS - 1)
    be = jnp.where(blk < n_used, be, be[jnp.maximum(n_used - 1, 0)]).astype(jnp.int32)
    eid = jnp.arange(N_EXPERTS, dtype=jnp.int32)
    later = (counts[None, :] > 0) & (eid[None, :] > eid[:, None])
    nxt_of = jnp.min(jnp.where(later, eid[None, :], N_EXPERTS), axis=1)
    nxt_of = jnp.where(nxt_of == N_EXPERTS, -1, nxt_of).astype(jnp.int32)
    mine = be[:, None] == eid[None, :]
    nxt = jnp.sum(mine * nxt_of[None, :], axis=1).astype(jnp.int32)
    group_end_row = jnp.sum(mine * (gstart + counts)[None, :], axis=1)
    nv = jnp.where(blk < n_used, jnp.clip(group_end_row - blk * tm, 0, tm), 0).astype(jnp.int32)
    return dest, row_tok, be, n_used.reshape(1), nxt, nv


def _moe_gather_kernel(tok_ref, nu_ref, x_hbm, g_ref, o_ref, buf, sem, *, rows):
    i = pl.program_id(0)
    n_used = nu_ref[0]

    def row_copy(slot, r, src_row):
        return pltpu.make_async_copy(x_hbm.at[pl.ds(src_row, 1)], buf.at[slot, pl.ds(r, 1)], sem.at[slot])

    def fetch(block, slot):
        def issue(r, c):
            row_copy(slot, r, tok_ref[block * rows + r]).start()
            return c

        lax.fori_loop(0, rows, issue, 0, unroll=8)

    @pl.when(i == 0)
    def _():
        fetch(0, 0)

    @pl.when(i + 1 < n_used)
    def _():
        fetch(i + 1, (i + 1) % 2)

    @pl.when(i < n_used)
    def _():
        slot = i % 2

        def drain(r, c):
            row_copy(slot, r, 0).wait()
            return c

        lax.fori_loop(0, rows, drain, 0, unroll=8)
        x = buf[slot]
        o_ref[...] = (x * lax.rsqrt(jnp.mean(x * x, axis=-1, keepdims=True) + EPS) * g_ref[...]).astype(o_ref.dtype)

    @pl.when(i >= n_used)
    def _():
        o_ref[...] = jnp.zeros(o_ref.shape, o_ref.dtype)


def moe_gather_norm(x, g, row_tok, n_used, *, tm, rows=128):
    T, D = x.shape
    n_rows = row_tok.shape[0]
    per = tm // rows
    return pl.pallas_call(
        functools.partial(_moe_gather_kernel, rows=rows),
        grid_spec=pltpu.PrefetchScalarGridSpec(
            num_scalar_prefetch=2,
            grid=(n_rows // rows,),
            in_specs=[pl.BlockSpec(memory_space=pl.ANY), pl.BlockSpec((1, D), lambda i, tok, nu: (0, 0))],
            out_specs=pl.BlockSpec((rows, D), lambda i, tok, nu: (i, 0)),
            scratch_shapes=[pltpu.VMEM((2, rows, D), F32), pltpu.SemaphoreType.DMA((2,))]),
        out_shape=jax.ShapeDtypeStruct((n_rows, D), BF16),
        compiler_params=_params(("arbitrary",), 10 * _nbytes((rows, D), F32)),
        name="moe_gather",
    )(row_tok, n_used * per, x, g.reshape(1, D))


def _stream_group_weights(be_ref, nu_ref, nxt_ref, w_hbms, stage, sem, w_bf16, *, tile, n_tiles):
    j = pl.program_id(0)
    i = pl.program_id(1)

    def copies(e, jj):
        cols = pl.ds(pl.multiple_of(jj * tile, tile), tile)
        return [pltpu.make_async_copy(w.at[e, :, cols], stage.at[n], sem.at[n]) for n, w in enumerate(w_hbms)]

    @pl.when((j == 0) & (i == 0))
    def _():
        for c in copies(be_ref[0], 0):
            c.start()

    first_of_group = (i < nu_ref[0]) & ((i == 0) | (be_ref[i] != be_ref[jnp.maximum(i - 1, 0)]))

    @pl.when(first_of_group)
    def _():
        for c in copies(be_ref[i], j):
            c.wait()
        for n, dst in enumerate(w_bf16):
            dst[...] = stage[n].astype(BF16)
        more_in_sweep = nxt_ref[i] >= 0
        e_next = jnp.where(more_in_sweep, nxt_ref[i], be_ref[0])
        j_next = jnp.where(more_in_sweep, j, j + 1)

        @pl.when(more_in_sweep | (j + 1 < n_tiles))
        def _():
            for c in copies(e_next, j_next):
                c.start()


def _for_valid_rows(nv, o_ref, compute):
    tm = o_ref.shape[0]
    for rows in range(MOE_ROW_STEP, tm + 1, MOE_ROW_STEP):
        @pl.when((nv > rows - MOE_ROW_STEP) & (nv <= rows))
        def _(rows=rows):
            o_ref[:rows, :] = compute(rows).astype(o_ref.dtype)
            if rows < tm:
                o_ref[rows:, :] = jnp.zeros((tm - rows, o_ref.shape[1]), o_ref.dtype)

    @pl.when(nv == 0)
    def _():
        o_ref[...] = jnp.zeros(o_ref.shape, o_ref.dtype)


def _moe_gateup_kernel(be_ref, nu_ref, nxt_ref, nv_ref, x_ref, wg_hbm, wu_hbm, o_ref, stage, sem, wgb, wub,
                       *, tf, n_tiles):
    _stream_group_weights(be_ref, nu_ref, nxt_ref, (wg_hbm, wu_hbm), stage, sem, (wgb, wub), tile=tf, n_tiles=n_tiles)

    def compute(rows):
        x = x_ref[:rows, :]
        g = jnp.dot(x, wgb[...], preferred_element_type=F32)
        u = jnp.dot(x, wub[...], preferred_element_type=F32)
        return g * jax.nn.sigmoid(g) * u

    _for_valid_rows(nv_ref[pl.program_id(1)], o_ref, compute)


def moe_gateup(xs, wg, wu, be, n_used, nxt, nv, *, tm, tf=512):
    n_rows, K = xs.shape
    F = wg.shape[2]
    tf = min(tf, F)
    nb = n_rows // tm
    xmap = lambda j, i, be, nu, *_: (jnp.minimum(i, nu[0] - 1), 0)
    hbm = pl.BlockSpec(memory_space=pl.ANY)
    vmem = (2 * _nbytes((tm, K), BF16) + 2 * _nbytes((K, tf), F32) + 4 * _nbytes((K, tf), BF16)
            + 6 * _nbytes((tm, tf), F32))
    return pl.pallas_call(
        functools.partial(_moe_gateup_kernel, tf=tf, n_tiles=F // tf),
        grid_spec=pltpu.PrefetchScalarGridSpec(
            num_scalar_prefetch=4,
            grid=(F // tf, nb),
            in_specs=[pl.BlockSpec((tm, K), xmap), hbm, hbm],
            out_specs=pl.BlockSpec((tm, tf), lambda j, i, *_: (i, j)),
            scratch_shapes=[pltpu.VMEM((2, K, tf), F32), pltpu.SemaphoreType.DMA((2,)),
                            pltpu.VMEM((K, tf), BF16), pltpu.VMEM((K, tf), BF16)]),
        out_shape=jax.ShapeDtypeStruct((n_rows, F), BF16),
        compiler_params=_params(("arbitrary", "arbitrary"), vmem),
        name="moe_gateup",
    )(be, n_used, nxt, nv, xs, wg, wu)


def _moe_down_kernel(be_ref, nu_ref, nxt_ref, nv_ref, x_ref, w_hbm, o_ref, stage, sem, wb, *, tn, n_tiles):
    _stream_group_weights(be_ref, nu_ref, nxt_ref, (w_hbm,), stage, sem, (wb,), tile=tn, n_tiles=n_tiles)
    _for_valid_rows(nv_ref[pl.program_id(1)], o_ref,
                    lambda rows: jnp.dot(x_ref[:rows, :], wb[...], preferred_element_type=F32))


def moe_down(hm, wd, be, n_used, nxt, nv, *, tm, tn=1024):
    n_rows, K = hm.shape
    N = wd.shape[2]
    tn = min(tn, N)
    nb = n_rows // tm
    xmap = lambda j, i, be, nu, *_: (jnp.minimum(i, nu[0] - 1), 0)
    vmem = (2 * _nbytes((tm, K), BF16) + _nbytes((K, tn), F32) + 2 * _nbytes((K, tn), BF16)
            + 4 * _nbytes((tm, tn), F32))
    return pl.pallas_call(
        functools.partial(_moe_down_kernel, tn=tn, n_tiles=N // tn),
        grid_spec=pltpu.PrefetchScalarGridSpec(
            num_scalar_prefetch=4,
            grid=(N // tn, nb),
            in_specs=[pl.BlockSpec((tm, K), xmap), pl.BlockSpec(memory_space=pl.ANY)],
            out_specs=pl.BlockSpec((tm, tn), lambda j, i, *_: (i, j)),
            scratch_shapes=[pltpu.VMEM((1, K, tn), F32), pltpu.SemaphoreType.DMA((1,)),
                            pltpu.VMEM((K, tn), BF16)]),
        out_shape=jax.ShapeDtypeStruct((n_rows, N), F32),
        compiler_params=_params(("arbitrary", "arbitrary"), vmem),
        name="moe_down",
    )(be, n_used, nxt, nv, hm, wd)


def _moe_combine_kernel(pos_ref, x_ref, gw_ref, ys_hbm, o_ref, buf, sem, *, rows):
    i = pl.program_id(0)
    n_blocks = pl.num_programs(0)

    def row_copy(slot, k, r, src_row):
        return pltpu.make_async_copy(ys_hbm.at[pl.ds(src_row, 1)], buf.at[slot, k, pl.ds(r, 1)], sem.at[slot])

    def fetch(block, slot):
        def issue(r, c):
            a = (block * rows + r) * TOP_K
            for k in range(TOP_K):
                row_copy(slot, k, r, pos_ref[a + k]).start()
            return c

        lax.fori_loop(0, rows, issue, 0, unroll=4)

    @pl.when(i == 0)
    def _():
        fetch(0, 0)

    @pl.when(i + 1 < n_blocks)
    def _():
        fetch(i + 1, (i + 1) % 2)

    slot = i % 2

    def drain(r, c):
        for k in range(TOP_K):
            row_copy(slot, k, r, 0).wait()
        return c

    lax.fori_loop(0, rows, drain, 0, unroll=4)
    out = x_ref[...]
    for k in range(TOP_K):
        out = out + gw_ref[:, k:k + 1] * buf[slot, k]
    o_ref[...] = out


def moe_combine(x, ys, dest, gw, *, rows=128):
    T, D = x.shape
    return pl.pallas_call(
        functools.partial(_moe_combine_kernel, rows=rows),
        grid_spec=pltpu.PrefetchScalarGridSpec(
            num_scalar_prefetch=1,
            grid=(T // rows,),
            in_specs=[pl.BlockSpec((rows, D), lambda i, pos: (i, 0)),
                      pl.BlockSpec((rows, TOP_K), lambda i, pos: (i, 0)),
                      pl.BlockSpec(memory_space=pl.ANY)],
            out_specs=pl.BlockSpec((rows, D), lambda i, pos: (i, 0)),
            scratch_shapes=[pltpu.VMEM((2, TOP_K, rows, D), F32), pltpu.SemaphoreType.DMA((2,))]),
        out_shape=jax.ShapeDtypeStruct((T, D), F32),
        compiler_params=_params(("arbitrary",), 12 * _nbytes((rows, D), F32)),
        name="moe_combine",
    )(dest, x, gw, ys)


def moe_block(x, g, w_router, b_router, w_egate, w_eup, w_edown, *, tm=512):
    idx, gw = moe_router(x, g, w_router, b_router)
    dest, row_tok, be, n_used, nxt, nv = _moe_dispatch(idx, tm)
    xs = moe_gather_norm(x, g, row_tok, n_used, tm=tm)
    hm = moe_gateup(xs, w_egate, w_eup, be, n_used, nxt, nv, tm=tm)
    ys = moe_down(hm, w_edown, be, n_used, nxt, nv, tm=tm)
    return moe_combine(x, ys, dest, gw)


def even_mixer_block(x, tabs_a, tabs_b, w_in, ga_q, ga_k, g_cq, w_uq, g_ckv, w_ukv, gb_q, gb_kn, gb_kr,
                     w_o, g_mix, *, batch, seq):
    a_width = A_HEADS * HEAD_DIM
    main_cols = 3 * a_width + Q_LORA + KV_LORA
    h = rmsnorm(x, g_mix)
    w_in_t = jnp.swapaxes(w_in, 0, 1)
    proj = matmul(h, w_in_t, n_out=main_cols, tm=2048, single_buffer_x=True, w_transposed=True, name="in_proj")
    w_kr = jnp.pad(w_in_t[main_cols:, :], ((0, LANES - ROPE_DIM), (0, 0)))
    krp = matmul(h, w_kr, w_transposed=True, name="in_proj_kr")
    o_a = mixer_a(proj, tabs_a, ga_q, ga_k, batch=batch, seq=seq)
    w3 = w_uq.reshape(Q_LORA, B_HEADS, QK_DIM)
    w_uq_p = jnp.concatenate(
        [w3[:, :, :NOPE_DIM].reshape(Q_LORA, B_HEADS * NOPE_DIM),
         jnp.pad(w3[:, :, NOPE_DIM:], ((0, 0), (0, 0), (0, LANES - ROPE_DIM))).reshape(Q_LORA, B_HEADS * LANES)],
        axis=1)
    q_up = norm_matmul(proj, g_cq, w_uq_p, width=Q_LORA, col_block=3 * a_width // Q_LORA, name="q_up")
    kv_up = norm_matmul(proj, g_ckv, w_ukv, width=KV_LORA, col_block=(3 * a_width + Q_LORA) // KV_LORA,
                        name="kv_up")
    pad_r = lambda v: jnp.pad(v, (0, LANES - ROPE_DIM)).reshape(1, LANES)
    qt, kf, vb = mla_prep(q_up, kv_up, krp, tabs_b, gb_q[:NOPE_DIM].reshape(1, LANES), pad_r(gb_q[NOPE_DIM:]),
                          gb_kn.reshape(1, LANES), pad_r(gb_kr), batch=batch, seq=seq)
    o_b = mla_attention(qt, kf, vb, batch=batch, seq=seq)
    return matmul((o_a, o_b), w_o, res=x, name="mixer_out")


def kernel(x, mem, positions, g_mem, w_mem_kv, g_mem_k, g_mix, g_x, w_xq, g_xq, w_xo, g_ffn,
           w_in, ga_q, ga_k, g_cq, w_uq, g_ckv, w_ukv, gb_q, gb_kn, gb_kr, w_o_even,
           w_gate, w_up, w_down, w_qkv, w_o_odd, w_router, b_router, w_egate, w_eup, w_edown):
    B, S, D = x.shape
    M = mem.shape[1]
    depth = g_mix.shape[0]
    T = B * S
    xt = x.reshape(T, D)
    tabs_a = _rope_tables(positions, ROT_DIM)
    tabs_b = _rope_tables(positions, ROPE_DIM)
    kv = matmul(rmsnorm(mem.reshape(B * M, D), g_mem), w_mem_kv, tm=B * M, name="mem_kv")
    mk, mv = memkv_post(kv, g_mem_k)
    for layer in range(depth):
        i = layer // 2
        if layer % 2 == 0:
            xt = even_mixer_block(xt, tabs_a, tabs_b, w_in[i], ga_q[i], ga_k[i], g_cq[i], w_uq[i], g_ckv[i],
                                  w_ukv[i], gb_q[i], gb_kn[i], gb_kr[i], w_o_even[i], g_mix[layer],
                                  batch=B, seq=S)
        else:
            h = rmsnorm(xt, g_mix[layer])
            qkv = matmul(h, w_qkv[i], out_dtype=BF16, tm=2048, single_buffer_x=True, name="qkv_proj")
            o = stickbreak_attention(qkv, batch=B, seq=S, heads=D // HEAD_DIM)
            xt = matmul(o, w_o_odd[i], res=xt, name="mixer_out")
        xattn = functools.partial(cross_attention_block, xt, mk, mv, g_x[layer], w_xq[layer], g_xq[layer],
                                  w_xo[layer], seq=S, mem_len=M)
        if layer % 2 == 0:
            xt, h = xattn(g_ffn[layer])
            hm = swiglu_gateup(h, w_gate[i], w_up[i])
            xt = matmul(hm, w_down[i], res=xt, tn=1024, tk=2048, name="swiglu_down")
        else:
            xt = moe_block(xattn(), g_ffn[layer], w_router[i], b_router[i], w_egate[i], w_eup[i], w_edown[i])
    return xt.reshape(B, S, D)
```

```python
import functools

import jax
import jax.numpy as jnp
from jax import lax
from jax.experimental import pallas as pl
from jax.experimental.pallas import tpu as pltpu

F32 = jnp.float32
BF16 = jnp.bfloat16

HEAD_DIM = 128
ROT_DIM = HEAD_DIM // 4
ROPE_THETA = 500000.0
BLOCK = 128
NEG_INF = -1e30
EPS = 1e-6
DILATED_PAIRS = ((128, 1), (512, 4), (2048, 16))
A_HEADS = 16
B_HEADS = 16
Q_LORA = 1536
KV_LORA = 512
NOPE_DIM = 128
ROPE_DIM = 64
QK_DIM = NOPE_DIM + ROPE_DIM
X_HEADS = 4
N_EXPERTS = 8
TOP_K = 2
MIXER_A_GROUP = 8
MOE_ROW_STEP = 128
LOG2_E = 1.4426950408889634
MLA_Q_SCALE = QK_DIM ** -0.5 * LOG2_E

LANES = 128
V7X_VMEM_BYTES = 64 * 1024 * 1024
VMEM_CAP = V7X_VMEM_BYTES - 6 * 1024 * 1024


def _params(semantics, vmem_bytes):
    return pltpu.CompilerParams(dimension_semantics=semantics,
                                vmem_limit_bytes=int(min(max(vmem_bytes, 32 * 1024 * 1024), VMEM_CAP)))


def _nbytes(shape, dtype):
    n = 1
    for s in shape:
        n *= s
    return n * jnp.dtype(dtype).itemsize


def _rmsnorm_kernel(x_ref, g_ref, o_ref):
    x = x_ref[...]
    ms = jnp.mean(x * x, axis=-1, keepdims=True)
    o_ref[...] = (x * lax.rsqrt(ms + EPS) * g_ref[...]).astype(o_ref.dtype)


def rmsnorm(x, g, *, tm=256):
    T, width = x.shape
    tm = min(tm, T)
    return pl.pallas_call(
        _rmsnorm_kernel,
        grid=(T // tm,),
        in_specs=[pl.BlockSpec((tm, width), lambda i: (i, 0)),
                  pl.BlockSpec((1, width), lambda i: (0, 0))],
        out_specs=pl.BlockSpec((tm, width), lambda i: (i, 0)),
        out_shape=jax.ShapeDtypeStruct((T, width), BF16),
        compiler_params=_params(("parallel",), 6 * _nbytes((tm, width), F32)),
        name="rmsnorm",
    )(x, g.reshape(1, width))


def _mm_kernel(*refs, nk, n_x, has_res, w_transposed):
    x_refs = refs[:n_x]
    w_ref = refs[n_x]
    rest = refs[n_x + 1:]
    res_ref = rest[0] if has_res else None
    o_ref = rest[1] if has_res else rest[0]
    acc_ref = rest[-1] if nk > 1 else None
    k = pl.program_id(2)
    w = w_ref[...].astype(BF16)

    def finish(part):
        out = part
        if has_res:
            out = out + res_ref[...]
        o_ref[...] = out.astype(o_ref.dtype)

    if n_x == 2:
        half = x_refs[0].shape[1]
        finish(jnp.dot(x_refs[0][...], w[:half], preferred_element_type=F32)
               + jnp.dot(x_refs[1][...], w[half:], preferred_element_type=F32))
    elif nk == 1 and w_transposed:
        finish(lax.dot_general(x_refs[0][...], w, (((1,), (1,)), ((), ())), preferred_element_type=F32))
    elif nk == 1:
        finish(jnp.dot(x_refs[0][...], w, preferred_element_type=F32))
    else:
        part = jnp.dot(x_refs[0][...], w, preferred_element_type=F32)

        @pl.when(k == 0)
        def _():
            acc_ref[...] = part

        @pl.when((k > 0) & (k < nk - 1))
        def _():
            acc_ref[...] += part

        @pl.when(k == nk - 1)
        def _():
            finish(acc_ref[...] + part)


def matmul(xs, w, *, n_out=None, res=None, out_dtype=F32, tm=1024, tn=512, tk=None, single_buffer_x=False,
           w_transposed=False, name="matmul"):
    if not isinstance(xs, (tuple, list)):
        xs = (xs,)
    n_x = len(xs)
    T = xs[0].shape[0]
    K = sum(x.shape[1] for x in xs)
    n_out = (w.shape[0] if w_transposed else w.shape[1]) if n_out is None else n_out
    tm = min(tm, T)
    tn = min(tn, n_out)
    if n_x == 2:
        assert tk is None and xs[1].shape[1] == xs[0].shape[1]
    tk = K if tk is None else tk
    nk = K // tk
    assert K == nk * tk and T % tm == 0 and n_out % tn == 0
    x_buffers = 2
    if n_x == 2:
        x_specs = [pl.BlockSpec((tm, K // 2), lambda i, j, k: (i, 0)) for _ in xs]
    elif nk == 1 and single_buffer_x:
        x_buffers = 1
        x_specs = [pl.BlockSpec((tm, tk), lambda i, j, k: (i, k), pipeline_mode=pl.Buffered(1))]
    else:
        x_specs = [pl.BlockSpec((tm, tk), lambda i, j, k: (i, k))]
    if w_transposed:
        assert n_x == 1 and nk == 1
        in_specs = x_specs + [pl.BlockSpec((tn, tk), lambda i, j, k: (j, k))]
    else:
        in_specs = x_specs + [pl.BlockSpec((tk, tn), lambda i, j, k: (k, j))]
    args = list(xs) + [w]
    if res is not None:
        in_specs.append(pl.BlockSpec((tm, tn), lambda i, j, k: (i, j)))
        args.append(res)
    scratch = [pltpu.VMEM((tm, tn), F32)] if nk > 1 else []
    vmem = (x_buffers * _nbytes((tm, tk), xs[0].dtype) + 2 * _nbytes((tk, tn), F32) + _nbytes((tk, tn), BF16)
            + (2 * _nbytes((tm, tn), F32) if res is not None else 0)
            + 2 * _nbytes((tm, tn), out_dtype) + 2 * _nbytes((tm, tn), F32))
    return pl.pallas_call(
        functools.partial(_mm_kernel, nk=nk, n_x=n_x, has_res=res is not None, w_transposed=w_transposed),
        grid=(T // tm, n_out // tn, nk),
        in_specs=in_specs,
        out_specs=pl.BlockSpec((tm, tn), lambda i, j, k: (i, j)),
        out_shape=jax.ShapeDtypeStruct((T, n_out), out_dtype),
        scratch_shapes=scratch,
        compiler_params=_params(("parallel", "parallel", "arbitrary"), vmem),
        name=name,
    )(*args)


def _norm_mm_kernel(x_ref, g_ref, w_ref, o_ref, xn):
    @pl.when(pl.program_id(1) == 0)
    def _():
        x = x_ref[...]
        xn[...] = (x * lax.rsqrt(jnp.mean(x * x, axis=-1, keepdims=True) + EPS) * g_ref[...]).astype(xn.dtype)

    o_ref[...] = jnp.dot(xn[...], w_ref[...].astype(BF16), preferred_element_type=F32)


def norm_matmul(x, g, w, *, width, col_block, tm=1024, tn=1024, name="norm_matmul"):
    T = x.shape[0]
    N = w.shape[1]
    tm, tn = min(tm, T), min(tn, N)
    vmem = (2 * _nbytes((tm, width), F32) + _nbytes((tm, width), BF16) + 2 * _nbytes((width, tn), F32)
            + _nbytes((width, tn), BF16) + 4 * _nbytes((tm, tn), F32))
    return pl.pallas_call(
        _norm_mm_kernel,
        grid=(T // tm, N // tn),
        in_specs=[pl.BlockSpec((tm, width), lambda i, j: (i, col_block)),
                  pl.BlockSpec((1, width), lambda i, j: (0, 0)),
                  pl.BlockSpec((width, tn), lambda i, j: (0, j))],
        out_specs=pl.BlockSpec((tm, tn), lambda i, j: (i, j)),
        out_shape=jax.ShapeDtypeStruct((T, N), F32),
        scratch_shapes=[pltpu.VMEM((tm, width), BF16)],
        compiler_params=_params(("parallel", "arbitrary"), vmem),
        name=name,
    )(x, g.reshape(1, width), w)


def _rope_tables(positions, dim):
    half = dim // 2
    inv_freq = ROPE_THETA ** (-jnp.arange(0, dim, 2, dtype=F32) / dim)
    ang = positions.astype(F32)[..., None] * inv_freq
    cos, sin = jnp.cos(ang), jnp.sin(ang)
    B, S = positions.shape
    ones = jnp.ones((B, S, LANES - dim), F32)
    zeros = jnp.zeros((B, S, LANES - dim), F32)
    zh = jnp.zeros((B, S, half), F32)
    c = jnp.concatenate([cos, cos, ones], axis=-1)
    sa = jnp.concatenate([-sin, zh, zeros], axis=-1)
    sb = jnp.concatenate([zh, sin, zeros], axis=-1)
    return [t.reshape(B * S, LANES) for t in (c, sa, sb)]


def _rope_lanes(y, c, sa, sb, half):
    return y * c + pltpu.roll(y, LANES - half, 1) * sa + pltpu.roll(y, half, 1) * sb


def _mixer_a_kernel(q_ref, k_ref, v_ref, c_ref, sa_ref, sb_ref, gq_ref, gk_ref, o_ref,
                    qs, ks, *state, seq):
    scale = HEAD_DIM ** -0.5 * LOG2_E
    c, sa, sb = c_ref[...], sa_ref[...], sb_ref[...]

    def prep(x, g):
        y = x * lax.rsqrt(jnp.mean(x * x, axis=-1, keepdims=True) + EPS) * g
        return _rope_lanes(y, c, sa, sb, ROT_DIM // 2)

    qs[...] = prep(q_ref[...], gq_ref[...])
    ks[...] = prep(k_ref[...], gk_ref[...])
    qi = lax.broadcasted_iota(jnp.int32, (BLOCK, 1), 0)
    kj = lax.broadcasted_iota(jnp.int32, (1, BLOCK), 1)
    trans_b = (((1,), (1,)), ((), ()))
    ones = jnp.ones((BLOCK, HEAD_DIM), BF16)

    for bi, (window, dil) in enumerate(DILATED_PAIRS):
        acc, m_sc, l_sc = state[3 * bi:3 * bi + 3]
        n_back = window // dil
        nb = seq // dil // BLOCK
        assert n_back <= BLOCK and nb * dil * BLOCK == seq
        cur_ok = (qi >= kj) & (qi - kj <= n_back)
        prev_ok = qi + BLOCK - kj <= n_back

        def rows_at(r, n, dil=dil):
            if dil == 1:
                return pl.ds(BLOCK * n, BLOCK)
            return pl.ds(r + dil * BLOCK * n, BLOCK, stride=dil)

        blocks = [(rows_at(r, n), rows_at(r, n - 1) if n > 0 else None) for r in range(dil) for n in range(nb)]
        for first in range(0, len(blocks), MIXER_A_GROUP):
            group = blocks[first:first + MIXER_A_GROUP]
            s_cur, s_prev = [], []
            for rows, prev in group:
                qb = qs[rows, :].astype(BF16)
                s_c = lax.dot_general(qb, ks[rows, :].astype(BF16), trans_b, preferred_element_type=F32) * scale
                s_cur.append(jnp.where(cur_ok, s_c, NEG_INF))
                if prev is None:
                    s_prev.append(None)
                else:
                    s_p = lax.dot_general(qb, ks[prev, :].astype(BF16), trans_b,
                                          preferred_element_type=F32) * scale
                    s_prev.append(jnp.where(prev_ok, s_p, NEG_INF))
            stats = []
            for s_c, s_p in zip(s_cur, s_prev):
                m = jnp.max(s_c, axis=-1, keepdims=True)
                if s_p is not None:
                    m = jnp.maximum(m, jnp.max(s_p, axis=-1, keepdims=True))
                p_c = jnp.exp2(s_c - m).astype(BF16)
                p_p = None if s_p is None else jnp.exp2(s_p - m).astype(BF16)
                stats.append((m, p_c, p_p))
            for (rows, prev), (m, p_c, p_p) in zip(group, stats):
                v_ext = jnp.concatenate([v_ref[rows, :].astype(BF16), ones], axis=1)
                o = jnp.dot(p_c, v_ext, preferred_element_type=F32)
                if p_p is not None:
                    v_ext = jnp.concatenate([v_ref[prev, :].astype(BF16), ones], axis=1)
                    o = o + jnp.dot(p_p, v_ext, preferred_element_type=F32)
                acc[rows, :] = o[:, :HEAD_DIM]
                l_sc[rows, :] = o[:, HEAD_DIM:]
                m_sc[rows, :] = jnp.broadcast_to(m, (BLOCK, HEAD_DIM))

    n_br = len(DILATED_PAIRS)
    m_all = state[1][...]
    for bi in range(1, n_br):
        m_all = jnp.maximum(m_all, state[3 * bi + 1][...])
    num = jnp.zeros(o_ref.shape, F32)
    den = jnp.zeros(o_ref.shape, F32)
    for bi in range(n_br):
        w = jnp.exp2(state[3 * bi + 1][...] - m_all)
        num = num + w * state[3 * bi][...]
        den = den + w * state[3 * bi + 2][...]
    o_ref[...] = (num / den).astype(o_ref.dtype)


def mixer_a(proj, tabs, gq, gk, *, batch, seq):
    T = batch * seq
    blk = (seq, HEAD_DIM)
    head = lambda off: pl.BlockSpec(blk, lambda b, h: (b, off + h))
    tab = pl.BlockSpec(blk, lambda b, h: (b, 0))
    gspec = pl.BlockSpec((1, HEAD_DIM), lambda b, h: (0, 0))
    return pl.pallas_call(
        functools.partial(_mixer_a_kernel, seq=seq),
        grid=(batch, A_HEADS),
        in_specs=[head(0), head(A_HEADS), head(2 * A_HEADS), tab, tab, tab, gspec, gspec],
        out_specs=pl.BlockSpec(blk, lambda b, h: (b, h)),
        out_shape=jax.ShapeDtypeStruct((T, A_HEADS * HEAD_DIM), BF16),
        scratch_shapes=[pltpu.VMEM(blk, F32) for _ in range(2 + 3 * len(DILATED_PAIRS))],
        compiler_params=_params(("parallel", "parallel"), 30 * _nbytes(blk, F32)),
        name="mixer_a",
    )(proj, proj, proj, *tabs, gq.reshape(1, HEAD_DIM), gk.reshape(1, HEAD_DIM))


def _mla_prep_kernel(qn_ref, qr_ref, kn_ref, v_ref, kr_ref, c_ref, sa_ref, sb_ref,
                     gqn_ref, gqr_ref, gkn_ref, gkr_ref, qt_ref, kf_ref, vb_ref, kr_sc):
    c, sa, sb = c_ref[...], sa_ref[...], sb_ref[...]
    half = ROPE_DIM // 2

    @pl.when(pl.program_id(1) == 0)
    def _():
        kr = kr_ref[...]
        kr = kr * lax.rsqrt(jnp.sum(kr * kr, axis=-1, keepdims=True) * (1.0 / ROPE_DIM) + EPS) * gkr_ref[...]
        kr_sc[...] = _rope_lanes(kr, c, sa, sb, half).astype(kr_sc.dtype)

    qn, qr = qn_ref[...], qr_ref[...]
    ss = jnp.sum(qn * qn, axis=-1, keepdims=True) + jnp.sum(qr * qr, axis=-1, keepdims=True)
    inv = lax.rsqrt(ss * (1.0 / QK_DIM) + EPS) * MLA_Q_SCALE
    qf = jnp.concatenate([qn * inv * gqn_ref[...], _rope_lanes(qr * inv * gqr_ref[...], c, sa, sb, half)], axis=1)
    qt_ref[...] = qf.astype(qt_ref.dtype).T
    kn = kn_ref[...]
    kn = kn * lax.rsqrt(jnp.mean(kn * kn, axis=-1, keepdims=True) + EPS) * gkn_ref[...]
    kf_ref[:, :NOPE_DIM] = kn.astype(kf_ref.dtype)
    kf_ref[:, NOPE_DIM:] = kr_sc[...]
    vb_ref[...] = v_ref[...].astype(vb_ref.dtype)


def mla_prep(q_up, kv_up, krp, tabs, gqn, gqr, gkn, gkr, *, batch, seq, tm=2048):
    T = q_up.shape[0]
    H = B_HEADS
    per_seq = seq // tm
    blk = lambda f: pl.BlockSpec((tm, LANES), f)
    g = pl.BlockSpec((1, LANES), lambda i, h: (0, 0))
    return pl.pallas_call(
        _mla_prep_kernel,
        grid=(T // tm, H),
        in_specs=[blk(lambda i, h: (i, h)), blk(lambda i, h: (i, H + h)),
                  blk(lambda i, h: (i, 2 * h)), blk(lambda i, h: (i, 2 * h + 1)),
                  blk(lambda i, h: (i, 0)), blk(lambda i, h: (i, 0)), blk(lambda i, h: (i, 0)),
                  blk(lambda i, h: (i, 0)), g, g, g, g],
        out_specs=[pl.BlockSpec((None, None, 2 * LANES, tm), lambda i, h: (i // per_seq, h, 0, i % per_seq)),
                   pl.BlockSpec((tm, 2 * LANES), lambda i, h: (i, h)),
                   pl.BlockSpec((tm, LANES), lambda i, h: (i, h))],
        out_shape=[jax.ShapeDtypeStruct((batch, H, 2 * LANES, seq), BF16),
                   jax.ShapeDtypeStruct((T, H * 2 * LANES), BF16),
                   jax.ShapeDtypeStruct((T, H * LANES), BF16)],
        scratch_shapes=[pltpu.VMEM((tm, LANES), BF16)],
        compiler_params=_params(("parallel", "arbitrary"), 40 * _nbytes((tm, LANES), F32)),
        name="mla_prep",
    )(q_up, q_up, kv_up, kv_up, krp, *tabs, gqn, gqr, gkn, gkr)


def _causal_attn_kernel(qt_ref, k_ref, v_ref, o_ref, s_sc, acc, *, tq, hg, dq, dv):
    qi = pl.program_id(2)
    key = lax.broadcasted_iota(jnp.int32, (tq, tq), 0)
    qry = lax.broadcasted_iota(jnp.int32, (tq, tq), 1)
    visible = key <= qry
    ones = jnp.ones((tq, dv), BF16)
    contract_rows = (((0,), (0,)), ((), ()))

    heads = range(hg)

    def scores(j, diagonal):
        rows = pl.ds(pl.multiple_of(j * tq, tq), tq)
        s = [jnp.dot(k_ref[rows, g * dq:(g + 1) * dq], qt_ref[g], preferred_element_type=F32) for g in heads]
        if diagonal:
            s = [jnp.where(visible, sg, NEG_INF) for sg in s]
        for g in heads:
            s_sc[g, j] = s[g]
        return s

    def col_max(j, ms):
        s = scores(j, False)
        return tuple(jnp.maximum(ms[g], jnp.max(s[g], axis=0, keepdims=True)) for g in heads)

    ms = lax.fori_loop(0, qi, col_max, tuple(jnp.max(sg, axis=0, keepdims=True) for sg in scores(qi, True)))

    def weighted(j):
        rows = pl.ds(pl.multiple_of(j * tq, tq), tq)
        p = [jnp.exp2(s_sc[g, j] - ms[g]).astype(BF16) for g in heads]
        return [lax.dot_general(p[g], jnp.concatenate([v_ref[rows, g * dv:(g + 1) * dv], ones], axis=1),
                                contract_rows, preferred_element_type=F32) for g in heads]

    for g, w in enumerate(weighted(qi)):
        acc[g] = w

    def below(j, carry):
        for g, w in enumerate(weighted(j)):
            acc[g] += w
        return carry

    lax.fori_loop(0, qi, below, 0)
    for g in heads:
        a = acc[g]
        o_ref[:, g * dv:(g + 1) * dv] = (a[:, :dv] / a[:, dv:]).astype(o_ref.dtype)


def mla_attention(qt, kf, vb, *, batch, seq, tq=512, hg=4):
    T = batch * seq
    nq = seq // tq
    dq = qt.shape[2]
    dv = vb.shape[1] // B_HEADS
    assert B_HEADS % hg == 0
    return pl.pallas_call(
        functools.partial(_causal_attn_kernel, tq=tq, hg=hg, dq=dq, dv=dv),
        grid=(batch, B_HEADS // hg, nq),
        in_specs=[pl.BlockSpec((None, hg, dq, tq), lambda b, h, i: (b, h, 0, i)),
                  pl.BlockSpec((seq, hg * dq), lambda b, h, i: (b, h)),
                  pl.BlockSpec((seq, hg * dv), lambda b, h, i: (b, h))],
        out_specs=pl.BlockSpec((tq, hg * dv), lambda b, h, i: (b * nq + i, h)),
        out_shape=jax.ShapeDtypeStruct((T, B_HEADS * dv), BF16),
        scratch_shapes=[pltpu.VMEM((hg, nq, tq, tq), F32), pltpu.VMEM((hg, tq, 2 * dv), F32)],
        compiler_params=_params(("parallel", "parallel", "parallel"),
                                _nbytes((hg, nq, tq, tq), F32) + 2 * _nbytes((seq, hg * (dq + dv)), BF16)
                                + 8 * hg * _nbytes((tq, tq), F32)),
        name="mla_attention",
    )(qt, kf, vb)


STICK_EXHAUSTED_LOG = -110.0


def _stickbreak_kernel(q_ref, k_ref, v_ref, o_ref, qt_sc, c_sc, acc, *, tq, hg, scale):
    qi = pl.program_id(2)
    c_sc[...] = jnp.zeros(c_sc.shape, F32)
    acc[...] = jnp.zeros(acc.shape, F32)
    for g in range(hg):
        qt_sc[g] = q_ref[:, g * HEAD_DIM:(g + 1) * HEAD_DIM].T
    key = lax.broadcasted_iota(jnp.int32, (tq, tq), 0)
    qry = lax.broadcasted_iota(jnp.int32, (tq, tq), 1)
    after = (qry > key).astype(BF16)
    earlier = key < qry
    contract_rows = (((0,), (0,)), ((), ()))

    def chunk(j, diagonal):
        rows = pl.ds(pl.multiple_of(j * tq, tq), tq)
        heads = range(hg)
        cols = [slice(g * HEAD_DIM, (g + 1) * HEAD_DIM) for g in heads]
        z = [jnp.dot(k_ref[rows, cols[g]], qt_sc[g], preferred_element_type=F32) * (scale * LOG2_E) for g in heads]
        log_beta, parts = [], []
        for g in heads:
            softplus = jnp.maximum(z[g], 0.0) + jnp.log2(1.0 + jnp.exp2(-jnp.abs(z[g])))
            log_keep = -softplus
            if diagonal:
                log_keep = jnp.where(earlier, log_keep, 0.0)
            parts.append(log_keep.astype(BF16))
            log_beta.append(z[g] - softplus + c_sc[g])
            c_sc[g] += jnp.sum(log_keep, axis=0, keepdims=True)
        within = jnp.dot(after, jnp.concatenate(parts, axis=1), preferred_element_type=F32)
        for g in heads:
            a = jnp.exp2(log_beta[g] + within[:, g * tq:(g + 1) * tq])
            if diagonal:
                a = jnp.where(earlier, a, 0.0)
            acc[:, cols[g]] += lax.dot_general(a.astype(BF16), v_ref[rows, cols[g]], contract_rows,
                                               preferred_element_type=F32)

    def stick_left():
        return jnp.max(c_sc[...]) > STICK_EXHAUSTED_LOG * LOG2_E

    chunk(qi, True)

    def more(state):
        j, left = state
        return (j >= 0) & left

    def step(state):
        j, _ = state
        chunk(j, False)
        return j - 1, stick_left()

    lax.while_loop(more, step, (qi - 1, stick_left()))
    o_ref[...] = acc[...].astype(o_ref.dtype)


def stickbreak_attention(qkv, *, batch, seq, heads, tq=256, hg=8):
    T = batch * seq
    nq = seq // tq
    ng = heads // hg
    assert heads == ng * hg
    blk = hg * HEAD_DIM
    return pl.pallas_call(
        functools.partial(_stickbreak_kernel, tq=tq, hg=hg, scale=HEAD_DIM ** -0.5),
        grid=(batch, ng, nq),
        in_specs=[pl.BlockSpec((tq, blk), lambda b, h, i: (b * nq + i, h)),
                  pl.BlockSpec((seq, blk), lambda b, h, i: (b, ng + h)),
                  pl.BlockSpec((seq, blk), lambda b, h, i: (b, 2 * ng + h))],
        out_specs=pl.BlockSpec((tq, blk), lambda b, h, i: (b * nq + i, h)),
        out_shape=jax.ShapeDtypeStruct((T, heads * HEAD_DIM), BF16),
        scratch_shapes=[pltpu.VMEM((hg, HEAD_DIM, tq), BF16), pltpu.VMEM((hg, 1, tq), F32),
                        pltpu.VMEM((tq, blk), F32)],
        compiler_params=_params(("parallel", "parallel", "parallel"), 0),
        name="stickbreak",
    )(qkv, qkv, qkv)


def _memkv_kernel(kv_ref, g_ref, mk_ref, mv_ref):
    for h in range(X_HEADS):
        cols = slice(h * HEAD_DIM, (h + 1) * HEAD_DIM)
        k = kv_ref[:, cols]
        k = k * lax.rsqrt(jnp.mean(k * k, axis=-1, keepdims=True) + EPS) * g_ref[...]
        mk_ref[:, cols] = k.astype(mk_ref.dtype)
    mv_ref[...] = kv_ref[:, X_HEADS * HEAD_DIM:].astype(mv_ref.dtype)


def memkv_post(kv, g):
    M = kv.shape[0]
    xd = X_HEADS * HEAD_DIM
    return pl.pallas_call(
        _memkv_kernel,
        grid=(1,),
        in_specs=[pl.BlockSpec((M, 2 * xd), lambda i: (0, 0)), pl.BlockSpec((1, HEAD_DIM), lambda i: (0, 0))],
        out_specs=[pl.BlockSpec((M, xd), lambda i: (0, 0)), pl.BlockSpec((M, xd), lambda i: (0, 0))],
        out_shape=[jax.ShapeDtypeStruct((M, xd), BF16), jax.ShapeDtypeStruct((M, xd), BF16)],
        compiler_params=_params(("arbitrary",), 0),
        name="memkv_post",
    )(kv, g.reshape(1, HEAD_DIM))


def _xattn_block_kernel(x_ref, gx_ref, wq_ref, gq_ref, mk_ref, mv_ref, wo_ref, *rest, emit_norm):
    if emit_norm:
        gn_ref, o_ref, hn_ref = rest
    else:
        (o_ref,) = rest
    scale = HEAD_DIM ** -0.5
    x = x_ref[...]
    h = (x * lax.rsqrt(jnp.mean(x * x, axis=-1, keepdims=True) + EPS) * gx_ref[...]).astype(BF16)
    q_all = jnp.dot(h, wq_ref[...], preferred_element_type=F32)
    heads = []
    for hd in range(X_HEADS):
        cols = slice(hd * HEAD_DIM, (hd + 1) * HEAD_DIM)
        q = q_all[:, cols]
        q = (q * lax.rsqrt(jnp.mean(q * q, axis=-1, keepdims=True) + EPS) * gq_ref[...]).astype(BF16)
        s = lax.dot_general(q, mk_ref[:, cols], (((1,), (1,)), ((), ())), preferred_element_type=F32) * scale
        p = jnp.exp(s - jnp.max(s, axis=-1, keepdims=True))
        p = p / jnp.sum(p, axis=-1, keepdims=True)
        heads.append(jnp.dot(p.astype(BF16), mv_ref[:, cols], preferred_element_type=F32).astype(BF16))
    y = x + jnp.dot(jnp.concatenate(heads, axis=1), wo_ref[...], preferred_element_type=F32)
    o_ref[...] = y
    if emit_norm:
        hn_ref[...] = (y * lax.rsqrt(jnp.mean(y * y, axis=-1, keepdims=True) + EPS) * gn_ref[...]).astype(hn_ref.dtype)


def cross_attention_block(x, mk, mv, g_x, w_xq, g_xq, w_xo, g_next=None, *, seq, mem_len, tm=256):
    T, D = x.shape
    xd = X_HEADS * HEAD_DIM
    per_seq = seq // tm
    emit_norm = g_next is not None
    row = lambda w: pl.BlockSpec((1, w), lambda i: (0, 0))
    tok = pl.BlockSpec((tm, D), lambda i: (i, 0))
    in_specs = [tok, row(D), pl.BlockSpec((D, xd), lambda i: (0, 0)), row(HEAD_DIM),
                pl.BlockSpec((mem_len, xd), lambda i: (i // per_seq, 0)),
                pl.BlockSpec((mem_len, xd), lambda i: (i // per_seq, 0)),
                pl.BlockSpec((xd, D), lambda i: (0, 0))]
    args = [x, g_x.reshape(1, D), w_xq.astype(BF16), g_xq.reshape(1, HEAD_DIM), mk, mv, w_xo.astype(BF16)]
    out_specs, out_shape = tok, jax.ShapeDtypeStruct((T, D), F32)
    if emit_norm:
        in_specs.append(row(D))
        args.append(g_next.reshape(1, D))
        out_specs = [tok, tok]
        out_shape = [out_shape, jax.ShapeDtypeStruct((T, D), BF16)]
    vmem = 10 * _nbytes((tm, D), F32) + 4 * _nbytes((D, xd), BF16)
    return pl.pallas_call(
        functools.partial(_xattn_block_kernel, emit_norm=emit_norm),
        grid=(T // tm,),
        in_specs=in_specs,
        out_specs=out_specs,
        out_shape=out_shape,
        compiler_params=_params(("parallel",), vmem),
        name="cross_attention",
    )(*args)


def _gateup_kernel(x_ref, wg_ref, wu_ref, o_ref):
    x = x_ref[...]
    g = jnp.dot(x, wg_ref[...].astype(BF16), preferred_element_type=F32)
    u = jnp.dot(x, wu_ref[...].astype(BF16), preferred_element_type=F32)
    o_ref[...] = (g * jax.nn.sigmoid(g) * u).astype(o_ref.dtype)


def swiglu_gateup(x, wg, wu, *, tm=2048, tf=256):
    T, K = x.shape
    F = wg.shape[1]
    vmem = (_nbytes((tm, K), BF16) + 4 * _nbytes((K, tf), F32) + 2 * _nbytes((K, tf), BF16)
            + 6 * _nbytes((tm, tf), F32))
    return pl.pallas_call(
        _gateup_kernel,
        grid=(T // tm, F // tf),
        in_specs=[pl.BlockSpec((tm, K), lambda i, j: (i, 0), pipeline_mode=pl.Buffered(1)),
                  pl.BlockSpec((K, tf), lambda i, j: (0, j)),
                  pl.BlockSpec((K, tf), lambda i, j: (0, j))],
        out_specs=pl.BlockSpec((tm, tf), lambda i, j: (i, j)),
        out_shape=jax.ShapeDtypeStruct((T, F), BF16),
        compiler_params=_params(("parallel", "parallel"), vmem),
        name="swiglu_gateup",
    )(x, wg, wu)


def _router_kernel(x_ref, g_ref, w_ref, b_ref, idx_ref, gw_ref):
    x = x_ref[...]
    h = x * lax.rsqrt(jnp.mean(x * x, axis=-1, keepdims=True) + EPS) * g_ref[...]
    logits = jnp.dot(h, w_ref[...], preferred_element_type=F32, precision=lax.Precision.HIGHEST) + b_ref[...]
    lane = lax.broadcasted_iota(jnp.int32, logits.shape, 1)
    logits = jnp.where(lane < N_EXPERTS, logits, -jnp.inf)
    m1 = jnp.max(logits, axis=-1, keepdims=True)
    i1 = jnp.min(jnp.where(logits == m1, lane, LANES), axis=-1, keepdims=True)
    rest = jnp.where(lane == i1, -jnp.inf, logits)
    m2 = jnp.max(rest, axis=-1, keepdims=True)
    i2 = jnp.min(jnp.where(rest == m2, lane, LANES), axis=-1, keepdims=True)
    e = jnp.exp(m2 - m1)
    w1 = 1.0 / (1.0 + e)
    w2 = e / (1.0 + e)
    idx_ref[...] = jnp.where(lane == 0, i1, jnp.where(lane == 1, i2, 0))
    gw_ref[...] = jnp.where(lane == 0, w1, jnp.where(lane == 1, w2, 0.0))


def moe_router(x, g, w_router, b_router, *, tm=256):
    T, D = x.shape
    E = w_router.shape[1]
    w = jnp.pad(w_router, ((0, 0), (0, LANES - E)))
    b = jnp.pad(b_router, (0, LANES - E)).reshape(1, LANES)
    idx, gw = pl.pallas_call(
        _router_kernel,
        grid=(T // tm,),
        in_specs=[pl.BlockSpec((tm, D), lambda i: (i, 0)), pl.BlockSpec((1, D), lambda i: (0, 0)),
                  pl.BlockSpec((D, LANES), lambda i: (0, 0)), pl.BlockSpec((1, LANES), lambda i: (0, 0))],
        out_specs=[pl.BlockSpec((tm, LANES), lambda i: (i, 0)), pl.BlockSpec((tm, LANES), lambda i: (i, 0))],
        out_shape=[jax.ShapeDtypeStruct((T, LANES), jnp.int32), jax.ShapeDtypeStruct((T, LANES), F32)],
        compiler_params=_params(("parallel",), 8 * _nbytes((tm, D), F32)),
        name="moe_router",
    )(x, g.reshape(1, D), w, b)
    return idx[:, :TOP_K], gw[:, :TOP_K]


def _moe_dispatch(idx, tm):
    T = idx.shape[0]
    A = T * TOP_K
    e_flat = idx.reshape(A)
    onehot = (e_flat[:, None] == jnp.arange(N_EXPERTS, dtype=jnp.int32)[None, :]).astype(jnp.int32)
    csum = jnp.cumsum(onehot, axis=0)
    pos_in = jnp.sum(csum * onehot, axis=1) - 1
    counts = csum[-1]
    padded = ((counts + tm - 1) // tm) * tm
    gend = jnp.cumsum(padded)
    gstart = gend - padded
    dest = (jnp.sum(onehot * gstart[None, :], axis=1) + pos_in).astype(jnp.int32)
    n_rows = A + N_EXPERTS * tm
    row_tok = jnp.zeros((n_rows,), jnp.int32).at[dest].set(jnp.arange(A, dtype=jnp.int32) // TOP_K)
    nb = n_rows // tm
    n_used = (gend[-1] // tm).astype(jnp.int32)
    blk = jnp.arange(nb, dtype=jnp.int32)
    be = jnp.sum((blk[:, None] * tm >= gend[None, :]).astype(jnp.int32), axis=1)
    be = jnp.minimum(be, N_EXPERTS - 1)
    be = jnp.where(blk < n_used, be, be[jnp.maximum(n_used - 1, 0)]).astype(jnp.int32)
    eid = jnp.arange(N_EXPERTS, dtype=jnp.int32)
    later = (counts[None, :] > 0) & (eid[None, :] > eid[:, None])
    nxt_of = jnp.min(jnp.where(later, eid[None, :], N_EXPERTS), axis=1)
    nxt_of = jnp.where(nxt_of == N_EXPERTS, -1, nxt_of).astype(jnp.int32)
    mine = be[:, None] == eid[None, :]
    nxt = jnp.sum(mine * nxt_of[None, :], axis=1).astype(jnp.int32)
    group_end_row = jnp.sum(mine * (gstart + counts)[None, :], axis=1)
    nv = jnp.where(blk < n_used, jnp.clip(group_end_row - blk * tm, 0, tm), 0).astype(jnp.int32)
    return dest, row_tok, be, n_used.reshape(1), nxt, nv


def _moe_gather_kernel(tok_ref, nu_ref, x_hbm, g_ref, o_ref, buf, sem, *, rows):
    i = pl.program_id(0)
    n_used = nu_ref[0]

    def row_copy(slot, r, src_row):
        return pltpu.make_async_copy(x_hbm.at[pl.ds(src_row, 1)], buf.at[slot, pl.ds(r, 1)], sem.at[slot])

    def fetch(block, slot):
        def issue(r, c):
            row_copy(slot, r, tok_ref[block * rows + r]).start()
            return c

        lax.fori_loop(0, rows, issue, 0, unroll=8)

    @pl.when(i == 0)
    def _():
        fetch(0, 0)

    @pl.when(i + 1 < n_used)
    def _():
        fetch(i + 1, (i + 1) % 2)

    @pl.when(i < n_used)
    def _():
        slot = i % 2

        def drain(r, c):
            row_copy(slot, r, 0).wait()
            return c

        lax.fori_loop(0, rows, drain, 0, unroll=8)
        x = buf[slot]
        o_ref[...] = (x * lax.rsqrt(jnp.mean(x * x, axis=-1, keepdims=True) + EPS) * g_ref[...]).astype(o_ref.dtype)

    @pl.when(i >= n_used)
    def _():
        o_ref[...] = jnp.zeros(o_ref.shape, o_ref.dtype)


def moe_gather_norm(x, g, row_tok, n_used, *, tm, rows=256):
    T, D = x.shape
    n_rows = row_tok.shape[0]
    per = tm // rows
    return pl.pallas_call(
        functools.partial(_moe_gather_kernel, rows=rows),
        grid_spec=pltpu.PrefetchScalarGridSpec(
            num_scalar_prefetch=2,
            grid=(n_rows // rows,),
            in_specs=[pl.BlockSpec(memory_space=pl.ANY), pl.BlockSpec((1, D), lambda i, tok, nu: (0, 0))],
            out_specs=pl.BlockSpec((rows, D), lambda i, tok, nu: (i, 0)),
            scratch_shapes=[pltpu.VMEM((2, rows, D), F32), pltpu.SemaphoreType.DMA((2,))]),
        out_shape=jax.ShapeDtypeStruct((n_rows, D), BF16),
        compiler_params=_params(("arbitrary",), 10 * _nbytes((rows, D), F32)),
        name="moe_gather",
    )(row_tok, n_used * per, x, g.reshape(1, D))


def _stream_group_weights(be_ref, nu_ref, nxt_ref, w_hbms, stage, sem, w_bf16, *, tile, n_tiles):
    j = pl.program_id(0)
    i = pl.program_id(1)

    def copies(e, jj):
        cols = pl.ds(pl.multiple_of(jj * tile, tile), tile)
        return [pltpu.make_async_copy(w.at[e, :, cols], stage.at[n], sem.at[n]) for n, w in enumerate(w_hbms)]

    @pl.when((j == 0) & (i == 0))
    def _():
        for c in copies(be_ref[0], 0):
            c.start()

    first_of_group = (i < nu_ref[0]) & ((i == 0) | (be_ref[i] != be_ref[jnp.maximum(i - 1, 0)]))

    @pl.when(first_of_group)
    def _():
        for c in copies(be_ref[i], j):
            c.wait()
        for n, dst in enumerate(w_bf16):
            dst[...] = stage[n].astype(BF16)
        more_in_sweep = nxt_ref[i] >= 0
        e_next = jnp.where(more_in_sweep, nxt_ref[i], be_ref[0])
        j_next = jnp.where(more_in_sweep, j, j + 1)

        @pl.when(more_in_sweep | (j + 1 < n_tiles))
        def _():
            for c in copies(e_next, j_next):
                c.start()


def _for_valid_rows(nv, o_ref, compute):
    tm = o_ref.shape[0]
    for rows in range(MOE_ROW_STEP, tm + 1, MOE_ROW_STEP):
        @pl.when((nv > rows - MOE_ROW_STEP) & (nv <= rows))
        def _(rows=rows):
            o_ref[:rows, :] = compute(rows).astype(o_ref.dtype)
            if rows < tm:
                o_ref[rows:, :] = jnp.zeros((tm - rows, o_ref.shape[1]), o_ref.dtype)

    @pl.when(nv == 0)
    def _():
        o_ref[...] = jnp.zeros(o_ref.shape, o_ref.dtype)


def _moe_gateup_kernel(be_ref, nu_ref, nxt_ref, nv_ref, x_ref, wg_hbm, wu_hbm, o_ref, stage, sem, wgb, wub,
                       *, tf, n_tiles):
    _stream_group_weights(be_ref, nu_ref, nxt_ref, (wg_hbm, wu_hbm), stage, sem, (wgb, wub), tile=tf, n_tiles=n_tiles)

    def compute(rows):
        x = x_ref[:rows, :]
        g = jnp.dot(x, wgb[...], preferred_element_type=F32)
        u = jnp.dot(x, wub[...], preferred_element_type=F32)
        return g * jax.nn.sigmoid(g) * u

    _for_valid_rows(nv_ref[pl.program_id(1)], o_ref, compute)


def moe_gateup(xs, wg, wu, be, n_used, nxt, nv, *, tm, tf=512):
    n_rows, K = xs.shape
    F = wg.shape[2]
    tf = min(tf, F)
    nb = n_rows // tm
    xmap = lambda j, i, be, nu, *_: (jnp.minimum(i, nu[0] - 1), 0)
    hbm = pl.BlockSpec(memory_space=pl.ANY)
    vmem = (2 * _nbytes((tm, K), BF16) + 2 * _nbytes((K, tf), F32) + 4 * _nbytes((K, tf), BF16)
            + 6 * _nbytes((tm, tf), F32))
    return pl.pallas_call(
        functools.partial(_moe_gateup_kernel, tf=tf, n_tiles=F // tf),
        grid_spec=pltpu.PrefetchScalarGridSpec(
            num_scalar_prefetch=4,
            grid=(F // tf, nb),
            in_specs=[pl.BlockSpec((tm, K), xmap), hbm, hbm],
            out_specs=pl.BlockSpec((tm, tf), lambda j, i, *_: (i, j)),
            scratch_shapes=[pltpu.VMEM((2, K, tf), F32), pltpu.SemaphoreType.DMA((2,)),
                            pltpu.VMEM((K, tf), BF16), pltpu.VMEM((K, tf), BF16)]),
        out_shape=jax.ShapeDtypeStruct((n_rows, F), BF16),
        compiler_params=_params(("arbitrary", "arbitrary"), vmem),
        name="moe_gateup",
    )(be, n_used, nxt, nv, xs, wg, wu)


def _moe_down_kernel(be_ref, nu_ref, nxt_ref, nv_ref, x_ref, w_hbm, o_ref, stage, sem, wb, *, tn, n_tiles):
    _stream_group_weights(be_ref, nu_ref, nxt_ref, (w_hbm,), stage, sem, (wb,), tile=tn, n_tiles=n_tiles)
    _for_valid_rows(nv_ref[pl.program_id(1)], o_ref,
                    lambda rows: jnp.dot(x_ref[:rows, :], wb[...], preferred_element_type=F32))


def moe_down(hm, wd, be, n_used, nxt, nv, *, tm, tn=1024):
    n_rows, K = hm.shape
    N = wd.shape[2]
    tn = min(tn, N)
    nb = n_rows // tm
    xmap = lambda j, i, be, nu, *_: (jnp.minimum(i, nu[0] - 1), 0)
    vmem = (2 * _nbytes((tm, K), BF16) + _nbytes((K, tn), F32) + 2 * _nbytes((K, tn), BF16)
            + 4 * _nbytes((tm, tn), F32))
    return pl.pallas_call(
        functools.partial(_moe_down_kernel, tn=tn, n_tiles=N // tn),
        grid_spec=pltpu.PrefetchScalarGridSpec(
            num_scalar_prefetch=4,
            grid=(N // tn, nb),
            in_specs=[pl.BlockSpec((tm, K), xmap), pl.BlockSpec(memory_space=pl.ANY)],
            out_specs=pl.BlockSpec((tm, tn), lambda j, i, *_: (i, j)),
            scratch_shapes=[pltpu.VMEM((1, K, tn), F32), pltpu.SemaphoreType.DMA((1,)),
                            pltpu.VMEM((K, tn), BF16)]),
        out_shape=jax.ShapeDtypeStruct((n_rows, N), F32),
        compiler_params=_params(("arbitrary", "arbitrary"), vmem),
        name="moe_down",
    )(be, n_used, nxt, nv, hm, wd)


def _moe_combine_kernel(pos_ref, x_ref, gw_ref, ys_hbm, o_ref, buf, sem, *, rows):
    i = pl.program_id(0)
    n_blocks = pl.num_programs(0)

    def row_copy(slot, k, r, src_row):
        return pltpu.make_async_copy(ys_hbm.at[pl.ds(src_row, 1)], buf.at[slot, k, pl.ds(r, 1)], sem.at[slot])

    def fetch(block, slot):
        def issue(r, c):
            a = (block * rows + r) * TOP_K
            for k in range(TOP_K):
                row_copy(slot, k, r, pos_ref[a + k]).start()
            return c

        lax.fori_loop(0, rows, issue, 0, unroll=4)

    @pl.when(i == 0)
    def _():
        fetch(0, 0)

    @pl.when(i + 1 < n_blocks)
    def _():
        fetch(i + 1, (i + 1) % 2)

    slot = i % 2

    def drain(r, c):
        for k in range(TOP_K):
            row_copy(slot, k, r, 0).wait()
        return c

    lax.fori_loop(0, rows, drain, 0, unroll=4)
    out = x_ref[...]
    for k in range(TOP_K):
        out = out + gw_ref[:, k:k + 1] * buf[slot, k]
    o_ref[...] = out


def moe_combine(x, ys, dest, gw, *, rows=256):
    T, D = x.shape
    return pl.pallas_call(
        functools.partial(_moe_combine_kernel, rows=rows),
        grid_spec=pltpu.PrefetchScalarGridSpec(
            num_scalar_prefetch=1,
            grid=(T // rows,),
            in_specs=[pl.BlockSpec((rows, D), lambda i, pos: (i, 0)),
                      pl.BlockSpec((rows, TOP_K), lambda i, pos: (i, 0)),
                      pl.BlockSpec(memory_space=pl.ANY)],
            out_specs=pl.BlockSpec((rows, D), lambda i, pos: (i, 0)),
            scratch_shapes=[pltpu.VMEM((2, TOP_K, rows, D), F32), pltpu.SemaphoreType.DMA((2,))]),
        out_shape=jax.ShapeDtypeStruct((T, D), F32),
        compiler_params=_params(("arbitrary",), 12 * _nbytes((rows, D), F32)),
        name="moe_combine",
    )(dest, x, gw, ys)


def moe_block(x, g, w_router, b_router, w_egate, w_eup, w_edown, *, tm=512):
    idx, gw = moe_router(x, g, w_router, b_router)
    dest, row_tok, be, n_used, nxt, nv = _moe_dispatch(idx, tm)
    xs = moe_gather_norm(x, g, row_tok, n_used, tm=tm)
    hm = moe_gateup(xs, w_egate, w_eup, be, n_used, nxt, nv, tm=tm)
    ys = moe_down(hm, w_edown, be, n_used, nxt, nv, tm=tm)
    return moe_combine(x, ys, dest, gw)


def even_mixer_block(x, tabs_a, tabs_b, w_in, ga_q, ga_k, g_cq, w_uq, g_ckv, w_ukv, gb_q, gb_kn, gb_kr,
                     w_o, g_mix, *, batch, seq):
    a_width = A_HEADS * HEAD_DIM
    main_cols = 3 * a_width + Q_LORA + KV_LORA
    h = rmsnorm(x, g_mix)
    w_in_t = jnp.swapaxes(w_in, 0, 1)
    proj = matmul(h, w_in_t, n_out=main_cols, tm=2048, single_buffer_x=True, w_transposed=True, name="in_proj")
    w_kr = jnp.pad(w_in_t[main_cols:, :], ((0, LANES - ROPE_DIM), (0, 0)))
    krp = matmul(h, w_kr, w_transposed=True, name="in_proj_kr")
    o_a = mixer_a(proj, tabs_a, ga_q, ga_k, batch=batch, seq=seq)
    w3 = w_uq.reshape(Q_LORA, B_HEADS, QK_DIM)
    w_uq_p = jnp.concatenate(
        [w3[:, :, :NOPE_DIM].reshape(Q_LORA, B_HEADS * NOPE_DIM),
         jnp.pad(w3[:, :, NOPE_DIM:], ((0, 0), (0, 0), (0, LANES - ROPE_DIM))).reshape(Q_LORA, B_HEADS * LANES)],
        axis=1)
    q_up = norm_matmul(proj, g_cq, w_uq_p, width=Q_LORA, col_block=3 * a_width // Q_LORA, name="q_up")
    kv_up = norm_matmul(proj, g_ckv, w_ukv, width=KV_LORA, col_block=(3 * a_width + Q_LORA) // KV_LORA,
                        name="kv_up")
    pad_r = lambda v: jnp.pad(v, (0, LANES - ROPE_DIM)).reshape(1, LANES)
    qt, kf, vb = mla_prep(q_up, kv_up, krp, tabs_b, gb_q[:NOPE_DIM].reshape(1, LANES), pad_r(gb_q[NOPE_DIM:]),
                          gb_kn.reshape(1, LANES), pad_r(gb_kr), batch=batch, seq=seq)
    o_b = mla_attention(qt, kf, vb, batch=batch, seq=seq)
    return matmul((o_a, o_b), w_o, res=x, name="mixer_out")


def kernel(x, mem, positions, g_mem, w_mem_kv, g_mem_k, g_mix, g_x, w_xq, g_xq, w_xo, g_ffn,
           w_in, ga_q, ga_k, g_cq, w_uq, g_ckv, w_ukv, gb_q, gb_kn, gb_kr, w_o_even,
           w_gate, w_up, w_down, w_qkv, w_o_odd, w_router, b_router, w_egate, w_eup, w_edown):
    B, S, D = x.shape
    M = mem.shape[1]
    depth = g_mix.shape[0]
    T = B * S
    xt = x.reshape(T, D)
    tabs_a = _rope_tables(positions, ROT_DIM)
    tabs_b = _rope_tables(positions, ROPE_DIM)
    kv = matmul(rmsnorm(mem.reshape(B * M, D), g_mem), w_mem_kv, tm=B * M, name="mem_kv")
    mk, mv = memkv_post(kv, g_mem_k)
    for layer in range(depth):
        i = layer // 2
        if layer % 2 == 0:
            xt = even_mixer_block(xt, tabs_a, tabs_b, w_in[i], ga_q[i], ga_k[i], g_cq[i], w_uq[i], g_ckv[i],
                                  w_ukv[i], gb_q[i], gb_kn[i], gb_kr[i], w_o_even[i], g_mix[layer],
                                  batch=B, seq=S)
        else:
            h = rmsnorm(xt, g_mix[layer])
            qkv = matmul(h, w_qkv[i], out_dtype=BF16, tm=2048, single_buffer_x=True, name="qkv_proj")
            o = stickbreak_attention(qkv, batch=B, seq=S, heads=D // HEAD_DIM)
            xt = matmul(o, w_o_odd[i], res=xt, name="mixer_out")
        xattn = functools.partial(cross_attention_block, xt, mk, mv, g_x[layer], w_xq[layer], g_xq[layer],
                                  w_xo[layer], seq=S, mem_len=M)
        if layer % 2 == 0:
            xt, h = xattn(g_ffn[layer])
            hm = swiglu_gateup(h, w_gate[i], w_up[i])
            xt = matmul(hm, w_down[i], res=xt, tn=1024, tk=2048, name="swiglu_down")
        else:
            xt = moe_block(xattn(), g_ffn[layer], w_router[i], b_router[i], w_egate[i], w_eup[i], w_edown[i])
    return xt.reshape(B, S, D)
```

```python
import functools

import jax
import jax.numpy as jnp
from jax import lax
from jax.experimental import pallas as pl
from jax.experimental.pallas import tpu as pltpu

F32 = jnp.float32
BF16 = jnp.bfloat16

HEAD_DIM = 128
ROT_DIM = HEAD_DIM // 4
ROPE_THETA = 500000.0
BLOCK = 128
NEG_INF = -1e30
EPS = 1e-6
DILATED_PAIRS = ((128, 1), (512, 4), (2048, 16))
A_HEADS = 16
B_HEADS = 16
Q_LORA = 1536
KV_LORA = 512
NOPE_DIM = 128
ROPE_DIM = 64
QK_DIM = NOPE_DIM + ROPE_DIM
X_HEADS = 4
N_EXPERTS = 8
TOP_K = 2
MIXER_A_GROUP = 8
MOE_ROW_STEP = 128
ROW_FETCH_SLOTS = 3
LOG2_E = 1.4426950408889634
MLA_Q_SCALE = QK_DIM ** -0.5 * LOG2_E

LANES = 128
V7X_VMEM_BYTES = 64 * 1024 * 1024
VMEM_CAP = V7X_VMEM_BYTES - 6 * 1024 * 1024


def _params(semantics, vmem_bytes):
    return pltpu.CompilerParams(dimension_semantics=semantics,
                                vmem_limit_bytes=int(min(max(vmem_bytes, 32 * 1024 * 1024), VMEM_CAP)))


def _nbytes(shape, dtype):
    n = 1
    for s in shape:
        n *= s
    return n * jnp.dtype(dtype).itemsize


def _rmsnorm_kernel(x_ref, g_ref, o_ref):
    x = x_ref[...]
    ms = jnp.mean(x * x, axis=-1, keepdims=True)
    o_ref[...] = (x * lax.rsqrt(ms + EPS) * g_ref[...]).astype(o_ref.dtype)


def rmsnorm(x, g, *, tm=256):
    T, width = x.shape
    tm = min(tm, T)
    return pl.pallas_call(
        _rmsnorm_kernel,
        grid=(T // tm,),
        in_specs=[pl.BlockSpec((tm, width), lambda i: (i, 0)),
                  pl.BlockSpec((1, width), lambda i: (0, 0))],
        out_specs=pl.BlockSpec((tm, width), lambda i: (i, 0)),
        out_shape=jax.ShapeDtypeStruct((T, width), BF16),
        compiler_params=_params(("parallel",), 6 * _nbytes((tm, width), F32)),
        name="rmsnorm",
    )(x, g.reshape(1, width))


def _mm_kernel(*refs, nk, n_x, has_res, w_transposed):
    x_refs = refs[:n_x]
    w_ref = refs[n_x]
    rest = refs[n_x + 1:]
    res_ref = rest[0] if has_res else None
    o_ref = rest[1] if has_res else rest[0]
    acc_ref = rest[-1] if nk > 1 else None
    k = pl.program_id(2)
    w = w_ref[...].astype(BF16)

    def finish(part):
        out = part
        if has_res:
            out = out + res_ref[...]
        o_ref[...] = out.astype(o_ref.dtype)

    if n_x == 2:
        half = x_refs[0].shape[1]
        finish(jnp.dot(x_refs[0][...], w[:half], preferred_element_type=F32)
               + jnp.dot(x_refs[1][...], w[half:], preferred_element_type=F32))
    elif nk == 1 and w_transposed:
        finish(lax.dot_general(x_refs[0][...], w, (((1,), (1,)), ((), ())), preferred_element_type=F32))
    elif nk == 1:
        finish(jnp.dot(x_refs[0][...], w, preferred_element_type=F32))
    else:
        part = jnp.dot(x_refs[0][...], w, preferred_element_type=F32)

        @pl.when(k == 0)
        def _():
            acc_ref[...] = part

        @pl.when((k > 0) & (k < nk - 1))
        def _():
            acc_ref[...] += part

        @pl.when(k == nk - 1)
        def _():
            finish(acc_ref[...] + part)


def matmul(xs, w, *, n_out=None, res=None, out_dtype=F32, tm=1024, tn=512, tk=None, single_buffer_x=False,
           w_transposed=False, name="matmul"):
    if not isinstance(xs, (tuple, list)):
        xs = (xs,)
    n_x = len(xs)
    T = xs[0].shape[0]
    K = sum(x.shape[1] for x in xs)
    n_out = (w.shape[0] if w_transposed else w.shape[1]) if n_out is None else n_out
    tm = min(tm, T)
    tn = min(tn, n_out)
    if n_x == 2:
        assert tk is None and xs[1].shape[1] == xs[0].shape[1]
    tk = K if tk is None else tk
    nk = K // tk
    assert K == nk * tk and T % tm == 0 and n_out % tn == 0
    x_buffers = 2
    if n_x == 2:
        x_specs = [pl.BlockSpec((tm, K // 2), lambda i, j, k: (i, 0)) for _ in xs]
    elif nk == 1 and single_buffer_x:
        x_buffers = 1
        x_specs = [pl.BlockSpec((tm, tk), lambda i, j, k: (i, k), pipeline_mode=pl.Buffered(1))]
    else:
        x_specs = [pl.BlockSpec((tm, tk), lambda i, j, k: (i, k))]
    if w_transposed:
        assert n_x == 1 and nk == 1
        in_specs = x_specs + [pl.BlockSpec((tn, tk), lambda i, j, k: (j, k))]
    else:
        in_specs = x_specs + [pl.BlockSpec((tk, tn), lambda i, j, k: (k, j))]
    args = list(xs) + [w]
    if res is not None:
        in_specs.append(pl.BlockSpec((tm, tn), lambda i, j, k: (i, j)))
        args.append(res)
    scratch = [pltpu.VMEM((tm, tn), F32)] if nk > 1 else []
    vmem = (x_buffers * _nbytes((tm, tk), xs[0].dtype) + 2 * _nbytes((tk, tn), F32) + _nbytes((tk, tn), BF16)
            + (2 * _nbytes((tm, tn), F32) if res is not None else 0)
            + 2 * _nbytes((tm, tn), out_dtype) + 2 * _nbytes((tm, tn), F32))
    return pl.pallas_call(
        functools.partial(_mm_kernel, nk=nk, n_x=n_x, has_res=res is not None, w_transposed=w_transposed),
        grid=(T // tm, n_out // tn, nk),
        in_specs=in_specs,
        out_specs=pl.BlockSpec((tm, tn), lambda i, j, k: (i, j)),
        out_shape=jax.ShapeDtypeStruct((T, n_out), out_dtype),
        scratch_shapes=scratch,
        compiler_params=_params(("parallel", "parallel", "arbitrary"), vmem),
        name=name,
    )(*args)


def _norm_mm_kernel(x_ref, g_ref, w_ref, o_ref, xn):
    @pl.when(pl.program_id(1) == 0)
    def _():
        x = x_ref[...]
        xn[...] = (x * lax.rsqrt(jnp.mean(x * x, axis=-1, keepdims=True) + EPS) * g_ref[...]).astype(xn.dtype)

    o_ref[...] = jnp.dot(xn[...], w_ref[...].astype(BF16), preferred_element_type=F32)


def norm_matmul(x, g, w, *, width, col_block, tm=1024, tn=1024, name="norm_matmul"):
    T = x.shape[0]
    N = w.shape[1]
    tm, tn = min(tm, T), min(tn, N)
    vmem = (2 * _nbytes((tm, width), F32) + _nbytes((tm, width), BF16) + 2 * _nbytes((width, tn), F32)
            + _nbytes((width, tn), BF16) + 4 * _nbytes((tm, tn), F32))
    return pl.pallas_call(
        _norm_mm_kernel,
        grid=(T // tm, N // tn),
        in_specs=[pl.BlockSpec((tm, width), lambda i, j: (i, col_block)),
                  pl.BlockSpec((1, width), lambda i, j: (0, 0)),
                  pl.BlockSpec((width, tn), lambda i, j: (0, j))],
        out_specs=pl.BlockSpec((tm, tn), lambda i, j: (i, j)),
        out_shape=jax.ShapeDtypeStruct((T, N), F32),
        scratch_shapes=[pltpu.VMEM((tm, width), BF16)],
        compiler_params=_params(("parallel", "arbitrary"), vmem),
        name=name,
    )(x, g.reshape(1, width), w)


def _rope_tables(positions, dim):
    half = dim // 2
    inv_freq = ROPE_THETA ** (-jnp.arange(0, dim, 2, dtype=F32) / dim)
    ang = positions.astype(F32)[..., None] * inv_freq
    cos, sin = jnp.cos(ang), jnp.sin(ang)
    B, S = positions.shape
    ones = jnp.ones((B, S, LANES - dim), F32)
    zeros = jnp.zeros((B, S, LANES - dim), F32)
    zh = jnp.zeros((B, S, half), F32)
    c = jnp.concatenate([cos, cos, ones], axis=-1)
    sa = jnp.concatenate([-sin, zh, zeros], axis=-1)
    sb = jnp.concatenate([zh, sin, zeros], axis=-1)
    return [t.reshape(B * S, LANES) for t in (c, sa, sb)]


def _rope_lanes(y, c, sa, sb, half):
    return y * c + pltpu.roll(y, LANES - half, 1) * sa + pltpu.roll(y, half, 1) * sb


def _mixer_a_kernel(q_ref, k_ref, v_ref, c_ref, sa_ref, sb_ref, gq_ref, gk_ref, o_ref,
                    qs, ks, *state, seq):
    scale = HEAD_DIM ** -0.5 * LOG2_E
    c, sa, sb = c_ref[...], sa_ref[...], sb_ref[...]

    def prep(x, g):
        y = x * lax.rsqrt(jnp.mean(x * x, axis=-1, keepdims=True) + EPS) * g
        return _rope_lanes(y, c, sa, sb, ROT_DIM // 2)

    qs[...] = prep(q_ref[...], gq_ref[...])
    ks[...] = prep(k_ref[...], gk_ref[...])
    qi = lax.broadcasted_iota(jnp.int32, (BLOCK, 1), 0)
    kj = lax.broadcasted_iota(jnp.int32, (1, BLOCK), 1)
    trans_b = (((1,), (1,)), ((), ()))
    ones = jnp.ones((BLOCK, HEAD_DIM), BF16)

    for bi, (window, dil) in enumerate(DILATED_PAIRS):
        acc, m_sc, l_sc = state[3 * bi:3 * bi + 3]
        n_back = window // dil
        nb = seq // dil // BLOCK
        assert n_back <= BLOCK and nb * dil * BLOCK == seq
        cur_ok = (qi >= kj) & (qi - kj <= n_back)
        prev_ok = qi + BLOCK - kj <= n_back

        def rows_at(r, n, dil=dil):
            if dil == 1:
                return pl.ds(BLOCK * n, BLOCK)
            return pl.ds(r + dil * BLOCK * n, BLOCK, stride=dil)

        blocks = [(rows_at(r, n), rows_at(r, n - 1) if n > 0 else None) for r in range(dil) for n in range(nb)]
        for first in range(0, len(blocks), MIXER_A_GROUP):
            group = blocks[first:first + MIXER_A_GROUP]
            s_cur, s_prev = [], []
            for rows, prev in group:
                qb = qs[rows, :].astype(BF16)
                s_c = lax.dot_general(qb, ks[rows, :].astype(BF16), trans_b, preferred_element_type=F32) * scale
                s_cur.append(jnp.where(cur_ok, s_c, NEG_INF))
                if prev is None:
                    s_prev.append(None)
                else:
                    s_p = lax.dot_general(qb, ks[prev, :].astype(BF16), trans_b,
                                          preferred_element_type=F32) * scale
                    s_prev.append(jnp.where(prev_ok, s_p, NEG_INF))
            stats = []
            for s_c, s_p in zip(s_cur, s_prev):
                m = jnp.max(s_c, axis=-1, keepdims=True)
                if s_p is not None:
                    m = jnp.maximum(m, jnp.max(s_p, axis=-1, keepdims=True))
                p_c = jnp.exp2(s_c - m).astype(BF16)
                p_p = None if s_p is None else jnp.exp2(s_p - m).astype(BF16)
                stats.append((m, p_c, p_p))
            for (rows, prev), (m, p_c, p_p) in zip(group, stats):
                v_ext = jnp.concatenate([v_ref[rows, :].astype(BF16), ones], axis=1)
                o = jnp.dot(p_c, v_ext, preferred_element_type=F32)
                if p_p is not None:
                    v_ext = jnp.concatenate([v_ref[prev, :].astype(BF16), ones], axis=1)
                    o = o + jnp.dot(p_p, v_ext, preferred_element_type=F32)
                acc[rows, :] = o[:, :HEAD_DIM]
                l_sc[rows, :] = o[:, HEAD_DIM:]
                m_sc[rows, :] = jnp.broadcast_to(m, (BLOCK, HEAD_DIM))

    n_br = len(DILATED_PAIRS)
    m_all = state[1][...]
    for bi in range(1, n_br):
        m_all = jnp.maximum(m_all, state[3 * bi + 1][...])
    num = jnp.zeros(o_ref.shape, F32)
    den = jnp.zeros(o_ref.shape, F32)
    for bi in range(n_br):
        w = jnp.exp2(state[3 * bi + 1][...] - m_all)
        num = num + w * state[3 * bi][...]
        den = den + w * state[3 * bi + 2][...]
    o_ref[...] = (num / den).astype(o_ref.dtype)


def mixer_a(proj, tabs, gq, gk, *, batch, seq):
    T = batch * seq
    blk = (seq, HEAD_DIM)
    head = lambda off: pl.BlockSpec(blk, lambda b, h: (b, off + h))
    tab = pl.BlockSpec(blk, lambda b, h: (b, 0))
    gspec = pl.BlockSpec((1, HEAD_DIM), lambda b, h: (0, 0))
    return pl.pallas_call(
        functools.partial(_mixer_a_kernel, seq=seq),
        grid=(batch, A_HEADS),
        in_specs=[head(0), head(A_HEADS), head(2 * A_HEADS), tab, tab, tab, gspec, gspec],
        out_specs=pl.BlockSpec(blk, lambda b, h: (b, h)),
        out_shape=jax.ShapeDtypeStruct((T, A_HEADS * HEAD_DIM), BF16),
        scratch_shapes=[pltpu.VMEM(blk, F32) for _ in range(2 + 3 * len(DILATED_PAIRS))],
        compiler_params=_params(("parallel", "parallel"), 30 * _nbytes(blk, F32)),
        name="mixer_a",
    )(proj, proj, proj, *tabs, gq.reshape(1, HEAD_DIM), gk.reshape(1, HEAD_DIM))


def _mla_prep_kernel(qn_ref, qr_ref, kn_ref, v_ref, kr_ref, c_ref, sa_ref, sb_ref,
                     gqn_ref, gqr_ref, gkn_ref, gkr_ref, qt_ref, kf_ref, vb_ref, kr_sc):
    c, sa, sb = c_ref[...], sa_ref[...], sb_ref[...]
    half = ROPE_DIM // 2

    @pl.when(pl.program_id(1) == 0)
    def _():
        kr = kr_ref[...]
        kr = kr * lax.rsqrt(jnp.sum(kr * kr, axis=-1, keepdims=True) * (1.0 / ROPE_DIM) + EPS) * gkr_ref[...]
        kr_sc[...] = _rope_lanes(kr, c, sa, sb, half).astype(kr_sc.dtype)

    qn, qr = qn_ref[...], qr_ref[...]
    ss = jnp.sum(qn * qn, axis=-1, keepdims=True) + jnp.sum(qr * qr, axis=-1, keepdims=True)
    inv = lax.rsqrt(ss * (1.0 / QK_DIM) + EPS) * MLA_Q_SCALE
    qf = jnp.concatenate([qn * inv * gqn_ref[...], _rope_lanes(qr * inv * gqr_ref[...], c, sa, sb, half)], axis=1)
    qt_ref[...] = qf.astype(qt_ref.dtype).T
    kn = kn_ref[...]
    kn = kn * lax.rsqrt(jnp.mean(kn * kn, axis=-1, keepdims=True) + EPS) * gkn_ref[...]
    kf_ref[:, :NOPE_DIM] = kn.astype(kf_ref.dtype)
    kf_ref[:, NOPE_DIM:] = kr_sc[...]
    vb_ref[...] = v_ref[...].astype(vb_ref.dtype)


def mla_prep(q_up, kv_up, krp, tabs, gqn, gqr, gkn, gkr, *, batch, seq, tm=2048):
    T = q_up.shape[0]
    H = B_HEADS
    per_seq = seq // tm
    blk = lambda f: pl.BlockSpec((tm, LANES), f)
    g = pl.BlockSpec((1, LANES), lambda i, h: (0, 0))
    return pl.pallas_call(
        _mla_prep_kernel,
        grid=(T // tm, H),
        in_specs=[blk(lambda i, h: (i, h)), blk(lambda i, h: (i, H + h)),
                  blk(lambda i, h: (i, 2 * h)), blk(lambda i, h: (i, 2 * h + 1)),
                  blk(lambda i, h: (i, 0)), blk(lambda i, h: (i, 0)), blk(lambda i, h: (i, 0)),
                  blk(lambda i, h: (i, 0)), g, g, g, g],
        out_specs=[pl.BlockSpec((None, None, 2 * LANES, tm), lambda i, h: (i // per_seq, h, 0, i % per_seq)),
                   pl.BlockSpec((tm, 2 * LANES), lambda i, h: (i, h)),
                   pl.BlockSpec((tm, LANES), lambda i, h: (i, h))],
        out_shape=[jax.ShapeDtypeStruct((batch, H, 2 * LANES, seq), BF16),
                   jax.ShapeDtypeStruct((T, H * 2 * LANES), BF16),
                   jax.ShapeDtypeStruct((T, H * LANES), BF16)],
        scratch_shapes=[pltpu.VMEM((tm, LANES), BF16)],
        compiler_params=_params(("parallel", "arbitrary"), 40 * _nbytes((tm, LANES), F32)),
        name="mla_prep",
    )(q_up, q_up, kv_up, kv_up, krp, *tabs, gqn, gqr, gkn, gkr)


def _causal_attn_kernel(qt_ref, k_ref, v_ref, o_ref, s_sc, acc, *, tq, hg, dq, dv):
    qi = pl.program_id(2)
    key = lax.broadcasted_iota(jnp.int32, (tq, tq), 0)
    qry = lax.broadcasted_iota(jnp.int32, (tq, tq), 1)
    visible = key <= qry
    ones = jnp.ones((tq, dv), BF16)
    contract_rows = (((0,), (0,)), ((), ()))

    heads = range(hg)

    def scores(j, diagonal):
        rows = pl.ds(pl.multiple_of(j * tq, tq), tq)
        s = [jnp.dot(k_ref[rows, g * dq:(g + 1) * dq], qt_ref[g], preferred_element_type=F32) for g in heads]
        if diagonal:
            s = [jnp.where(visible, sg, NEG_INF) for sg in s]
        for g in heads:
            s_sc[g, j] = s[g]
        return s

    def col_max(j, ms):
        s = scores(j, False)
        return tuple(jnp.maximum(ms[g], jnp.max(s[g], axis=0, keepdims=True)) for g in heads)

    ms = lax.fori_loop(0, qi, col_max, tuple(jnp.max(sg, axis=0, keepdims=True) for sg in scores(qi, True)))

    def weighted(j):
        rows = pl.ds(pl.multiple_of(j * tq, tq), tq)
        p = [jnp.exp2(s_sc[g, j] - ms[g]).astype(BF16) for g in heads]
        return [lax.dot_general(p[g], jnp.concatenate([v_ref[rows, g * dv:(g + 1) * dv], ones], axis=1),
                                contract_rows, preferred_element_type=F32) for g in heads]

    for g, w in enumerate(weighted(qi)):
        acc[g] = w

    def below(j, carry):
        for g, w in enumerate(weighted(j)):
            acc[g] += w
        return carry

    lax.fori_loop(0, qi, below, 0)
    for g in heads:
        a = acc[g]
        o_ref[:, g * dv:(g + 1) * dv] = (a[:, :dv] / a[:, dv:]).astype(o_ref.dtype)


def mla_attention(qt, kf, vb, *, batch, seq, tq=512, hg=4):
    T = batch * seq
    nq = seq // tq
    dq = qt.shape[2]
    dv = vb.shape[1] // B_HEADS
    assert B_HEADS % hg == 0
    return pl.pallas_call(
        functools.partial(_causal_attn_kernel, tq=tq, hg=hg, dq=dq, dv=dv),
        grid=(batch, B_HEADS // hg, nq),
        in_specs=[pl.BlockSpec((None, hg, dq, tq), lambda b, h, i: (b, h, 0, i)),
                  pl.BlockSpec((seq, hg * dq), lambda b, h, i: (b, h)),
                  pl.BlockSpec((seq, hg * dv), lambda b, h, i: (b, h))],
        out_specs=pl.BlockSpec((tq, hg * dv), lambda b, h, i: (b * nq + i, h)),
        out_shape=jax.ShapeDtypeStruct((T, B_HEADS * dv), BF16),
        scratch_shapes=[pltpu.VMEM((hg, nq, tq, tq), F32), pltpu.VMEM((hg, tq, 2 * dv), F32)],
        compiler_params=_params(("parallel", "parallel", "parallel"),
                                _nbytes((hg, nq, tq, tq), F32) + 2 * _nbytes((seq, hg * (dq + dv)), BF16)
                                + 8 * hg * _nbytes((tq, tq), F32)),
        name="mla_attention",
    )(qt, kf, vb)


STICK_EXHAUSTED_LOG = -110.0


def _stickbreak_kernel(q_ref, k_ref, v_ref, o_ref, qt_sc, c_sc, acc, *, tq, hg, scale):
    qi = pl.program_id(2)
    c_sc[...] = jnp.zeros(c_sc.shape, F32)
    acc[...] = jnp.zeros(acc.shape, F32)
    for g in range(hg):
        qt_sc[g] = q_ref[:, g * HEAD_DIM:(g + 1) * HEAD_DIM].T
    key = lax.broadcasted_iota(jnp.int32, (tq, tq), 0)
    qry = lax.broadcasted_iota(jnp.int32, (tq, tq), 1)
    after = (qry > key).astype(BF16)
    earlier = key < qry
    contract_rows = (((0,), (0,)), ((), ()))

    def chunk(j, diagonal):
        rows = pl.ds(pl.multiple_of(j * tq, tq), tq)
        heads = range(hg)
        cols = [slice(g * HEAD_DIM, (g + 1) * HEAD_DIM) for g in heads]
        z = [jnp.dot(k_ref[rows, cols[g]], qt_sc[g], preferred_element_type=F32) * (scale * LOG2_E) for g in heads]
        log_beta, parts = [], []
        for g in heads:
            softplus = jnp.maximum(z[g], 0.0) + jnp.log2(1.0 + jnp.exp2(-jnp.abs(z[g])))
            log_keep = -softplus
            if diagonal:
                log_keep = jnp.where(earlier, log_keep, 0.0)
            parts.append(log_keep.astype(BF16))
            log_beta.append(z[g] - softplus + c_sc[g])
            c_sc[g] += jnp.sum(log_keep, axis=0, keepdims=True)
        within = jnp.dot(after, jnp.concatenate(parts, axis=1), preferred_element_type=F32)
        for g in heads:
            a = jnp.exp2(log_beta[g] + within[:, g * tq:(g + 1) * tq])
            if diagonal:
                a = jnp.where(earlier, a, 0.0)
            acc[:, cols[g]] += lax.dot_general(a.astype(BF16), v_ref[rows, cols[g]], contract_rows,
                                               preferred_element_type=F32)

    def stick_left():
        return jnp.max(c_sc[...]) > STICK_EXHAUSTED_LOG * LOG2_E

    chunk(qi, True)

    def more(state):
        j, left = state
        return (j >= 0) & left

    def step(state):
        j, _ = state
        chunk(j, False)
        return j - 1, stick_left()

    lax.while_loop(more, step, (qi - 1, stick_left()))
    o_ref[...] = acc[...].astype(o_ref.dtype)


def stickbreak_attention(qkv, *, batch, seq, heads, tq=256, hg=8):
    T = batch * seq
    nq = seq // tq
    ng = heads // hg
    assert heads == ng * hg
    blk = hg * HEAD_DIM
    return pl.pallas_call(
        functools.partial(_stickbreak_kernel, tq=tq, hg=hg, scale=HEAD_DIM ** -0.5),
        grid=(batch, ng, nq),
        in_specs=[pl.BlockSpec((tq, blk), lambda b, h, i: (b * nq + i, h)),
                  pl.BlockSpec((seq, blk), lambda b, h, i: (b, ng + h)),
                  pl.BlockSpec((seq, blk), lambda b, h, i: (b, 2 * ng + h))],
        out_specs=pl.BlockSpec((tq, blk), lambda b, h, i: (b * nq + i, h)),
        out_shape=jax.ShapeDtypeStruct((T, heads * HEAD_DIM), BF16),
        scratch_shapes=[pltpu.VMEM((hg, HEAD_DIM, tq), BF16), pltpu.VMEM((hg, 1, tq), F32),
                        pltpu.VMEM((tq, blk), F32)],
        compiler_params=_params(("parallel", "parallel", "parallel"), 0),
        name="stickbreak",
    )(qkv, qkv, qkv)


def _memkv_kernel(kv_ref, g_ref, mk_ref, mv_ref):
    for h in range(X_HEADS):
        cols = slice(h * HEAD_DIM, (h + 1) * HEAD_DIM)
        k = kv_ref[:, cols]
        k = k * lax.rsqrt(jnp.mean(k * k, axis=-1, keepdims=True) + EPS) * g_ref[...]
        mk_ref[:, cols] = k.astype(mk_ref.dtype)
    mv_ref[...] = kv_ref[:, X_HEADS * HEAD_DIM:].astype(mv_ref.dtype)


def memkv_post(kv, g):
    M = kv.shape[0]
    xd = X_HEADS * HEAD_DIM
    return pl.pallas_call(
        _memkv_kernel,
        grid=(1,),
        in_specs=[pl.BlockSpec((M, 2 * xd), lambda i: (0, 0)), pl.BlockSpec((1, HEAD_DIM), lambda i: (0, 0))],
        out_specs=[pl.BlockSpec((M, xd), lambda i: (0, 0)), pl.BlockSpec((M, xd), lambda i: (0, 0))],
        out_shape=[jax.ShapeDtypeStruct((M, xd), BF16), jax.ShapeDtypeStruct((M, xd), BF16)],
        compiler_params=_params(("arbitrary",), 0),
        name="memkv_post",
    )(kv, g.reshape(1, HEAD_DIM))


def _xattn_block_kernel(x_ref, gx_ref, wq_ref, gq_ref, mk_ref, mv_ref, wo_ref, *rest, emit_norm):
    if emit_norm:
        gn_ref, o_ref, hn_ref = rest
    else:
        (o_ref,) = rest
    scale = HEAD_DIM ** -0.5
    x = x_ref[...]
    h = (x * lax.rsqrt(jnp.mean(x * x, axis=-1, keepdims=True) + EPS) * gx_ref[...]).astype(BF16)
    q_all = jnp.dot(h, wq_ref[...], preferred_element_type=F32)
    heads = []
    for hd in range(X_HEADS):
        cols = slice(hd * HEAD_DIM, (hd + 1) * HEAD_DIM)
        q = q_all[:, cols]
        q = (q * lax.rsqrt(jnp.mean(q * q, axis=-1, keepdims=True) + EPS) * gq_ref[...]).astype(BF16)
        s = lax.dot_general(q, mk_ref[:, cols], (((1,), (1,)), ((), ())), preferred_element_type=F32) * scale
        p = jnp.exp(s - jnp.max(s, axis=-1, keepdims=True))
        p = p / jnp.sum(p, axis=-1, keepdims=True)
        heads.append(jnp.dot(p.astype(BF16), mv_ref[:, cols], preferred_element_type=F32).astype(BF16))
    y = x + jnp.dot(jnp.concatenate(heads, axis=1), wo_ref[...], preferred_element_type=F32)
    o_ref[...] = y
    if emit_norm:
        hn_ref[...] = (y * lax.rsqrt(jnp.mean(y * y, axis=-1, keepdims=True) + EPS) * gn_ref[...]).astype(hn_ref.dtype)


def cross_attention_block(x, mk, mv, g_x, w_xq, g_xq, w_xo, g_next=None, *, seq, mem_len, tm=256):
    T, D = x.shape
    xd = X_HEADS * HEAD_DIM
    per_seq = seq // tm
    emit_norm = g_next is not None
    row = lambda w: pl.BlockSpec((1, w), lambda i: (0, 0))
    tok = pl.BlockSpec((tm, D), lambda i: (i, 0))
    in_specs = [tok, row(D), pl.BlockSpec((D, xd), lambda i: (0, 0)), row(HEAD_DIM),
                pl.BlockSpec((mem_len, xd), lambda i: (i // per_seq, 0)),
                pl.BlockSpec((mem_len, xd), lambda i: (i // per_seq, 0)),
                pl.BlockSpec((xd, D), lambda i: (0, 0))]
    args = [x, g_x.reshape(1, D), w_xq.astype(BF16), g_xq.reshape(1, HEAD_DIM), mk, mv, w_xo.astype(BF16)]
    out_specs, out_shape = tok, jax.ShapeDtypeStruct((T, D), F32)
    if emit_norm:
        in_specs.append(row(D))
        args.append(g_next.reshape(1, D))
        out_specs = [tok, tok]
        out_shape = [out_shape, jax.ShapeDtypeStruct((T, D), BF16)]
    vmem = 10 * _nbytes((tm, D), F32) + 4 * _nbytes((D, xd), BF16)
    return pl.pallas_call(
        functools.partial(_xattn_block_kernel, emit_norm=emit_norm),
        grid=(T // tm,),
        in_specs=in_specs,
        out_specs=out_specs,
        out_shape=out_shape,
        compiler_params=_params(("parallel",), vmem),
        name="cross_attention",
    )(*args)


def _gateup_kernel(x_ref, wg_ref, wu_ref, o_ref):
    x = x_ref[...]
    g = jnp.dot(x, wg_ref[...].astype(BF16), preferred_element_type=F32)
    u = jnp.dot(x, wu_ref[...].astype(BF16), preferred_element_type=F32)
    o_ref[...] = (g * jax.nn.sigmoid(g) * u).astype(o_ref.dtype)


def swiglu_gateup(x, wg, wu, *, tm=2048, tf=256):
    T, K = x.shape
    F = wg.shape[1]
    vmem = (_nbytes((tm, K), BF16) + 4 * _nbytes((K, tf), F32) + 2 * _nbytes((K, tf), BF16)
            + 6 * _nbytes((tm, tf), F32))
    return pl.pallas_call(
        _gateup_kernel,
        grid=(T // tm, F // tf),
        in_specs=[pl.BlockSpec((tm, K), lambda i, j: (i, 0), pipeline_mode=pl.Buffered(1)),
                  pl.BlockSpec((K, tf), lambda i, j: (0, j)),
                  pl.BlockSpec((K, tf), lambda i, j: (0, j))],
        out_specs=pl.BlockSpec((tm, tf), lambda i, j: (i, j)),
        out_shape=jax.ShapeDtypeStruct((T, F), BF16),
        compiler_params=_params(("parallel", "parallel"), vmem),
        name="swiglu_gateup",
    )(x, wg, wu)


def _router_kernel(x_ref, g_ref, w_ref, b_ref, idx_ref, gw_ref):
    x = x_ref[...]
    h = x * lax.rsqrt(jnp.mean(x * x, axis=-1, keepdims=True) + EPS) * g_ref[...]
    logits = jnp.dot(h, w_ref[...], preferred_element_type=F32, precision=lax.Precision.HIGHEST) + b_ref[...]
    lane = lax.broadcasted_iota(jnp.int32, logits.shape, 1)
    logits = jnp.where(lane < N_EXPERTS, logits, -jnp.inf)
    m1 = jnp.max(logits, axis=-1, keepdims=True)
    i1 = jnp.min(jnp.where(logits == m1, lane, LANES), axis=-1, keepdims=True)
    rest = jnp.where(lane == i1, -jnp.inf, logits)
    m2 = jnp.max(rest, axis=-1, keepdims=True)
    i2 = jnp.min(jnp.where(rest == m2, lane, LANES), axis=-1, keepdims=True)
    e = jnp.exp(m2 - m1)
    w1 = 1.0 / (1.0 + e)
    w2 = e / (1.0 + e)
    idx_ref[...] = jnp.where(lane == 0, i1, jnp.where(lane == 1, i2, 0))
    gw_ref[...] = jnp.where(lane == 0, w1, jnp.where(lane == 1, w2, 0.0))


def moe_router(x, g, w_router, b_router, *, tm=256):
    T, D = x.shape
    E = w_router.shape[1]
    w = jnp.pad(w_router, ((0, 0), (0, LANES - E)))
    b = jnp.pad(b_router, (0, LANES - E)).reshape(1, LANES)
    idx, gw = pl.pallas_call(
        _router_kernel,
        grid=(T // tm,),
        in_specs=[pl.BlockSpec((tm, D), lambda i: (i, 0)), pl.BlockSpec((1, D), lambda i: (0, 0)),
                  pl.BlockSpec((D, LANES), lambda i: (0, 0)), pl.BlockSpec((1, LANES), lambda i: (0, 0))],
        out_specs=[pl.BlockSpec((tm, LANES), lambda i: (i, 0)), pl.BlockSpec((tm, LANES), lambda i: (i, 0))],
        out_shape=[jax.ShapeDtypeStruct((T, LANES), jnp.int32), jax.ShapeDtypeStruct((T, LANES), F32)],
        compiler_params=_params(("parallel",), 8 * _nbytes((tm, D), F32)),
        name="moe_router",
    )(x, g.reshape(1, D), w, b)
    return idx[:, :TOP_K], gw[:, :TOP_K]


def _moe_dispatch(idx, tm):
    T = idx.shape[0]
    A = T * TOP_K
    e_flat = idx.reshape(A)
    onehot = (e_flat[:, None] == jnp.arange(N_EXPERTS, dtype=jnp.int32)[None, :]).astype(jnp.int32)
    csum = jnp.cumsum(onehot, axis=0)
    pos_in = jnp.sum(csum * onehot, axis=1) - 1
    counts = csum[-1]
    padded = ((counts + tm - 1) // tm) * tm
    gend = jnp.cumsum(padded)
    gstart = gend - padded
    dest = (jnp.sum(onehot * gstart[None, :], axis=1) + pos_in).astype(jnp.int32)
    n_rows = A + N_EXPERTS * tm
    row_tok = jnp.zeros((n_rows,), jnp.int32).at[dest].set(jnp.arange(A, dtype=jnp.int32) // TOP_K)
    nb = n_rows // tm
    n_used = (gend[-1] // tm).astype(jnp.int32)
    blk = jnp.arange(nb, dtype=jnp.int32)
    be = jnp.sum((blk[:, None] * tm >= gend[None, :]).astype(jnp.int32), axis=1)
    be = jnp.minimum(be, N_EXPERTS - 1)
    be = jnp.where(blk < n_used, be, be[jnp.maximum(n_used - 1, 0)]).astype(jnp.int32)
    eid = jnp.arange(N_EXPERTS, dtype=jnp.int32)
    later = (counts[None, :] > 0) & (eid[None, :] > eid[:, None])
    nxt_of = jnp.min(jnp.where(later, eid[None, :], N_EXPERTS), axis=1)
    nxt_of = jnp.where(nxt_of == N_EXPERTS, -1, nxt_of).astype(jnp.int32)
    mine = be[:, None] == eid[None, :]
    nxt = jnp.sum(mine * nxt_of[None, :], axis=1).astype(jnp.int32)
    group_end_row = jnp.sum(mine * (gstart + counts)[None, :], axis=1)
    nv = jnp.where(blk < n_used, jnp.clip(group_end_row - blk * tm, 0, tm), 0).astype(jnp.int32)
    return dest, row_tok, be, n_used.reshape(1), nxt, nv


def _moe_gather_kernel(tok_ref, nu_ref, x_hbm, g_ref, o_ref, buf, sem, *, rows):
    i = pl.program_id(0)
    n_used = nu_ref[0]
    ahead = ROW_FETCH_SLOTS - 1

    def row_copy(slot, r, src_row):
        return pltpu.make_async_copy(x_hbm.at[pl.ds(src_row, 1)], buf.at[slot, pl.ds(r, 1)], sem.at[slot])

    def fetch(block):
        slot = block % ROW_FETCH_SLOTS

        def issue(r, c):
            row_copy(slot, r, tok_ref[block * rows + r]).start()
            return c

        lax.fori_loop(0, rows, issue, 0, unroll=8)

    for first in range(ahead):
        @pl.when((i == 0) & (first < n_used))
        def _(first=first):
            fetch(first)

    @pl.when(i + ahead < n_used)
    def _():
        fetch(i + ahead)

    @pl.when(i < n_used)
    def _():
        slot = i % ROW_FETCH_SLOTS

        def drain(r, c):
            row_copy(slot, r, 0).wait()
            return c

        lax.fori_loop(0, rows, drain, 0, unroll=8)
        x = buf[slot]
        o_ref[...] = (x * lax.rsqrt(jnp.mean(x * x, axis=-1, keepdims=True) + EPS) * g_ref[...]).astype(o_ref.dtype)

    @pl.when(i >= n_used)
    def _():
        o_ref[...] = jnp.zeros(o_ref.shape, o_ref.dtype)


def moe_gather_norm(x, g, row_tok, n_used, *, tm, rows=256):
    T, D = x.shape
    n_rows = row_tok.shape[0]
    per = tm // rows
    return pl.pallas_call(
        functools.partial(_moe_gather_kernel, rows=rows),
        grid_spec=pltpu.PrefetchScalarGridSpec(
            num_scalar_prefetch=2,
            grid=(n_rows // rows,),
            in_specs=[pl.BlockSpec(memory_space=pl.ANY), pl.BlockSpec((1, D), lambda i, tok, nu: (0, 0))],
            out_specs=pl.BlockSpec((rows, D), lambda i, tok, nu: (i, 0)),
            scratch_shapes=[pltpu.VMEM((ROW_FETCH_SLOTS, rows, D), F32),
                            pltpu.SemaphoreType.DMA((ROW_FETCH_SLOTS,))]),
        out_shape=jax.ShapeDtypeStruct((n_rows, D), BF16),
        compiler_params=_params(("arbitrary",), (ROW_FETCH_SLOTS + 8) * _nbytes((rows, D), F32)),
        name="moe_gather",
    )(row_tok, n_used * per, x, g.reshape(1, D))


def _stream_group_weights(be_ref, nu_ref, nxt_ref, w_hbms, stage, sem, w_bf16, *, tile, n_tiles):
    j = pl.program_id(0)
    i = pl.program_id(1)

    def copies(e, jj):
        cols = pl.ds(pl.multiple_of(jj * tile, tile), tile)
        return [pltpu.make_async_copy(w.at[e, :, cols], stage.at[n], sem.at[n]) for n, w in enumerate(w_hbms)]

    @pl.when((j == 0) & (i == 0))
    def _():
        for c in copies(be_ref[0], 0):
            c.start()

    first_of_group = (i < nu_ref[0]) & ((i == 0) | (be_ref[i] != be_ref[jnp.maximum(i - 1, 0)]))

    @pl.when(first_of_group)
    def _():
        for c in copies(be_ref[i], j):
            c.wait()
        for n, dst in enumerate(w_bf16):
            dst[...] = stage[n].astype(BF16)
        more_in_sweep = nxt_ref[i] >= 0
        e_next = jnp.where(more_in_sweep, nxt_ref[i], be_ref[0])
        j_next = jnp.where(more_in_sweep, j, j + 1)

        @pl.when(more_in_sweep | (j + 1 < n_tiles))
        def _():
            for c in copies(e_next, j_next):
                c.start()


def _for_valid_rows(nv, o_ref, compute):
    tm = o_ref.shape[0]
    for rows in range(MOE_ROW_STEP, tm + 1, MOE_ROW_STEP):
        @pl.when((nv > rows - MOE_ROW_STEP) & (nv <= rows))
        def _(rows=rows):
            o_ref[:rows, :] = compute(rows).astype(o_ref.dtype)
            if rows < tm:
                o_ref[rows:, :] = jnp.zeros((tm - rows, o_ref.shape[1]), o_ref.dtype)

    @pl.when(nv == 0)
    def _():
        o_ref[...] = jnp.zeros(o_ref.shape, o_ref.dtype)


def _moe_gateup_kernel(be_ref, nu_ref, nxt_ref, nv_ref, x_ref, wg_hbm, wu_hbm, o_ref, stage, sem, wgb, wub,
                       *, tf, n_tiles):
    _stream_group_weights(be_ref, nu_ref, nxt_ref, (wg_hbm, wu_hbm), stage, sem, (wgb, wub), tile=tf, n_tiles=n_tiles)

    def compute(rows):
        x = x_ref[:rows, :]
        g = jnp.dot(x, wgb[...], preferred_element_type=F32)
        u = jnp.dot(x, wub[...], preferred_element_type=F32)
        return g * jax.nn.sigmoid(g) * u

    _for_valid_rows(nv_ref[pl.program_id(1)], o_ref, compute)


def moe_gateup(xs, wg, wu, be, n_used, nxt, nv, *, tm, tf=512):
    n_rows, K = xs.shape
    F = wg.shape[2]
    tf = min(tf, F)
    nb = n_rows // tm
    xmap = lambda j, i, be, nu, *_: (jnp.minimum(i, nu[0] - 1), 0)
    hbm = pl.BlockSpec(memory_space=pl.ANY)
    vmem = (2 * _nbytes((tm, K), BF16) + 2 * _nbytes((K, tf), F32) + 4 * _nbytes((K, tf), BF16)
            + 6 * _nbytes((tm, tf), F32))
    return pl.pallas_call(
        functools.partial(_moe_gateup_kernel, tf=tf, n_tiles=F // tf),
        grid_spec=pltpu.PrefetchScalarGridSpec(
            num_scalar_prefetch=4,
            grid=(F // tf, nb),
            in_specs=[pl.BlockSpec((tm, K), xmap), hbm, hbm],
            out_specs=pl.BlockSpec((tm, tf), lambda j, i, *_: (i, j)),
            scratch_shapes=[pltpu.VMEM((2, K, tf), F32), pltpu.SemaphoreType.DMA((2,)),
                            pltpu.VMEM((K, tf), BF16), pltpu.VMEM((K, tf), BF16)]),
        out_shape=jax.ShapeDtypeStruct((n_rows, F), BF16),
        compiler_params=_params(("arbitrary", "arbitrary"), vmem),
        name="moe_gateup",
    )(be, n_used, nxt, nv, xs, wg, wu)


def _moe_down_kernel(be_ref, nu_ref, nxt_ref, nv_ref, x_ref, w_hbm, o_ref, stage, sem, wb, *, tn, n_tiles):
    _stream_group_weights(be_ref, nu_ref, nxt_ref, (w_hbm,), stage, sem, (wb,), tile=tn, n_tiles=n_tiles)
    _for_valid_rows(nv_ref[pl.program_id(1)], o_ref,
                    lambda rows: jnp.dot(x_ref[:rows, :], wb[...], preferred_element_type=F32))


def moe_down(hm, wd, be, n_used, nxt, nv, *, tm, tn=1024):
    n_rows, K = hm.shape
    N = wd.shape[2]
    tn = min(tn, N)
    nb = n_rows // tm
    xmap = lambda j, i, be, nu, *_: (jnp.minimum(i, nu[0] - 1), 0)
    vmem = (2 * _nbytes((tm, K), BF16) + _nbytes((K, tn), F32) + 2 * _nbytes((K, tn), BF16)
            + 4 * _nbytes((tm, tn), F32))
    return pl.pallas_call(
        functools.partial(_moe_down_kernel, tn=tn, n_tiles=N // tn),
        grid_spec=pltpu.PrefetchScalarGridSpec(
            num_scalar_prefetch=4,
            grid=(N // tn, nb),
            in_specs=[pl.BlockSpec((tm, K), xmap), pl.BlockSpec(memory_space=pl.ANY)],
            out_specs=pl.BlockSpec((tm, tn), lambda j, i, *_: (i, j)),
            scratch_shapes=[pltpu.VMEM((1, K, tn), F32), pltpu.SemaphoreType.DMA((1,)),
                            pltpu.VMEM((K, tn), BF16)]),
        out_shape=jax.ShapeDtypeStruct((n_rows, N), F32),
        compiler_params=_params(("arbitrary", "arbitrary"), vmem),
        name="moe_down",
    )(be, n_used, nxt, nv, hm, wd)


def _moe_combine_kernel(pos_ref, x_ref, gw_ref, ys_hbm, o_ref, buf, sem, *, rows):
    i = pl.program_id(0)
    n_blocks = pl.num_programs(0)
    ahead = ROW_FETCH_SLOTS - 1

    def row_copy(slot, k, r, src_row):
        return pltpu.make_async_copy(ys_hbm.at[pl.ds(src_row, 1)], buf.at[slot, k, pl.ds(r, 1)], sem.at[slot])

    def fetch(block):
        slot = block % ROW_FETCH_SLOTS

        def issue(r, c):
            a = (block * rows + r) * TOP_K
            for k in range(TOP_K):
                row_copy(slot, k, r, pos_ref[a + k]).start()
            return c

        lax.fori_loop(0, rows, issue, 0, unroll=4)

    for first in range(ahead):
        @pl.when((i == 0) & (first < n_blocks))
        def _(first=first):
            fetch(first)

    @pl.when(i + ahead < n_blocks)
    def _():
        fetch(i + ahead)

    slot = i % ROW_FETCH_SLOTS

    def drain(r, c):
        for k in range(TOP_K):
            row_copy(slot, k, r, 0).wait()
        return c

    lax.fori_loop(0, rows, drain, 0, unroll=4)
    out = x_ref[...]
    for k in range(TOP_K):
        out = out + gw_ref[:, k:k + 1] * buf[slot, k]
    o_ref[...] = out


def moe_combine(x, ys, dest, gw, *, rows=256):
    T, D = x.shape
    return pl.pallas_call(
        functools.partial(_moe_combine_kernel, rows=rows),
        grid_spec=pltpu.PrefetchScalarGridSpec(
            num_scalar_prefetch=1,
            grid=(T // rows,),
            in_specs=[pl.BlockSpec((rows, D), lambda i, pos: (i, 0)),
                      pl.BlockSpec((rows, TOP_K), lambda i, pos: (i, 0)),
                      pl.BlockSpec(memory_space=pl.ANY)],
            out_specs=pl.BlockSpec((rows, D), lambda i, pos: (i, 0)),
            scratch_shapes=[pltpu.VMEM((ROW_FETCH_SLOTS, TOP_K, rows, D), F32),
                            pltpu.SemaphoreType.DMA((ROW_FETCH_SLOTS,))]),
        out_shape=jax.ShapeDtypeStruct((T, D), F32),
        compiler_params=_params(("arbitrary",), (ROW_FETCH_SLOTS * TOP_K + 8) * _nbytes((rows, D), F32)),
        name="moe_combine",
    )(dest, x, gw, ys)


def moe_block(x, g, w_router, b_router, w_egate, w_eup, w_edown, *, tm=512):
    idx, gw = moe_router(x, g, w_router, b_router)
    dest, row_tok, be, n_used, nxt, nv = _moe_dispatch(idx, tm)
    xs = moe_gather_norm(x, g, row_tok, n_used, tm=tm)
    hm = moe_gateup(xs, w_egate, w_eup, be, n_used, nxt, nv, tm=tm)
    ys = moe_down(hm, w_edown, be, n_used, nxt, nv, tm=tm)
    return moe_combine(x, ys, dest, gw)


def even_mixer_block(x, tabs_a, tabs_b, w_in, ga_q, ga_k, g_cq, w_uq, g_ckv, w_ukv, gb_q, gb_kn, gb_kr,
                     w_o, g_mix, *, batch, seq):
    a_width = A_HEADS * HEAD_DIM
    main_cols = 3 * a_width + Q_LORA + KV_LORA
    h = rmsnorm(x, g_mix)
    w_in_t = jnp.swapaxes(w_in, 0, 1)
    proj = matmul(h, w_in_t, n_out=main_cols, tm=2048, single_buffer_x=True, w_transposed=True, name="in_proj")
    w_kr = jnp.pad(w_in_t[main_cols:, :], ((0, LANES - ROPE_DIM), (0, 0)))
    krp = matmul(h, w_kr, w_transposed=True, name="in_proj_kr")
    o_a = mixer_a(proj, tabs_a, ga_q, ga_k, batch=batch, seq=seq)
    w3 = w_uq.reshape(Q_LORA, B_HEADS, QK_DIM)
    w_uq_p = jnp.concatenate(
        [w3[:, :, :NOPE_DIM].reshape(Q_LORA, B_HEADS * NOPE_DIM),
         jnp.pad(w3[:, :, NOPE_DIM:], ((0, 0), (0, 0), (0, LANES - ROPE_DIM))).reshape(Q_LORA, B_HEADS * LANES)],
        axis=1)
    q_up = norm_matmul(proj, g_cq, w_uq_p, width=Q_LORA, col_block=3 * a_width // Q_LORA, name="q_up")
    kv_up = norm_matmul(proj, g_ckv, w_ukv, width=KV_LORA, col_block=(3 * a_width + Q_LORA) // KV_LORA,
                        name="kv_up")
    pad_r = lambda v: jnp.pad(v, (0, LANES - ROPE_DIM)).reshape(1, LANES)
    qt, kf, vb = mla_prep(q_up, kv_up, krp, tabs_b, gb_q[:NOPE_DIM].reshape(1, LANES), pad_r(gb_q[NOPE_DIM:]),
                          gb_kn.reshape(1, LANES), pad_r(gb_kr), batch=batch, seq=seq)
    o_b = mla_attention(qt, kf, vb, batch=batch, seq=seq)
    return matmul((o_a, o_b), w_o, res=x, name="mixer_out")


def kernel(x, mem, positions, g_mem, w_mem_kv, g_mem_k, g_mix, g_x, w_xq, g_xq, w_xo, g_ffn,
           w_in, ga_q, ga_k, g_cq, w_uq, g_ckv, w_ukv, gb_q, gb_kn, gb_kr, w_o_even,
           w_gate, w_up, w_down, w_qkv, w_o_odd, w_router, b_router, w_egate, w_eup, w_edown):
    B, S, D = x.shape
    M = mem.shape[1]
    depth = g_mix.shape[0]
    T = B * S
    xt = x.reshape(T, D)
    tabs_a = _rope_tables(positions, ROT_DIM)
    tabs_b = _rope_tables(positions, ROPE_DIM)
    kv = matmul(rmsnorm(mem.reshape(B * M, D), g_mem), w_mem_kv, tm=B * M, name="mem_kv")
    mk, mv = memkv_post(kv, g_mem_k)
    for layer in range(depth):
        i = layer // 2
        if layer % 2 == 0:
            xt = even_mixer_block(xt, tabs_a, tabs_b, w_in[i], ga_q[i], ga_k[i], g_cq[i], w_uq[i], g_ckv[i],
                                  w_ukv[i], gb_q[i], gb_kn[i], gb_kr[i], w_o_even[i], g_mix[layer],
                                  batch=B, seq=S)
        else:
            h = rmsnorm(xt, g_mix[layer])
            qkv = matmul(h, w_qkv[i], out_dtype=BF16, tm=2048, single_buffer_x=True, name="qkv_proj")
            o = stickbreak_attention(qkv, batch=B, seq=S, heads=D // HEAD_DIM)
            xt = matmul(o, w_o_odd[i], res=xt, name="mixer_out")
        xattn = functools.partial(cross_attention_block, xt, mk, mv, g_x[layer], w_xq[layer], g_xq[layer],
                                  w_xo[layer], seq=S, mem_len=M)
        if layer % 2 == 0:
            xt, h = xattn(g_ffn[layer])
            hm = swiglu_gateup(h, w_gate[i], w_up[i])
            xt = matmul(hm, w_down[i], res=xt, tn=1024, tk=2048, name="swiglu_down")
        else:
            xt = moe_block(xattn(), g_ffn[layer], w_router[i], b_router[i], w_egate[i], w_eup[i], w_edown[i])
    return xt.reshape(B, S, D)
```

```python
import functools

import jax
import jax.numpy as jnp
from jax import lax
from jax.experimental import pallas as pl
from jax.experimental.pallas import tpu as pltpu

F32 = jnp.float32
BF16 = jnp.bfloat16

HEAD_DIM = 128
ROT_DIM = HEAD_DIM // 4
ROPE_THETA = 500000.0
BLOCK = 128
NEG_INF = -1e30
EPS = 1e-6
DILATED_PAIRS = ((128, 1), (512, 4), (2048, 16))
A_HEADS = 16
B_HEADS = 16
Q_LORA = 1536
KV_LORA = 512
NOPE_DIM = 128
ROPE_DIM = 64
QK_DIM = NOPE_DIM + ROPE_DIM
X_HEADS = 4
N_EXPERTS = 8
TOP_K = 2
MIXER_A_GROUP = 8
MOE_ROW_STEP = 128
LOG2_E = 1.4426950408889634
MLA_Q_SCALE = QK_DIM ** -0.5 * LOG2_E

LANES = 128
V7X_VMEM_BYTES = 64 * 1024 * 1024
VMEM_CAP = V7X_VMEM_BYTES - 6 * 1024 * 1024


def _params(semantics, vmem_bytes):
    return pltpu.CompilerParams(dimension_semantics=semantics,
                                vmem_limit_bytes=int(min(max(vmem_bytes, 32 * 1024 * 1024), VMEM_CAP)))


def _nbytes(shape, dtype):
    n = 1
    for s in shape:
        n *= s
    return n * jnp.dtype(dtype).itemsize


def _rmsnorm_kernel(x_ref, g_ref, o_ref):
    x = x_ref[...]
    ms = jnp.mean(x * x, axis=-1, keepdims=True)
    o_ref[...] = (x * lax.rsqrt(ms + EPS) * g_ref[...]).astype(o_ref.dtype)


def rmsnorm(x, g, *, tm=256):
    T, width = x.shape
    tm = min(tm, T)
    return pl.pallas_call(
        _rmsnorm_kernel,
        grid=(T // tm,),
        in_specs=[pl.BlockSpec((tm, width), lambda i: (i, 0)),
                  pl.BlockSpec((1, width), lambda i: (0, 0))],
        out_specs=pl.BlockSpec((tm, width), lambda i: (i, 0)),
        out_shape=jax.ShapeDtypeStruct((T, width), BF16),
        compiler_params=_params(("parallel",), 6 * _nbytes((tm, width), F32)),
        name="rmsnorm",
    )(x, g.reshape(1, width))


def _mm_kernel(*refs, nk, n_x, has_res, w_transposed):
    x_refs = refs[:n_x]
    w_ref = refs[n_x]
    rest = refs[n_x + 1:]
    res_ref = rest[0] if has_res else None
    o_ref = rest[1] if has_res else rest[0]
    acc_ref = rest[-1] if nk > 1 else None
    k = pl.program_id(2)
    w = w_ref[...].astype(BF16)

    def finish(part):
        out = part
        if has_res:
            out = out + res_ref[...]
        o_ref[...] = out.astype(o_ref.dtype)

    if n_x == 2:
        half = x_refs[0].shape[1]
        finish(jnp.dot(x_refs[0][...], w[:half], preferred_element_type=F32)
               + jnp.dot(x_refs[1][...], w[half:], preferred_element_type=F32))
    elif nk == 1 and w_transposed:
        finish(lax.dot_general(x_refs[0][...], w, (((1,), (1,)), ((), ())), preferred_element_type=F32))
    elif nk == 1:
        finish(jnp.dot(x_refs[0][...], w, preferred_element_type=F32))
    else:
        part = jnp.dot(x_refs[0][...], w, preferred_element_type=F32)

        @pl.when(k == 0)
        def _():
            acc_ref[...] = part

        @pl.when((k > 0) & (k < nk - 1))
        def _():
            acc_ref[...] += part

        @pl.when(k == nk - 1)
        def _():
            finish(acc_ref[...] + part)


def matmul(xs, w, *, n_out=None, res=None, out_dtype=F32, tm=1024, tn=512, tk=None, single_buffer_x=False,
           w_transposed=False, name="matmul"):
    if not isinstance(xs, (tuple, list)):
        xs = (xs,)
    n_x = len(xs)
    T = xs[0].shape[0]
    K = sum(x.shape[1] for x in xs)
    n_out = (w.shape[0] if w_transposed else w.shape[1]) if n_out is None else n_out
    tm = min(tm, T)
    tn = min(tn, n_out)
    if n_x == 2:
        assert tk is None and xs[1].shape[1] == xs[0].shape[1]
    tk = K if tk is None else tk
    nk = K // tk
    assert K == nk * tk and T % tm == 0 and n_out % tn == 0
    x_buffers = 2
    if n_x == 2:
        x_specs = [pl.BlockSpec((tm, K // 2), lambda i, j, k: (i, 0)) for _ in xs]
    elif nk == 1 and single_buffer_x:
        x_buffers = 1
        x_specs = [pl.BlockSpec((tm, tk), lambda i, j, k: (i, k), pipeline_mode=pl.Buffered(1))]
    else:
        x_specs = [pl.BlockSpec((tm, tk), lambda i, j, k: (i, k))]
    if w_transposed:
        assert n_x == 1 and nk == 1
        in_specs = x_specs + [pl.BlockSpec((tn, tk), lambda i, j, k: (j, k))]
    else:
        in_specs = x_specs + [pl.BlockSpec((tk, tn), lambda i, j, k: (k, j))]
    args = list(xs) + [w]
    if res is not None:
        in_specs.append(pl.BlockSpec((tm, tn), lambda i, j, k: (i, j)))
        args.append(res)
    scratch = [pltpu.VMEM((tm, tn), F32)] if nk > 1 else []
    vmem = (x_buffers * _nbytes((tm, tk), xs[0].dtype) + 2 * _nbytes((tk, tn), F32) + _nbytes((tk, tn), BF16)
            + (2 * _nbytes((tm, tn), F32) if res is not None else 0)
            + 2 * _nbytes((tm, tn), out_dtype) + 2 * _nbytes((tm, tn), F32))
    return pl.pallas_call(
        functools.partial(_mm_kernel, nk=nk, n_x=n_x, has_res=res is not None, w_transposed=w_transposed),
        grid=(T // tm, n_out // tn, nk),
        in_specs=in_specs,
        out_specs=pl.BlockSpec((tm, tn), lambda i, j, k: (i, j)),
        out_shape=jax.ShapeDtypeStruct((T, n_out), out_dtype),
        scratch_shapes=scratch,
        compiler_params=_params(("parallel", "parallel", "arbitrary"), vmem),
        name=name,
    )(*args)


def _norm_mm_kernel(x_ref, g_ref, w_ref, o_ref, xn):
    @pl.when(pl.program_id(1) == 0)
    def _():
        x = x_ref[...]
        xn[...] = (x * lax.rsqrt(jnp.mean(x * x, axis=-1, keepdims=True) + EPS) * g_ref[...]).astype(xn.dtype)

    o_ref[...] = jnp.dot(xn[...], w_ref[...].astype(BF16), preferred_element_type=F32)


def norm_matmul(x, g, w, *, width, col_block, tm=1024, tn=1024, name="norm_matmul"):
    T = x.shape[0]
    N = w.shape[1]
    tm, tn = min(tm, T), min(tn, N)
    vmem = (2 * _nbytes((tm, width), F32) + _nbytes((tm, width), BF16) + 2 * _nbytes((width, tn), F32)
            + _nbytes((width, tn), BF16) + 4 * _nbytes((tm, tn), F32))
    return pl.pallas_call(
        _norm_mm_kernel,
        grid=(T // tm, N // tn),
        in_specs=[pl.BlockSpec((tm, width), lambda i, j: (i, col_block)),
                  pl.BlockSpec((1, width), lambda i, j: (0, 0)),
                  pl.BlockSpec((width, tn), lambda i, j: (0, j))],
        out_specs=pl.BlockSpec((tm, tn), lambda i, j: (i, j)),
        out_shape=jax.ShapeDtypeStruct((T, N), F32),
        scratch_shapes=[pltpu.VMEM((tm, width), BF16)],
        compiler_params=_params(("parallel", "arbitrary"), vmem),
        name=name,
    )(x, g.reshape(1, width), w)


def _rope_tables(positions, dim):
    half = dim // 2
    inv_freq = ROPE_THETA ** (-jnp.arange(0, dim, 2, dtype=F32) / dim)
    ang = positions.astype(F32)[..., None] * inv_freq
    cos, sin = jnp.cos(ang), jnp.sin(ang)
    B, S = positions.shape
    ones = jnp.ones((B, S, LANES - dim), F32)
    zeros = jnp.zeros((B, S, LANES - dim), F32)
    zh = jnp.zeros((B, S, half), F32)
    c = jnp.concatenate([cos, cos, ones], axis=-1)
    sa = jnp.concatenate([-sin, zh, zeros], axis=-1)
    sb = jnp.concatenate([zh, sin, zeros], axis=-1)
    return [t.reshape(B * S, LANES) for t in (c, sa, sb)]


def _rope_lanes(y, c, sa, sb, half):
    return y * c + pltpu.roll(y, LANES - half, 1) * sa + pltpu.roll(y, half, 1) * sb


def _mixer_a_kernel(q_ref, k_ref, v_ref, c_ref, sa_ref, sb_ref, gq_ref, gk_ref, o_ref,
                    qs, ks, *state, seq):
    scale = HEAD_DIM ** -0.5 * LOG2_E
    c, sa, sb = c_ref[...], sa_ref[...], sb_ref[...]

    def prep(x, g):
        y = x * lax.rsqrt(jnp.mean(x * x, axis=-1, keepdims=True) + EPS) * g
        return _rope_lanes(y, c, sa, sb, ROT_DIM // 2)

    qs[...] = prep(q_ref[...], gq_ref[...])
    ks[...] = prep(k_ref[...], gk_ref[...])
    qi = lax.broadcasted_iota(jnp.int32, (BLOCK, 1), 0)
    kj = lax.broadcasted_iota(jnp.int32, (1, BLOCK), 1)
    trans_b = (((1,), (1,)), ((), ()))
    ones = jnp.ones((BLOCK, HEAD_DIM), BF16)
    ones2 = jnp.ones((2 * BLOCK, HEAD_DIM), BF16)

    for bi, (window, dil) in enumerate(DILATED_PAIRS):
        acc, m_sc, l_sc = state[3 * bi:3 * bi + 3]
        n_back = window // dil
        nb = seq // dil // BLOCK
        assert n_back <= BLOCK and nb * dil * BLOCK == seq
        cur_ok = (qi >= kj) & (qi - kj <= n_back)
        prev_ok = qi + BLOCK - kj <= n_back
        both_ok = jnp.concatenate([jnp.broadcast_to(prev_ok, (BLOCK, BLOCK)),
                                   jnp.broadcast_to(cur_ok, (BLOCK, BLOCK))], axis=1)

        def rows_at(r, n, dil=dil):
            if dil == 1:
                return pl.ds(BLOCK * n, BLOCK)
            return pl.ds(r + dil * BLOCK * n, BLOCK, stride=dil)

        blocks = [(rows_at(r, n), rows_at(r, n - 1) if n > 0 else None) for r in range(dil) for n in range(nb)]
        for first in range(0, len(blocks), MIXER_A_GROUP):
            group = blocks[first:first + MIXER_A_GROUP]
            scores = []
            for rows, prev in group:
                qb = qs[rows, :].astype(BF16)
                kk = ks[rows, :].astype(BF16)
                ok = cur_ok
                if prev is not None:
                    kk = jnp.concatenate([ks[prev, :].astype(BF16), kk], axis=0)
                    ok = both_ok
                s = lax.dot_general(qb, kk, trans_b, preferred_element_type=F32) * scale
                scores.append(jnp.where(ok, s, NEG_INF))
            stats = []
            for s in scores:
                m = jnp.max(s, axis=-1, keepdims=True)
                stats.append((m, jnp.exp2(s - m).astype(BF16)))
            for (rows, prev), (m, p) in zip(group, stats):
                vv = v_ref[rows, :].astype(BF16)
                if prev is not None:
                    vv = jnp.concatenate([v_ref[prev, :].astype(BF16), vv], axis=0)
                o = jnp.dot(p, jnp.concatenate([vv, ones if prev is None else ones2], axis=1),
                            preferred_element_type=F32)
                acc[rows, :] = o[:, :HEAD_DIM]
                l_sc[rows, :] = o[:, HEAD_DIM:]
                m_sc[rows, :] = jnp.broadcast_to(m, (BLOCK, HEAD_DIM))

    n_br = len(DILATED_PAIRS)
    m_all = state[1][...]
    for bi in range(1, n_br):
        m_all = jnp.maximum(m_all, state[3 * bi + 1][...])
    num = jnp.zeros(o_ref.shape, F32)
    den = jnp.zeros(o_ref.shape, F32)
    for bi in range(n_br):
        w = jnp.exp2(state[3 * bi + 1][...] - m_all)
        num = num + w * state[3 * bi][...]
        den = den + w * state[3 * bi + 2][...]
    o_ref[...] = (num / den).astype(o_ref.dtype)


def mixer_a(proj, tabs, gq, gk, *, batch, seq):
    T = batch * seq
    blk = (seq, HEAD_DIM)
    head = lambda off: pl.BlockSpec(blk, lambda b, h: (b, off + h))
    tab = pl.BlockSpec(blk, lambda b, h: (b, 0))
    gspec = pl.BlockSpec((1, HEAD_DIM), lambda b, h: (0, 0))
    return pl.pallas_call(
        functools.partial(_mixer_a_kernel, seq=seq),
        grid=(batch, A_HEADS),
        in_specs=[head(0), head(A_HEADS), head(2 * A_HEADS), tab, tab, tab, gspec, gspec],
        out_specs=pl.BlockSpec(blk, lambda b, h: (b, h)),
        out_shape=jax.ShapeDtypeStruct((T, A_HEADS * HEAD_DIM), BF16),
        scratch_shapes=[pltpu.VMEM(blk, F32) for _ in range(2 + 3 * len(DILATED_PAIRS))],
        compiler_params=_params(("parallel", "parallel"), 30 * _nbytes(blk, F32)),
        name="mixer_a",
    )(proj, proj, proj, *tabs, gq.reshape(1, HEAD_DIM), gk.reshape(1, HEAD_DIM))


def _mla_prep_kernel(qn_ref, qr_ref, kn_ref, v_ref, kr_ref, c_ref, sa_ref, sb_ref,
                     gqn_ref, gqr_ref, gkn_ref, gkr_ref, qt_ref, kf_ref, vb_ref, kr_sc):
    c, sa, sb = c_ref[...], sa_ref[...], sb_ref[...]
    half = ROPE_DIM // 2

    @pl.when(pl.program_id(1) == 0)
    def _():
        kr = kr_ref[...]
        kr = kr * lax.rsqrt(jnp.sum(kr * kr, axis=-1, keepdims=True) * (1.0 / ROPE_DIM) + EPS) * gkr_ref[...]
        kr_sc[...] = _rope_lanes(kr, c, sa, sb, half).astype(kr_sc.dtype)

    qn, qr = qn_ref[...], qr_ref[...]
    ss = jnp.sum(qn * qn, axis=-1, keepdims=True) + jnp.sum(qr * qr, axis=-1, keepdims=True)
    inv = lax.rsqrt(ss * (1.0 / QK_DIM) + EPS) * MLA_Q_SCALE
    qf = jnp.concatenate([qn * inv * gqn_ref[...], _rope_lanes(qr * inv * gqr_ref[...], c, sa, sb, half)], axis=1)
    qt_ref[...] = qf.astype(qt_ref.dtype).T
    kn = kn_ref[...]
    kn = kn * lax.rsqrt(jnp.mean(kn * kn, axis=-1, keepdims=True) + EPS) * gkn_ref[...]
    kf_ref[:, :NOPE_DIM] = kn.astype(kf_ref.dtype)
    kf_ref[:, NOPE_DIM:] = kr_sc[...]
    vb_ref[...] = v_ref[...].astype(vb_ref.dtype)


def mla_prep(q_up, kv_up, krp, tabs, gqn, gqr, gkn, gkr, *, batch, seq, tm=2048):
    T = q_up.shape[0]
    H = B_HEADS
    per_seq = seq // tm
    blk = lambda f: pl.BlockSpec((tm, LANES), f)
    g = pl.BlockSpec((1, LANES), lambda i, h: (0, 0))
    return pl.pallas_call(
        _mla_prep_kernel,
        grid=(T // tm, H),
        in_specs=[blk(lambda i, h: (i, h)), blk(lambda i, h: (i, H + h)),
                  blk(lambda i, h: (i, 2 * h)), blk(lambda i, h: (i, 2 * h + 1)),
                  blk(lambda i, h: (i, 0)), blk(lambda i, h: (i, 0)), blk(lambda i, h: (i, 0)),
                  blk(lambda i, h: (i, 0)), g, g, g, g],
        out_specs=[pl.BlockSpec((None, None, 2 * LANES, tm), lambda i, h: (i // per_seq, h, 0, i % per_seq)),
                   pl.BlockSpec((tm, 2 * LANES), lambda i, h: (i, h)),
                   pl.BlockSpec((tm, LANES), lambda i, h: (i, h))],
        out_shape=[jax.ShapeDtypeStruct((batch, H, 2 * LANES, seq), BF16),
                   jax.ShapeDtypeStruct((T, H * 2 * LANES), BF16),
                   jax.ShapeDtypeStruct((T, H * LANES), BF16)],
        scratch_shapes=[pltpu.VMEM((tm, LANES), BF16)],
        compiler_params=_params(("parallel", "arbitrary"), 40 * _nbytes((tm, LANES), F32)),
        name="mla_prep",
    )(q_up, q_up, kv_up, kv_up, krp, *tabs, gqn, gqr, gkn, gkr)


def _causal_attn_kernel(qt_ref, k_ref, v_ref, o_ref, s_sc, acc, *, tq, hg, dq, dv):
    qi = pl.program_id(2)
    key = lax.broadcasted_iota(jnp.int32, (tq, tq), 0)
    qry = lax.broadcasted_iota(jnp.int32, (tq, tq), 1)
    visible = key <= qry
    ones = jnp.ones((tq, dv), BF16)
    contract_rows = (((0,), (0,)), ((), ()))

    heads = range(hg)

    def scores(j, diagonal):
        rows = pl.ds(pl.multiple_of(j * tq, tq), tq)
        s = [jnp.dot(k_ref[rows, g * dq:(g + 1) * dq], qt_ref[g], preferred_element_type=F32) for g in heads]
        if diagonal:
            s = [jnp.where(visible, sg, NEG_INF) for sg in s]
        for g in heads:
            s_sc[g, j] = s[g]
        return s

    def col_max(j, ms):
        s = scores(j, False)
        return tuple(jnp.maximum(ms[g], jnp.max(s[g], axis=0, keepdims=True)) for g in heads)

    ms = lax.fori_loop(0, qi, col_max, tuple(jnp.max(sg, axis=0, keepdims=True) for sg in scores(qi, True)))

    def weighted(j):
        rows = pl.ds(pl.multiple_of(j * tq, tq), tq)
        p = [jnp.exp2(s_sc[g, j] - ms[g]).astype(BF16) for g in heads]
        return [lax.dot_general(p[g], jnp.concatenate([v_ref[rows, g * dv:(g + 1) * dv], ones], axis=1),
                                contract_rows, preferred_element_type=F32) for g in heads]

    for g, w in enumerate(weighted(qi)):
        acc[g] = w

    def below(j, carry):
        for g, w in enumerate(weighted(j)):
            acc[g] += w
        return carry

    lax.fori_loop(0, qi, below, 0)
    for g in heads:
        a = acc[g]
        o_ref[:, g * dv:(g + 1) * dv] = (a[:, :dv] / a[:, dv:]).astype(o_ref.dtype)


def mla_attention(qt, kf, vb, *, batch, seq, tq=512, hg=4):
    T = batch * seq
    nq = seq // tq
    dq = qt.shape[2]
    dv = vb.shape[1] // B_HEADS
    assert B_HEADS % hg == 0
    return pl.pallas_call(
        functools.partial(_causal_attn_kernel, tq=tq, hg=hg, dq=dq, dv=dv),
        grid=(batch, B_HEADS // hg, nq),
        in_specs=[pl.BlockSpec((None, hg, dq, tq), lambda b, h, i: (b, h, 0, i)),
                  pl.BlockSpec((seq, hg * dq), lambda b, h, i: (b, h)),
                  pl.BlockSpec((seq, hg * dv), lambda b, h, i: (b, h))],
        out_specs=pl.BlockSpec((tq, hg * dv), lambda b, h, i: (b * nq + i, h)),
        out_shape=jax.ShapeDtypeStruct((T, B_HEADS * dv), BF16),
        scratch_shapes=[pltpu.VMEM((hg, nq, tq, tq), F32), pltpu.VMEM((hg, tq, 2 * dv), F32)],
        compiler_params=_params(("parallel", "parallel", "parallel"),
                                _nbytes((hg, nq, tq, tq), F32) + 2 * _nbytes((seq, hg * (dq + dv)), BF16)
                                + 8 * hg * _nbytes((tq, tq), F32)),
        name="mla_attention",
    )(qt, kf, vb)


STICK_EXHAUSTED_LOG = -110.0


def _stickbreak_kernel(q_ref, k_ref, v_ref, o_ref, qt_sc, c_sc, acc, *, tq, hg, scale):
    qi = pl.program_id(2)
    c_sc[...] = jnp.zeros(c_sc.shape, F32)
    acc[...] = jnp.zeros(acc.shape, F32)
    for g in range(hg):
        qt_sc[g] = q_ref[:, g * HEAD_DIM:(g + 1) * HEAD_DIM].T
    key = lax.broadcasted_iota(jnp.int32, (tq, tq), 0)
    qry = lax.broadcasted_iota(jnp.int32, (tq, tq), 1)
    after = (qry > key).astype(BF16)
    earlier = key < qry
    contract_rows = (((0,), (0,)), ((), ()))

    def chunk(j, diagonal):
        rows = pl.ds(pl.multiple_of(j * tq, tq), tq)
        heads = range(hg)
        cols = [slice(g * HEAD_DIM, (g + 1) * HEAD_DIM) for g in heads]
        z = [jnp.dot(k_ref[rows, cols[g]], qt_sc[g], preferred_element_type=F32) * (scale * LOG2_E) for g in heads]
        log_beta, parts = [], []
        for g in heads:
            softplus = jnp.maximum(z[g], 0.0) + jnp.log2(1.0 + jnp.exp2(-jnp.abs(z[g])))
            log_keep = -softplus
            if diagonal:
                log_keep = jnp.where(earlier, log_keep, 0.0)
            parts.append(log_keep.astype(BF16))
            log_beta.append(z[g] - softplus + c_sc[g])
            c_sc[g] += jnp.sum(log_keep, axis=0, keepdims=True)
        within = jnp.dot(after, jnp.concatenate(parts, axis=1), preferred_element_type=F32)
        for g in heads:
            a = jnp.exp2(log_beta[g] + within[:, g * tq:(g + 1) * tq])
            if diagonal:
                a = jnp.where(earlier, a, 0.0)
            acc[:, cols[g]] += lax.dot_general(a.astype(BF16), v_ref[rows, cols[g]], contract_rows,
                                               preferred_element_type=F32)

    def stick_left():
        return jnp.max(c_sc[...]) > STICK_EXHAUSTED_LOG * LOG2_E

    chunk(qi, True)

    def more(state):
        j, left = state
        return (j >= 0) & left

    def step(state):
        j, _ = state
        chunk(j, False)
        return j - 1, stick_left()

    lax.while_loop(more, step, (qi - 1, stick_left()))
    o_ref[...] = acc[...].astype(o_ref.dtype)


def stickbreak_attention(qkv, *, batch, seq, heads, tq=256, hg=8):
    T = batch * seq
    nq = seq // tq
    ng = heads // hg
    assert heads == ng * hg
    blk = hg * HEAD_DIM
    return pl.pallas_call(
        functools.partial(_stickbreak_kernel, tq=tq, hg=hg, scale=HEAD_DIM ** -0.5),
        grid=(batch, ng, nq),
        in_specs=[pl.BlockSpec((tq, blk), lambda b, h, i: (b * nq + i, h)),
                  pl.BlockSpec((seq, blk), lambda b, h, i: (b, ng + h)),
                  pl.BlockSpec((seq, blk), lambda b, h, i: (b, 2 * ng + h))],
        out_specs=pl.BlockSpec((tq, blk), lambda b, h, i: (b * nq + i, h)),
        out_shape=jax.ShapeDtypeStruct((T, heads * HEAD_DIM), BF16),
        scratch_shapes=[pltpu.VMEM((hg, HEAD_DIM, tq), BF16), pltpu.VMEM((hg, 1, tq), F32),
                        pltpu.VMEM((tq, blk), F32)],
        compiler_params=_params(("parallel", "parallel", "parallel"), 0),
        name="stickbreak",
    )(qkv, qkv, qkv)


def _memkv_kernel(kv_ref, g_ref, mk_ref, mv_ref):
    for h in range(X_HEADS):
        cols = slice(h * HEAD_DIM, (h + 1) * HEAD_DIM)
        k = kv_ref[:, cols]
        k = k * lax.rsqrt(jnp.mean(k * k, axis=-1, keepdims=True) + EPS) * g_ref[...]
        mk_ref[:, cols] = k.astype(mk_ref.dtype)
    mv_ref[...] = kv_ref[:, X_HEADS * HEAD_DIM:].astype(mv_ref.dtype)


def memkv_post(kv, g):
    M = kv.shape[0]
    xd = X_HEADS * HEAD_DIM
    return pl.pallas_call(
        _memkv_kernel,
        grid=(1,),
        in_specs=[pl.BlockSpec((M, 2 * xd), lambda i: (0, 0)), pl.BlockSpec((1, HEAD_DIM), lambda i: (0, 0))],
        out_specs=[pl.BlockSpec((M, xd), lambda i: (0, 0)), pl.BlockSpec((M, xd), lambda i: (0, 0))],
        out_shape=[jax.ShapeDtypeStruct((M, xd), BF16), jax.ShapeDtypeStruct((M, xd), BF16)],
        compiler_params=_params(("arbitrary",), 0),
        name="memkv_post",
    )(kv, g.reshape(1, HEAD_DIM))


def _xattn_block_kernel(x_ref, gx_ref, wq_ref, gq_ref, mk_ref, mv_ref, wo_ref, *rest, emit_norm):
    if emit_norm:
        gn_ref, o_ref, hn_ref = rest
    else:
        (o_ref,) = rest
    scale = HEAD_DIM ** -0.5
    x = x_ref[...]
    h = (x * lax.rsqrt(jnp.mean(x * x, axis=-1, keepdims=True) + EPS) * gx_ref[...]).astype(BF16)
    q_all = jnp.dot(h, wq_ref[...], preferred_element_type=F32)
    heads = []
    for hd in range(X_HEADS):
        cols = slice(hd * HEAD_DIM, (hd + 1) * HEAD_DIM)
        q = q_all[:, cols]
        q = (q * lax.rsqrt(jnp.mean(q * q, axis=-1, keepdims=True) + EPS) * gq_ref[...]).astype(BF16)
        s = lax.dot_general(q, mk_ref[:, cols], (((1,), (1,)), ((), ())), preferred_element_type=F32) * scale
        p = jnp.exp(s - jnp.max(s, axis=-1, keepdims=True))
        p = p / jnp.sum(p, axis=-1, keepdims=True)
        heads.append(jnp.dot(p.astype(BF16), mv_ref[:, cols], preferred_element_type=F32).astype(BF16))
    y = x + jnp.dot(jnp.concatenate(heads, axis=1), wo_ref[...], preferred_element_type=F32)
    o_ref[...] = y
    if emit_norm:
        hn_ref[...] = (y * lax.rsqrt(jnp.mean(y * y, axis=-1, keepdims=True) + EPS) * gn_ref[...]).astype(hn_ref.dtype)


def cross_attention_block(x, mk, mv, g_x, w_xq, g_xq, w_xo, g_next=None, *, seq, mem_len, tm=256):
    T, D = x.shape
    xd = X_HEADS * HEAD_DIM
    per_seq = seq // tm
    emit_norm = g_next is not None
    row = lambda w: pl.BlockSpec((1, w), lambda i: (0, 0))
    tok = pl.BlockSpec((tm, D), lambda i: (i, 0))
    in_specs = [tok, row(D), pl.BlockSpec((D, xd), lambda i: (0, 0)), row(HEAD_DIM),
                pl.BlockSpec((mem_len, xd), lambda i: (i // per_seq, 0)),
                pl.BlockSpec((mem_len, xd), lambda i: (i // per_seq, 0)),
                pl.BlockSpec((xd, D), lambda i: (0, 0))]
    args = [x, g_x.reshape(1, D), w_xq.astype(BF16), g_xq.reshape(1, HEAD_DIM), mk, mv, w_xo.astype(BF16)]
    out_specs, out_shape = tok, jax.ShapeDtypeStruct((T, D), F32)
    if emit_norm:
        in_specs.append(row(D))
        args.append(g_next.reshape(1, D))
        out_specs = [tok, tok]
        out_shape = [out_shape, jax.ShapeDtypeStruct((T, D), BF16)]
    vmem = 10 * _nbytes((tm, D), F32) + 4 * _nbytes((D, xd), BF16)
    return pl.pallas_call(
        functools.partial(_xattn_block_kernel, emit_norm=emit_norm),
        grid=(T // tm,),
        in_specs=in_specs,
        out_specs=out_specs,
        out_shape=out_shape,
        compiler_params=_params(("parallel",), vmem),
        name="cross_attention",
    )(*args)


def _gateup_kernel(x_ref, wg_ref, wu_ref, o_ref):
    x = x_ref[...]
    g = jnp.dot(x, wg_ref[...].astype(BF16), preferred_element_type=F32)
    u = jnp.dot(x, wu_ref[...].astype(BF16), preferred_element_type=F32)
    o_ref[...] = (g * jax.nn.sigmoid(g) * u).astype(o_ref.dtype)


def swiglu_gateup(x, wg, wu, *, tm=2048, tf=256):
    T, K = x.shape
    F = wg.shape[1]
    vmem = (_nbytes((tm, K), BF16) + 4 * _nbytes((K, tf), F32) + 2 * _nbytes((K, tf), BF16)
            + 6 * _nbytes((tm, tf), F32))
    return pl.pallas_call(
        _gateup_kernel,
        grid=(T // tm, F // tf),
        in_specs=[pl.BlockSpec((tm, K), lambda i, j: (i, 0), pipeline_mode=pl.Buffered(1)),
                  pl.BlockSpec((K, tf), lambda i, j: (0, j)),
                  pl.BlockSpec((K, tf), lambda i, j: (0, j))],
        out_specs=pl.BlockSpec((tm, tf), lambda i, j: (i, j)),
        out_shape=jax.ShapeDtypeStruct((T, F), BF16),
        compiler_params=_params(("parallel", "parallel"), vmem),
        name="swiglu_gateup",
    )(x, wg, wu)


def _router_kernel(x_ref, g_ref, w_ref, b_ref, idx_ref, gw_ref):
    x = x_ref[...]
    h = x * lax.rsqrt(jnp.mean(x * x, axis=-1, keepdims=True) + EPS) * g_ref[...]
    logits = jnp.dot(h, w_ref[...], preferred_element_type=F32, precision=lax.Precision.HIGHEST) + b_ref[...]
    lane = lax.broadcasted_iota(jnp.int32, logits.shape, 1)
    logits = jnp.where(lane < N_EXPERTS, logits, -jnp.inf)
    m1 = jnp.max(logits, axis=-1, keepdims=True)
    i1 = jnp.min(jnp.where(logits == m1, lane, LANES), axis=-1, keepdims=True)
    rest = jnp.where(lane == i1, -jnp.inf, logits)
    m2 = jnp.max(rest, axis=-1, keepdims=True)
    i2 = jnp.min(jnp.where(rest == m2, lane, LANES), axis=-1, keepdims=True)
    e = jnp.exp(m2 - m1)
    w1 = 1.0 / (1.0 + e)
    w2 = e / (1.0 + e)
    idx_ref[...] = jnp.where(lane == 0, i1, jnp.where(lane == 1, i2, 0))
    gw_ref[...] = jnp.where(lane == 0, w1, jnp.where(lane == 1, w2, 0.0))


def moe_router(x, g, w_router, b_router, *, tm=256):
    T, D = x.shape
    E = w_router.shape[1]
    w = jnp.pad(w_router, ((0, 0), (0, LANES - E)))
    b = jnp.pad(b_router, (0, LANES - E)).reshape(1, LANES)
    idx, gw = pl.pallas_call(
        _router_kernel,
        grid=(T // tm,),
        in_specs=[pl.BlockSpec((tm, D), lambda i: (i, 0)), pl.BlockSpec((1, D), lambda i: (0, 0)),
                  pl.BlockSpec((D, LANES), lambda i: (0, 0)), pl.BlockSpec((1, LANES), lambda i: (0, 0))],
        out_specs=[pl.BlockSpec((tm, LANES), lambda i: (i, 0)), pl.BlockSpec((tm, LANES), lambda i: (i, 0))],
        out_shape=[jax.ShapeDtypeStruct((T, LANES), jnp.int32), jax.ShapeDtypeStruct((T, LANES), F32)],
        compiler_params=_params(("parallel",), 8 * _nbytes((tm, D), F32)),
        name="moe_router",
    )(x, g.reshape(1, D), w, b)
    return idx[:, :TOP_K], gw[:, :TOP_K]


def _moe_dispatch(idx, tm):
    T = idx.shape[0]
    A = T * TOP_K
    e_flat = idx.reshape(A)
    onehot = (e_flat[:, None] == jnp.arange(N_EXPERTS, dtype=jnp.int32)[None, :]).astype(jnp.int32)
    csum = jnp.cumsum(onehot, axis=0)
    pos_in = jnp.sum(csum * onehot, axis=1) - 1
    counts = csum[-1]
    padded = ((counts + tm - 1) // tm) * tm
    gend = jnp.cumsum(padded)
    gstart = gend - padded
    dest = (jnp.sum(onehot * gstart[None, :], axis=1) + pos_in).astype(jnp.int32)
    n_rows = A + N_EXPERTS * tm
    row_tok = jnp.zeros((n_rows,), jnp.int32).at[dest].set(jnp.arange(A, dtype=jnp.int32) // TOP_K)
    nb = n_rows // tm
    n_used = (gend[-1] // tm).astype(jnp.int32)
    blk = jnp.arange(nb, dtype=jnp.int32)
    be = jnp.sum((blk[:, None] * tm >= gend[None, :]).astype(jnp.int32), axis=1)
    be = jnp.minimum(be, N_EXPERTS - 1)
    be = jnp.where(blk < n_used, be, be[jnp.maximum(n_used - 1, 0)]).astype(jnp.int32)
    eid = jnp.arange(N_EXPERTS, dtype=jnp.int32)
    later = (counts[None, :] > 0) & (eid[None, :] > eid[:, None])
    nxt_of = jnp.min(jnp.where(later, eid[None, :], N_EXPERTS), axis=1)
    nxt_of = jnp.where(nxt_of == N_EXPERTS, -1, nxt_of).astype(jnp.int32)
    mine = be[:, None] == eid[None, :]
    nxt = jnp.sum(mine * nxt_of[None, :], axis=1).astype(jnp.int32)
    group_end_row = jnp.sum(mine * (gstart + counts)[None, :], axis=1)
    nv = jnp.where(blk < n_used, jnp.clip(group_end_row - blk * tm, 0, tm), 0).astype(jnp.int32)
    return dest, row_tok, be, n_used.reshape(1), nxt, nv


def _moe_gather_kernel(tok_ref, nu_ref, x_hbm, g_ref, o_ref, buf, sem, *, rows):
    i = pl.program_id(0)
    n_used = nu_ref[0]

    def row_copy(slot, r, src_row):
        return pltpu.make_async_copy(x_hbm.at[pl.ds(src_row, 1)], buf.at[slot, pl.ds(r, 1)], sem.at[slot])

    def fetch(block, slot):
        def issue(r, c):
            row_copy(slot, r, tok_ref[block * rows + r]).start()
            return c

        lax.fori_loop(0, rows, issue, 0, unroll=8)

    @pl.when(i == 0)
    def _():
        fetch(0, 0)

    @pl.when(i + 1 < n_used)
    def _():
        fetch(i + 1, (i + 1) % 2)

    @pl.when(i < n_used)
    def _():
        slot = i % 2

        def drain(r, c):
            row_copy(slot, r, 0).wait()
            return c

        lax.fori_loop(0, rows, drain, 0, unroll=8)
        x = buf[slot]
        o_ref[...] = (x * lax.rsqrt(jnp.mean(x * x, axis=-1, keepdims=True) + EPS) * g_ref[...]).astype(o_ref.dtype)

    @pl.when(i >= n_used)
    def _():
        o_ref[...] = jnp.zeros(o_ref.shape, o_ref.dtype)


def moe_gather_norm(x, g, row_tok, n_used, *, tm, rows=256):
    T, D = x.shape
    n_rows = row_tok.shape[0]
    per = tm // rows
    return pl.pallas_call(
        functools.partial(_moe_gather_kernel, rows=rows),
        grid_spec=pltpu.PrefetchScalarGridSpec(
            num_scalar_prefetch=2,
            grid=(n_rows // rows,),
            in_specs=[pl.BlockSpec(memory_space=pl.ANY), pl.BlockSpec((1, D), lambda i, tok, nu: (0, 0))],
            out_specs=pl.BlockSpec((rows, D), lambda i, tok, nu: (i, 0)),
            scratch_shapes=[pltpu.VMEM((2, rows, D), F32), pltpu.SemaphoreType.DMA((2,))]),
        out_shape=jax.ShapeDtypeStruct((n_rows, D), BF16),
        compiler_params=_params(("arbitrary",), 10 * _nbytes((rows, D), F32)),
        name="moe_gather",
    )(row_tok, n_used * per, x, g.reshape(1, D))


def _stream_group_weights(be_ref, nu_ref, nxt_ref, w_hbms, stage, sem, w_bf16, *, tile, n_tiles):
    j = pl.program_id(0)
    i = pl.program_id(1)

    def copies(e, jj):
        cols = pl.ds(pl.multiple_of(jj * tile, tile), tile)
        return [pltpu.make_async_copy(w.at[e, :, cols], stage.at[n], sem.at[n]) for n, w in enumerate(w_hbms)]

    @pl.when((j == 0) & (i == 0))
    def _():
        for c in copies(be_ref[0], 0):
            c.start()

    first_of_group = (i < nu_ref[0]) & ((i == 0) | (be_ref[i] != be_ref[jnp.maximum(i - 1, 0)]))

    @pl.when(first_of_group)
    def _():
        for c in copies(be_ref[i], j):
            c.wait()
        for n, dst in enumerate(w_bf16):
            dst[...] = stage[n].astype(BF16)
        more_in_sweep = nxt_ref[i] >= 0
        e_next = jnp.where(more_in_sweep, nxt_ref[i], be_ref[0])
        j_next = jnp.where(more_in_sweep, j, j + 1)

        @pl.when(more_in_sweep | (j + 1 < n_tiles))
        def _():
            for c in copies(e_next, j_next):
                c.start()


def _for_valid_rows(nv, o_ref, compute):
    tm = o_ref.shape[0]
    for rows in range(MOE_ROW_STEP, tm + 1, MOE_ROW_STEP):
        @pl.when((nv > rows - MOE_ROW_STEP) & (nv <= rows))
        def _(rows=rows):
            o_ref[:rows, :] = compute(rows).astype(o_ref.dtype)
            if rows < tm:
                o_ref[rows:, :] = jnp.zeros((tm - rows, o_ref.shape[1]), o_ref.dtype)

    @pl.when(nv == 0)
    def _():
        o_ref[...] = jnp.zeros(o_ref.shape, o_ref.dtype)


def _moe_gateup_kernel(be_ref, nu_ref, nxt_ref, nv_ref, x_ref, wg_hbm, wu_hbm, o_ref, stage, sem, wgb, wub,
                       *, tf, n_tiles):
    _stream_group_weights(be_ref, nu_ref, nxt_ref, (wg_hbm, wu_hbm), stage, sem, (wgb, wub), tile=tf, n_tiles=n_tiles)

    def compute(rows):
        x = x_ref[:rows, :]
        g = jnp.dot(x, wgb[...], preferred_element_type=F32)
        u = jnp.dot(x, wub[...], preferred_element_type=F32)
        return g * jax.nn.sigmoid(g) * u

    _for_valid_rows(nv_ref[pl.program_id(1)], o_ref, compute)


def moe_gateup(xs, wg, wu, be, n_used, nxt, nv, *, tm, tf=512):
    n_rows, K = xs.shape
    F = wg.shape[2]
    tf = min(tf, F)
    nb = n_rows // tm
    xmap = lambda j, i, be, nu, *_: (jnp.minimum(i, nu[0] - 1), 0)
    hbm = pl.BlockSpec(memory_space=pl.ANY)
    vmem = (2 * _nbytes((tm, K), BF16) + 2 * _nbytes((K, tf), F32) + 4 * _nbytes((K, tf), BF16)
            + 6 * _nbytes((tm, tf), F32))
    return pl.pallas_call(
        functools.partial(_moe_gateup_kernel, tf=tf, n_tiles=F // tf),
        grid_spec=pltpu.PrefetchScalarGridSpec(
            num_scalar_prefetch=4,
            grid=(F // tf, nb),
            in_specs=[pl.BlockSpec((tm, K), xmap), hbm, hbm],
            out_specs=pl.BlockSpec((tm, tf), lambda j, i, *_: (i, j)),
            scratch_shapes=[pltpu.VMEM((2, K, tf), F32), pltpu.SemaphoreType.DMA((2,)),
                            pltpu.VMEM((K, tf), BF16), pltpu.VMEM((K, tf), BF16)]),
        out_shape=jax.ShapeDtypeStruct((n_rows, F), BF16),
        compiler_params=_params(("arbitrary", "arbitrary"), vmem),
        name="moe_gateup",
    )(be, n_used, nxt, nv, xs, wg, wu)


def _moe_down_kernel(be_ref, nu_ref, nxt_ref, nv_ref, x_ref, w_hbm, o_ref, stage, sem, wb, *, tn, n_tiles):
    _stream_group_weights(be_ref, nu_ref, nxt_ref, (w_hbm,), stage, sem, (wb,), tile=tn, n_tiles=n_tiles)
    _for_valid_rows(nv_ref[pl.program_id(1)], o_ref,
                    lambda rows: jnp.dot(x_ref[:rows, :], wb[...], preferred_element_type=F32))


def moe_down(hm, wd, be, n_used, nxt, nv, *, tm, tn=1024):
    n_rows, K = hm.shape
    N = wd.shape[2]
    tn = min(tn, N)
    nb = n_rows // tm
    xmap = lambda j, i, be, nu, *_: (jnp.minimum(i, nu[0] - 1), 0)
    vmem = (2 * _nbytes((tm, K), BF16) + _nbytes((K, tn), F32) + 2 * _nbytes((K, tn), BF16)
            + 4 * _nbytes((tm, tn), F32))
    return pl.pallas_call(
        functools.partial(_moe_down_kernel, tn=tn, n_tiles=N // tn),
        grid_spec=pltpu.PrefetchScalarGridSpec(
            num_scalar_prefetch=4,
            grid=(N // tn, nb),
            in_specs=[pl.BlockSpec((tm, K), xmap), pl.BlockSpec(memory_space=pl.ANY)],
            out_specs=pl.BlockSpec((tm, tn), lambda j, i, *_: (i, j)),
            scratch_shapes=[pltpu.VMEM((1, K, tn), F32), pltpu.SemaphoreType.DMA((1,)),
                            pltpu.VMEM((K, tn), BF16)]),
        out_shape=jax.ShapeDtypeStruct((n_rows, N), F32),
        compiler_params=_params(("arbitrary", "arbitrary"), vmem),
        name="moe_down",
    )(be, n_used, nxt, nv, hm, wd)


def _moe_combine_kernel(pos_ref, x_ref, gw_ref, ys_hbm, o_ref, buf, sem, *, rows):
    i = pl.program_id(0)
    n_blocks = pl.num_programs(0)

    def row_copy(slot, k, r, src_row):
        return pltpu.make_async_copy(ys_hbm.at[pl.ds(src_row, 1)], buf.at[slot, k, pl.ds(r, 1)], sem.at[slot])

    def fetch(block, slot):
        def issue(r, c):
            a = (block * rows + r) * TOP_K
            for k in range(TOP_K):
                row_copy(slot, k, r, pos_ref[a + k]).start()
            return c

        lax.fori_loop(0, rows, issue, 0, unroll=4)

    @pl.when(i == 0)
    def _():
        fetch(0, 0)

    @pl.when(i + 1 < n_blocks)
    def _():
        fetch(i + 1, (i + 1) % 2)

    slot = i % 2

    def drain(r, c):
        for k in range(TOP_K):
            row_copy(slot, k, r, 0).wait()
        return c

    lax.fori_loop(0, rows, drain, 0, unroll=4)
    out = x_ref[...]
    for k in range(TOP_K):
        out = out + gw_ref[:, k:k + 1] * buf[slot, k]
    o_ref[...] = out


def moe_combine(x, ys, dest, gw, *, rows=256):
    T, D = x.shape
    return pl.pallas_call(
        functools.partial(_moe_combine_kernel, rows=rows),
        grid_spec=pltpu.PrefetchScalarGridSpec(
            num_scalar_prefetch=1,
            grid=(T // rows,),
            in_specs=[pl.BlockSpec((rows, D), lambda i, pos: (i, 0)),
                      pl.BlockSpec((rows, TOP_K), lambda i, pos: (i, 0)),
                      pl.BlockSpec(memory_space=pl.ANY)],
            out_specs=pl.BlockSpec((rows, D), lambda i, pos: (i, 0)),
            scratch_shapes=[pltpu.VMEM((2, TOP_K, rows, D), F32), pltpu.SemaphoreType.DMA((2,))]),
        out_shape=jax.ShapeDtypeStruct((T, D), F32),
        compiler_params=_params(("arbitrary",), 12 * _nbytes((rows, D), F32)),
        name="moe_combine",
    )(dest, x, gw, ys)


def moe_block(x, g, w_router, b_router, w_egate, w_eup, w_edown, *, tm=512):
    idx, gw = moe_router(x, g, w_router, b_router)
    dest, row_tok, be, n_used, nxt, nv = _moe_dispatch(idx, tm)
    xs = moe_gather_norm(x, g, row_tok, n_used, tm=tm)
    hm = moe_gateup(xs, w_egate, w_eup, be, n_used, nxt, nv, tm=tm)
    ys = moe_down(hm, w_edown, be, n_used, nxt, nv, tm=tm)
    return moe_combine(x, ys, dest, gw)


def even_mixer_block(x, tabs_a, tabs_b, w_in, ga_q, ga_k, g_cq, w_uq, g_ckv, w_ukv, gb_q, gb_kn, gb_kr,
                     w_o, g_mix, *, batch, seq):
    a_width = A_HEADS * HEAD_DIM
    main_cols = 3 * a_width + Q_LORA + KV_LORA
    h = rmsnorm(x, g_mix)
    w_in_t = jnp.swapaxes(w_in, 0, 1)
    proj = matmul(h, w_in_t, n_out=main_cols, tm=2048, single_buffer_x=True, w_transposed=True, name="in_proj")
    w_kr = jnp.pad(w_in_t[main_cols:, :], ((0, LANES - ROPE_DIM), (0, 0)))
    krp = matmul(h, w_kr, w_transposed=True, name="in_proj_kr")
    o_a = mixer_a(proj, tabs_a, ga_q, ga_k, batch=batch, seq=seq)
    w3 = w_uq.reshape(Q_LORA, B_HEADS, QK_DIM)
    w_uq_p = jnp.concatenate(
        [w3[:, :, :NOPE_DIM].reshape(Q_LORA, B_HEADS * NOPE_DIM),
         jnp.pad(w3[:, :, NOPE_DIM:], ((0, 0), (0, 0), (0, LANES - ROPE_DIM))).reshape(Q_LORA, B_HEADS * LANES)],
        axis=1)
    q_up = norm_matmul(proj, g_cq, w_uq_p, width=Q_LORA, col_block=3 * a_width // Q_LORA, name="q_up")
    kv_up = norm_matmul(proj, g_ckv, w_ukv, width=KV_LORA, col_block=(3 * a_width + Q_LORA) // KV_LORA,
                        name="kv_up")
    pad_r = lambda v: jnp.pad(v, (0, LANES - ROPE_DIM)).reshape(1, LANES)
    qt, kf, vb = mla_prep(q_up, kv_up, krp, tabs_b, gb_q[:NOPE_DIM].reshape(1, LANES), pad_r(gb_q[NOPE_DIM:]),
                          gb_kn.reshape(1, LANES), pad_r(gb_kr), batch=batch, seq=seq)
    o_b = mla_attention(qt, kf, vb, batch=batch, seq=seq)
    return matmul((o_a, o_b), w_o, res=x, name="mixer_out")


def kernel(x, mem, positions, g_mem, w_mem_kv, g_mem_k, g_mix, g_x, w_xq, g_xq, w_xo, g_ffn,
           w_in, ga_q, ga_k, g_cq, w_uq, g_ckv, w_ukv, gb_q, gb_kn, gb_kr, w_o_even,
           w_gate, w_up, w_down, w_qkv, w_o_odd, w_router, b_router, w_egate, w_eup, w_edown):
    B, S, D = x.shape
    M = mem.shape[1]
    depth = g_mix.shape[0]
    T = B * S
    xt = x.reshape(T, D)
    tabs_a = _rope_tables(positions, ROT_DIM)
    tabs_b = _rope_tables(positions, ROPE_DIM)
    kv = matmul(rmsnorm(mem.reshape(B * M, D), g_mem), w_mem_kv, tm=B * M, name="mem_kv")
    mk, mv = memkv_post(kv, g_mem_k)
    for layer in range(depth):
        i = layer // 2
        if layer % 2 == 0:
            xt = even_mixer_block(xt, tabs_a, tabs_b, w_in[i], ga_q[i], ga_k[i], g_cq[i], w_uq[i], g_ckv[i],
                                  w_ukv[i], gb_q[i], gb_kn[i], gb_kr[i], w_o_even[i], g_mix[layer],
                                  batch=B, seq=S)
        else:
            h = rmsnorm(xt, g_mix[layer])
            qkv = matmul(h, w_qkv[i], out_dtype=BF16, tm=2048, single_buffer_x=True, name="qkv_proj")
            o = stickbreak_attention(qkv, batch=B, seq=S, heads=D // HEAD_DIM)
            xt = matmul(o, w_o_odd[i], res=xt, name="mixer_out")
        xattn = functools.partial(cross_attention_block, xt, mk, mv, g_x[layer], w_xq[layer], g_xq[layer],
                                  w_xo[layer], seq=S, mem_len=M)
        if layer % 2 == 0:
            xt, h = xattn(g_ffn[layer])
            hm = swiglu_gateup(h, w_gate[i], w_up[i])
            xt = matmul(hm, w_down[i], res=xt, tn=1024, tk=2048, name="swiglu_down")
        else:
            xt = moe_block(xattn(), g_ffn[layer], w_router[i], b_router[i], w_egate[i], w_eup[i], w_edown[i])
    return xt.reshape(B, S, D)
```
